```python
import math
import jax, jax.numpy as jnp
from jax import lax
import numpy as np

D_MODEL = 1024
BATCH = 4
SEQ = 4096
DEPTH = 1
DEC_BATCH = 32
DEC_SEQ = 16
PAST_LEN = 1024

CHUNK = 64
N_META = 16
Q_BLOCK = 128
SB_HEADS = 16
SB_HEAD_DIM = 64
SB_WIDTH = SB_HEADS * SB_HEAD_DIM
SSD_WIDTH = 2 * D_MODEL
SSD_HEAD_DIM = 64
SSD_HEADS = SSD_WIDTH // SSD_HEAD_DIM
SSD_GROUPS = 4
SSD_HEADS_PER_GROUP = SSD_HEADS // SSD_GROUPS
SSD_STATE = 128
SSD_CONV = 4
SSD_CONV_CH = SSD_WIDTH + 2 * SSD_GROUPS * SSD_STATE
D_FF = ((8 * D_MODEL // 3 + 127) // 128) * 128
FFN_CONV = 3
N_BRANCH = 2
IN_COLS = 3 * SB_WIDTH + SSD_WIDTH + SSD_CONV_CH + SSD_HEADS + N_BRANCH * D_MODEL
EPS = 1e-6

kernel_name = 'streaming_stickbreak_ssd_hybrid_step'


def rms_norm(x, g):
    xf = x.astype(jnp.float32)
    y = xf * lax.rsqrt(jnp.mean(xf * xf, axis=-1, keepdims=True) + EPS)
    return (y * g.astype(jnp.float32)).astype(x.dtype)


def causal_dwconv(x, prev, w, bias):
    k_width, length = w.shape[0], x.shape[1]
    xp = jnp.concatenate([prev.astype(x.dtype), x], axis=1)
    y = bias
    for i in range(k_width):
        y = y + xp[:, i:i + length] * w[i]
    return y, xp[:, length:]


def stick_breaking(q, k, v, q_pos, k_pos):
    z = jnp.einsum('bqhd,bshd->bhqs', q, k).astype(jnp.float32) * (SB_HEAD_DIM ** -0.5)
    reach = k_pos[None, :] < q_pos[:, None]
    log_beta = jax.nn.log_sigmoid(z)
    log_keep = jnp.where(reach, jax.nn.log_sigmoid(-z), 0.0)
    later = lax.cumsum(log_keep, axis=3, reverse=True) - log_keep
    w = jnp.where(reach, jnp.exp(log_beta + later), 0.0)
    return jnp.einsum('bhqs,bshd->bqhd', w.astype(v.dtype), v)


def stick_breaking_prompt(q, k, v):
    b, length, h, d = q.shape
    nb = -(-length // Q_BLOCK)
    pad = nb * Q_BLOCK - length
    qb = jnp.pad(q, ((0, 0), (0, pad), (0, 0), (0, 0))).reshape(b, nb, Q_BLOCK, h, d).transpose(1, 0, 2, 3, 4)
    q_pos = jnp.arange(nb * Q_BLOCK).reshape(nb, Q_BLOCK)
    k_pos = jnp.arange(length)
    out = lax.map(lambda blk: stick_breaking(blk[0], k, v, blk[1], k_pos), (qb, q_pos))
    return out.transpose(1, 0, 2, 3, 4).reshape(b, nb * Q_BLOCK, h, d)[:, :length]


def ssd_scan(x, dt, a_head, b_in, c_in, h0, pad_left):
    bsz, length = x.shape[:2]
    G, R, P, N = SSD_GROUPS, SSD_HEADS_PER_GROUP, SSD_HEAD_DIM, SSD_STATE
    pad_right = (-(pad_left + length)) % CHUNK
    nc = (pad_left + length + pad_right) // CHUNK

    def chunked(t):
        t = jnp.pad(t, [(0, 0), (pad_left, pad_right)] + [(0, 0)] * (t.ndim - 2))
        return t.reshape((bsz, nc, CHUNK) + t.shape[2:])

    xc = chunked(x).reshape(bsz, nc, CHUNK, G, R, P)
    dtc = chunked(dt).reshape(bsz, nc, CHUNK, G, R)
    bc, cc = chunked(b_in), chunked(c_in)
    dt_t = jnp.moveaxis(dtc, 2, -1)
    a_cum = jnp.cumsum(dt_t * a_head.reshape(G, R)[..., None], axis=-1)
    causal = jnp.tril(jnp.ones((CHUNK, CHUNK), dtype=bool))
    decay_ts = jnp.exp(jnp.where(causal, a_cum[..., :, None] - a_cum[..., None, :], -jnp.inf))
    cb = jnp.einsum('bctgn,bcsgn->bcgts', cc, bc)
    y_diag = jnp.einsum('bcgrts,bcsgrp->bctgrp', cb[:, :, :, None] * decay_ts * dt_t[..., None, :], xc)
    to_end = jnp.exp(a_cum[..., -1:] - a_cum) * dt_t
    chunk_states = jnp.einsum('bcgrs,bcsgn,bcsgrp->bcgrpn', to_end, bc, xc)
    chunk_decay = jnp.exp(a_cum[..., -1])

    def step(h, inp):
        dec, st = inp
        return dec[..., None, None] * h + st, h

    h_init = h0.reshape(bsz, G, R, P, N).astype(jnp.float32)
    h_final, h_in = lax.scan(step, h_init, (jnp.moveaxis(chunk_decay, 1, 0), jnp.moveaxis(chunk_states, 1, 0)))
    h_in = jnp.moveaxis(h_in, 0, 1)
    y_off = jnp.einsum('bctgn,bcgrpn,bcgrt->bctgrp', cc, h_in, jnp.exp(a_cum))
    y = (y_diag + y_off).reshape(bsz, nc * CHUNK, G * R, P)[:, pad_left:pad_left + length]
    return y.astype(x.dtype), h_final.reshape(bsz, G * R, P, N).astype(h0.dtype)


def ssd_mix(xbc, z, dt_raw, conv_prev, h0, conv_w, conv_b, dt_bias, a_log, d_skip, g_norm, pad_left):
    bsz, length, _ = xbc.shape
    xbc, conv_state = causal_dwconv(xbc, conv_prev, conv_w, conv_b)
    xbc = jax.nn.silu(xbc)
    xs, b_in, c_in = jnp.split(xbc, [SSD_WIDTH, SSD_WIDTH + SSD_GROUPS * SSD_STATE], axis=-1)
    xs = xs.reshape(bsz, length, SSD_HEADS, SSD_HEAD_DIM)
    b_in = b_in.reshape(bsz, length, SSD_GROUPS, SSD_STATE)
    c_in = c_in.reshape(bsz, length, SSD_GROUPS, SSD_STATE)
    dt = jax.nn.softplus(dt_raw.astype(jnp.float32) + dt_bias.astype(jnp.float32))
    a_head = -jnp.exp(a_log.astype(jnp.float32))
    y, h_final = ssd_scan(xs, dt, a_head, b_in, c_in, h0, pad_left)
    y = (y + d_skip[:, None] * xs).reshape(bsz, length, SSD_WIDTH) * jax.nn.silu(z)
    y = rms_norm(y.reshape(bsz, length, SSD_GROUPS, SSD_WIDTH // SSD_GROUPS), g_norm.reshape(SSD_GROUPS, -1))
    return y.reshape(bsz, length, SSD_WIDTH), h_final, conv_state


def trunk_layer(h, p, kv_past, ssd_h0, ssd_conv_prev, ffn_conv_prev, pad_left):
    bsz, length, _ = h.shape
    u = rms_norm(h, p['g_mix'])
    proj = u @ p['w_in']
    sizes = [SB_WIDTH, SB_WIDTH, SB_WIDTH, SSD_WIDTH, SSD_CONV_CH, SSD_HEADS, N_BRANCH * D_MODEL]
    q, k, v, z, xbc, dt_raw, gate_logits = jnp.split(proj, np.cumsum(sizes)[:-1].tolist(), axis=-1)
    q, k, v = (t.reshape(bsz, length, SB_HEADS, SB_HEAD_DIM) for t in (q, k, v))
    if kv_past is None:
        att = stick_breaking_prompt(q, k, v)
    else:
        n_past = kv_past[0].shape[1]
        k_all = jnp.concatenate([kv_past[0].astype(k.dtype), k], axis=1)
        v_all = jnp.concatenate([kv_past[1].astype(v.dtype), v], axis=1)
        att = stick_breaking(q, k_all, v_all, n_past + jnp.arange(length), jnp.arange(n_past + length))
    ssd, ssd_state, ssd_conv = ssd_mix(xbc, z, dt_raw, ssd_conv_prev, ssd_h0, p['conv_w_ssd'], p['conv_b_ssd'],
                                       p['dt_bias'], p['a_log'], p['d_skip'], p['g_ssd_norm'], pad_left)
    gate_att, gate_ssd = jnp.split(jax.nn.sigmoid(gate_logits), 2, axis=-1)
    merged = (gate_att * (att.reshape(bsz, length, SB_WIDTH) @ p['w_br_att'])
              + gate_ssd * (ssd @ p['w_br_ssd']))
    h = h + merged @ p['w_out']
    u = rms_norm(h, p['g_ffn'])
    up, ffn_conv = causal_dwconv(u @ p['w_up'], ffn_conv_prev, p['conv_w_ffn'], p['conv_b_ffn'])
    gate, val = jnp.split(up, 2, axis=-1)
    h = h + (jax.nn.silu(gate) * val) @ p['w_down']
    return h, k, v, ssd_state, ssd_conv, ffn_conv


def setup_inputs(seed: int = 0) -> dict:
    key = jax.random.key(seed)
    ks = jax.random.split(key, 26)
    f32 = jnp.float32

    def nrm(k, shape, scale):
        return scale * jax.random.normal(k, shape, f32)

    def gain(k, shape):
        return 1.0 + 0.01 * jax.random.normal(k, shape, f32)

    n_rows = N_META + PAST_LEN
    dt0 = jnp.exp(jax.random.uniform(ks[10], (DEPTH, SSD_HEADS), f32, math.log(1e-3), math.log(1e-1)))
    return {
        'x_prompt': nrm(ks[0], (BATCH, SEQ, D_MODEL), 1.0),
        'x_sample': nrm(ks[1], (DEC_BATCH, DEC_SEQ, D_MODEL), 1.0),
        'cache_k': nrm(ks[2], (DEPTH, DEC_BATCH, n_rows, SB_HEADS, SB_HEAD_DIM), 1.0),
        'cache_v': nrm(ks[3], (DEPTH, DEC_BATCH, n_rows, SB_HEADS, SB_HEAD_DIM), 1.0),
        'state_ssm': nrm(ks[4], (DEPTH, DEC_BATCH, SSD_HEADS, SSD_HEAD_DIM, SSD_STATE), 0.1),
        'state_ssm_conv': nrm(ks[5], (DEPTH, DEC_BATCH, SSD_CONV - 1, SSD_CONV_CH), 1.0),
        'state_ffn_conv': nrm(ks[6], (DEPTH, DEC_BATCH, FFN_CONV - 1, 2 * D_FF), 1.0),
        'meta_tokens': nrm(ks[7], (N_META, D_MODEL), 1.0),
        'g_mix': gain(ks[8], (DEPTH, D_MODEL)),
        'w_in': nrm(ks[9], (DEPTH, D_MODEL, IN_COLS), D_MODEL ** -0.5),
        'conv_w_ssd': nrm(ks[11], (DEPTH, SSD_CONV, SSD_CONV_CH), SSD_CONV ** -0.5),
        'conv_b_ssd': nrm(ks[12], (DEPTH, SSD_CONV_CH), 0.02),
        'dt_bias': dt0 + jnp.log(-jnp.expm1(-dt0)),
        'a_log': jnp.log(jax.random.uniform(ks[13], (DEPTH, SSD_HEADS), f32, 1.0, 16.0)),
        'd_skip': 1.0 + 0.1 * jax.random.normal(ks[14], (DEPTH, SSD_HEADS), f32),
        'g_ssd_norm': gain(ks[15], (DEPTH, SSD_WIDTH)),
        'w_br_att': nrm(ks[16], (DEPTH, SB_WIDTH, D_MODEL), SB_WIDTH ** -0.5),
        'w_br_ssd': nrm(ks[17], (DEPTH, SSD_WIDTH, D_MODEL), SSD_WIDTH ** -0.5),
        'w_out': nrm(ks[18], (DEPTH, D_MODEL, D_MODEL), D_MODEL ** -0.5),
        'g_ffn': gain(ks[19], (DEPTH, D_MODEL)),
        'w_up': nrm(ks[20], (DEPTH, D_MODEL, 2 * D_FF), D_MODEL ** -0.5),
        'conv_w_ffn': nrm(ks[21], (DEPTH, FFN_CONV, 2 * D_FF), FFN_CONV ** -0.5),
        'conv_b_ffn': nrm(ks[22], (DEPTH, 2 * D_FF), 0.02),
        'w_down': nrm(ks[23], (DEPTH, D_FF, D_MODEL), D_FF ** -0.5),
        'g_final': gain(ks[24], (D_MODEL,)),
    }


def reference(x_prompt, x_sample, cache_k, cache_v, state_ssm, state_ssm_conv, state_ffn_conv, meta_tokens,
              g_mix, w_in, conv_w_ssd, conv_b_ssd, dt_bias, a_log, d_skip, g_ssd_norm, w_br_att, w_br_ssd,
              w_out, g_ffn, w_up, conv_w_ffn, conv_b_ffn, w_down, g_final):
    bp, bs = x_prompt.shape[0], x_sample.shape[0]
    dtype = x_prompt.dtype
    meta = jnp.broadcast_to(meta_tokens.astype(dtype)[None], (bp, N_META, D_MODEL))
    hp = jnp.concatenate([meta, x_prompt], axis=1)
    hs = x_sample
    meta_pad = (-N_META) % CHUNK
    kp_l, vp_l, sp_l, cp_l, fp_l = [], [], [], [], []
    ks_l, vs_l, ss_l, cs_l, fs_l = [], [], [], [], []
    for l in range(DEPTH):
        p = {'g_mix': g_mix[l], 'w_in': w_in[l], 'conv_w_ssd': conv_w_ssd[l], 'conv_b_ssd': conv_b_ssd[l],
             'dt_bias': dt_bias[l], 'a_log': a_log[l], 'd_skip': d_skip[l], 'g_ssd_norm': g_ssd_norm[l],
             'w_br_att': w_br_att[l], 'w_br_ssd': w_br_ssd[l], 'w_out': w_out[l], 'g_ffn': g_ffn[l],
             'w_up': w_up[l], 'conv_w_ffn': conv_w_ffn[l], 'conv_b_ffn': conv_b_ffn[l], 'w_down': w_down[l]}
        hp, kp, vp, sp, cp, fp = trunk_layer(
            hp, p, None,
            jnp.zeros((bp, SSD_HEADS, SSD_HEAD_DIM, SSD_STATE), dtype),
            jnp.zeros((bp, SSD_CONV - 1, SSD_CONV_CH), dtype),
            jnp.zeros((bp, FFN_CONV - 1, 2 * D_FF), dtype),
            meta_pad)
        hs, ks_, vs_, ss, cs, fs = trunk_layer(
            hs, p, (cache_k[l], cache_v[l]), state_ssm[l], state_ssm_conv[l], state_ffn_conv[l], 0)
        kp_l.append(kp); vp_l.append(vp); sp_l.append(sp); cp_l.append(cp); fp_l.append(fp)
        ks_l.append(ks_); vs_l.append(vs_); ss_l.append(ss); cs_l.append(cs); fs_l.append(fs)
    y_prompt = rms_norm(hp, g_final)[:, N_META:]
    y_sample = rms_norm(hs, g_final)
    return (y_prompt, y_sample,
            jnp.stack(kp_l), jnp.stack(vp_l), jnp.stack(sp_l), jnp.stack(cp_l), jnp.stack(fp_l),
            jnp.stack(ks_l), jnp.stack(vs_l), jnp.stack(ss_l), jnp.stack(cs_l), jnp.stack(fs_l))
```

```python
import functools

import numpy as np
import jax
import jax.numpy as jnp
from jax import lax
from jax.experimental import pallas as pl
from jax.experimental.pallas import tpu as pltpu

F32 = jnp.float32
BF16 = jnp.bfloat16

D_MODEL = 1024
N_META = 16
SB_HEADS = 16
SB_HEAD_DIM = 64
SB_WIDTH = SB_HEADS * SB_HEAD_DIM
SSD_WIDTH = 2 * D_MODEL
SSD_HEAD_DIM = 64
SSD_HEADS = SSD_WIDTH // SSD_HEAD_DIM
SSD_GROUPS = 4
SSD_STATE = 128
SSD_CONV = 4
SSD_CONV_CH = SSD_WIDTH + 2 * SSD_GROUPS * SSD_STATE
D_FF = 2816
FFN_CONV = 3
EPS = 1e-6

LANES = 128
SSD_CHUNK = 64
KEY_TILE = 128
GROUP_W = SSD_WIDTH // SSD_GROUPS
DEAD_LOG = -104.0
VMEM_LIMIT = 56 * 1024 * 1024


def _cparams(sem):
    return pltpu.CompilerParams(dimension_semantics=sem, vmem_limit_bytes=VMEM_LIMIT)


def _pick(n, cands):
    for c in cands:
        if n % c == 0:
            return c
    return n


def _split2(x):
    hi = x.astype(BF16)
    lo = (x - hi.astype(F32)).astype(BF16)
    return hi, lo


def _split3(x):
    hi = x.astype(BF16)
    r = x - hi.astype(F32)
    mid = r.astype(BF16)
    lo = (r - mid.astype(F32)).astype(BF16)
    return hi, mid, lo


def _dot(a, b):
    return jnp.dot(a, b, preferred_element_type=F32)


def _dot_nt(a, b):
    return lax.dot_general(a, b, (((1,), (1,)), ((), ())), preferred_element_type=F32)


def _softplus(x):
    return jnp.maximum(x, 0.0) + jnp.log(1.0 + jnp.exp(-jnp.abs(x)))


def _sigmoid(x):
    return 1.0 / (1.0 + jnp.exp(-x))


def _norm_mm_kernel(x_ref, g_ref, w_ref, o_ref, u_ref):
    @pl.when(pl.program_id(1) == 0)
    def _():
        x = x_ref[...]
        ms = jnp.mean(x * x, axis=-1, keepdims=True)
        u_ref[...] = (x * lax.rsqrt(ms + EPS) * g_ref[...]).astype(BF16)

    o_ref[...] = _dot(u_ref[...], w_ref[...]).astype(o_ref.dtype)


def norm_mm(x, g, w, out_dtype):
    m, d = x.shape
    n = w.shape[1]
    tm = _pick(m, (1024, 512, 256, 128))
    tn = _pick(n, (1024, 1408, 512, 256, 128))
    return pl.pallas_call(
        _norm_mm_kernel,
        grid=(m // tm, n // tn),
        in_specs=[pl.BlockSpec((tm, d), lambda i, j: (i, 0)),
                  pl.BlockSpec((1, d), lambda i, j: (0, 0)),
                  pl.BlockSpec((d, tn), lambda i, j: (0, j))],
        out_specs=pl.BlockSpec((tm, tn), lambda i, j: (i, j)),
        out_shape=jax.ShapeDtypeStruct((m, n), out_dtype),
        scratch_shapes=[pltpu.VMEM((tm, d), BF16)],
        compiler_params=_cparams(("parallel", "arbitrary")),
    )(x, g, w)


def _attn_kernel(*refs, tq, has_cache, n_cache_tiles):
    if has_cache:
        q_ref, kn_ref, vn_ref, u2_ref, kc_ref, vc_ref, o_ref = refs[:7]
        scr = refs[7:]
    else:
        q_ref, kn_ref, vn_ref, u2_ref, o_ref = refs[:5]
        kc_ref = vc_ref = None
        scr = refs[5:]

    lane = lax.broadcasted_iota(jnp.int32, (tq, LANES), 1)
    row = lax.broadcasted_iota(jnp.int32, (tq, LANES), 0)
    reach_diag = lane < row
    u2 = u2_ref[...]
    q = q_ref[0]

    if tq == KEY_TILE:
        kd = kn_ref[0].astype(BF16)
        vd = vn_ref[0].astype(BF16)
    else:
        kd_ref, vd_ref = scr
        kd_ref[...] = jnp.zeros_like(kd_ref)
        vd_ref[...] = jnp.zeros_like(vd_ref)
        kd_ref[0:tq, :] = kn_ref[0].astype(BF16)
        vd_ref[0:tq, :] = vn_ref[0].astype(BF16)
        kd = kd_ref[...]
        vd = vd_ref[...]

    def tile(qh, kt, vt, carry, acc, mask):
        z = _dot_nt(qh, kt)
        soft = jnp.log(1.0 + jnp.exp(-jnp.abs(z)))
        log_beta = jnp.minimum(z, 0.0) - soft
        log_keep = jnp.where(mask, log_beta - z, 0.0)
        hi, lo = _split2(log_keep)
        r = _dot(jnp.concatenate([hi, lo], axis=1), u2)
        w = jnp.where(mask, jnp.exp(log_beta + r[:, :LANES] + carry), 0.0)
        acc = acc + _dot(w.astype(BF16), vt)
        return carry + r[:, LANES:], acc

    if has_cache:
        j0 = pl.program_id(2) - 1 if n_cache_tiles is None else n_cache_tiles - 1

    outs = []
    for hh in range(2):
        in_head = (lane >= hh * SB_HEAD_DIM) & (lane < (hh + 1) * SB_HEAD_DIM)
        qh = jnp.where(in_head, q, jnp.zeros_like(q)) * jnp.asarray(SB_HEAD_DIM ** -0.5, BF16)
        zeros = jnp.zeros((tq, LANES), F32)
        carry, acc = tile(qh, kd, vd, zeros, zeros, reach_diag)
        if has_cache:
            def cond(s):
                j, alive, _, _ = s
                return jnp.logical_and(j >= -1, alive > 0)

            def body(s, qh=qh):
                j, _, carry, acc = s
                off = pl.multiple_of(jnp.where(j >= 0, N_META + j * KEY_TILE, 0), 8)
                kt = kc_ref[0, pl.ds(off, KEY_TILE), :].astype(BF16)
                vt = vc_ref[0, pl.ds(off, KEY_TILE), :].astype(BF16)
                mask = jnp.logical_or(lane < N_META, j >= 0)
                carry, acc = tile(qh, kt, vt, carry, acc, mask)
                alive = (jnp.max(carry) > DEAD_LOG).astype(jnp.int32)
                return j - 1, alive, carry, acc

            alive0 = (jnp.max(carry) > DEAD_LOG).astype(jnp.int32)
            _, _, carry, acc = lax.while_loop(cond, body, (j0, alive0, carry, acc))
        outs.append(acc)
    o_ref[0] = jnp.where(lane < SB_HEAD_DIM, outs[0], outs[1]).astype(o_ref.dtype)


def _cumsum_rhs():
    s = np.arange(KEY_TILE)
    strict = (s[:, None] > s[None, :]).astype(np.float32)
    half = np.concatenate([strict, np.ones((KEY_TILE, KEY_TILE), np.float32)], axis=1)
    return jnp.asarray(np.concatenate([half, half], axis=0), BF16)


def attention(q_src, q_col0, k_new, v_new, k_cache, v_cache, n_cache_tiles):
    b, s, _ = q_src.shape
    tq = min(s, KEY_TILE)
    has_cache = k_cache is not None
    nhp = SB_WIDTH // LANES
    in_specs = [pl.BlockSpec((1, tq, LANES), lambda bi, hp, qi: (bi, qi, q_col0 + hp)),
                pl.BlockSpec((1, tq, LANES), lambda bi, hp, qi: (bi, qi, hp)),
                pl.BlockSpec((1, tq, LANES), lambda bi, hp, qi: (bi, qi, hp)),
                pl.BlockSpec((2 * KEY_TILE, 2 * KEY_TILE), lambda bi, hp, qi: (0, 0))]
    args = [q_src, k_new, v_new, _cumsum_rhs()]
    if has_cache:
        lc = k_cache.shape[1]
        in_specs += [pl.BlockSpec((1, lc, LANES), lambda bi, hp, qi: (bi, 0, hp))] * 2
        args += [k_cache, v_cache]
    scratch = [] if tq == KEY_TILE else [pltpu.VMEM((KEY_TILE, LANES), BF16)] * 2
    return pl.pallas_call(
        functools.partial(_attn_kernel, tq=tq, has_cache=has_cache, n_cache_tiles=n_cache_tiles),
        grid=(b, nhp, s // tq),
        in_specs=in_specs,
        out_specs=pl.BlockSpec((1, tq, LANES), lambda bi, hp, qi: (bi, qi, hp)),
        out_shape=jax.ShapeDtypeStruct((b, s, SB_WIDTH), BF16),
        scratch_shapes=scratch,
        compiler_params=_cparams(("parallel", "parallel", "arbitrary")),
    )(*args)


def _ssd_kernel(xbc_ref, z_ref, dt_ref, dtT_ref, prev_ref, h0_ref, cw_ref, cb_ref, dtb_ref, dtbT_ref,
                alog_ref, alogT_ref, dskip_ref, gn_ref, e3_ref, ltri_ref, ublk_ref,
                y_ref, hfin_ref, cbuf_ref, st_ref, *, rows, n_chunks):
    Q = SSD_CHUNK
    c = pl.program_id(1)
    n_blk = SSD_WIDTH // LANES

    @pl.when(c == 0)
    def _():
        cbuf_ref[8 - (SSD_CONV - 1):8, :] = prev_ref[0]
        for j in range(n_blk):
            st_ref[:, j * LANES:(j + 1) * LANES] = h0_ref[0, j * LANES:(j + 1) * LANES, :].T

    def pad_rows(v):
        if rows == Q:
            return v
        return jnp.concatenate([v, jnp.zeros((Q - rows, v.shape[1]), v.dtype)], axis=0)

    cbuf_ref[8:8 + Q, :] = pad_rows(xbc_ref[0].astype(F32))
    conv = cb_ref[...]
    for i in range(SSD_CONV):
        conv = conv + cbuf_ref[8 - (SSD_CONV - 1) + i:8 - (SSD_CONV - 1) + i + Q, :] * cw_ref[i:i + 1, :]
    cbuf_ref[8 - (SSD_CONV - 1):8, :] = cbuf_ref[8 + Q - (SSD_CONV - 1):8 + Q, :]
    xc = conv * _sigmoid(conv)
    xs = xc[:, :SSD_WIDTH]
    b_all = xc[:, SSD_WIDTH:SSD_WIDTH + SSD_GROUPS * SSD_STATE]
    c_all = xc[:, SSD_WIDTH + SSD_GROUPS * SSD_STATE:]

    dt = _softplus(pad_rows(dt_ref[0]) + dtb_ref[...])
    dt_t = _softplus(dtT_ref[0, 0] + dtbT_ref[...])
    if rows < Q:
        rowi = lax.broadcasted_iota(jnp.int32, (Q, 1), 0)
        dt = jnp.where(rowi < rows, dt, 0.0)
        xs = jnp.where(rowi < rows, xs, 0.0)
        lane_t = lax.broadcasted_iota(jnp.int32, dt_t.shape, 1)
        dt_t = jnp.where(lane_t % Q < rows, dt_t, 0.0)
    da = dt * (-jnp.exp(alog_ref[...]))
    da_t = dt_t * (-jnp.exp(alogT_ref[...]))

    a_cum = _dot(ltri_ref[...], jnp.concatenate(_split3(da), axis=0))
    a_cum_t = _dot(jnp.concatenate(_split3(da_t), axis=1), ublk_ref[...])
    dt_exp = _dot(jnp.concatenate(_split2(dt), axis=1), e3_ref[0:2 * LANES, :])
    a_exp = _dot(jnp.concatenate(_split3(a_cum), axis=1), e3_ref[...])
    a_last = a_exp[Q - 1:Q, :]

    xdt = xs * dt_exp
    xdt_b = xdt.astype(BF16)
    xw = (xdt * jnp.exp(a_last - a_exp)).astype(BF16)
    xw_pad = jnp.concatenate([xw, jnp.zeros_like(xw)], axis=0)
    chunk_decay = jnp.exp(a_last)
    grow = jnp.exp(a_exp)

    lane = lax.broadcasted_iota(jnp.int32, (Q, LANES), 1)
    rowq = lax.broadcasted_iota(jnp.int32, (Q, LANES), 0)
    causal2 = (lane % Q) <= rowq
    lo_half = lane < SSD_HEAD_DIM

    y_parts = []
    for g in range(SSD_GROUPS):
        bg = b_all[:, g * SSD_STATE:(g + 1) * SSD_STATE]
        cg = c_all[:, g * SSD_STATE:(g + 1) * SSD_STATE].astype(BF16)
        bg_b = bg.astype(BF16)
        cb2 = _dot_nt(cg, jnp.concatenate([bg_b, bg_b], axis=0))
        gs = slice(g * GROUP_W, (g + 1) * GROUP_W)
        st_g = st_ref[:, gs]
        y_off = _dot(cg, st_g.astype(BF16)) * grow[:, gs]
        bg_t = jnp.concatenate([bg, jnp.zeros_like(bg)], axis=0).T.astype(BF16)
        st_ref[:, gs] = st_g * chunk_decay[:, gs] + _dot(bg_t, xw_pad[:, gs])
        pair_out = []
        for k in range(GROUP_W // LANES):
            i = g * (GROUP_W // LANES) + k
            ps = slice(i * LANES, (i + 1) * LANES)
            decay = jnp.exp(jnp.minimum(a_exp[:, ps] - a_cum_t[i:i + 1, :], 0.0))
            m2 = jnp.where(causal2, cb2 * decay, 0.0).astype(BF16)
            xp = xdt_b[:, ps]
            zero = jnp.zeros_like(xp)
            xbd = jnp.concatenate([jnp.where(lo_half, xp, zero), jnp.where(lo_half, zero, xp)], axis=0)
            pair_out.append(_dot(m2, xbd))
        y_diag = jnp.concatenate(pair_out, axis=1)
        y = y_diag + y_off + dskip_ref[:, gs] * xs[:, gs]
        zg = pad_rows(z_ref[0, :, gs].astype(F32))
        y = y * (zg * _sigmoid(zg))
        ms = jnp.mean(y * y, axis=-1, keepdims=True)
        y_parts.append(y * lax.rsqrt(ms + EPS) * gn_ref[:, gs])
    y_all = jnp.concatenate(y_parts, axis=1)
    y_ref[0] = y_all[:rows].astype(y_ref.dtype)

    @pl.when(c == n_chunks - 1)
    def _():
        for j in range(n_blk):
            hfin_ref[0, j * LANES:(j + 1) * LANES, :] = st_ref[:, j * LANES:(j + 1) * LANES].T


def _ssd_constants():
    Q = SSD_CHUNK
    e = np.zeros((LANES, SSD_WIDTH), np.float32)
    for h in range(SSD_HEADS):
        e[h, h * SSD_HEAD_DIM:(h + 1) * SSD_HEAD_DIM] = 1.0
    e3 = np.concatenate([e, e, e], axis=0)
    t = np.arange(Q)
    ltri = (t[:, None] >= t[None, :]).astype(np.float32)
    ltri3 = np.concatenate([ltri, ltri, ltri], axis=1)
    ublk = np.zeros((LANES, LANES), np.float32)
    ublk[:Q, :Q] = ltri.T
    ublk[Q:, Q:] = ltri.T
    ublk3 = np.concatenate([ublk, ublk, ublk], axis=0)
    return jnp.asarray(e3, BF16), jnp.asarray(ltri3, BF16), jnp.asarray(ublk3, BF16)


def ssd_mixer(big, xbc_blk, z_blk, dt_raw, conv_prev, h0, p):
    b, s, _ = big.shape
    Q = SSD_CHUNK
    rows = min(s, Q)
    n_chunks = -(-s // Q)
    dtp = dt_raw[:, :, :SSD_HEADS]
    if s < Q:
        dtp = jnp.pad(dtp, ((0, 0), (0, Q - s), (0, 0)))
    dt_t = dtp.reshape(b, n_chunks, Q, SSD_HEADS).transpose(0, 1, 3, 2).reshape(b, n_chunks, SSD_HEADS // 2, 2 * Q)
    e3, ltri3, ublk3 = _ssd_constants()
    const = lambda shape: pl.BlockSpec(shape, lambda bi, ci: (0,) * len(shape))
    in_specs = [
        pl.BlockSpec((1, rows, SSD_CONV_CH), lambda bi, ci: (bi, ci, xbc_blk)),
        pl.BlockSpec((1, rows, SSD_WIDTH), lambda bi, ci: (bi, ci, z_blk)),
        pl.BlockSpec((1, rows, LANES), lambda bi, ci: (bi, ci, 0)),
        pl.BlockSpec((1, 1, SSD_HEADS // 2, 2 * Q), lambda bi, ci: (bi, ci, 0, 0)),
        pl.BlockSpec((1, SSD_CONV - 1, SSD_CONV_CH), lambda bi, ci: (bi, 0, 0)),
        pl.BlockSpec((1, SSD_WIDTH, SSD_STATE), lambda bi, ci: (bi, 0, 0)),
        const((SSD_CONV, SSD_CONV_CH)), const((1, SSD_CONV_CH)),
        const((1, LANES)), const((SSD_HEADS // 2, 2 * Q)),
        const((1, LANES)), const((SSD_HEADS // 2, 2 * Q)),
        const((1, SSD_WIDTH)), const((1, SSD_WIDTH)),
        const(e3.shape), const(ltri3.shape), const(ublk3.shape),
    ]
    y, h_fin = pl.pallas_call(
        functools.partial(_ssd_kernel, rows=rows, n_chunks=n_chunks),
        grid=(b, n_chunks),
        in_specs=in_specs,
        out_specs=[pl.BlockSpec((1, rows, SSD_WIDTH), lambda bi, ci: (bi, ci, 0)),
                   pl.BlockSpec((1, SSD_WIDTH, SSD_STATE), lambda bi, ci: (bi, 0, 0))],
        out_shape=[jax.ShapeDtypeStruct((b, s, SSD_WIDTH), BF16),
                   jax.ShapeDtypeStruct((b, SSD_WIDTH, SSD_STATE), F32)],
        scratch_shapes=[pltpu.VMEM((8 + Q, SSD_CONV_CH), F32), pltpu.VMEM((SSD_STATE, SSD_WIDTH), F32)],
        compiler_params=_cparams(("parallel", "arbitrary")),
    )(big, big, dt_raw, dt_t, conv_prev, h0, p['conv_w_ssd'], p['conv_b_ssd'], p['dt_bias'], p['dt_bias_t'],
      p['a_log'], p['a_log_t'], p['d_skip'], p['g_ssd_norm'], e3, ltri3, ublk3)
    return y, h_fin


def _merge_kernel(att_ref, ssd_ref, ga_ref, gs_ref, h_ref, wa_ref, ws_ref, wo_ref, o_ref):
    a = _dot(att_ref[...], wa_ref[...])
    s = _dot(ssd_ref[...], ws_ref[...])
    merged = _sigmoid(ga_ref[...].astype(F32)) * a + _sigmoid(gs_ref[...].astype(F32)) * s
    o_ref[...] = h_ref[...] + _dot(merged.astype(BF16), wo_ref[...])


def merge(att, ssd, big, gate_blk, h, p):
    m = h.shape[0]
    tm = _pick(m, (512, 256, 128))
    row = lambda w: pl.BlockSpec((tm, w), lambda i: (i, 0))
    full = lambda a: pl.BlockSpec(a.shape, lambda i: (0, 0))
    return pl.pallas_call(
        _merge_kernel,
        grid=(m // tm,),
        in_specs=[row(SB_WIDTH), row(SSD_WIDTH),
                  pl.BlockSpec((tm, D_MODEL), lambda i: (i, gate_blk)),
                  pl.BlockSpec((tm, D_MODEL), lambda i: (i, gate_blk + 1)),
                  row(D_MODEL), full(p['w_br_att']), full(p['w_br_ssd']), full(p['w_out'])],
        out_specs=row(D_MODEL),
        out_shape=jax.ShapeDtypeStruct((m, D_MODEL), F32),
        compiler_params=_cparams(("parallel",)),
    )(att, ssd, big, big, h, p['w_br_att'], p['w_br_ssd'], p['w_out'])


def _ffn_act_kernel(ug_ref, uv_ref, pg_ref, pv_ref, wg_ref, wv_ref, bg_ref, bv_ref, o_ref, gbuf_ref, vbuf_ref,
                    *, ts):
    t = pl.program_id(1)
    k1 = FFN_CONV - 1

    def conv(u_ref, p_ref, w_ref, b_ref, buf_ref):
        @pl.when(t == 0)
        def _():
            buf_ref[8 - k1:8, :] = p_ref[0]

        buf_ref[8:8 + ts, :] = u_ref[0].astype(F32)
        y = b_ref[...]
        for i in range(FFN_CONV):
            y = y + buf_ref[8 - k1 + i:8 - k1 + i + ts, :] * w_ref[i:i + 1, :]
        buf_ref[8 - k1:8, :] = buf_ref[8 + ts - k1:8 + ts, :]
        return y

    gate = conv(ug_ref, pg_ref, wg_ref, bg_ref, gbuf_ref)
    val = conv(uv_ref, pv_ref, wv_ref, bv_ref, vbuf_ref)
    o_ref[0] = (gate * _sigmoid(gate) * val).astype(o_ref.dtype)


def ffn_act(up, prev, p):
    b, s, _ = up.shape
    ts = _pick(s, (512, 256, 128))
    half = lambda blk: pl.BlockSpec((1, ts, D_FF), lambda bi, ti: (bi, ti, blk))
    prevs = lambda blk: pl.BlockSpec((1, FFN_CONV - 1, D_FF), lambda bi, ti: (bi, 0, blk))
    wspec = lambda blk: pl.BlockSpec((FFN_CONV, D_FF), lambda bi, ti: (0, blk))
    bspec = lambda blk: pl.BlockSpec((1, D_FF), lambda bi, ti: (0, blk))
    return pl.pallas_call(
        functools.partial(_ffn_act_kernel, ts=ts),
        grid=(b, s // ts),
        in_specs=[half(0), half(1), prevs(0), prevs(1), wspec(0), wspec(1), bspec(0), bspec(1)],
        out_specs=pl.BlockSpec((1, ts, D_FF), lambda bi, ti: (bi, ti, 0)),
        out_shape=jax.ShapeDtypeStruct((b, s, D_FF), BF16),
        scratch_shapes=[pltpu.VMEM((8 + ts, D_FF), F32)] * 2,
        compiler_params=_cparams(("parallel", "arbitrary")),
    )(up, up, prev, prev, p['conv_w_ffn'], p['conv_w_ffn'], p['conv_b_ffn'], p['conv_b_ffn'])


def _down_kernel(a_ref, w_ref, h_ref, g_ref, o_ref):
    h = h_ref[...] + _dot(a_ref[...], w_ref[...])
    ms = jnp.mean(h * h, axis=-1, keepdims=True)
    o_ref[...] = h * lax.rsqrt(ms + EPS) * g_ref[...]


def down_norm(act, h, p):
    m = h.shape[0]
    tm = _pick(m, (512, 256, 128))
    return pl.pallas_call(
        _down_kernel,
        grid=(m // tm,),
        in_specs=[pl.BlockSpec((tm, D_FF), lambda i: (i, 0)),
                  pl.BlockSpec((D_FF, D_MODEL), lambda i: (0, 0)),
                  pl.BlockSpec((tm, D_MODEL), lambda i: (i, 0)),
                  pl.BlockSpec((1, D_MODEL), lambda i: (0, 0))],
        out_specs=pl.BlockSpec((tm, D_MODEL), lambda i: (i, 0)),
        out_shape=jax.ShapeDtypeStruct((m, D_MODEL), F32),
        compiler_params=_cparams(("parallel",)),
    )(act, p['w_down'], h, p['g_final'])


BIG_Z, BIG_Q, BIG_XBC, BIG_GATE = 0, SSD_WIDTH, SSD_WIDTH + SB_WIDTH, SSD_WIDTH + SB_WIDTH + SSD_CONV_CH


def stream_layer(h, kv_cache, n_cache_tiles, ssd_h0, ssd_conv_prev, ffn_conv_prev, p):
    b, s, _ = h.shape
    m = b * s
    hf = h.reshape(m, D_MODEL)
    kv = norm_mm(hf, p['g_mix'], p['w_kv'], F32)
    big = norm_mm(hf, p['g_mix'], p['w_big'], BF16)
    dt_raw = norm_mm(hf, p['g_mix'], p['w_dt'], F32)
    k = kv[:, :SB_WIDTH].reshape(b, s, SB_WIDTH)
    v = kv[:, SB_WIDTH:].reshape(b, s, SB_WIDTH)
    big3 = big.reshape(b, s, -1)
    return big3, k, v, dt_raw.reshape(b, s, LANES), hf


def stream_rest(big3, k, v, dt_raw, hf, kv_cache, n_cache_tiles, ssd_h0, ssd_conv_prev, ffn_conv_prev, p):
    b, s, _ = big3.shape
    m = b * s
    kc, vc = kv_cache if kv_cache is not None else (None, None)
    att = attention(big3, BIG_Q // LANES, k, v, kc, vc, n_cache_tiles)
    ssd, ssd_state = ssd_mixer(big3, BIG_XBC // SSD_CONV_CH, BIG_Z // SSD_WIDTH, dt_raw, ssd_conv_prev, ssd_h0, p)
    h1 = merge(att.reshape(m, SB_WIDTH), ssd.reshape(m, SSD_WIDTH), big3.reshape(m, -1), BIG_GATE // D_MODEL, hf, p)
    up = norm_mm(h1, p['g_ffn'], p['w_up'], BF16).reshape(b, s, 2 * D_FF)
    act = ffn_act(up, ffn_conv_prev, p)
    y = down_norm(act.reshape(m, D_FF), h1, p).reshape(b, s, D_MODEL)
    xbc_rows = jnp.concatenate([ssd_conv_prev, big3[:, :, BIG_XBC:BIG_XBC + SSD_CONV_CH][:, -(SSD_CONV - 1):].astype(F32)],
                               axis=1)[:, -(SSD_CONV - 1):]
    up_rows = jnp.concatenate([ffn_conv_prev, up[:, -(FFN_CONV - 1):].astype(F32)], axis=1)[:, -(FFN_CONV - 1):]
    return y, ssd_state, xbc_rows, up_rows


def _prep_params(g_mix, w_in, conv_w_ssd, conv_b_ssd, dt_bias, a_log, d_skip, g_ssd_norm, w_br_att, w_br_ssd,
                 w_out, g_ffn, w_up, conv_w_ffn, conv_b_ffn, w_down, g_final):
    c = np.cumsum([SB_WIDTH, SB_WIDTH, SB_WIDTH, SSD_WIDTH, SSD_CONV_CH, SSD_HEADS]).tolist()
    wq, wk, wv, wz, wx, wdt, wg = (w_in[:, a:b_] for a, b_ in zip([0] + c, c + [w_in.shape[1]]))
    Q = SSD_CHUNK
    lanes_t = lambda v: jnp.repeat(v.reshape(SSD_HEADS // 2, 2, 1), Q, axis=2).reshape(SSD_HEADS // 2, 2 * Q)
    pad_l = lambda v: jnp.pad(v, (0, LANES - SSD_HEADS)).reshape(1, LANES)
    return {
        'g_mix': g_mix.reshape(1, -1), 'g_ffn': g_ffn.reshape(1, -1), 'g_final': g_final.reshape(1, -1),
        'w_kv': jnp.concatenate([wk, wv], axis=1).astype(BF16),
        'w_big': jnp.concatenate([wz, wq, wx, wg], axis=1).astype(BF16),
        'w_dt': jnp.pad(wdt, ((0, 0), (0, LANES - SSD_HEADS))).astype(BF16),
        'conv_w_ssd': conv_w_ssd, 'conv_b_ssd': conv_b_ssd.reshape(1, -1),
        'dt_bias': pad_l(dt_bias), 'dt_bias_t': lanes_t(dt_bias),
        'a_log': pad_l(a_log), 'a_log_t': lanes_t(a_log),
        'd_skip': jnp.repeat(d_skip, SSD_HEAD_DIM).reshape(1, -1), 'g_ssd_norm': g_ssd_norm.reshape(1, -1),
        'w_br_att': w_br_att.astype(BF16), 'w_br_ssd': w_br_ssd.astype(BF16), 'w_out': w_out.astype(BF16),
        'w_up': w_up.astype(BF16), 'conv_w_ffn': conv_w_ffn, 'conv_b_ffn': conv_b_ffn.reshape(1, -1),
        'w_down': w_down.astype(BF16),
    }


def kernel(x_prompt, x_sample, cache_k, cache_v, state_ssm, state_ssm_conv, state_ffn_conv, meta_tokens, g_mix, w_in, conv_w_ssd, conv_b_ssd, dt_bias, a_log, d_skip, g_ssd_norm, w_br_att, w_br_ssd, w_out, g_ffn, w_up, conv_w_ffn, conv_b_ffn, w_down, g_final):
    bp, bs = x_prompt.shape[0], x_sample.shape[0]
    p = _prep_params(g_mix[0], w_in[0], conv_w_ssd[0], conv_b_ssd[0], dt_bias[0], a_log[0], d_skip[0],
                     g_ssd_norm[0], w_br_att[0], w_br_ssd[0], w_out[0], g_ffn[0], w_up[0], conv_w_ffn[0],
                     conv_b_ffn[0], w_down[0], g_final)
    state_shape = (SSD_WIDTH, SSD_STATE)

    hm = meta_tokens[None]
    big_m, k_m, v_m, dt_m, hf_m = stream_layer(hm, None, 0, None, None, None, p)
    _, ssm_m, conv_m, ffn_m = stream_rest(
        big_m, k_m, v_m, dt_m, hf_m, None, 0, jnp.zeros((1,) + state_shape, F32),
        jnp.zeros((1, SSD_CONV - 1, SSD_CONV_CH), F32), jnp.zeros((1, FFN_CONV - 1, 2 * D_FF), F32), p)

    big_x, k_x, v_x, dt_x, hf_x = stream_layer(x_prompt, None, None, None, None, None, p)
    rep = lambda a: jnp.broadcast_to(a, (bp,) + a.shape[1:])
    k_prompt = jnp.concatenate([rep(k_m), k_x], axis=1)
    v_prompt = jnp.concatenate([rep(v_m), v_x], axis=1)
    y_prompt, ssm_p, conv_p, ffn_p = stream_rest(
        big_x, k_x, v_x, dt_x, hf_x, (k_prompt, v_prompt), None, rep(ssm_m), rep(conv_m), rep(ffn_m), p)

    n_rows = cache_k.shape[2]
    big_s, k_s, v_s, dt_s, hf_s = stream_layer(x_sample, None, None, None, None, None, p)
    y_sample, ssm_s, conv_s, ffn_s = stream_rest(
        big_s, k_s, v_s, dt_s, hf_s,
        (cache_k[0].reshape(bs, n_rows, SB_WIDTH), cache_v[0].reshape(bs, n_rows, SB_WIDTH)),
        (n_rows - N_META) // KEY_TILE,
        state_ssm[0].reshape((bs,) + state_shape), state_ssm_conv[0], state_ffn_conv[0], p)

    heads = lambda a: a.reshape(a.shape[0], a.shape[1], SB_HEADS, SB_HEAD_DIM)[None]
    state5 = lambda a: a.reshape(a.shape[0], SSD_HEADS, SSD_HEAD_DIM, SSD_STATE)[None]
    return (y_prompt, y_sample, heads(k_prompt), heads(v_prompt), state5(ssm_p), conv_p[None], ffn_p[None],
            heads(k_s), heads(v_s), state5(ssm_s), conv_s[None], ffn_s[None])
```

```python
import functools

import numpy as np
import jax
import jax.numpy as jnp
from jax import lax
from jax.experimental import pallas as pl
from jax.experimental.pallas import tpu as pltpu

F32 = jnp.float32
BF16 = jnp.bfloat16

D_MODEL = 1024
N_META = 16
SB_HEADS = 16
SB_HEAD_DIM = 64
SB_WIDTH = SB_HEADS * SB_HEAD_DIM
SSD_WIDTH = 2 * D_MODEL
SSD_HEAD_DIM = 64
SSD_HEADS = SSD_WIDTH // SSD_HEAD_DIM
SSD_GROUPS = 4
SSD_STATE = 128
SSD_CONV = 4
SSD_CONV_CH = SSD_WIDTH + 2 * SSD_GROUPS * SSD_STATE
D_FF = 2816
FFN_CONV = 3
EPS = 1e-6

LANES = 128
SSD_CHUNK = 64
KEY_TILE = 128
GROUP_W = SSD_WIDTH // SSD_GROUPS
DEAD_LOG = -104.0
ATTN_STATIC_TILES = 2
VMEM_LIMIT = 56 * 1024 * 1024


def _cparams(sem):
    return pltpu.CompilerParams(dimension_semantics=sem, vmem_limit_bytes=VMEM_LIMIT)


def _pick(n, cands):
    for c in cands:
        if n % c == 0:
            return c
    return n


def _split2(x):
    hi = x.astype(BF16)
    lo = (x - hi.astype(F32)).astype(BF16)
    return hi, lo


def _split3(x):
    hi = x.astype(BF16)
    r = x - hi.astype(F32)
    mid = r.astype(BF16)
    lo = (r - mid.astype(F32)).astype(BF16)
    return hi, mid, lo


def _dot(a, b):
    return jnp.dot(a, b, preferred_element_type=F32)


def _dot_nt(a, b):
    return lax.dot_general(a, b, (((1,), (1,)), ((), ())), preferred_element_type=F32)


def _softplus(x):
    return jnp.maximum(x, 0.0) + jnp.log(1.0 + jnp.exp(-jnp.abs(x)))


def _sigmoid(x):
    return 1.0 / (1.0 + jnp.exp(-x))


def _norm_mm_kernel(x_ref, g_ref, w_ref, o_ref, u_ref):
    @pl.when(pl.program_id(1) == 0)
    def _():
        x = x_ref[...]
        ms = jnp.mean(x * x, axis=-1, keepdims=True)
        u_ref[...] = (x * lax.rsqrt(ms + EPS) * g_ref[...]).astype(BF16)

    o_ref[...] = _dot(u_ref[...], w_ref[...]).astype(o_ref.dtype)


def norm_mm(x, g, w, out_dtype):
    m, d = x.shape
    n = w.shape[1]
    tm = _pick(m, (1024, 512, 256, 128))
    tn = _pick(n, (1024, 1408, 512, 256, 128))
    return pl.pallas_call(
        _norm_mm_kernel,
        grid=(m // tm, n // tn),
        in_specs=[pl.BlockSpec((tm, d), lambda i, j: (i, 0)),
                  pl.BlockSpec((1, d), lambda i, j: (0, 0)),
                  pl.BlockSpec((d, tn), lambda i, j: (0, j))],
        out_specs=pl.BlockSpec((tm, tn), lambda i, j: (i, j)),
        out_shape=jax.ShapeDtypeStruct((m, n), out_dtype),
        scratch_shapes=[pltpu.VMEM((tm, d), BF16)],
        compiler_params=_cparams(("parallel", "arbitrary")),
    )(x, g, w)


def _attn_kernel(*refs, tq, width, has_cache, n_cache_tiles):
    if has_cache:
        q_ref, kn_ref, vn_ref, u2_ref, kc_ref, vc_ref, o_ref = refs[:7]
        scr = refs[7:]
    else:
        q_ref, kn_ref, vn_ref, u2_ref, o_ref = refs[:5]
        kc_ref = vc_ref = None
        scr = refs[5:]
    n_heads = width // SB_HEAD_DIM

    lane = lax.broadcasted_iota(jnp.int32, (tq, LANES), 1)
    row = lax.broadcasted_iota(jnp.int32, (tq, LANES), 0)
    lo_half = lane < SB_HEAD_DIM
    u2 = u2_ref[...]
    q = q_ref[0]

    if tq == KEY_TILE:
        kd = kn_ref[0].astype(BF16)
        vd = vn_ref[0].astype(BF16)
    else:
        kd_ref, vd_ref = scr
        kd_ref[...] = jnp.zeros_like(kd_ref)
        vd_ref[...] = jnp.zeros_like(vd_ref)
        kd_ref[0:tq, :] = kn_ref[0].astype(BF16)
        vd_ref[0:tq, :] = vn_ref[0].astype(BF16)
        kd = kd_ref[...]
        vd = vd_ref[...]

    qs = []
    for h in range(n_heads):
        qp = q[:, (h // 2) * LANES:(h // 2 + 1) * LANES]
        keep = lo_half if h % 2 == 0 else jnp.logical_not(lo_half)
        qs.append(jnp.where(keep, qp, jnp.zeros_like(qp)) * jnp.asarray(SB_HEAD_DIM ** -0.5, BF16))

    def tiles_all_heads(tiles, carries, accs):
        units = [(t, h) for t in range(len(tiles)) for h in range(n_heads)]
        ps = lambda h: slice((h // 2) * LANES, (h // 2 + 1) * LANES)
        z = {u: _dot_nt(qs[u[1]], tiles[u[0]][0][:, ps(u[1])]) for u in units}
        log_beta, cat = {}, {}
        for u in units:
            mask = tiles[u[0]][2]
            soft = jnp.log(1.0 + jnp.exp(-jnp.abs(z[u])))
            log_beta[u] = jnp.minimum(z[u], 0.0) - soft
            log_keep = log_beta[u] - z[u]
            if mask is not None:
                log_keep = jnp.where(mask, log_keep, 0.0)
            cat[u] = jnp.concatenate(_split2(log_keep), axis=1)
        r = {u: _dot(cat[u], u2) for u in units}
        carries, accs = list(carries), list(accs)
        w = {}
        for t, h in units:
            mask = tiles[t][2]
            wt = jnp.exp(log_beta[(t, h)] + r[(t, h)][:, :LANES] + carries[h])
            if mask is not None:
                wt = jnp.where(mask, wt, 0.0)
            w[(t, h)] = wt.astype(BF16)
            carries[h] = carries[h] + r[(t, h)][:, LANES:]
        for t, h in units:
            accs[h] = accs[h] + _dot(w[(t, h)], tiles[t][1][:, ps(h)])
        return carries, accs

    def any_alive(carries):
        m = carries[0]
        for c in carries[1:]:
            m = jnp.maximum(m, c)
        return (jnp.max(m) > DEAD_LOG).astype(jnp.int32)

    def cache_tile(j):
        if isinstance(j, int):
            off = N_META + j * KEY_TILE if j >= 0 else 0
            mask = None if j >= 0 else (lane < N_META if j == -1 else lane < 0)
        else:
            off = pl.multiple_of(jnp.where(j >= 0, N_META + j * KEY_TILE, 0), 8)
            mask = jnp.logical_or(j >= 0, jnp.logical_and(j == -1, lane < N_META))
        kt = kc_ref[0, pl.ds(off, KEY_TILE), :].astype(BF16)
        vt = vc_ref[0, pl.ds(off, KEY_TILE), :].astype(BF16)
        return kt, vt, mask

    zeros = jnp.zeros((tq, LANES), F32)
    tiles = [(kd, vd, lane < row)]
    if has_cache:
        j0 = pl.program_id(2) - 1 if n_cache_tiles is None else n_cache_tiles - 1
        for _ in range(ATTN_STATIC_TILES):
            tiles.append(cache_tile(j0))
            j0 = j0 - 1
    carries, accs = tiles_all_heads(tiles, [zeros] * n_heads, [zeros] * n_heads)
    if has_cache:
        def cond(s):
            return jnp.logical_and(s[0] >= -1, s[1] > 0)

        def body(s):
            j, _, carries, accs = s
            carries, accs = tiles_all_heads([cache_tile(j)], carries, accs)
            return j - 1, any_alive(carries), tuple(carries), tuple(accs)

        state = (jnp.asarray(j0, jnp.int32), any_alive(carries), tuple(carries), tuple(accs))
        _, _, carries, accs = lax.while_loop(cond, body, state)
    for p in range(n_heads // 2):
        o_ref[0, :, p * LANES:(p + 1) * LANES] = jnp.where(lo_half, accs[2 * p], accs[2 * p + 1]).astype(o_ref.dtype)


def _cumsum_rhs():
    s = np.arange(KEY_TILE)
    strict = (s[:, None] > s[None, :]).astype(np.float32)
    half = np.concatenate([strict, np.ones((KEY_TILE, KEY_TILE), np.float32)], axis=1)
    return jnp.asarray(np.concatenate([half, half], axis=0), BF16)


def attention(q_src, q_col0, k_new, v_new, k_cache, v_cache, n_cache_tiles, width):
    b, s, _ = q_src.shape
    tq = min(s, KEY_TILE)
    has_cache = k_cache is not None
    qb0 = q_col0 // width
    blk = lambda bi, hp, qi: (bi, qi, hp)
    in_specs = [pl.BlockSpec((1, tq, width), lambda bi, hp, qi: (bi, qi, qb0 + hp)),
                pl.BlockSpec((1, tq, width), blk),
                pl.BlockSpec((1, tq, width), blk),
                pl.BlockSpec((2 * KEY_TILE, 2 * KEY_TILE), lambda bi, hp, qi: (0, 0))]
    args = [q_src, k_new, v_new, _cumsum_rhs()]
    if has_cache:
        lc = k_cache.shape[1]
        in_specs += [pl.BlockSpec((1, lc, width), lambda bi, hp, qi: (bi, 0, hp))] * 2
        args += [k_cache, v_cache]
    scratch = [] if tq == KEY_TILE else [pltpu.VMEM((KEY_TILE, width), BF16)] * 2
    return pl.pallas_call(
        functools.partial(_attn_kernel, tq=tq, width=width, has_cache=has_cache, n_cache_tiles=n_cache_tiles),
        grid=(b, SB_WIDTH // width, s // tq),
        in_specs=in_specs,
        out_specs=pl.BlockSpec((1, tq, width), blk),
        out_shape=jax.ShapeDtypeStruct((b, s, SB_WIDTH), BF16),
        scratch_shapes=scratch,
        compiler_params=_cparams(("parallel", "parallel", "arbitrary")),
        name="stick_breaking",
    )(*args)


def _ssd_kernel(xbc_ref, z_ref, dt_ref, dtT_ref, prev_ref, h0_ref, cw_ref, cb_ref, dtb_ref, dtbT_ref,
                alog_ref, alogT_ref, dskip_ref, gn_ref, e3_ref, ltri_ref, ublk_ref,
                y_ref, hfin_ref, cbuf_ref, st_ref, *, rows, n_chunks):
    Q = SSD_CHUNK
    c = pl.program_id(1)
    n_blk = SSD_WIDTH // LANES

    @pl.when(c == 0)
    def _():
        cbuf_ref[8 - (SSD_CONV - 1):8, :] = prev_ref[0]
        for j in range(n_blk):
            st_ref[:, j * LANES:(j + 1) * LANES] = h0_ref[0, j * LANES:(j + 1) * LANES, :].T

    def pad_rows(v):
        if rows == Q:
            return v
        return jnp.concatenate([v, jnp.zeros((Q - rows, v.shape[1]), v.dtype)], axis=0)

    cbuf_ref[8:8 + Q, :] = pad_rows(xbc_ref[0].astype(F32))
    conv = cb_ref[...]
    for i in range(SSD_CONV):
        conv = conv + cbuf_ref[8 - (SSD_CONV - 1) + i:8 - (SSD_CONV - 1) + i + Q, :] * cw_ref[i:i + 1, :]
    cbuf_ref[8 - (SSD_CONV - 1):8, :] = cbuf_ref[8 + Q - (SSD_CONV - 1):8 + Q, :]
    xc = conv * _sigmoid(conv)
    xs = xc[:, :SSD_WIDTH]
    b_all = xc[:, SSD_WIDTH:SSD_WIDTH + SSD_GROUPS * SSD_STATE]
    c_all = xc[:, SSD_WIDTH + SSD_GROUPS * SSD_STATE:]

    dt = _softplus(pad_rows(dt_ref[0]) + dtb_ref[...])
    dt_t = _softplus(dtT_ref[0, 0] + dtbT_ref[...])
    if rows < Q:
        rowi = lax.broadcasted_iota(jnp.int32, (Q, 1), 0)
        dt = jnp.where(rowi < rows, dt, 0.0)
        xs = jnp.where(rowi < rows, xs, 0.0)
        lane_t = lax.broadcasted_iota(jnp.int32, dt_t.shape, 1)
        dt_t = jnp.where(lane_t % Q < rows, dt_t, 0.0)
    da = dt * (-jnp.exp(alog_ref[...]))
    da_t = dt_t * (-jnp.exp(alogT_ref[...]))

    a_cum = _dot(ltri_ref[...], jnp.concatenate(_split3(da), axis=0))
    a_cum_t = _dot(jnp.concatenate(_split3(da_t), axis=1), ublk_ref[...])
    dt_exp = _dot(jnp.concatenate(_split2(dt), axis=1), e3_ref[0:2 * LANES, :])
    a_exp = _dot(jnp.concatenate(_split3(a_cum), axis=1), e3_ref[...])
    a_last = a_exp[Q - 1:Q, :]

    xdt = xs * dt_exp
    xdt_b = xdt.astype(BF16)
    xw = (xdt * jnp.exp(a_last - a_exp)).astype(BF16)
    xw_pad = jnp.concatenate([xw, jnp.zeros_like(xw)], axis=0)
    chunk_decay = jnp.exp(a_last)
    grow = jnp.exp(a_exp)

    lane = lax.broadcasted_iota(jnp.int32, (Q, LANES), 1)
    rowq = lax.broadcasted_iota(jnp.int32, (Q, LANES), 0)
    causal2 = (lane % Q) <= rowq
    lo_half = lane < SSD_HEAD_DIM

    y_parts = []
    for g in range(SSD_GROUPS):
        bg = b_all[:, g * SSD_STATE:(g + 1) * SSD_STATE]
        cg = c_all[:, g * SSD_STATE:(g + 1) * SSD_STATE].astype(BF16)
        bg_b = bg.astype(BF16)
        cb2 = _dot_nt(cg, jnp.concatenate([bg_b, bg_b], axis=0))
        gs = slice(g * GROUP_W, (g + 1) * GROUP_W)
        st_g = st_ref[:, gs]
        y_off = _dot(cg, st_g.astype(BF16)) * grow[:, gs]
        bg_t = jnp.concatenate([bg, jnp.zeros_like(bg)], axis=0).T.astype(BF16)
        st_ref[:, gs] = st_g * chunk_decay[:, gs] + _dot(bg_t, xw_pad[:, gs])
        pair_out = []
        for k in range(GROUP_W // LANES):
            i = g * (GROUP_W // LANES) + k
            ps = slice(i * LANES, (i + 1) * LANES)
            decay = jnp.exp(jnp.minimum(a_exp[:, ps] - a_cum_t[i:i + 1, :], 0.0))
            m2 = jnp.where(causal2, cb2 * decay, 0.0).astype(BF16)
            xp = xdt_b[:, ps]
            zero = jnp.zeros_like(xp)
            xbd = jnp.concatenate([jnp.where(lo_half, xp, zero), jnp.where(lo_half, zero, xp)], axis=0)
            pair_out.append(_dot(m2, xbd))
        y_diag = jnp.concatenate(pair_out, axis=1)
        y = y_diag + y_off + dskip_ref[:, gs] * xs[:, gs]
        zg = pad_rows(z_ref[0, :, gs].astype(F32))
        y = y * (zg * _sigmoid(zg))
        ms = jnp.mean(y * y, axis=-1, keepdims=True)
        y_parts.append(y * lax.rsqrt(ms + EPS) * gn_ref[:, gs])
    y_all = jnp.concatenate(y_parts, axis=1)
    y_ref[0] = y_all[:rows].astype(y_ref.dtype)

    @pl.when(c == n_chunks - 1)
    def _():
        for j in range(n_blk):
            hfin_ref[0, j * LANES:(j + 1) * LANES, :] = st_ref[:, j * LANES:(j + 1) * LANES].T


def _ssd_constants():
    Q = SSD_CHUNK
    e = np.zeros((LANES, SSD_WIDTH), np.float32)
    for h in range(SSD_HEADS):
        e[h, h * SSD_HEAD_DIM:(h + 1) * SSD_HEAD_DIM] = 1.0
    e3 = np.concatenate([e, e, e], axis=0)
    t = np.arange(Q)
    ltri = (t[:, None] >= t[None, :]).astype(np.float32)
    ltri3 = np.concatenate([ltri, ltri, ltri], axis=1)
    ublk = np.zeros((LANES, LANES), np.float32)
    ublk[:Q, :Q] = ltri.T
    ublk[Q:, Q:] = ltri.T
    ublk3 = np.concatenate([ublk, ublk, ublk], axis=0)
    return jnp.asarray(e3, BF16), jnp.asarray(ltri3, BF16), jnp.asarray(ublk3, BF16)


def ssd_mixer(big, xbc_blk, z_blk, dt_raw, conv_prev, h0, p):
    b, s, _ = big.shape
    Q = SSD_CHUNK
    rows = min(s, Q)
    n_chunks = -(-s // Q)
    dtp = dt_raw[:, :, :SSD_HEADS]
    if s < Q:
        dtp = jnp.pad(dtp, ((0, 0), (0, Q - s), (0, 0)))
    dt_t = dtp.reshape(b, n_chunks, Q, SSD_HEADS).transpose(0, 1, 3, 2).reshape(b, n_chunks, SSD_HEADS // 2, 2 * Q)
    e3, ltri3, ublk3 = _ssd_constants()
    const = lambda shape: pl.BlockSpec(shape, lambda bi, ci: (0,) * len(shape))
    in_specs = [
        pl.BlockSpec((1, rows, SSD_CONV_CH), lambda bi, ci: (bi, ci, xbc_blk)),
        pl.BlockSpec((1, rows, SSD_WIDTH), lambda bi, ci: (bi, ci, z_blk)),
        pl.BlockSpec((1, rows, LANES), lambda bi, ci: (bi, ci, 0)),
        pl.BlockSpec((1, 1, SSD_HEADS // 2, 2 * Q), lambda bi, ci: (bi, ci, 0, 0)),
        pl.BlockSpec((1, SSD_CONV - 1, SSD_CONV_CH), lambda bi, ci: (bi, 0, 0)),
        pl.BlockSpec((1, SSD_WIDTH, SSD_STATE), lambda bi, ci: (bi, 0, 0)),
        const((SSD_CONV, SSD_CONV_CH)), const((1, SSD_CONV_CH)),
        const((1, LANES)), const((SSD_HEADS // 2, 2 * Q)),
        const((1, LANES)), const((SSD_HEADS // 2, 2 * Q)),
        const((1, SSD_WIDTH)), const((1, SSD_WIDTH)),
        const(e3.shape), const(ltri3.shape), const(ublk3.shape),
    ]
    y, h_fin = pl.pallas_call(
        functools.partial(_ssd_kernel, rows=rows, n_chunks=n_chunks),
        grid=(b, n_chunks),
        in_specs=in_specs,
        out_specs=[pl.BlockSpec((1, rows, SSD_WIDTH), lambda bi, ci: (bi, ci, 0)),
                   pl.BlockSpec((1, SSD_WIDTH, SSD_STATE), lambda bi, ci: (bi, 0, 0))],
        out_shape=[jax.ShapeDtypeStruct((b, s, SSD_WIDTH), BF16),
                   jax.ShapeDtypeStruct((b, SSD_WIDTH, SSD_STATE), F32)],
        scratch_shapes=[pltpu.VMEM((8 + Q, SSD_CONV_CH), F32), pltpu.VMEM((SSD_STATE, SSD_WIDTH), F32)],
        compiler_params=_cparams(("parallel", "arbitrary")),
    )(big, big, dt_raw, dt_t, conv_prev, h0, p['conv_w_ssd'], p['conv_b_ssd'], p['dt_bias'], p['dt_bias_t'],
      p['a_log'], p['a_log_t'], p['d_skip'], p['g_ssd_norm'], e3, ltri3, ublk3)
    return y, h_fin


def _merge_kernel(att_ref, ssd_ref, ga_ref, gs_ref, h_ref, wa_ref, ws_ref, wo_ref, o_ref):
    a = _dot(att_ref[...], wa_ref[...])
    s = _dot(ssd_ref[...], ws_ref[...])
    merged = _sigmoid(ga_ref[...].astype(F32)) * a + _sigmoid(gs_ref[...].astype(F32)) * s
    o_ref[...] = h_ref[...] + _dot(merged.astype(BF16), wo_ref[...])


def merge(att, ssd, big, gate_blk, h, p):
    m = h.shape[0]
    tm = _pick(m, (512, 256, 128))
    row = lambda w: pl.BlockSpec((tm, w), lambda i: (i, 0))
    full = lambda a: pl.BlockSpec(a.shape, lambda i: (0, 0))
    return pl.pallas_call(
        _merge_kernel,
        grid=(m // tm,),
        in_specs=[row(SB_WIDTH), row(SSD_WIDTH),
                  pl.BlockSpec((tm, D_MODEL), lambda i: (i, gate_blk)),
                  pl.BlockSpec((tm, D_MODEL), lambda i: (i, gate_blk + 1)),
                  row(D_MODEL), full(p['w_br_att']), full(p['w_br_ssd']), full(p['w_out'])],
        out_specs=row(D_MODEL),
        out_shape=jax.ShapeDtypeStruct((m, D_MODEL), F32),
        compiler_params=_cparams(("parallel",)),
    )(att, ssd, big, big, h, p['w_br_att'], p['w_br_ssd'], p['w_out'])


def _ffn_act_kernel(ug_ref, uv_ref, pg_ref, pv_ref, wg_ref, wv_ref, bg_ref, bv_ref, o_ref, gbuf_ref, vbuf_ref,
                    *, ts):
    t = pl.program_id(1)
    k1 = FFN_CONV - 1

    def conv(u_ref, p_ref, w_ref, b_ref, buf_ref):
        @pl.when(t == 0)
        def _():
            buf_ref[8 - k1:8, :] = p_ref[0]

        buf_ref[8:8 + ts, :] = u_ref[0].astype(F32)
        y = b_ref[...]
        for i in range(FFN_CONV):
            y = y + buf_ref[8 - k1 + i:8 - k1 + i + ts, :] * w_ref[i:i + 1, :]
        buf_ref[8 - k1:8, :] = buf_ref[8 + ts - k1:8 + ts, :]
        return y

    gate = conv(ug_ref, pg_ref, wg_ref, bg_ref, gbuf_ref)
    val = conv(uv_ref, pv_ref, wv_ref, bv_ref, vbuf_ref)
    o_ref[0] = (gate * _sigmoid(gate) * val).astype(o_ref.dtype)


def ffn_act(up, prev, p):
    b, s, _ = up.shape
    ts = _pick(s, (512, 256, 128))
    half = lambda blk: pl.BlockSpec((1, ts, D_FF), lambda bi, ti: (bi, ti, blk))
    prevs = lambda blk: pl.BlockSpec((1, FFN_CONV - 1, D_FF), lambda bi, ti: (bi, 0, blk))
    wspec = lambda blk: pl.BlockSpec((FFN_CONV, D_FF), lambda bi, ti: (0, blk))
    bspec = lambda blk: pl.BlockSpec((1, D_FF), lambda bi, ti: (0, blk))
    return pl.pallas_call(
        functools.partial(_ffn_act_kernel, ts=ts),
        grid=(b, s // ts),
        in_specs=[half(0), half(1), prevs(0), prevs(1), wspec(0), wspec(1), bspec(0), bspec(1)],
        out_specs=pl.BlockSpec((1, ts, D_FF), lambda bi, ti: (bi, ti, 0)),
        out_shape=jax.ShapeDtypeStruct((b, s, D_FF), BF16),
        scratch_shapes=[pltpu.VMEM((8 + ts, D_FF), F32)] * 2,
        compiler_params=_cparams(("parallel", "arbitrary")),
    )(up, up, prev, prev, p['conv_w_ffn'], p['conv_w_ffn'], p['conv_b_ffn'], p['conv_b_ffn'])


def _down_kernel(a_ref, w_ref, h_ref, g_ref, o_ref):
    h = h_ref[...] + _dot(a_ref[...], w_ref[...])
    ms = jnp.mean(h * h, axis=-1, keepdims=True)
    o_ref[...] = h * lax.rsqrt(ms + EPS) * g_ref[...]


def down_norm(act, h, p):
    m = h.shape[0]
    tm = _pick(m, (512, 256, 128))
    return pl.pallas_call(
        _down_kernel,
        grid=(m // tm,),
        in_specs=[pl.BlockSpec((tm, D_FF), lambda i: (i, 0)),
                  pl.BlockSpec((D_FF, D_MODEL), lambda i: (0, 0)),
                  pl.BlockSpec((tm, D_MODEL), lambda i: (i, 0)),
                  pl.BlockSpec((1, D_MODEL), lambda i: (0, 0))],
        out_specs=pl.BlockSpec((tm, D_MODEL), lambda i: (i, 0)),
        out_shape=jax.ShapeDtypeStruct((m, D_MODEL), F32),
        compiler_params=_cparams(("parallel",)),
    )(act, p['w_down'], h, p['g_final'])


BIG_Z, BIG_Q, BIG_XBC, BIG_GATE = 0, SSD_WIDTH, SSD_WIDTH + SB_WIDTH, SSD_WIDTH + SB_WIDTH + SSD_CONV_CH


def stream_layer(h, kv_cache, n_cache_tiles, ssd_h0, ssd_conv_prev, ffn_conv_prev, p):
    b, s, _ = h.shape
    m = b * s
    hf = h.reshape(m, D_MODEL)
    kv = norm_mm(hf, p['g_mix'], p['w_kv'], F32)
    big = norm_mm(hf, p['g_mix'], p['w_big'], BF16)
    dt_raw = norm_mm(hf, p['g_mix'], p['w_dt'], F32)
    k = kv[:, :SB_WIDTH].reshape(b, s, SB_WIDTH)
    v = kv[:, SB_WIDTH:].reshape(b, s, SB_WIDTH)
    big3 = big.reshape(b, s, -1)
    return big3, k, v, dt_raw.reshape(b, s, LANES), hf


def stream_rest(big3, k, v, dt_raw, hf, kv_cache, n_cache_tiles, ssd_h0, ssd_conv_prev, ffn_conv_prev, p):
    b, s, _ = big3.shape
    m = b * s
    kc, vc = kv_cache if kv_cache is not None else (None, None)
    attn_width = 4 * SB_HEAD_DIM if s > KEY_TILE else SB_WIDTH
    att = attention(big3, BIG_Q, k, v, kc, vc, n_cache_tiles, attn_width)
    ssd, ssd_state = ssd_mixer(big3, BIG_XBC // SSD_CONV_CH, BIG_Z // SSD_WIDTH, dt_raw, ssd_conv_prev, ssd_h0, p)
    h1 = merge(att.reshape(m, SB_WIDTH), ssd.reshape(m, SSD_WIDTH), big3.reshape(m, -1), BIG_GATE // D_MODEL, hf, p)
    up = norm_mm(h1, p['g_ffn'], p['w_up'], BF16).reshape(b, s, 2 * D_FF)
    act = ffn_act(up, ffn_conv_prev, p)
    y = down_norm(act.reshape(m, D_FF), h1, p).reshape(b, s, D_MODEL)
    xbc_rows = jnp.concatenate([ssd_conv_prev, big3[:, :, BIG_XBC:BIG_XBC + SSD_CONV_CH][:, -(SSD_CONV - 1):].astype(F32)],
                               axis=1)[:, -(SSD_CONV - 1):]
    up_rows = jnp.concatenate([ffn_conv_prev, up[:, -(FFN_CONV - 1):].astype(F32)], axis=1)[:, -(FFN_CONV - 1):]
    return y, ssd_state, xbc_rows, up_rows


def _prep_params(g_mix, w_in, conv_w_ssd, conv_b_ssd, dt_bias, a_log, d_skip, g_ssd_norm, w_br_att, w_br_ssd,
                 w_out, g_ffn, w_up, conv_w_ffn, conv_b_ffn, w_down, g_final):
    c = np.cumsum([SB_WIDTH, SB_WIDTH, SB_WIDTH, SSD_WIDTH, SSD_CONV_CH, SSD_HEADS]).tolist()
    wq, wk, wv, wz, wx, wdt, wg = (w_in[:, a:b_] for a, b_ in zip([0] + c, c + [w_in.shape[1]]))
    Q = SSD_CHUNK
    lanes_t = lambda v: jnp.repeat(v.reshape(SSD_HEADS // 2, 2, 1), Q, axis=2).reshape(SSD_HEADS // 2, 2 * Q)
    pad_l = lambda v: jnp.pad(v, (0, LANES - SSD_HEADS)).reshape(1, LANES)
    return {
        'g_mix': g_mix.reshape(1, -1), 'g_ffn': g_ffn.reshape(1, -1), 'g_final': g_final.reshape(1, -1),
        'w_kv': jnp.concatenate([wk, wv], axis=1).astype(BF16),
        'w_big': jnp.concatenate([wz, wq, wx, wg], axis=1).astype(BF16),
        'w_dt': jnp.pad(wdt, ((0, 0), (0, LANES - SSD_HEADS))).astype(BF16),
        'conv_w_ssd': conv_w_ssd, 'conv_b_ssd': conv_b_ssd.reshape(1, -1),
        'dt_bias': pad_l(dt_bias), 'dt_bias_t': lanes_t(dt_bias),
        'a_log': pad_l(a_log), 'a_log_t': lanes_t(a_log),
        'd_skip': jnp.repeat(d_skip, SSD_HEAD_DIM).reshape(1, -1), 'g_ssd_norm': g_ssd_norm.reshape(1, -1),
        'w_br_att': w_br_att.astype(BF16), 'w_br_ssd': w_br_ssd.astype(BF16), 'w_out': w_out.astype(BF16),
        'w_up': w_up.astype(BF16), 'conv_w_ffn': conv_w_ffn, 'conv_b_ffn': conv_b_ffn.reshape(1, -1),
        'w_down': w_down.astype(BF16),
    }


def kernel(x_prompt, x_sample, cache_k, cache_v, state_ssm, state_ssm_conv, state_ffn_conv, meta_tokens, g_mix, w_in, conv_w_ssd, conv_b_ssd, dt_bias, a_log, d_skip, g_ssd_norm, w_br_att, w_br_ssd, w_out, g_ffn, w_up, conv_w_ffn, conv_b_ffn, w_down, g_final):
    bp, bs = x_prompt.shape[0], x_sample.shape[0]
    p = _prep_params(g_mix[0], w_in[0], conv_w_ssd[0], conv_b_ssd[0], dt_bias[0], a_log[0], d_skip[0],
                     g_ssd_norm[0], w_br_att[0], w_br_ssd[0], w_out[0], g_ffn[0], w_up[0], conv_w_ffn[0],
                     conv_b_ffn[0], w_down[0], g_final)
    state_shape = (SSD_WIDTH, SSD_STATE)

    hm = meta_tokens[None]
    big_m, k_m, v_m, dt_m, hf_m = stream_layer(hm, None, 0, None, None, None, p)
    _, ssm_m, conv_m, ffn_m = stream_rest(
        big_m, k_m, v_m, dt_m, hf_m, None, 0, jnp.zeros((1,) + state_shape, F32),
        jnp.zeros((1, SSD_CONV - 1, SSD_CONV_CH), F32), jnp.zeros((1, FFN_CONV - 1, 2 * D_FF), F32), p)

    big_x, k_x, v_x, dt_x, hf_x = stream_layer(x_prompt, None, None, None, None, None, p)
    rep = lambda a: jnp.broadcast_to(a, (bp,) + a.shape[1:])
    k_prompt = jnp.concatenate([rep(k_m), k_x], axis=1)
    v_prompt = jnp.concatenate([rep(v_m), v_x], axis=1)
    y_prompt, ssm_p, conv_p, ffn_p = stream_rest(
        big_x, k_x, v_x, dt_x, hf_x, (k_prompt, v_prompt), None, rep(ssm_m), rep(conv_m), rep(ffn_m), p)

    n_rows = cache_k.shape[2]
    big_s, k_s, v_s, dt_s, hf_s = stream_layer(x_sample, None, None, None, None, None, p)
    y_sample, ssm_s, conv_s, ffn_s = stream_rest(
        big_s, k_s, v_s, dt_s, hf_s,
        (cache_k[0].reshape(bs, n_rows, SB_WIDTH), cache_v[0].reshape(bs, n_rows, SB_WIDTH)),
        (n_rows - N_META) // KEY_TILE,
        state_ssm[0].reshape((bs,) + state_shape), state_ssm_conv[0], state_ffn_conv[0], p)

    heads = lambda a: a.reshape(a.shape[0], a.shape[1], SB_HEADS, SB_HEAD_DIM)[None]
    state5 = lambda a: a.reshape(a.shape[0], SSD_HEADS, SSD_HEAD_DIM, SSD_STATE)[None]
    return (y_prompt, y_sample, heads(k_prompt), heads(v_prompt), state5(ssm_p), conv_p[None], ffn_p[None],
            heads(k_s), heads(v_s), state5(ssm_s), conv_s[None], ffn_s[None])
```

```python
import functools

import numpy as np
import jax
import jax.numpy as jnp
from jax import lax
from jax.experimental import pallas as pl
from jax.experimental.pallas import tpu as pltpu

F32 = jnp.float32
BF16 = jnp.bfloat16

D_MODEL = 1024
N_META = 16
SB_HEADS = 16
SB_HEAD_DIM = 64
SB_WIDTH = SB_HEADS * SB_HEAD_DIM
SSD_WIDTH = 2 * D_MODEL
SSD_HEAD_DIM = 64
SSD_HEADS = SSD_WIDTH // SSD_HEAD_DIM
SSD_GROUPS = 4
SSD_STATE = 128
SSD_CONV = 4
SSD_CONV_CH = SSD_WIDTH + 2 * SSD_GROUPS * SSD_STATE
D_FF = 2816
FFN_CONV = 3
EPS = 1e-6

LANES = 128
SSD_CHUNK = 64
KEY_TILE = 128
GROUP_W = SSD_WIDTH // SSD_GROUPS
DEAD_LOG = -104.0
ATTN_STATIC_TILES = 2
VMEM_LIMIT = 56 * 1024 * 1024


def _cparams(sem):
    return pltpu.CompilerParams(dimension_semantics=sem, vmem_limit_bytes=VMEM_LIMIT)


def _pick(n, cands):
    for c in cands:
        if n % c == 0:
            return c
    return n


def _split2(x):
    hi = x.astype(BF16)
    lo = (x - hi.astype(F32)).astype(BF16)
    return hi, lo


def _split3(x):
    hi = x.astype(BF16)
    r = x - hi.astype(F32)
    mid = r.astype(BF16)
    lo = (r - mid.astype(F32)).astype(BF16)
    return hi, mid, lo


def _dot(a, b):
    return jnp.dot(a, b, preferred_element_type=F32)


def _dot_nt(a, b):
    return lax.dot_general(a, b, (((1,), (1,)), ((), ())), preferred_element_type=F32)


def _softplus(x):
    return jnp.maximum(x, 0.0) + jnp.log(1.0 + jnp.exp(-jnp.abs(x)))


def _sigmoid(x):
    return 1.0 / (1.0 + jnp.exp(-x))


def _rms_norm_bf16(x, g):
    ms = jnp.mean(x * x, axis=-1, keepdims=True)
    return (x * lax.rsqrt(ms + EPS) * g).astype(BF16)


def _norm_mm_kernel(x_ref, g_ref, w_ref, o_ref, u_ref):
    @pl.when(pl.program_id(1) == 0)
    def _():
        u_ref[...] = _rms_norm_bf16(x_ref[...], g_ref[...])

    o_ref[...] = _dot(u_ref[...], w_ref[...].astype(BF16)).astype(o_ref.dtype)


PROJ_TN = 512


def norm_mm(x, g, w, out_dtype):
    m, d = x.shape
    n = w.shape[1]
    tm = _pick(m, (2048, 1024, 512, 256, 128))
    tn = PROJ_TN
    return pl.pallas_call(
        _norm_mm_kernel,
        grid=(m // tm, n // tn),
        in_specs=[pl.BlockSpec((tm, d), lambda i, j: (i, 0)),
                  pl.BlockSpec((1, d), lambda i, j: (0, 0)),
                  pl.BlockSpec((d, tn), lambda i, j: (0, j))],
        out_specs=pl.BlockSpec((tm, tn), lambda i, j: (i, j)),
        out_shape=jax.ShapeDtypeStruct((m, n), out_dtype),
        scratch_shapes=[pltpu.VMEM((tm, d), BF16)],
        compiler_params=_cparams(("parallel", "arbitrary")),
        name="norm_up_proj",
    )(x, g, w)


_NQ = SB_WIDTH // PROJ_TN
_J_K, _J_V, _J_Z = _NQ, 2 * _NQ, 3 * _NQ
_J_X = _J_Z + SSD_WIDTH // PROJ_TN
_J_G = _J_X + SSD_CONV_CH // PROJ_TN
_J_END = _J_G + 2 * D_MODEL // PROJ_TN
BIG_Z, BIG_Q, BIG_XBC, BIG_GATE = 0, SSD_WIDTH, SSD_WIDTH + SB_WIDTH, SSD_WIDTH + SB_WIDTH + SSD_CONV_CH
BIG_W = BIG_GATE + 2 * D_MODEL


def _big_block(j):
    return jnp.where(j < _J_K, BIG_Q // PROJ_TN + j,
                     jnp.where(j < _J_Z, BIG_Q // PROJ_TN + _NQ - 1,
                               jnp.where(j < _J_X, j - _J_Z + BIG_Z // PROJ_TN,
                                         jnp.where(j < _J_G, j - _J_X + BIG_XBC // PROJ_TN,
                                                   j - _J_G + BIG_GATE // PROJ_TN))))


def _in_proj_kernel(x_ref, g_ref, win_ref, wgate_ref, wdt_ref, k_ref, v_ref, big_ref, dt_ref, u_ref):
    j = pl.program_id(1)

    @pl.when(j == 0)
    def _():
        u_ref[...] = _rms_norm_bf16(x_ref[...], g_ref[...])
        dt_ref[...] = _dot(u_ref[...], wdt_ref[...].astype(BF16))

    def proj(w_ref):
        return _dot(u_ref[...], w_ref[...].astype(BF16))

    @pl.when(jnp.logical_and(j >= _J_K, j < _J_V))
    def _():
        k_ref[...] = proj(win_ref)

    @pl.when(jnp.logical_and(j >= _J_V, j < _J_Z))
    def _():
        v_ref[...] = proj(win_ref)

    @pl.when(jnp.logical_or(j < _J_K, jnp.logical_and(j >= _J_Z, j < _J_G)))
    def _():
        big_ref[...] = proj(win_ref).astype(BF16)

    @pl.when(j >= _J_G)
    def _():
        big_ref[...] = proj(wgate_ref).astype(BF16)


def in_proj(x, g, w_in, w_tail):
    m, d = x.shape
    tm = _pick(m, (2048, 1024, 512, 256, 128))
    tn = PROJ_TN
    clip = lambda j, lo, n: jnp.clip(j - lo, 0, n - 1)
    return pl.pallas_call(
        _in_proj_kernel,
        grid=(m // tm, _J_END),
        in_specs=[pl.BlockSpec((tm, d), lambda i, j: (i, 0), pipeline_mode=pl.Buffered(1)),
                  pl.BlockSpec((1, d), lambda i, j: (0, 0)),
                  pl.BlockSpec((d, tn), lambda i, j: (0, jnp.minimum(j, _J_G - 1))),
                  pl.BlockSpec((d, tn), lambda i, j: (0, clip(j, _J_G, _J_END - _J_G))),
                  pl.BlockSpec((d, LANES), lambda i, j: (0, 2 * D_MODEL // LANES))],
        out_specs=[pl.BlockSpec((tm, tn), lambda i, j: (i, clip(j, _J_K, _NQ))),
                   pl.BlockSpec((tm, tn), lambda i, j: (i, clip(j, _J_V, _NQ))),
                   pl.BlockSpec((tm, tn), lambda i, j: (i, _big_block(j))),
                   pl.BlockSpec((tm, LANES), lambda i, j: (i, 0))],
        out_shape=[jax.ShapeDtypeStruct((m, SB_WIDTH), F32), jax.ShapeDtypeStruct((m, SB_WIDTH), F32),
                   jax.ShapeDtypeStruct((m, BIG_W), BF16), jax.ShapeDtypeStruct((m, LANES), F32)],
        scratch_shapes=[pltpu.VMEM((tm, d), BF16)],
        compiler_params=_cparams(("parallel", "arbitrary")),
        name="in_proj",
    )(x, g, w_in, w_tail, w_tail)


def _attn_kernel(*refs, tq, width, has_cache, n_cache_tiles):
    if has_cache:
        q_ref, kn_ref, vn_ref, u2_ref, kc_ref, vc_ref, o_ref = refs[:7]
        scr = refs[7:]
    else:
        q_ref, kn_ref, vn_ref, u2_ref, o_ref = refs[:5]
        kc_ref = vc_ref = None
        scr = refs[5:]
    n_heads = width // SB_HEAD_DIM

    lane = lax.broadcasted_iota(jnp.int32, (tq, LANES), 1)
    row = lax.broadcasted_iota(jnp.int32, (tq, LANES), 0)
    lo_half = lane < SB_HEAD_DIM
    u2 = u2_ref[...]
    q = q_ref[0]

    if tq == KEY_TILE:
        kd = kn_ref[0].astype(BF16)
        vd = vn_ref[0].astype(BF16)
    else:
        kd_ref, vd_ref = scr
        kd_ref[...] = jnp.zeros_like(kd_ref)
        vd_ref[...] = jnp.zeros_like(vd_ref)
        kd_ref[0:tq, :] = kn_ref[0].astype(BF16)
        vd_ref[0:tq, :] = vn_ref[0].astype(BF16)
        kd = kd_ref[...]
        vd = vd_ref[...]

    qs = []
    for h in range(n_heads):
        qp = q[:, (h // 2) * LANES:(h // 2 + 1) * LANES]
        keep = lo_half if h % 2 == 0 else jnp.logical_not(lo_half)
        qs.append(jnp.where(keep, qp, jnp.zeros_like(qp)) * jnp.asarray(SB_HEAD_DIM ** -0.5, BF16))

    def tiles_all_heads(tiles, carries, accs):
        units = [(t, h) for t in range(len(tiles)) for h in range(n_heads)]
        ps = lambda h: slice((h // 2) * LANES, (h // 2 + 1) * LANES)
        z = {u: _dot_nt(qs[u[1]], tiles[u[0]][0][:, ps(u[1])]) for u in units}
        log_beta, cat = {}, {}
        for u in units:
            mask = tiles[u[0]][2]
            soft = jnp.log(1.0 + jnp.exp(-jnp.abs(z[u])))
            log_beta[u] = jnp.minimum(z[u], 0.0) - soft
            log_keep = log_beta[u] - z[u]
            if mask is not None:
                log_keep = jnp.where(mask, log_keep, 0.0)
            cat[u] = jnp.concatenate(_split2(log_keep), axis=1)
        r = {u: _dot(cat[u], u2) for u in units}
        carries, accs = list(carries), list(accs)
        w = {}
        for t, h in units:
            mask = tiles[t][2]
            wt = jnp.exp(log_beta[(t, h)] + r[(t, h)][:, :LANES] + carries[h])
            if mask is not None:
                wt = jnp.where(mask, wt, 0.0)
            w[(t, h)] = wt.astype(BF16)
            carries[h] = carries[h] + r[(t, h)][:, LANES:]
        for t, h in units:
            accs[h] = accs[h] + _dot(w[(t, h)], tiles[t][1][:, ps(h)])
        return carries, accs

    def any_alive(carries):
        m = carries[0]
        for c in carries[1:]:
            m = jnp.maximum(m, c)
        return (jnp.max(m) > DEAD_LOG).astype(jnp.int32)

    def cache_tile(j):
        if isinstance(j, int):
            off = N_META + j * KEY_TILE if j >= 0 else 0
            mask = None if j >= 0 else (lane < N_META if j == -1 else lane < 0)
        else:
            off = pl.multiple_of(jnp.where(j >= 0, N_META + j * KEY_TILE, 0), 8)
            mask = jnp.logical_or(j >= 0, jnp.logical_and(j == -1, lane < N_META))
        kt = kc_ref[0, pl.ds(off, KEY_TILE), :].astype(BF16)
        vt = vc_ref[0, pl.ds(off, KEY_TILE), :].astype(BF16)
        return kt, vt, mask

    zeros = jnp.zeros((tq, LANES), F32)
    tiles = [(kd, vd, lane < row)]
    if has_cache:
        j0 = pl.program_id(2) - 1 if n_cache_tiles is None else n_cache_tiles - 1
        for _ in range(ATTN_STATIC_TILES):
            tiles.append(cache_tile(j0))
            j0 = j0 - 1
    carries, accs = tiles_all_heads(tiles, [zeros] * n_heads, [zeros] * n_heads)
    if has_cache:
        def cond(s):
            return jnp.logical_and(s[0] >= -1, s[1] > 0)

        def body(s):
            j, _, carries, accs = s
            carries, accs = tiles_all_heads([cache_tile(j)], carries, accs)
            return j - 1, any_alive(carries), tuple(carries), tuple(accs)

        state = (jnp.asarray(j0, jnp.int32), any_alive(carries), tuple(carries), tuple(accs))
        _, _, carries, accs = lax.while_loop(cond, body, state)
    for p in range(n_heads // 2):
        o_ref[0, :, p * LANES:(p + 1) * LANES] = jnp.where(lo_half, accs[2 * p], accs[2 * p + 1]).astype(o_ref.dtype)


def _cumsum_rhs():
    s = np.arange(KEY_TILE)
    strict = (s[:, None] > s[None, :]).astype(np.float32)
    half = np.concatenate([strict, np.ones((KEY_TILE, KEY_TILE), np.float32)], axis=1)
    return jnp.asarray(np.concatenate([half, half], axis=0), BF16)


def attention(q_src, q_col0, k_new, v_new, k_cache, v_cache, n_cache_tiles, width):
    b, s, _ = q_src.shape
    tq = min(s, KEY_TILE)
    has_cache = k_cache is not None
    qb0 = q_col0 // width
    blk = lambda bi, hp, qi: (bi, qi, hp)
    in_specs = [pl.BlockSpec((1, tq, width), lambda bi, hp, qi: (bi, qi, qb0 + hp)),
                pl.BlockSpec((1, tq, width), blk),
                pl.BlockSpec((1, tq, width), blk),
                pl.BlockSpec((2 * KEY_TILE, 2 * KEY_TILE), lambda bi, hp, qi: (0, 0))]
    args = [q_src, k_new, v_new, _cumsum_rhs()]
    if has_cache:
        lc = k_cache.shape[1]
        in_specs += [pl.BlockSpec((1, lc, width), lambda bi, hp, qi: (bi, 0, hp))] * 2
        args += [k_cache, v_cache]
    scratch = [] if tq == KEY_TILE else [pltpu.VMEM((KEY_TILE, width), BF16)] * 2
    return pl.pallas_call(
        functools.partial(_attn_kernel, tq=tq, width=width, has_cache=has_cache, n_cache_tiles=n_cache_tiles),
        grid=(b, SB_WIDTH // width, s // tq),
        in_specs=in_specs,
        out_specs=pl.BlockSpec((1, tq, width), blk),
        out_shape=jax.ShapeDtypeStruct((b, s, SB_WIDTH), BF16),
        scratch_shapes=scratch,
        compiler_params=_cparams(("parallel", "parallel", "arbitrary")),
        name="stick_breaking",
    )(*args)


def _ssd_kernel(xbc_ref, z_ref, dt_ref, dtT_ref, prev_ref, h0_ref, cw_ref, cb_ref, dtb_ref, dtbT_ref,
                alog_ref, alogT_ref, dskip_ref, gn_ref, e3_ref, ltri_ref, ublk_ref,
                y_ref, hfin_ref, cbuf_ref, st_ref, *, rows, n_chunks):
    Q = SSD_CHUNK
    c = pl.program_id(1)
    n_blk = SSD_WIDTH // LANES

    @pl.when(c == 0)
    def _():
        cbuf_ref[8 - (SSD_CONV - 1):8, :] = prev_ref[0]
        for j in range(n_blk):
            st_ref[:, j * LANES:(j + 1) * LANES] = h0_ref[0, j * LANES:(j + 1) * LANES, :].T

    def pad_rows(v):
        if rows == Q:
            return v
        return jnp.concatenate([v, jnp.zeros((Q - rows, v.shape[1]), v.dtype)], axis=0)

    cbuf_ref[8:8 + Q, :] = pad_rows(xbc_ref[0].astype(F32))
    conv = cb_ref[...]
    for i in range(SSD_CONV):
        conv = conv + cbuf_ref[8 - (SSD_CONV - 1) + i:8 - (SSD_CONV - 1) + i + Q, :] * cw_ref[i:i + 1, :]
    cbuf_ref[8 - (SSD_CONV - 1):8, :] = cbuf_ref[8 + Q - (SSD_CONV - 1):8 + Q, :]
    xc = conv * _sigmoid(conv)
    xs = xc[:, :SSD_WIDTH]
    b_all = xc[:, SSD_WIDTH:SSD_WIDTH + SSD_GROUPS * SSD_STATE]
    c_all = xc[:, SSD_WIDTH + SSD_GROUPS * SSD_STATE:]

    dt = _softplus(pad_rows(dt_ref[0]) + dtb_ref[...])
    dt_t = _softplus(dtT_ref[0, 0] + dtbT_ref[...])
    if rows < Q:
        rowi = lax.broadcasted_iota(jnp.int32, (Q, 1), 0)
        dt = jnp.where(rowi < rows, dt, 0.0)
        xs = jnp.where(rowi < rows, xs, 0.0)
        lane_t = lax.broadcasted_iota(jnp.int32, dt_t.shape, 1)
        dt_t = jnp.where(lane_t % Q < rows, dt_t, 0.0)
    da = dt * (-jnp.exp(alog_ref[...]))
    da_t = dt_t * (-jnp.exp(alogT_ref[...]))

    a_cum = _dot(ltri_ref[...], jnp.concatenate(_split3(da), axis=0))
    a_cum_t = _dot(jnp.concatenate(_split3(da_t), axis=1), ublk_ref[...])
    dt_exp = _dot(jnp.concatenate(_split2(dt), axis=1), e3_ref[0:2 * LANES, :])
    a_exp = _dot(jnp.concatenate(_split3(a_cum), axis=1), e3_ref[...])
    a_last = a_exp[Q - 1:Q, :]

    xdt = xs * dt_exp
    xdt_b = xdt.astype(BF16)
    xw = (xdt * jnp.exp(a_last - a_exp)).astype(BF16)
    xw_pad = jnp.concatenate([xw, jnp.zeros_like(xw)], axis=0)
    chunk_decay = jnp.exp(a_last)
    grow = jnp.exp(a_exp)

    lane = lax.broadcasted_iota(jnp.int32, (Q, LANES), 1)
    rowq = lax.broadcasted_iota(jnp.int32, (Q, LANES), 0)
    causal2 = (lane % Q) <= rowq
    lo_half = lane < SSD_HEAD_DIM

    y_parts = []
    for g in range(SSD_GROUPS):
        bg = b_all[:, g * SSD_STATE:(g + 1) * SSD_STATE]
        cg = c_all[:, g * SSD_STATE:(g + 1) * SSD_STATE].astype(BF16)
        bg_b = bg.astype(BF16)
        cb2 = _dot_nt(cg, jnp.concatenate([bg_b, bg_b], axis=0))
        gs = slice(g * GROUP_W, (g + 1) * GROUP_W)
        st_g = st_ref[:, gs]
        y_off = _dot(cg, st_g.astype(BF16)) * grow[:, gs]
        bg_t = jnp.concatenate([bg, jnp.zeros_like(bg)], axis=0).T.astype(BF16)
        st_ref[:, gs] = st_g * chunk_decay[:, gs] + _dot(bg_t, xw_pad[:, gs])
        pair_out = []
        for k in range(GROUP_W // LANES):
            i = g * (GROUP_W // LANES) + k
            ps = slice(i * LANES, (i + 1) * LANES)
            decay = jnp.exp(jnp.minimum(a_exp[:, ps] - a_cum_t[i:i + 1, :], 0.0))
            m2 = jnp.where(causal2, cb2 * decay, 0.0).astype(BF16)
            xp = xdt_b[:, ps]
            zero = jnp.zeros_like(xp)
            xbd = jnp.concatenate([jnp.where(lo_half, xp, zero), jnp.where(lo_half, zero, xp)], axis=0)
            pair_out.append(_dot(m2, xbd))
        y_diag = jnp.concatenate(pair_out, axis=1)
        y = y_diag + y_off + dskip_ref[:, gs] * xs[:, gs]
        zg = pad_rows(z_ref[0, :, gs].astype(F32))
        y = y * (zg * _sigmoid(zg))
        ms = jnp.mean(y * y, axis=-1, keepdims=True)
        y_parts.append(y * lax.rsqrt(ms + EPS) * gn_ref[:, gs])
    y_all = jnp.concatenate(y_parts, axis=1)
    y_ref[0] = y_all[:rows].astype(y_ref.dtype)

    @pl.when(c == n_chunks - 1)
    def _():
        for j in range(n_blk):
            hfin_ref[0, j * LANES:(j + 1) * LANES, :] = st_ref[:, j * LANES:(j + 1) * LANES].T


def _ssd_constants():
    Q = SSD_CHUNK
    e = np.zeros((LANES, SSD_WIDTH), np.float32)
    for h in range(SSD_HEADS):
        e[h, h * SSD_HEAD_DIM:(h + 1) * SSD_HEAD_DIM] = 1.0
    e3 = np.concatenate([e, e, e], axis=0)
    t = np.arange(Q)
    ltri = (t[:, None] >= t[None, :]).astype(np.float32)
    ltri3 = np.concatenate([ltri, ltri, ltri], axis=1)
    ublk = np.zeros((LANES, LANES), np.float32)
    ublk[:Q, :Q] = ltri.T
    ublk[Q:, Q:] = ltri.T
    ublk3 = np.concatenate([ublk, ublk, ublk], axis=0)
    return jnp.asarray(e3, BF16), jnp.asarray(ltri3, BF16), jnp.asarray(ublk3, BF16)


def ssd_mixer(big, xbc_blk, z_blk, dt_raw, conv_prev, h0, p):
    b, s, _ = big.shape
    Q = SSD_CHUNK
    rows = min(s, Q)
    n_chunks = -(-s // Q)
    dtp = dt_raw[:, :, :SSD_HEADS]
    if s < Q:
        dtp = jnp.pad(dtp, ((0, 0), (0, Q - s), (0, 0)))
    dt_t = dtp.reshape(b, n_chunks, Q, SSD_HEADS).transpose(0, 1, 3, 2).reshape(b, n_chunks, SSD_HEADS // 2, 2 * Q)
    e3, ltri3, ublk3 = _ssd_constants()
    const = lambda shape: pl.BlockSpec(shape, lambda bi, ci: (0,) * len(shape))
    in_specs = [
        pl.BlockSpec((1, rows, SSD_CONV_CH), lambda bi, ci: (bi, ci, xbc_blk)),
        pl.BlockSpec((1, rows, SSD_WIDTH), lambda bi, ci: (bi, ci, z_blk)),
        pl.BlockSpec((1, rows, LANES), lambda bi, ci: (bi, ci, 0)),
        pl.BlockSpec((1, 1, SSD_HEADS // 2, 2 * Q), lambda bi, ci: (bi, ci, 0, 0)),
        pl.BlockSpec((1, SSD_CONV - 1, SSD_CONV_CH), lambda bi, ci: (bi, 0, 0)),
        pl.BlockSpec((1, SSD_WIDTH, SSD_STATE), lambda bi, ci: (bi, 0, 0)),
        const((SSD_CONV, SSD_CONV_CH)), const((1, SSD_CONV_CH)),
        const((1, LANES)), const((SSD_HEADS // 2, 2 * Q)),
        const((1, LANES)), const((SSD_HEADS // 2, 2 * Q)),
        const((1, SSD_WIDTH)), const((1, SSD_WIDTH)),
        const(e3.shape), const(ltri3.shape), const(ublk3.shape),
    ]
    y, h_fin = pl.pallas_call(
        functools.partial(_ssd_kernel, rows=rows, n_chunks=n_chunks),
        grid=(b, n_chunks),
        in_specs=in_specs,
        out_specs=[pl.BlockSpec((1, rows, SSD_WIDTH), lambda bi, ci: (bi, ci, 0)),
                   pl.BlockSpec((1, SSD_WIDTH, SSD_STATE), lambda bi, ci: (bi, 0, 0))],
        out_shape=[jax.ShapeDtypeStruct((b, s, SSD_WIDTH), BF16),
                   jax.ShapeDtypeStruct((b, SSD_WIDTH, SSD_STATE), F32)],
        scratch_shapes=[pltpu.VMEM((8 + Q, SSD_CONV_CH), F32), pltpu.VMEM((SSD_STATE, SSD_WIDTH), F32)],
        compiler_params=_cparams(("parallel", "arbitrary")),
        name="ssd_mixer",
    )(big, big, dt_raw, dt_t, conv_prev, h0, p['conv_w_ssd'], p['conv_b_ssd'], p['dt_bias'], p['dt_bias_t'],
      p['a_log'], p['a_log_t'], p['d_skip'], p['g_ssd_norm'], e3, ltri3, ublk3)
    return y, h_fin


def _merge_kernel(att_ref, ssd_ref, ga_ref, gs_ref, h_ref, wa_ref, ws_ref, wo_ref, o_ref, wa_b, ws_b, wo_b):
    @pl.when(pl.program_id(0) == 0)
    def _():
        wa_b[...] = wa_ref[...].astype(BF16)
        ws_b[...] = ws_ref[...].astype(BF16)
        wo_b[...] = wo_ref[...].astype(BF16)

    a = _dot(att_ref[...], wa_b[...])
    s = _dot(ssd_ref[...], ws_b[...])
    merged = _sigmoid(ga_ref[...].astype(F32)) * a + _sigmoid(gs_ref[...].astype(F32)) * s
    o_ref[...] = h_ref[...] + _dot(merged.astype(BF16), wo_b[...])


def _resident(a):
    return pl.BlockSpec(a.shape, lambda i: (0,) * a.ndim, pipeline_mode=pl.Buffered(1))


def merge(att, ssd, big, h, p):
    m = h.shape[0]
    tm = _pick(m, (512, 256, 128))
    row = lambda w: pl.BlockSpec((tm, w), lambda i: (i, 0))
    gate_blk = BIG_GATE // D_MODEL
    ws = [p['w_br_att'], p['w_br_ssd'], p['w_out']]
    return pl.pallas_call(
        _merge_kernel,
        grid=(m // tm,),
        in_specs=[row(SB_WIDTH), row(SSD_WIDTH),
                  pl.BlockSpec((tm, D_MODEL), lambda i: (i, gate_blk)),
                  pl.BlockSpec((tm, D_MODEL), lambda i: (i, gate_blk + 1)),
                  row(D_MODEL)] + [_resident(w) for w in ws],
        out_specs=row(D_MODEL),
        out_shape=jax.ShapeDtypeStruct((m, D_MODEL), F32),
        scratch_shapes=[pltpu.VMEM(w.shape, BF16) for w in ws],
        compiler_params=_cparams(("arbitrary",)),
        name="merge_out_proj",
    )(att, ssd, big, big, h, *ws)


def _ffn_act_kernel(ug_ref, uv_ref, pg_ref, pv_ref, wg_ref, wv_ref, bg_ref, bv_ref, o_ref, gbuf_ref, vbuf_ref,
                    *, ts):
    t = pl.program_id(1)
    k1 = FFN_CONV - 1

    def conv(u_ref, p_ref, w_ref, b_ref, buf_ref):
        @pl.when(t == 0)
        def _():
            buf_ref[8 - k1:8, :] = p_ref[0]

        buf_ref[8:8 + ts, :] = u_ref[0].astype(F32)
        y = b_ref[...]
        for i in range(FFN_CONV):
            y = y + buf_ref[8 - k1 + i:8 - k1 + i + ts, :] * w_ref[i:i + 1, :]
        buf_ref[8 - k1:8, :] = buf_ref[8 + ts - k1:8 + ts, :]
        return y

    gate = conv(ug_ref, pg_ref, wg_ref, bg_ref, gbuf_ref)
    val = conv(uv_ref, pv_ref, wv_ref, bv_ref, vbuf_ref)
    o_ref[0] = (gate * _sigmoid(gate) * val).astype(o_ref.dtype)


def ffn_act(up, prev, p):
    b, s, _ = up.shape
    ts = _pick(s, (512, 256, 128))
    half = lambda blk: pl.BlockSpec((1, ts, D_FF), lambda bi, ti: (bi, ti, blk))
    prevs = lambda blk: pl.BlockSpec((1, FFN_CONV - 1, D_FF), lambda bi, ti: (bi, 0, blk))
    wspec = lambda blk: pl.BlockSpec((FFN_CONV, D_FF), lambda bi, ti: (0, blk))
    bspec = lambda blk: pl.BlockSpec((1, D_FF), lambda bi, ti: (0, blk))
    return pl.pallas_call(
        functools.partial(_ffn_act_kernel, ts=ts),
        grid=(b, s // ts),
        in_specs=[half(0), half(1), prevs(0), prevs(1), wspec(0), wspec(1), bspec(0), bspec(1)],
        out_specs=pl.BlockSpec((1, ts, D_FF), lambda bi, ti: (bi, ti, 0)),
        out_shape=jax.ShapeDtypeStruct((b, s, D_FF), BF16),
        scratch_shapes=[pltpu.VMEM((8 + ts, D_FF), F32)] * 2,
        compiler_params=_cparams(("parallel", "arbitrary")),
        name="ffn_conv_act",
    )(up, up, prev, prev, p['conv_w_ffn'], p['conv_w_ffn'], p['conv_b_ffn'], p['conv_b_ffn'])


def _down_kernel(a_ref, w_ref, h_ref, g_ref, o_ref, w_b):
    @pl.when(pl.program_id(0) == 0)
    def _():
        w_b[...] = w_ref[...].astype(BF16)

    h = h_ref[...] + _dot(a_ref[...], w_b[...])
    ms = jnp.mean(h * h, axis=-1, keepdims=True)
    o_ref[...] = h * lax.rsqrt(ms + EPS) * g_ref[...]


def down_norm(act, h, p):
    m = h.shape[0]
    tm = _pick(m, (512, 256, 128))
    return pl.pallas_call(
        _down_kernel,
        grid=(m // tm,),
        in_specs=[pl.BlockSpec((tm, D_FF), lambda i: (i, 0)),
                  _resident(p['w_down']),
                  pl.BlockSpec((tm, D_MODEL), lambda i: (i, 0)),
                  pl.BlockSpec((1, D_MODEL), lambda i: (0, 0))],
        out_specs=pl.BlockSpec((tm, D_MODEL), lambda i: (i, 0)),
        out_shape=jax.ShapeDtypeStruct((m, D_MODEL), F32),
        scratch_shapes=[pltpu.VMEM(p['w_down'].shape, BF16)],
        compiler_params=_cparams(("arbitrary",)),
        name="down_proj_norm",
    )(act, p['w_down'], h, p['g_final'])


def stream_layer(h, p):
    b, s, _ = h.shape
    hf = h.reshape(b * s, D_MODEL)
    k, v, big, dt_raw = in_proj(hf, p['g_mix'], p['w_in'], p['w_tail'])
    return (big.reshape(b, s, BIG_W), k.reshape(b, s, SB_WIDTH), v.reshape(b, s, SB_WIDTH),
            dt_raw.reshape(b, s, LANES), hf)


def stream_rest(big3, k, v, dt_raw, hf, kv_cache, n_cache_tiles, ssd_h0, ssd_conv_prev, ffn_conv_prev, p):
    b, s, _ = big3.shape
    m = b * s
    kc, vc = kv_cache if kv_cache is not None else (None, None)
    attn_width = 4 * SB_HEAD_DIM if s > KEY_TILE else SB_WIDTH
    att = attention(big3, BIG_Q, k, v, kc, vc, n_cache_tiles, attn_width)
    ssd, ssd_state = ssd_mixer(big3, BIG_XBC // SSD_CONV_CH, BIG_Z // SSD_WIDTH, dt_raw, ssd_conv_prev, ssd_h0, p)
    h1 = merge(att.reshape(m, SB_WIDTH), ssd.reshape(m, SSD_WIDTH), big3.reshape(m, BIG_W), hf, p)
    up = norm_mm(h1, p['g_ffn'], p['w_up'], BF16).reshape(b, s, 2 * D_FF)
    act = ffn_act(up, ffn_conv_prev, p)
    y = down_norm(act.reshape(m, D_FF), h1, p).reshape(b, s, D_MODEL)
    xbc_rows = jnp.concatenate([ssd_conv_prev, big3[:, :, BIG_XBC:BIG_XBC + SSD_CONV_CH][:, -(SSD_CONV - 1):].astype(F32)],
                               axis=1)[:, -(SSD_CONV - 1):]
    up_rows = jnp.concatenate([ffn_conv_prev, up[:, -(FFN_CONV - 1):].astype(F32)], axis=1)[:, -(FFN_CONV - 1):]
    return y, ssd_state, xbc_rows, up_rows


def _prep_params(g_mix, w_in, conv_w_ssd, conv_b_ssd, dt_bias, a_log, d_skip, g_ssd_norm, w_br_att, w_br_ssd,
                 w_out, g_ffn, w_up, conv_w_ffn, conv_b_ffn, w_down, g_final):
    dt0 = 3 * SB_WIDTH + SSD_WIDTH + SSD_CONV_CH
    w_tail = jnp.concatenate([w_in[:, dt0 + SSD_HEADS:], w_in[:, dt0:dt0 + SSD_HEADS],
                              jnp.zeros((D_MODEL, LANES - SSD_HEADS), F32)], axis=1)
    Q = SSD_CHUNK
    lanes_t = lambda v: jnp.repeat(v.reshape(SSD_HEADS // 2, 2, 1), Q, axis=2).reshape(SSD_HEADS // 2, 2 * Q)
    pad_l = lambda v: jnp.pad(v, (0, LANES - SSD_HEADS)).reshape(1, LANES)
    return {
        'g_mix': g_mix.reshape(1, -1), 'g_ffn': g_ffn.reshape(1, -1), 'g_final': g_final.reshape(1, -1),
        'w_in': w_in, 'w_tail': w_tail,
        'conv_w_ssd': conv_w_ssd, 'conv_b_ssd': conv_b_ssd.reshape(1, -1),
        'dt_bias': pad_l(dt_bias), 'dt_bias_t': lanes_t(dt_bias),
        'a_log': pad_l(a_log), 'a_log_t': lanes_t(a_log),
        'd_skip': jnp.repeat(d_skip, SSD_HEAD_DIM).reshape(1, -1), 'g_ssd_norm': g_ssd_norm.reshape(1, -1),
        'w_br_att': w_br_att, 'w_br_ssd': w_br_ssd, 'w_out': w_out,
        'w_up': w_up, 'conv_w_ffn': conv_w_ffn, 'conv_b_ffn': conv_b_ffn.reshape(1, -1),
        'w_down': w_down,
    }


def kernel(x_prompt, x_sample, cache_k, cache_v, state_ssm, state_ssm_conv, state_ffn_conv, meta_tokens, g_mix, w_in, conv_w_ssd, conv_b_ssd, dt_bias, a_log, d_skip, g_ssd_norm, w_br_att, w_br_ssd, w_out, g_ffn, w_up, conv_w_ffn, conv_b_ffn, w_down, g_final):
    bp, bs = x_prompt.shape[0], x_sample.shape[0]
    p = _prep_params(g_mix[0], w_in[0], conv_w_ssd[0], conv_b_ssd[0], dt_bias[0], a_log[0], d_skip[0],
                     g_ssd_norm[0], w_br_att[0], w_br_ssd[0], w_out[0], g_ffn[0], w_up[0], conv_w_ffn[0],
                     conv_b_ffn[0], w_down[0], g_final)
    state_shape = (SSD_WIDTH, SSD_STATE)

    hm = meta_tokens[None]
    big_m, k_m, v_m, dt_m, hf_m = stream_layer(hm, p)
    _, ssm_m, conv_m, ffn_m = stream_rest(
        big_m, k_m, v_m, dt_m, hf_m, None, 0, jnp.zeros((1,) + state_shape, F32),
        jnp.zeros((1, SSD_CONV - 1, SSD_CONV_CH), F32), jnp.zeros((1, FFN_CONV - 1, 2 * D_FF), F32), p)

    big_x, k_x, v_x, dt_x, hf_x = stream_layer(x_prompt, p)
    rep = lambda a: jnp.broadcast_to(a, (bp,) + a.shape[1:])
    k_prompt = jnp.concatenate([rep(k_m), k_x], axis=1)
    v_prompt = jnp.concatenate([rep(v_m), v_x], axis=1)
    y_prompt, ssm_p, conv_p, ffn_p = stream_rest(
        big_x, k_x, v_x, dt_x, hf_x, (k_prompt, v_prompt), None, rep(ssm_m), rep(conv_m), rep(ffn_m), p)

    n_rows = cache_k.shape[2]
    big_s, k_s, v_s, dt_s, hf_s = stream_layer(x_sample, p)
    y_sample, ssm_s, conv_s, ffn_s = stream_rest(
        big_s, k_s, v_s, dt_s, hf_s,
        (cache_k[0].reshape(bs, n_rows, SB_WIDTH), cache_v[0].reshape(bs, n_rows, SB_WIDTH)),
        (n_rows - N_META) // KEY_TILE,
        state_ssm[0].reshape((bs,) + state_shape), state_ssm_conv[0], state_ffn_conv[0], p)

    heads = lambda a: a.reshape(a.shape[0], a.shape[1], SB_HEADS, SB_HEAD_DIM)[None]
    state5 = lambda a: a.reshape(a.shape[0], SSD_HEADS, SSD_HEAD_DIM, SSD_STATE)[None]
    return (y_prompt, y_sample, heads(k_prompt), heads(v_prompt), state5(ssm_p), conv_p[None], ffn_p[None],
            heads(k_s), heads(v_s), state5(ssm_s), conv_s[None], ffn_s[None])
```

```python
import functools

import numpy as np
import jax
import jax.numpy as jnp
from jax import lax
from jax.experimental import pallas as pl
from jax.experimental.pallas import tpu as pltpu

F32 = jnp.float32
BF16 = jnp.bfloat16

D_MODEL = 1024
N_META = 16
SB_HEADS = 16
SB_HEAD_DIM = 64
SB_WIDTH = SB_HEADS * SB_HEAD_DIM
SSD_WIDTH = 2 * D_MODEL
SSD_HEAD_DIM = 64
SSD_HEADS = SSD_WIDTH // SSD_HEAD_DIM
SSD_GROUPS = 4
SSD_STATE = 128
SSD_CONV = 4
SSD_CONV_CH = SSD_WIDTH + 2 * SSD_GROUPS * SSD_STATE
D_FF = 2816
FFN_CONV = 3
EPS = 1e-6

LANES = 128
SSD_CHUNK = 64
KEY_TILE = 128
GROUP_W = SSD_WIDTH // SSD_GROUPS
DEAD_LOG = -104.0
ATTN_STATIC_TILES = 2
VMEM_LIMIT = 56 * 1024 * 1024


def _cparams(sem):
    return pltpu.CompilerParams(dimension_semantics=sem, vmem_limit_bytes=VMEM_LIMIT)


def _pick(n, cands):
    for c in cands:
        if n % c == 0:
            return c
    return n


def _split2(x):
    hi = x.astype(BF16)
    lo = (x - hi.astype(F32)).astype(BF16)
    return hi, lo


def _split3(x):
    hi = x.astype(BF16)
    r = x - hi.astype(F32)
    mid = r.astype(BF16)
    lo = (r - mid.astype(F32)).astype(BF16)
    return hi, mid, lo


def _dot(a, b):
    return jnp.dot(a, b, preferred_element_type=F32)


def _dot_nt(a, b):
    return lax.dot_general(a, b, (((1,), (1,)), ((), ())), preferred_element_type=F32)


def _softplus(x):
    return jnp.maximum(x, 0.0) + jnp.log(1.0 + jnp.exp(-jnp.abs(x)))


def _sigmoid(x):
    return 1.0 / (1.0 + jnp.exp(-x))


def _rms_norm_bf16(x, g):
    ms = jnp.mean(x * x, axis=-1, keepdims=True)
    return (x * lax.rsqrt(ms + EPS) * g).astype(BF16)


def _norm_mm_kernel(x_ref, g_ref, w_ref, o_ref, u_ref):
    @pl.when(pl.program_id(1) == 0)
    def _():
        u_ref[...] = _rms_norm_bf16(x_ref[...], g_ref[...])

    o_ref[...] = _dot(u_ref[...], w_ref[...].astype(BF16)).astype(o_ref.dtype)


PROJ_TN = 512


def norm_mm(x, g, w, out_dtype):
    m, d = x.shape
    n = w.shape[1]
    tm = _pick(m, (2048, 1024, 512, 256, 128))
    tn = PROJ_TN
    return pl.pallas_call(
        _norm_mm_kernel,
        grid=(m // tm, n // tn),
        in_specs=[pl.BlockSpec((tm, d), lambda i, j: (i, 0)),
                  pl.BlockSpec((1, d), lambda i, j: (0, 0)),
                  pl.BlockSpec((d, tn), lambda i, j: (0, j))],
        out_specs=pl.BlockSpec((tm, tn), lambda i, j: (i, j)),
        out_shape=jax.ShapeDtypeStruct((m, n), out_dtype),
        scratch_shapes=[pltpu.VMEM((tm, d), BF16)],
        compiler_params=_cparams(("parallel", "arbitrary")),
        name="norm_up_proj",
    )(x, g, w)


_NQ = SB_WIDTH // PROJ_TN
_J_K, _J_V, _J_Z = _NQ, 2 * _NQ, 3 * _NQ
_J_X = _J_Z + SSD_WIDTH // PROJ_TN
_J_G = _J_X + SSD_CONV_CH // PROJ_TN
_J_END = _J_G + 2 * D_MODEL // PROJ_TN
BIG_Z, BIG_Q, BIG_XBC, BIG_GATE = 0, SSD_WIDTH, SSD_WIDTH + SB_WIDTH, SSD_WIDTH + SB_WIDTH + SSD_CONV_CH
BIG_W = BIG_GATE + 2 * D_MODEL


def _big_block(j):
    return jnp.where(j < _J_K, BIG_Q // PROJ_TN + j,
                     jnp.where(j < _J_Z, BIG_Q // PROJ_TN + _NQ - 1,
                               jnp.where(j < _J_X, j - _J_Z + BIG_Z // PROJ_TN,
                                         jnp.where(j < _J_G, j - _J_X + BIG_XBC // PROJ_TN,
                                                   j - _J_G + BIG_GATE // PROJ_TN))))


def _sweep_step(j, with_kv):
    return j if with_kv else jnp.where(j >= _J_K, j + (_J_Z - _J_K), j)


def _in_proj_kernel(*refs, with_kv):
    if with_kv:
        x_ref, g_ref, win_ref, wgate_ref, wdt_ref, k_ref, v_ref, big_ref, dt_ref, u_ref = refs
    else:
        x_ref, g_ref, win_ref, wgate_ref, wdt_ref, big_ref, dt_ref, u_ref = refs
    j = _sweep_step(pl.program_id(1), with_kv)

    def proj(w_ref):
        return _dot_nt(u_ref[...], w_ref[...].astype(BF16))

    @pl.when(pl.program_id(1) == 0)
    def _():
        u_ref[...] = _rms_norm_bf16(x_ref[...], g_ref[...])
        dt_ref[...] = proj(wdt_ref)

    if with_kv:
        @pl.when(jnp.logical_and(j >= _J_K, j < _J_V))
        def _():
            k_ref[...] = proj(win_ref)

        @pl.when(jnp.logical_and(j >= _J_V, j < _J_Z))
        def _():
            v_ref[...] = proj(win_ref)

    @pl.when(jnp.logical_or(j < _J_K, jnp.logical_and(j >= _J_Z, j < _J_G)))
    def _():
        big_ref[...] = proj(win_ref).astype(BF16)

    @pl.when(j >= _J_G)
    def _():
        big_ref[...] = proj(wgate_ref).astype(BF16)


def in_proj(x, g, w_t, w_tail_t, with_kv):
    m, d = x.shape
    tm = _pick(m, (2048, 1024, 512, 256, 128))
    tn = PROJ_TN
    step = functools.partial(_sweep_step, with_kv=with_kv)
    clip = lambda j, lo, n: jnp.clip(step(j) - lo, 0, n - 1)
    kv_specs = [pl.BlockSpec((tm, tn), lambda i, j: (i, clip(j, _J_K, _NQ))),
                pl.BlockSpec((tm, tn), lambda i, j: (i, clip(j, _J_V, _NQ)))]
    kv_shapes = [jax.ShapeDtypeStruct((m, SB_WIDTH), F32)] * 2
    return pl.pallas_call(
        functools.partial(_in_proj_kernel, with_kv=with_kv),
        grid=(m // tm, _J_END if with_kv else _J_END - (_J_Z - _J_K)),
        in_specs=[pl.BlockSpec((tm, d), lambda i, j: (i, 0), pipeline_mode=pl.Buffered(1)),
                  pl.BlockSpec((1, d), lambda i, j: (0, 0)),
                  pl.BlockSpec((tn, d), lambda i, j: (jnp.minimum(step(j), _J_G - 1), 0)),
                  pl.BlockSpec((tn, d), lambda i, j: (clip(j, _J_G, _J_END - _J_G), 0)),
                  pl.BlockSpec((LANES, d), lambda i, j: (2 * D_MODEL // LANES, 0))],
        out_specs=(kv_specs if with_kv else []) + [
            pl.BlockSpec((tm, tn), lambda i, j: (i, _big_block(step(j)))),
            pl.BlockSpec((tm, LANES), lambda i, j: (i, 0))],
        out_shape=(kv_shapes if with_kv else []) + [
            jax.ShapeDtypeStruct((m, BIG_W), BF16), jax.ShapeDtypeStruct((m, LANES), F32)],
        scratch_shapes=[pltpu.VMEM((tm, d), BF16)],
        compiler_params=_cparams(("parallel", "arbitrary")),
        name="in_proj",
    )(x, g, w_t, w_tail_t, w_tail_t)


KVT_TS = 1024


def _kv_t_kernel(x16_ref, xt_ref, meta_ref, g_ref, w_ref, kt_ref, vt_ref, kl_ref, vl_ref, u_ref):
    j = pl.program_id(1)
    n = pl.program_id(2)

    @pl.when(n == 0)
    def _():
        first = jnp.where(j == 0, meta_ref[...], x16_ref[0])
        x = jnp.concatenate([first, xt_ref[0, :KVT_TS - N_META, :]], axis=0)
        u_ref[...] = _rms_norm_bf16(x, g_ref[...])

    w = w_ref[...].astype(BF16)
    out = _dot_nt(w, u_ref[...])
    last = j == pl.num_programs(1) - 1

    @pl.when(n == 0)
    def _():
        kt_ref[0] = out

    @pl.when(n == 1)
    def _():
        vt_ref[0] = out

    @pl.when(jnp.logical_and(last, n == 0))
    def _():
        kl_ref[0] = _dot_nt(u_ref[0:N_META, :], w)

    @pl.when(jnp.logical_and(last, n == 1))
    def _():
        vl_ref[0] = _dot_nt(u_ref[0:N_META, :], w)


def kv_transposed(x, meta, g, w_t):
    b, s, d = x.shape
    n_pos = N_META + s
    n_tiles = -(-n_pos // KVT_TS)
    assert n_pos - (n_tiles - 1) * KVT_TS == N_META
    per_tile = KVT_TS // N_META
    out_spec = pl.BlockSpec((1, SB_WIDTH, KVT_TS), lambda bi, j, n: (bi, 0, j))
    last_spec = pl.BlockSpec((1, N_META, SB_WIDTH), lambda bi, j, n: (bi, 0, 0))
    return pl.pallas_call(
        _kv_t_kernel,
        grid=(b, n_tiles, 2),
        in_specs=[pl.BlockSpec((1, N_META, d), lambda bi, j, n: (bi, jnp.maximum(j * per_tile - 1, 0), 0)),
                  pl.BlockSpec((1, KVT_TS, d), lambda bi, j, n: (bi, jnp.minimum(j, s // KVT_TS - 1), 0)),
                  pl.BlockSpec((N_META, d), lambda bi, j, n: (0, 0)),
                  pl.BlockSpec((1, d), lambda bi, j, n: (0, 0)),
                  pl.BlockSpec((SB_WIDTH, d), lambda bi, j, n: (1 + n, 0))],
        out_specs=[out_spec, out_spec, last_spec, last_spec],
        out_shape=[jax.ShapeDtypeStruct((b, SB_WIDTH, n_pos), F32)] * 2
        + [jax.ShapeDtypeStruct((b, N_META, SB_WIDTH), F32)] * 2,
        scratch_shapes=[pltpu.VMEM((KVT_TS, d), BF16)],
        compiler_params=_cparams(("parallel", "arbitrary", "arbitrary")),
        name="kv_transposed",
    )(x, x, meta, g, w_t)


def _stick_breaking_tiles(qs, u2, tiles, carries, accs):
    n_heads = len(qs)
    units = [(t, h) for t in range(len(tiles)) for h in range(n_heads)]
    ps = lambda h: slice((h // 2) * LANES, (h // 2 + 1) * LANES)

    def scores(t, h):
        kind, keys = tiles[t][0], tiles[t][1]
        return _dot_nt(qs[h], keys[:, ps(h)]) if kind == 'rows' else _dot(qs[h], keys[ps(h), :])

    z = {u: scores(*u) for u in units}
    log_beta, cat = {}, {}
    for u in units:
        mask = tiles[u[0]][3]
        soft = jnp.log(1.0 + jnp.exp(-jnp.abs(z[u])))
        log_beta[u] = jnp.minimum(z[u], 0.0) - soft
        log_keep = log_beta[u] - z[u]
        if mask is not None:
            log_keep = jnp.where(mask, log_keep, 0.0)
        cat[u] = jnp.concatenate(_split2(log_keep), axis=1)
    r = {u: _dot(cat[u], u2) for u in units}
    carries, accs = list(carries), list(accs)
    w = {}
    for t, h in units:
        mask = tiles[t][3]
        wt = jnp.exp(log_beta[(t, h)] + r[(t, h)][:, :LANES] + carries[h])
        if mask is not None:
            wt = jnp.where(mask, wt, 0.0)
        w[(t, h)] = wt.astype(BF16)
        carries[h] = carries[h] + r[(t, h)][:, LANES:]
    for t, h in units:
        kind, vals = tiles[t][0], tiles[t][2]
        pv = _dot(w[(t, h)], vals[:, ps(h)]) if kind == 'rows' else _dot_nt(w[(t, h)], vals[ps(h), :])
        accs[h] = accs[h] + pv
    return carries, accs


def _head_queries(q, lo_half):
    qs = []
    for h in range(q.shape[1] // SB_HEAD_DIM):
        qp = q[:, (h // 2) * LANES:(h // 2 + 1) * LANES]
        keep = lo_half if h % 2 == 0 else jnp.logical_not(lo_half)
        qs.append(jnp.where(keep, qp, jnp.zeros_like(qp)) * jnp.asarray(SB_HEAD_DIM ** -0.5, BF16))
    return qs


def _any_alive(carries):
    m = carries[0]
    for c in carries[1:]:
        m = jnp.maximum(m, c)
    return (jnp.max(m) > DEAD_LOG).astype(jnp.int32)


def _attend(qs, u2, first_tiles, kt_ref, vt_ref, j0, lane):
    tq = qs[0].shape[0]

    def cache_tile(j):
        if isinstance(j, int):
            off, mask = max(j, 0) * KEY_TILE, (None if j >= 0 else lane < 0)
        else:
            off, mask = pl.multiple_of(jnp.maximum(j, 0) * KEY_TILE, KEY_TILE), j >= 0
            mask = jnp.logical_and(mask, lane >= 0)
        kt = kt_ref[0, :, pl.ds(off, KEY_TILE)].astype(BF16)
        vt = vt_ref[0, :, pl.ds(off, KEY_TILE)].astype(BF16)
        return 'cols', kt, vt, mask

    tiles = list(first_tiles)
    if kt_ref is not None:
        for _ in range(ATTN_STATIC_TILES):
            tiles.append(cache_tile(j0))
            j0 = j0 - 1
    zeros = jnp.zeros((tq, LANES), F32)
    carries, accs = _stick_breaking_tiles(qs, u2, tiles, [zeros] * len(qs), [zeros] * len(qs))
    if kt_ref is not None:
        def cond(s):
            return jnp.logical_and(s[0] >= 0, s[1] > 0)

        def body(s):
            j, _, carries, accs = s
            carries, accs = _stick_breaking_tiles(qs, u2, [cache_tile(j)], carries, accs)
            return j - 1, _any_alive(carries), tuple(carries), tuple(accs)

        state = (jnp.asarray(j0, jnp.int32), _any_alive(carries), tuple(carries), tuple(accs))
        _, _, carries, accs = lax.while_loop(cond, body, state)
    return accs


def _pair_outputs(accs, lo_half, dtype):
    return [jnp.where(lo_half, accs[2 * p], accs[2 * p + 1]).astype(dtype) for p in range(len(accs) // 2)]


def _attn_prompt_kernel(q16_ref, qt_ref, qm_ref, u2_ref, kt_ref, vt_ref, o_ref):
    m = pl.program_id(2)
    tq = KEY_TILE
    lane = lax.broadcasted_iota(jnp.int32, (tq, LANES), 1)
    row = lax.broadcasted_iota(jnp.int32, (tq, LANES), 0)
    lo_half = lane < SB_HEAD_DIM
    first = jnp.where(m == 0, qm_ref[0], q16_ref[0])
    q = jnp.concatenate([first, qt_ref[0, :tq - N_META, :]], axis=0)
    qs = _head_queries(q, lo_half)
    off = pl.multiple_of(m * KEY_TILE, KEY_TILE)
    diag = ('cols', kt_ref[0, :, pl.ds(off, KEY_TILE)].astype(BF16),
            vt_ref[0, :, pl.ds(off, KEY_TILE)].astype(BF16), lane < row)
    accs = _attend(qs, u2_ref[...], [diag], kt_ref, vt_ref, m - 1, lane)
    out = jnp.concatenate(_pair_outputs(accs, lo_half, o_ref.dtype), axis=1)

    @pl.when(m == 0)
    def _():
        o_ref[0, 0:tq - N_META, :] = out[N_META:, :]
        o_ref[0, o_ref.shape[1] - N_META:, :] = jnp.zeros((N_META, o_ref.shape[2]), o_ref.dtype)

    @pl.when(m > 0)
    def _():
        o_ref[0, pl.ds(pl.multiple_of(m * KEY_TILE - N_META, N_META), tq), :] = out


def _attn_step_kernel(*refs, n_free, n_cache_tiles):
    if n_cache_tiles:
        q_ref, kd_ref, vd_ref, u2_ref, kt_ref, vt_ref, o_ref, kd_scr, vd_scr = refs
    else:
        q_ref, kd_ref, vd_ref, u2_ref, o_ref, kd_scr, vd_scr = refs
        kt_ref = vt_ref = None
    tq = q_ref.shape[1]
    n_rows = kd_ref.shape[1]
    lane = lax.broadcasted_iota(jnp.int32, (tq, LANES), 1)
    row = lax.broadcasted_iota(jnp.int32, (tq, LANES), 0)
    lo_half = lane < SB_HEAD_DIM
    kd_scr[...] = jnp.zeros_like(kd_scr)
    vd_scr[...] = jnp.zeros_like(vd_scr)
    kd_scr[0:n_rows, :] = kd_ref[0].astype(BF16)
    vd_scr[0:n_rows, :] = vd_ref[0].astype(BF16)
    diag = ('rows', kd_scr[...], vd_scr[...], lane < row + n_free)
    qs = _head_queries(q_ref[0], lo_half)
    accs = _attend(qs, u2_ref[...], [diag], kt_ref, vt_ref, n_cache_tiles - 1, lane)
    for p, out in enumerate(_pair_outputs(accs, lo_half, o_ref.dtype)):
        o_ref[0, :, p * LANES:(p + 1) * LANES] = out


def _cumsum_rhs():
    s = np.arange(KEY_TILE)
    strict = (s[:, None] > s[None, :]).astype(np.float32)
    half = np.concatenate([strict, np.ones((KEY_TILE, KEY_TILE), np.float32)], axis=1)
    return jnp.asarray(np.concatenate([half, half], axis=0), BF16)


ATTN_PROMPT_WIDTH = 4 * SB_HEAD_DIM


def attention_prompt(big, big_meta, k_t, v_t):
    b, s, _ = big.shape
    width = ATTN_PROMPT_WIDTH
    qb0 = BIG_Q // width
    per_tile = KEY_TILE // N_META
    n_pos = k_t.shape[2]
    cache = pl.BlockSpec((1, width, n_pos), lambda bi, hp, m: (bi, hp, 0))
    return pl.pallas_call(
        _attn_prompt_kernel,
        grid=(b, SB_WIDTH // width, s // KEY_TILE),
        in_specs=[pl.BlockSpec((1, N_META, width), lambda bi, hp, m: (bi, jnp.maximum(m * per_tile - 1, 0), qb0 + hp)),
                  pl.BlockSpec((1, KEY_TILE, width), lambda bi, hp, m: (bi, m, qb0 + hp)),
                  pl.BlockSpec((1, N_META, width), lambda bi, hp, m: (0, 0, qb0 + hp)),
                  pl.BlockSpec((2 * KEY_TILE, 2 * KEY_TILE), lambda bi, hp, m: (0, 0)),
                  cache, cache],
        out_specs=pl.BlockSpec((1, s, width), lambda bi, hp, m: (bi, 0, hp)),
        out_shape=jax.ShapeDtypeStruct((b, s, SB_WIDTH), BF16),
        compiler_params=_cparams(("parallel", "parallel", "arbitrary")),
        name="stick_breaking_prompt",
    )(big, big, big_meta, _cumsum_rhs(), k_t, v_t)


def attention_step(big, q_row_blk, k_rows, v_rows, k_t, v_t, n_cache_tiles, width):
    b = big.shape[0]
    n_rows = k_rows.shape[1]
    qb0 = BIG_Q // width
    rows = pl.BlockSpec((1, n_rows, width), lambda bi, hp: (bi, 0, hp))
    in_specs = [pl.BlockSpec((1, N_META, width), lambda bi, hp: (bi, q_row_blk, qb0 + hp)), rows, rows,
                pl.BlockSpec((2 * KEY_TILE, 2 * KEY_TILE), lambda bi, hp: (0, 0))]
    args = [big, k_rows, v_rows, _cumsum_rhs()]
    if n_cache_tiles:
        in_specs += [pl.BlockSpec((1, width, k_t.shape[2]), lambda bi, hp: (bi, hp, 0))] * 2
        args += [k_t, v_t]
    return pl.pallas_call(
        functools.partial(_attn_step_kernel, n_free=n_rows - N_META, n_cache_tiles=n_cache_tiles),
        grid=(b, SB_WIDTH // width),
        in_specs=in_specs,
        out_specs=pl.BlockSpec((1, N_META, width), lambda bi, hp: (bi, 0, hp)),
        out_shape=jax.ShapeDtypeStruct((b, N_META, SB_WIDTH), BF16),
        scratch_shapes=[pltpu.VMEM((KEY_TILE, width), BF16)] * 2,
        compiler_params=_cparams(("parallel", "parallel")),
        name="stick_breaking_step",
    )(*args)


def _ssd_kernel(xbc_ref, z_ref, dt_ref, dtT_ref, prev_ref, h0_ref, cw_ref, cb_ref, dtb_ref, dtbT_ref,
                alog_ref, alogT_ref, dskip_ref, gn_ref, e3_ref, ltri_ref, ublk_ref,
                y_ref, hfin_ref, cbuf_ref, st_ref, *, rows, n_chunks):
    Q = SSD_CHUNK
    c = pl.program_id(1)
    n_blk = SSD_WIDTH // LANES

    @pl.when(c == 0)
    def _():
        cbuf_ref[8 - (SSD_CONV - 1):8, :] = prev_ref[0]
        for j in range(n_blk):
            st_ref[:, j * LANES:(j + 1) * LANES] = h0_ref[0, j * LANES:(j + 1) * LANES, :].T

    def pad_rows(v):
        if rows == Q:
            return v
        return jnp.concatenate([v, jnp.zeros((Q - rows, v.shape[1]), v.dtype)], axis=0)

    cbuf_ref[8:8 + Q, :] = pad_rows(xbc_ref[0].astype(F32))
    conv = cb_ref[...]
    for i in range(SSD_CONV):
        conv = conv + cbuf_ref[8 - (SSD_CONV - 1) + i:8 - (SSD_CONV - 1) + i + Q, :] * cw_ref[i:i + 1, :]
    cbuf_ref[8 - (SSD_CONV - 1):8, :] = cbuf_ref[8 + Q - (SSD_CONV - 1):8 + Q, :]
    xc = conv * _sigmoid(conv)
    xs = xc[:, :SSD_WIDTH]
    b_all = xc[:, SSD_WIDTH:SSD_WIDTH + SSD_GROUPS * SSD_STATE]
    c_all = xc[:, SSD_WIDTH + SSD_GROUPS * SSD_STATE:]

    dt = _softplus(pad_rows(dt_ref[0]) + dtb_ref[...])
    dt_t = _softplus(dtT_ref[0, 0] + dtbT_ref[...])
    if rows < Q:
        rowi = lax.broadcasted_iota(jnp.int32, (Q, 1), 0)
        dt = jnp.where(rowi < rows, dt, 0.0)
        xs = jnp.where(rowi < rows, xs, 0.0)
        lane_t = lax.broadcasted_iota(jnp.int32, dt_t.shape, 1)
        dt_t = jnp.where(lane_t % Q < rows, dt_t, 0.0)
    da = dt * (-jnp.exp(alog_ref[...]))
    da_t = dt_t * (-jnp.exp(alogT_ref[...]))

    a_cum = _dot(ltri_ref[...], jnp.concatenate(_split3(da), axis=0))
    a_cum_t = _dot(jnp.concatenate(_split3(da_t), axis=1), ublk_ref[...])
    dt_exp = _dot(jnp.concatenate(_split2(dt), axis=1), e3_ref[0:2 * LANES, :])
    a_exp = _dot(jnp.concatenate(_split3(a_cum), axis=1), e3_ref[...])
    a_last = a_exp[Q - 1:Q, :]

    xdt = xs * dt_exp
    xdt_b = xdt.astype(BF16)
    xw = (xdt * jnp.exp(a_last - a_exp)).astype(BF16)
    xw_pad = jnp.concatenate([xw, jnp.zeros_like(xw)], axis=0)
    chunk_decay = jnp.exp(a_last)
    grow = jnp.exp(a_exp)

    lane = lax.broadcasted_iota(jnp.int32, (Q, LANES), 1)
    rowq = lax.broadcasted_iota(jnp.int32, (Q, LANES), 0)
    causal2 = (lane % Q) <= rowq
    lo_half = lane < SSD_HEAD_DIM

    y_parts = []
    for g in range(SSD_GROUPS):
        bg = b_all[:, g * SSD_STATE:(g + 1) * SSD_STATE]
        cg = c_all[:, g * SSD_STATE:(g + 1) * SSD_STATE].astype(BF16)
        bg_b = bg.astype(BF16)
        cb2 = _dot_nt(cg, jnp.concatenate([bg_b, bg_b], axis=0))
        gs = slice(g * GROUP_W, (g + 1) * GROUP_W)
        st_g = st_ref[:, gs]
        y_off = _dot(cg, st_g.astype(BF16)) * grow[:, gs]
        bg_t = jnp.concatenate([bg, jnp.zeros_like(bg)], axis=0).T.astype(BF16)
        st_ref[:, gs] = st_g * chunk_decay[:, gs] + _dot(bg_t, xw_pad[:, gs])
        pair_out = []
        for k in range(GROUP_W // LANES):
            i = g * (GROUP_W // LANES) + k
            ps = slice(i * LANES, (i + 1) * LANES)
            decay = jnp.exp(jnp.minimum(a_exp[:, ps] - a_cum_t[i:i + 1, :], 0.0))
            m2 = jnp.where(causal2, cb2 * decay, 0.0).astype(BF16)
            xp = xdt_b[:, ps]
            zero = jnp.zeros_like(xp)
            xbd = jnp.concatenate([jnp.where(lo_half, xp, zero), jnp.where(lo_half, zero, xp)], axis=0)
            pair_out.append(_dot(m2, xbd))
        y_diag = jnp.concatenate(pair_out, axis=1)
        y = y_diag + y_off + dskip_ref[:, gs] * xs[:, gs]
        zg = pad_rows(z_ref[0, :, gs].astype(F32))
        y = y * (zg * _sigmoid(zg))
        ms = jnp.mean(y * y, axis=-1, keepdims=True)
        y_parts.append(y * lax.rsqrt(ms + EPS) * gn_ref[:, gs])
    y_all = jnp.concatenate(y_parts, axis=1)
    y_ref[0] = y_all[:rows].astype(y_ref.dtype)

    @pl.when(c == n_chunks - 1)
    def _():
        for j in range(n_blk):
            hfin_ref[0, j * LANES:(j + 1) * LANES, :] = st_ref[:, j * LANES:(j + 1) * LANES].T


def _ssd_constants():
    Q = SSD_CHUNK
    e = np.zeros((LANES, SSD_WIDTH), np.float32)
    for h in range(SSD_HEADS):
        e[h, h * SSD_HEAD_DIM:(h + 1) * SSD_HEAD_DIM] = 1.0
    e3 = np.concatenate([e, e, e], axis=0)
    t = np.arange(Q)
    ltri = (t[:, None] >= t[None, :]).astype(np.float32)
    ltri3 = np.concatenate([ltri, ltri, ltri], axis=1)
    ublk = np.zeros((LANES, LANES), np.float32)
    ublk[:Q, :Q] = ltri.T
    ublk[Q:, Q:] = ltri.T
    ublk3 = np.concatenate([ublk, ublk, ublk], axis=0)
    return jnp.asarray(e3, BF16), jnp.asarray(ltri3, BF16), jnp.asarray(ublk3, BF16)


def ssd_mixer(big, xbc_blk, z_blk, dt_raw, conv_prev, h0, p):
    b, s, _ = big.shape
    Q = SSD_CHUNK
    rows = min(s, Q)
    n_chunks = -(-s // Q)
    dtp = dt_raw[:, :, :SSD_HEADS]
    if s < Q:
        dtp = jnp.pad(dtp, ((0, 0), (0, Q - s), (0, 0)))
    dt_t = dtp.reshape(b, n_chunks, Q, SSD_HEADS).transpose(0, 1, 3, 2).reshape(b, n_chunks, SSD_HEADS // 2, 2 * Q)
    e3, ltri3, ublk3 = _ssd_constants()
    const = lambda shape: pl.BlockSpec(shape, lambda bi, ci: (0,) * len(shape))
    in_specs = [
        pl.BlockSpec((1, rows, SSD_CONV_CH), lambda bi, ci: (bi, ci, xbc_blk)),
        pl.BlockSpec((1, rows, SSD_WIDTH), lambda bi, ci: (bi, ci, z_blk)),
        pl.BlockSpec((1, rows, LANES), lambda bi, ci: (bi, ci, 0)),
        pl.BlockSpec((1, 1, SSD_HEADS // 2, 2 * Q), lambda bi, ci: (bi, ci, 0, 0)),
        pl.BlockSpec((1, SSD_CONV - 1, SSD_CONV_CH), lambda bi, ci: (bi, 0, 0)),
        pl.BlockSpec((1, SSD_WIDTH, SSD_STATE), lambda bi, ci: (bi, 0, 0)),
        const((SSD_CONV, SSD_CONV_CH)), const((1, SSD_CONV_CH)),
        const((1, LANES)), const((SSD_HEADS // 2, 2 * Q)),
        const((1, LANES)), const((SSD_HEADS // 2, 2 * Q)),
        const((1, SSD_WIDTH)), const((1, SSD_WIDTH)),
        const(e3.shape), const(ltri3.shape), const(ublk3.shape),
    ]
    y, h_fin = pl.pallas_call(
        functools.partial(_ssd_kernel, rows=rows, n_chunks=n_chunks),
        grid=(b, n_chunks),
        in_specs=in_specs,
        out_specs=[pl.BlockSpec((1, rows, SSD_WIDTH), lambda bi, ci: (bi, ci, 0)),
                   pl.BlockSpec((1, SSD_WIDTH, SSD_STATE), lambda bi, ci: (bi, 0, 0))],
        out_shape=[jax.ShapeDtypeStruct((b, s, SSD_WIDTH), BF16),
                   jax.ShapeDtypeStruct((b, SSD_WIDTH, SSD_STATE), F32)],
        scratch_shapes=[pltpu.VMEM((8 + Q, SSD_CONV_CH), F32), pltpu.VMEM((SSD_STATE, SSD_WIDTH), F32)],
        compiler_params=_cparams(("parallel", "arbitrary")),
        name="ssd_mixer",
    )(big, big, dt_raw, dt_t, conv_prev, h0, p['conv_w_ssd'], p['conv_b_ssd'], p['dt_bias'], p['dt_bias_t'],
      p['a_log'], p['a_log_t'], p['d_skip'], p['g_ssd_norm'], e3, ltri3, ublk3)
    return y, h_fin


def _merge_kernel(att_ref, ssd_ref, ga_ref, gs_ref, h_ref, wa_ref, ws_ref, wo_ref, o_ref, wa_b, ws_b, wo_b):
    @pl.when(pl.program_id(0) == 0)
    def _():
        wa_b[...] = wa_ref[...].astype(BF16)
        ws_b[...] = ws_ref[...].astype(BF16)
        wo_b[...] = wo_ref[...].astype(BF16)

    a = _dot(att_ref[...], wa_b[...])
    s = _dot(ssd_ref[...], ws_b[...])
    merged = _sigmoid(ga_ref[...].astype(F32)) * a + _sigmoid(gs_ref[...].astype(F32)) * s
    o_ref[...] = h_ref[...] + _dot(merged.astype(BF16), wo_b[...])


def _resident(a):
    return pl.BlockSpec(a.shape, lambda i: (0,) * a.ndim, pipeline_mode=pl.Buffered(1))


def merge(att, ssd, big, h, p):
    m = h.shape[0]
    tm = _pick(m, (512, 256, 128))
    row = lambda w: pl.BlockSpec((tm, w), lambda i: (i, 0))
    gate_blk = BIG_GATE // D_MODEL
    ws = [p['w_br_att'], p['w_br_ssd'], p['w_out']]
    return pl.pallas_call(
        _merge_kernel,
        grid=(m // tm,),
        in_specs=[row(SB_WIDTH), row(SSD_WIDTH),
                  pl.BlockSpec((tm, D_MODEL), lambda i: (i, gate_blk)),
                  pl.BlockSpec((tm, D_MODEL), lambda i: (i, gate_blk + 1)),
                  row(D_MODEL)] + [_resident(w) for w in ws],
        out_specs=row(D_MODEL),
        out_shape=jax.ShapeDtypeStruct((m, D_MODEL), F32),
        scratch_shapes=[pltpu.VMEM(w.shape, BF16) for w in ws],
        compiler_params=_cparams(("arbitrary",)),
        name="merge_out_proj",
    )(att, ssd, big, big, h, *ws)


def _ffn_act_kernel(ug_ref, uv_ref, pg_ref, pv_ref, wg_ref, wv_ref, bg_ref, bv_ref, o_ref, gbuf_ref, vbuf_ref,
                    *, ts):
    t = pl.program_id(1)
    k1 = FFN_CONV - 1

    def conv(u_ref, p_ref, w_ref, b_ref, buf_ref):
        @pl.when(t == 0)
        def _():
            buf_ref[8 - k1:8, :] = p_ref[0]

        buf_ref[8:8 + ts, :] = u_ref[0].astype(F32)
        y = b_ref[...]
        for i in range(FFN_CONV):
            y = y + buf_ref[8 - k1 + i:8 - k1 + i + ts, :] * w_ref[i:i + 1, :]
        buf_ref[8 - k1:8, :] = buf_ref[8 + ts - k1:8 + ts, :]
        return y

    gate = conv(ug_ref, pg_ref, wg_ref, bg_ref, gbuf_ref)
    val = conv(uv_ref, pv_ref, wv_ref, bv_ref, vbuf_ref)
    o_ref[0] = (gate * _sigmoid(gate) * val).astype(o_ref.dtype)


def ffn_act(up, prev, p):
    b, s, _ = up.shape
    ts = _pick(s, (512, 256, 128))
    half = lambda blk: pl.BlockSpec((1, ts, D_FF), lambda bi, ti: (bi, ti, blk))
    prevs = lambda blk: pl.BlockSpec((1, FFN_CONV - 1, D_FF), lambda bi, ti: (bi, 0, blk))
    wspec = lambda blk: pl.BlockSpec((FFN_CONV, D_FF), lambda bi, ti: (0, blk))
    bspec = lambda blk: pl.BlockSpec((1, D_FF), lambda bi, ti: (0, blk))
    return pl.pallas_call(
        functools.partial(_ffn_act_kernel, ts=ts),
        grid=(b, s // ts),
        in_specs=[half(0), half(1), prevs(0), prevs(1), wspec(0), wspec(1), bspec(0), bspec(1)],
        out_specs=pl.BlockSpec((1, ts, D_FF), lambda bi, ti: (bi, ti, 0)),
        out_shape=jax.ShapeDtypeStruct((b, s, D_FF), BF16),
        scratch_shapes=[pltpu.VMEM((8 + ts, D_FF), F32)] * 2,
        compiler_params=_cparams(("parallel", "arbitrary")),
        name="ffn_conv_act",
    )(up, up, prev, prev, p['conv_w_ffn'], p['conv_w_ffn'], p['conv_b_ffn'], p['conv_b_ffn'])


def _down_kernel(a_ref, w_ref, h_ref, g_ref, o_ref, w_b):
    @pl.when(pl.program_id(0) == 0)
    def _():
        w_b[...] = w_ref[...].astype(BF16)

    h = h_ref[...] + _dot(a_ref[...], w_b[...])
    ms = jnp.mean(h * h, axis=-1, keepdims=True)
    o_ref[...] = h * lax.rsqrt(ms + EPS) * g_ref[...]


def down_norm(act, h, p):
    m = h.shape[0]
    tm = _pick(m, (512, 256, 128))
    return pl.pallas_call(
        _down_kernel,
        grid=(m // tm,),
        in_specs=[pl.BlockSpec((tm, D_FF), lambda i: (i, 0)),
                  _resident(p['w_down']),
                  pl.BlockSpec((tm, D_MODEL), lambda i: (i, 0)),
                  pl.BlockSpec((1, D_MODEL), lambda i: (0, 0))],
        out_specs=pl.BlockSpec((tm, D_MODEL), lambda i: (i, 0)),
        out_shape=jax.ShapeDtypeStruct((m, D_MODEL), F32),
        scratch_shapes=[pltpu.VMEM(p['w_down'].shape, BF16)],
        compiler_params=_cparams(("arbitrary",)),
        name="down_proj_norm",
    )(act, p['w_down'], h, p['g_final'])


def mix_and_ffn(big3, att, dt_raw, hf, ssd_h0, ssd_conv_prev, ffn_conv_prev, p):
    b, s, _ = big3.shape
    m = b * s
    ssd, ssd_state = ssd_mixer(big3, BIG_XBC // SSD_CONV_CH, BIG_Z // SSD_WIDTH, dt_raw, ssd_conv_prev, ssd_h0, p)
    h1 = merge(att.reshape(m, SB_WIDTH), ssd.reshape(m, SSD_WIDTH), big3.reshape(m, BIG_W), hf, p)
    up = norm_mm(h1, p['g_ffn'], p['w_up'], BF16).reshape(b, s, 2 * D_FF)
    act = ffn_act(up, ffn_conv_prev, p)
    y = down_norm(act.reshape(m, D_FF), h1, p).reshape(b, s, D_MODEL)
    xbc_rows = jnp.concatenate([ssd_conv_prev, big3[:, :, BIG_XBC:BIG_XBC + SSD_CONV_CH][:, -(SSD_CONV - 1):].astype(F32)],
                               axis=1)[:, -(SSD_CONV - 1):]
    up_rows = jnp.concatenate([ffn_conv_prev, up[:, -(FFN_CONV - 1):].astype(F32)], axis=1)[:, -(FFN_CONV - 1):]
    return y, ssd_state, xbc_rows, up_rows


def _prep_params(g_mix, w_in, conv_w_ssd, conv_b_ssd, dt_bias, a_log, d_skip, g_ssd_norm, w_br_att, w_br_ssd,
                 w_out, g_ffn, w_up, conv_w_ffn, conv_b_ffn, w_down, g_final):
    w_t = w_in.T
    dt0 = 3 * SB_WIDTH + SSD_WIDTH + SSD_CONV_CH
    w_tail_t = jnp.concatenate([w_t[dt0 + SSD_HEADS:], w_t[dt0:dt0 + SSD_HEADS],
                                jnp.zeros((LANES - SSD_HEADS, D_MODEL), F32)], axis=0)
    Q = SSD_CHUNK
    lanes_t = lambda v: jnp.repeat(v.reshape(SSD_HEADS // 2, 2, 1), Q, axis=2).reshape(SSD_HEADS // 2, 2 * Q)
    pad_l = lambda v: jnp.pad(v, (0, LANES - SSD_HEADS)).reshape(1, LANES)
    return {
        'g_mix': g_mix.reshape(1, -1), 'g_ffn': g_ffn.reshape(1, -1), 'g_final': g_final.reshape(1, -1),
        'w_t': w_t, 'w_tail_t': w_tail_t,
        'conv_w_ssd': conv_w_ssd, 'conv_b_ssd': conv_b_ssd.reshape(1, -1),
        'dt_bias': pad_l(dt_bias), 'dt_bias_t': lanes_t(dt_bias),
        'a_log': pad_l(a_log), 'a_log_t': lanes_t(a_log),
        'd_skip': jnp.repeat(d_skip, SSD_HEAD_DIM).reshape(1, -1), 'g_ssd_norm': g_ssd_norm.reshape(1, -1),
        'w_br_att': w_br_att, 'w_br_ssd': w_br_ssd, 'w_out': w_out,
        'w_up': w_up, 'conv_w_ffn': conv_w_ffn, 'conv_b_ffn': conv_b_ffn.reshape(1, -1),
        'w_down': w_down,
    }


def kernel(x_prompt, x_sample, cache_k, cache_v, state_ssm, state_ssm_conv, state_ffn_conv, meta_tokens, g_mix, w_in, conv_w_ssd, conv_b_ssd, dt_bias, a_log, d_skip, g_ssd_norm, w_br_att, w_br_ssd, w_out, g_ffn, w_up, conv_w_ffn, conv_b_ffn, w_down, g_final):
    bp, seq, _ = x_prompt.shape
    bs, dec = x_sample.shape[:2]
    assert dec == N_META and seq % KVT_TS == 0
    p = _prep_params(g_mix[0], w_in[0], conv_w_ssd[0], conv_b_ssd[0], dt_bias[0], a_log[0], d_skip[0],
                     g_ssd_norm[0], w_br_att[0], w_br_ssd[0], w_out[0], g_ffn[0], w_up[0], conv_w_ffn[0],
                     conv_b_ffn[0], w_down[0], g_final)
    proj = functools.partial(in_proj, g=p['g_mix'], w_t=p['w_t'], w_tail_t=p['w_tail_t'])
    state_shape = (SSD_WIDTH, SSD_STATE)
    rows = lambda a, b: a.reshape(b, -1, a.shape[-1])

    k_m, v_m, big_m, dt_m = proj(meta_tokens, with_kv=True)
    big_m, k_m, v_m = big_m[None], k_m[None], v_m[None]
    att_m = attention_step(big_m, 0, k_m, v_m, None, None, 0, SB_WIDTH)
    _, ssm_m, conv_m, ffn_m = mix_and_ffn(
        big_m, att_m, dt_m[None], meta_tokens, jnp.zeros((1,) + state_shape, F32),
        jnp.zeros((1, SSD_CONV - 1, SSD_CONV_CH), F32), jnp.zeros((1, FFN_CONV - 1, 2 * D_FF), F32), p)

    xf = x_prompt.reshape(bp * seq, D_MODEL)
    big_x, dt_x = proj(xf, with_kv=False)
    big_x = rows(big_x, bp)
    k_t, v_t, k_end, v_end = kv_transposed(x_prompt, meta_tokens, p['g_mix'], p['w_t'])
    att_x = attention_prompt(big_x, big_m, k_t, v_t)
    att_end = attention_step(big_x, seq // N_META - 1, k_end, v_end, k_t, v_t, seq // KEY_TILE, ATTN_PROMPT_WIDTH)
    att_x = lax.dynamic_update_slice(att_x, att_end, (0, seq - N_META, 0))
    rep = lambda a: jnp.broadcast_to(a, (bp,) + a.shape[1:])
    y_prompt, ssm_p, conv_p, ffn_p = mix_and_ffn(
        big_x, att_x, rows(dt_x, bp), xf, rep(ssm_m), rep(conv_m), rep(ffn_m), p)

    n_rows = cache_k.shape[2]
    n_tiles = (n_rows - N_META) // KEY_TILE
    assert n_tiles * KEY_TILE + N_META == n_rows
    k_s, v_s, big_s, dt_s = proj(x_sample.reshape(bs * dec, D_MODEL), with_kv=True)
    big_s, k_s, v_s = rows(big_s, bs), rows(k_s, bs), rows(v_s, bs)
    cache_t = lambda c: c[0].transpose(0, 2, 3, 1).reshape(bs, SB_WIDTH, n_rows)
    nearest = lambda c, new: jnp.concatenate([c[0][:, n_tiles * KEY_TILE:].reshape(bs, N_META, SB_WIDTH), new], axis=1)
    att_s = attention_step(big_s, 0, nearest(cache_k, k_s), nearest(cache_v, v_s), cache_t(cache_k), cache_t(cache_v),
                           n_tiles, SB_WIDTH)
    y_sample, ssm_s, conv_s, ffn_s = mix_and_ffn(
        big_s, att_s, rows(dt_s, bs), x_sample.reshape(bs * dec, D_MODEL),
        state_ssm[0].reshape((bs,) + state_shape), state_ssm_conv[0], state_ffn_conv[0], p)

    heads = lambda a: a.reshape(a.shape[0], a.shape[1], SB_HEADS, SB_HEAD_DIM)[None]
    heads_t = lambda a: a.reshape(a.shape[0], SB_HEADS, SB_HEAD_DIM, a.shape[2]).transpose(0, 3, 1, 2)[None]
    state5 = lambda a: a.reshape(a.shape[0], SSD_HEADS, SSD_HEAD_DIM, SSD_STATE)[None]
    return (y_prompt, y_sample, heads_t(k_t), heads_t(v_t), state5(ssm_p), conv_p[None], ffn_p[None],
            heads(k_s), heads(v_s), state5(ssm_s), conv_s[None], ffn_s[None])
```

```python
import functools

import numpy as np
import jax
import jax.numpy as jnp
from jax import lax
from jax.experimental import pallas as pl
from jax.experimental.pallas import tpu as pltpu

F32 = jnp.float32
BF16 = jnp.bfloat16

D_MODEL = 1024
N_META = 16
SB_HEADS = 16
SB_HEAD_DIM = 64
SB_WIDTH = SB_HEADS * SB_HEAD_DIM
SSD_WIDTH = 2 * D_MODEL
SSD_HEAD_DIM = 64
SSD_HEADS = SSD_WIDTH // SSD_HEAD_DIM
SSD_GROUPS = 4
SSD_STATE = 128
SSD_CONV = 4
SSD_CONV_CH = SSD_WIDTH + 2 * SSD_GROUPS * SSD_STATE
D_FF = 2816
FFN_CONV = 3
EPS = 1e-6

LANES = 128
SSD_CHUNK = 64
KEY_TILE = 128
GROUP_W = SSD_WIDTH // SSD_GROUPS
DEAD_LOG = -104.0
LOG2E = 1.4426950408889634
ATTN_STATIC_TILES = 2
VMEM_LIMIT = 56 * 1024 * 1024


def _cparams(sem):
    return pltpu.CompilerParams(dimension_semantics=sem, vmem_limit_bytes=VMEM_LIMIT)


def _pick(n, cands):
    for c in cands:
        if n % c == 0:
            return c
    return n


def _split2(x):
    hi = x.astype(BF16)
    lo = (x - hi.astype(F32)).astype(BF16)
    return hi, lo


def _split3(x):
    hi = x.astype(BF16)
    r = x - hi.astype(F32)
    mid = r.astype(BF16)
    lo = (r - mid.astype(F32)).astype(BF16)
    return hi, mid, lo


def _dot(a, b):
    return jnp.dot(a, b, preferred_element_type=F32)


def _dot_nt(a, b):
    return lax.dot_general(a, b, (((1,), (1,)), ((), ())), preferred_element_type=F32)


def _softplus(x):
    return jnp.maximum(x, 0.0) + jnp.log(1.0 + jnp.exp(-jnp.abs(x)))


def _sigmoid(x):
    return 1.0 / (1.0 + jnp.exp(-x))


def _rms_norm_bf16(x, g):
    ms = jnp.mean(x * x, axis=-1, keepdims=True)
    return (x * lax.rsqrt(ms + EPS) * g).astype(BF16)


def _norm_mm_kernel(x_ref, g_ref, w_ref, o_ref, u_ref):
    @pl.when(pl.program_id(1) == 0)
    def _():
        u_ref[...] = _rms_norm_bf16(x_ref[...], g_ref[...])

    o_ref[...] = _dot(u_ref[...], w_ref[...].astype(BF16)).astype(o_ref.dtype)


PROJ_TN = 512


def norm_mm(x, g, w, out_dtype):
    m, d = x.shape
    n = w.shape[1]
    tm = _pick(m, (2048, 1024, 512, 256, 128))
    tn = PROJ_TN
    return pl.pallas_call(
        _norm_mm_kernel,
        grid=(m // tm, n // tn),
        in_specs=[pl.BlockSpec((tm, d), lambda i, j: (i, 0)),
                  pl.BlockSpec((1, d), lambda i, j: (0, 0)),
                  pl.BlockSpec((d, tn), lambda i, j: (0, j))],
        out_specs=pl.BlockSpec((tm, tn), lambda i, j: (i, j)),
        out_shape=jax.ShapeDtypeStruct((m, n), out_dtype),
        scratch_shapes=[pltpu.VMEM((tm, d), BF16)],
        compiler_params=_cparams(("parallel", "arbitrary")),
        name="norm_up_proj",
    )(x, g, w)


_NQ = SB_WIDTH // PROJ_TN
_J_K, _J_V, _J_Z = _NQ, 2 * _NQ, 3 * _NQ
_J_X = _J_Z + SSD_WIDTH // PROJ_TN
_J_G = _J_X + SSD_CONV_CH // PROJ_TN
_J_END = _J_G + 2 * D_MODEL // PROJ_TN
BIG_Z, BIG_Q, BIG_XBC, BIG_GATE = 0, SSD_WIDTH, SSD_WIDTH + SB_WIDTH, SSD_WIDTH + SB_WIDTH + SSD_CONV_CH
BIG_W = BIG_GATE + 2 * D_MODEL


def _big_block(j):
    return jnp.where(j < _J_K, BIG_Q // PROJ_TN + j,
                     jnp.where(j < _J_Z, BIG_Q // PROJ_TN + _NQ - 1,
                               jnp.where(j < _J_X, j - _J_Z + BIG_Z // PROJ_TN,
                                         jnp.where(j < _J_G, j - _J_X + BIG_XBC // PROJ_TN,
                                                   j - _J_G + BIG_GATE // PROJ_TN))))


def _sweep_step(j, with_kv):
    return j if with_kv else jnp.where(j >= _J_K, j + (_J_Z - _J_K), j)


def _in_proj_kernel(*refs, with_kv):
    if with_kv:
        x_ref, g_ref, win_ref, wgate_ref, wdt_ref, k_ref, v_ref, big_ref, dt_ref, u_ref = refs
    else:
        x_ref, g_ref, win_ref, wgate_ref, wdt_ref, big_ref, dt_ref, u_ref = refs
    j = _sweep_step(pl.program_id(1), with_kv)

    def proj(w_ref):
        return _dot_nt(u_ref[...], w_ref[...].astype(BF16))

    @pl.when(pl.program_id(1) == 0)
    def _():
        u_ref[...] = _rms_norm_bf16(x_ref[...], g_ref[...])
        dt_ref[...] = proj(wdt_ref)

    if with_kv:
        @pl.when(jnp.logical_and(j >= _J_K, j < _J_V))
        def _():
            k_ref[...] = proj(win_ref)

        @pl.when(jnp.logical_and(j >= _J_V, j < _J_Z))
        def _():
            v_ref[...] = proj(win_ref)

    @pl.when(jnp.logical_or(j < _J_K, jnp.logical_and(j >= _J_Z, j < _J_G)))
    def _():
        big_ref[...] = proj(win_ref).astype(BF16)

    @pl.when(j >= _J_G)
    def _():
        big_ref[...] = proj(wgate_ref).astype(BF16)


def in_proj(x, g, w_t, w_tail_t, with_kv):
    m, d = x.shape
    tm = _pick(m, (2048, 1024, 512, 256, 128))
    tn = PROJ_TN
    step = functools.partial(_sweep_step, with_kv=with_kv)
    clip = lambda j, lo, n: jnp.clip(step(j) - lo, 0, n - 1)
    kv_specs = [pl.BlockSpec((tm, tn), lambda i, j: (i, clip(j, _J_K, _NQ))),
                pl.BlockSpec((tm, tn), lambda i, j: (i, clip(j, _J_V, _NQ)))]
    kv_shapes = [jax.ShapeDtypeStruct((m, SB_WIDTH), F32)] * 2
    return pl.pallas_call(
        functools.partial(_in_proj_kernel, with_kv=with_kv),
        grid=(m // tm, _J_END if with_kv else _J_END - (_J_Z - _J_K)),
        in_specs=[pl.BlockSpec((tm, d), lambda i, j: (i, 0), pipeline_mode=pl.Buffered(1)),
                  pl.BlockSpec((1, d), lambda i, j: (0, 0)),
                  pl.BlockSpec((tn, d), lambda i, j: (jnp.minimum(step(j), _J_G - 1), 0)),
                  pl.BlockSpec((tn, d), lambda i, j: (clip(j, _J_G, _J_END - _J_G), 0)),
                  pl.BlockSpec((LANES, d), lambda i, j: (2 * D_MODEL // LANES, 0))],
        out_specs=(kv_specs if with_kv else []) + [
            pl.BlockSpec((tm, tn), lambda i, j: (i, _big_block(step(j)))),
            pl.BlockSpec((tm, LANES), lambda i, j: (i, 0))],
        out_shape=(kv_shapes if with_kv else []) + [
            jax.ShapeDtypeStruct((m, BIG_W), BF16), jax.ShapeDtypeStruct((m, LANES), F32)],
        scratch_shapes=[pltpu.VMEM((tm, d), BF16)],
        compiler_params=_cparams(("parallel", "arbitrary")),
        name="in_proj",
    )(x, g, w_t, w_tail_t, w_tail_t)


KVT_TS = 1024


def _kv_t_kernel(x16_ref, xt_ref, meta_ref, g_ref, wk_ref, wv_ref, kt_ref, vt_ref, kl_ref, vl_ref, wk_b, wv_b):
    j = pl.program_id(1)

    @pl.when(jnp.logical_and(pl.program_id(0) == 0, j == 0))
    def _():
        wk_b[...] = wk_ref[...].astype(BF16)
        wv_b[...] = wv_ref[...].astype(BF16)

    first = jnp.where(j == 0, meta_ref[...], x16_ref[0])
    x = jnp.concatenate([first, xt_ref[0, :KVT_TS - N_META, :]], axis=0)
    u = _rms_norm_bf16(x, g_ref[...])
    kt_ref[0] = _dot_nt(wk_b[...], u)
    vt_ref[0] = _dot_nt(wv_b[...], u)

    @pl.when(j == pl.num_programs(1) - 1)
    def _():
        kl_ref[0] = _dot_nt(u[0:N_META, :], wk_b[...])
        vl_ref[0] = _dot_nt(u[0:N_META, :], wv_b[...])


def kv_transposed(x, meta, g, w_t):
    b, s, d = x.shape
    n_pos = N_META + s
    n_tiles = -(-n_pos // KVT_TS)
    assert n_pos - (n_tiles - 1) * KVT_TS == N_META
    per_tile = KVT_TS // N_META
    out_spec = pl.BlockSpec((1, SB_WIDTH, KVT_TS), lambda bi, j: (bi, 0, j))
    last_spec = pl.BlockSpec((1, N_META, SB_WIDTH), lambda bi, j: (bi, 0, 0))
    w_spec = lambda blk: pl.BlockSpec((SB_WIDTH, d), lambda bi, j: (blk, 0), pipeline_mode=pl.Buffered(1))
    return pl.pallas_call(
        _kv_t_kernel,
        grid=(b, n_tiles),
        in_specs=[pl.BlockSpec((1, N_META, d), lambda bi, j: (bi, jnp.maximum(j * per_tile - 1, 0), 0)),
                  pl.BlockSpec((1, KVT_TS, d), lambda bi, j: (bi, jnp.minimum(j, s // KVT_TS - 1), 0)),
                  pl.BlockSpec((N_META, d), lambda bi, j: (0, 0)),
                  pl.BlockSpec((1, d), lambda bi, j: (0, 0)),
                  w_spec(1), w_spec(2)],
        out_specs=[out_spec, out_spec, last_spec, last_spec],
        out_shape=[jax.ShapeDtypeStruct((b, SB_WIDTH, n_pos), F32)] * 2
        + [jax.ShapeDtypeStruct((b, N_META, SB_WIDTH), F32)] * 2,
        scratch_shapes=[pltpu.VMEM((SB_WIDTH, d), BF16)] * 2,
        compiler_params=_cparams(("arbitrary", "arbitrary")),
        name="kv_transposed",
    )(x, x, meta, g, w_t, w_t)


def _stick_breaking_tiles(qs, u2, tiles, carries, accs):
    n_heads = len(qs)
    units = [(t, h) for t in range(len(tiles)) for h in range(n_heads)]
    ps = lambda h: slice((h // 2) * LANES, (h // 2 + 1) * LANES)

    def scores(t, h):
        kind, keys = tiles[t][0], tiles[t][1]
        return _dot_nt(qs[h], keys[:, ps(h)]) if kind == 'rows' else _dot(qs[h], keys[ps(h), :])

    z = {u: scores(*u) for u in units}
    log_beta, cat = {}, {}
    for u in units:
        mask = tiles[u[0]][3]
        soft = jnp.log(1.0 + jnp.exp(-jnp.abs(z[u])))
        log_beta[u] = jnp.minimum(z[u], 0.0) - soft
        log_keep = log_beta[u] - z[u]
        if mask is not None:
            log_keep = jnp.where(mask, log_keep, 0.0)
        cat[u] = jnp.concatenate(_split2(log_keep), axis=1)
    r = {u: _dot(cat[u], u2) for u in units}
    carries, accs = list(carries), list(accs)
    w = {}
    for t, h in units:
        mask = tiles[t][3]
        wt = jnp.exp(log_beta[(t, h)] + r[(t, h)][:, :LANES] + carries[h])
        if mask is not None:
            wt = jnp.where(mask, wt, 0.0)
        w[(t, h)] = wt.astype(BF16)
        carries[h] = carries[h] + r[(t, h)][:, LANES:]
    for t, h in units:
        kind, vals = tiles[t][0], tiles[t][2]
        pv = _dot(w[(t, h)], vals[:, ps(h)]) if kind == 'rows' else _dot_nt(w[(t, h)], vals[ps(h), :])
        accs[h] = accs[h] + pv
    return carries, accs


def _head_queries(q, lo_half):
    qs = []
    for h in range(q.shape[1] // SB_HEAD_DIM):
        qp = q[:, (h // 2) * LANES:(h // 2 + 1) * LANES]
        keep = lo_half if h % 2 == 0 else jnp.logical_not(lo_half)
        qs.append(jnp.where(keep, qp, jnp.zeros_like(qp)) * jnp.asarray(SB_HEAD_DIM ** -0.5, BF16))
    return qs


def _any_alive(carries):
    m = carries[0]
    for c in carries[1:]:
        m = jnp.maximum(m, c)
    return (jnp.max(m) > DEAD_LOG).astype(jnp.int32)


def _attend(qs, u2, first_tiles, kt_ref, vt_ref, j0, lane):
    tq = qs[0].shape[0]

    def cache_tile(j):
        if isinstance(j, int):
            off, mask = max(j, 0) * KEY_TILE, (None if j >= 0 else lane < 0)
        else:
            off, mask = pl.multiple_of(jnp.maximum(j, 0) * KEY_TILE, KEY_TILE), j >= 0
            mask = jnp.logical_and(mask, lane >= 0)
        kt = kt_ref[0, :, pl.ds(off, KEY_TILE)].astype(BF16)
        vt = vt_ref[0, :, pl.ds(off, KEY_TILE)].astype(BF16)
        return 'cols', kt, vt, mask

    tiles = list(first_tiles)
    if kt_ref is not None:
        for _ in range(ATTN_STATIC_TILES):
            tiles.append(cache_tile(j0))
            j0 = j0 - 1
    zeros = jnp.zeros((tq, LANES), F32)
    carries, accs = _stick_breaking_tiles(qs, u2, tiles, [zeros] * len(qs), [zeros] * len(qs))
    if kt_ref is not None:
        def cond(s):
            return jnp.logical_and(s[0] >= 0, s[1] > 0)

        def body(s):
            j, _, carries, accs = s
            carries, accs = _stick_breaking_tiles(qs, u2, [cache_tile(j)], carries, accs)
            return j - 1, _any_alive(carries), tuple(carries), tuple(accs)

        state = (jnp.asarray(j0, jnp.int32), _any_alive(carries), tuple(carries), tuple(accs))
        _, _, carries, accs = lax.while_loop(cond, body, state)
    return accs


def _pair_outputs(accs, lo_half, dtype):
    return [jnp.where(lo_half, accs[2 * p], accs[2 * p + 1]).astype(dtype) for p in range(len(accs) // 2)]


def _attn_prompt_kernel(q16_ref, qt_ref, qm_ref, u2_ref, kt_ref, vt_ref, o_ref):
    m = pl.program_id(2)
    tq = KEY_TILE
    lane = lax.broadcasted_iota(jnp.int32, (tq, LANES), 1)
    row = lax.broadcasted_iota(jnp.int32, (tq, LANES), 0)
    lo_half = lane < SB_HEAD_DIM
    first = jnp.where(m == 0, qm_ref[0], q16_ref[0])
    q = jnp.concatenate([first, qt_ref[0, :tq - N_META, :]], axis=0)
    qs = _head_queries(q, lo_half)
    off = pl.multiple_of(m * KEY_TILE, KEY_TILE)
    diag = ('cols', kt_ref[0, :, pl.ds(off, KEY_TILE)].astype(BF16),
            vt_ref[0, :, pl.ds(off, KEY_TILE)].astype(BF16), lane < row)
    accs = _attend(qs, u2_ref[...], [diag], kt_ref, vt_ref, m - 1, lane)
    out = jnp.concatenate(_pair_outputs(accs, lo_half, o_ref.dtype), axis=1)

    @pl.when(m == 0)
    def _():
        o_ref[0, 0:tq - N_META, :] = out[N_META:, :]
        o_ref[0, o_ref.shape[1] - N_META:, :] = jnp.zeros((N_META, o_ref.shape[2]), o_ref.dtype)

    @pl.when(m > 0)
    def _():
        o_ref[0, pl.ds(pl.multiple_of(m * KEY_TILE - N_META, N_META), tq), :] = out


def _attn_step_kernel(*refs, n_free, n_cache_tiles):
    if n_cache_tiles:
        q_ref, kd_ref, vd_ref, u2_ref, kt_ref, vt_ref, o_ref, kd_scr, vd_scr = refs
    else:
        q_ref, kd_ref, vd_ref, u2_ref, o_ref, kd_scr, vd_scr = refs
        kt_ref = vt_ref = None
    tq = q_ref.shape[1]
    n_rows = kd_ref.shape[1]
    lane = lax.broadcasted_iota(jnp.int32, (tq, LANES), 1)
    row = lax.broadcasted_iota(jnp.int32, (tq, LANES), 0)
    lo_half = lane < SB_HEAD_DIM
    kd_scr[...] = jnp.zeros_like(kd_scr)
    vd_scr[...] = jnp.zeros_like(vd_scr)
    kd_scr[0:n_rows, :] = kd_ref[0].astype(BF16)
    vd_scr[0:n_rows, :] = vd_ref[0].astype(BF16)
    diag = ('rows', kd_scr[...], vd_scr[...], lane < row + n_free)
    qs = _head_queries(q_ref[0], lo_half)
    accs = _attend(qs, u2_ref[...], [diag], kt_ref, vt_ref, n_cache_tiles - 1, lane)
    for p, out in enumerate(_pair_outputs(accs, lo_half, o_ref.dtype)):
        o_ref[0, :, p * LANES:(p + 1) * LANES] = out


def _cumsum_rhs():
    s = np.arange(KEY_TILE)
    strict = (s[:, None] > s[None, :]).astype(np.float32)
    half = np.concatenate([strict, np.ones((KEY_TILE, KEY_TILE), np.float32)], axis=1)
    return jnp.asarray(np.concatenate([half, half], axis=0), BF16)


ATTN_PROMPT_WIDTH = 4 * SB_HEAD_DIM


def attention_prompt(big, big_meta, k_t, v_t):
    b, s, _ = big.shape
    width = ATTN_PROMPT_WIDTH
    qb0 = BIG_Q // width
    per_tile = KEY_TILE // N_META
    n_pos = k_t.shape[2]
    cache = pl.BlockSpec((1, width, n_pos), lambda bi, hp, m: (bi, hp, 0))
    return pl.pallas_call(
        _attn_prompt_kernel,
        grid=(b, SB_WIDTH // width, s // KEY_TILE),
        in_specs=[pl.BlockSpec((1, N_META, width), lambda bi, hp, m: (bi, jnp.maximum(m * per_tile - 1, 0), qb0 + hp)),
                  pl.BlockSpec((1, KEY_TILE, width), lambda bi, hp, m: (bi, m, qb0 + hp)),
                  pl.BlockSpec((1, N_META, width), lambda bi, hp, m: (0, 0, qb0 + hp)),
                  pl.BlockSpec((2 * KEY_TILE, 2 * KEY_TILE), lambda bi, hp, m: (0, 0)),
                  cache, cache],
        out_specs=pl.BlockSpec((1, s, width), lambda bi, hp, m: (bi, 0, hp)),
        out_shape=jax.ShapeDtypeStruct((b, s, SB_WIDTH), BF16),
        compiler_params=_cparams(("parallel", "parallel", "arbitrary")),
        name="stick_breaking_prompt",
    )(big, big, big_meta, _cumsum_rhs(), k_t, v_t)


def attention_step(big, q_row_blk, k_rows, v_rows, k_t, v_t, n_cache_tiles, width):
    b = big.shape[0]
    n_rows = k_rows.shape[1]
    qb0 = BIG_Q // width
    rows = pl.BlockSpec((1, n_rows, width), lambda bi, hp: (bi, 0, hp))
    in_specs = [pl.BlockSpec((1, N_META, width), lambda bi, hp: (bi, q_row_blk, qb0 + hp)), rows, rows,
                pl.BlockSpec((2 * KEY_TILE, 2 * KEY_TILE), lambda bi, hp: (0, 0))]
    args = [big, k_rows, v_rows, _cumsum_rhs()]
    if n_cache_tiles:
        in_specs += [pl.BlockSpec((1, width, k_t.shape[2]), lambda bi, hp: (bi, hp, 0))] * 2
        args += [k_t, v_t]
    return pl.pallas_call(
        functools.partial(_attn_step_kernel, n_free=n_rows - N_META, n_cache_tiles=n_cache_tiles),
        grid=(b, SB_WIDTH // width),
        in_specs=in_specs,
        out_specs=pl.BlockSpec((1, N_META, width), lambda bi, hp: (bi, 0, hp)),
        out_shape=jax.ShapeDtypeStruct((b, N_META, SB_WIDTH), BF16),
        scratch_shapes=[pltpu.VMEM((KEY_TILE, width), BF16)] * 2,
        compiler_params=_cparams(("parallel", "parallel")),
        name="stick_breaking_step",
    )(*args)


def _rows8(op, x, r8):
    return op(x.reshape(x.shape[0] // 8, 8, x.shape[1]), r8[None]).reshape(x.shape)


def _ssd_kernel(xbc_ref, x16_ref, z_ref, dt_ref, dtT_ref, prev_ref, h0_ref, cw_ref, cb_ref, dtb_ref, dtbT_ref,
                alog_ref, alogT_ref, dskip_ref, gn_ref, e3_ref, ltri_ref, ublk_ref, shift_ref,
                y_ref, hfin_ref, st_ref, *, rows, n_chunks):
    Q = SSD_CHUNK
    c = pl.program_id(1)
    n_blk = SSD_WIDTH // LANES
    mul, sub = jnp.multiply, jnp.subtract

    @pl.when(c == 0)
    def _():
        for j in range(n_blk):
            st_ref[:, j * LANES:(j + 1) * LANES] = h0_ref[0, j * LANES:(j + 1) * LANES, :].T

    def pad_rows(v):
        if rows == Q:
            return v
        return jnp.concatenate([v, jnp.zeros((Q - rows, v.shape[1]), v.dtype)], axis=0)

    hi, lo = _split2(prev_ref[0])
    hi = jnp.where(c == 0, hi, x16_ref[0])
    lo = jnp.where(c == 0, lo, jnp.zeros_like(lo))
    window = jnp.concatenate([pad_rows(xbc_ref[0]), hi, lo, jnp.zeros((Q - 32, SSD_CONV_CH), BF16)], axis=0)
    shifted = _dot(shift_ref[...], window)
    conv = _rows8(mul, shifted[0:Q], cw_ref[0:8, :])
    for i in range(1, SSD_CONV):
        conv = conv + _rows8(mul, shifted[i * Q:(i + 1) * Q], cw_ref[8 * i:8 * i + 8, :])
    conv = _rows8(jnp.add, conv, cb_ref[...])
    xc = conv * _sigmoid(conv)
    xs = xc[:, :SSD_WIDTH]
    b_all = xc[:, SSD_WIDTH:SSD_WIDTH + SSD_GROUPS * SSD_STATE]
    c_all = xc[:, SSD_WIDTH + SSD_GROUPS * SSD_STATE:]

    dt = _softplus(pad_rows(dt_ref[0]) + dtb_ref[...])
    dt_t = _softplus(dtT_ref[0, 0] + dtbT_ref[...])
    if rows < Q:
        rowi = lax.broadcasted_iota(jnp.int32, (Q, 1), 0)
        dt = jnp.where(rowi < rows, dt, 0.0)
        xs = jnp.where(rowi < rows, xs, 0.0)
        lane_t = lax.broadcasted_iota(jnp.int32, dt_t.shape, 1)
        dt_t = jnp.where(lane_t % Q < rows, dt_t, 0.0)
    da = dt * (-LOG2E * jnp.exp(alog_ref[...]))
    da_t = dt_t * (-LOG2E * jnp.exp(alogT_ref[...]))

    a_cum = _dot(ltri_ref[...], jnp.concatenate(_split3(da), axis=0))
    a_cum_t = _dot(jnp.concatenate(_split3(da_t), axis=1), ublk_ref[...])
    dt_exp = _dot(jnp.concatenate(_split2(dt), axis=1), e3_ref[0:2 * LANES, :])
    a_exp = _dot(jnp.concatenate(_split3(a_cum), axis=1), e3_ref[...])
    a_last = jnp.broadcast_to(a_exp[Q - 1:Q, :], (8, SSD_WIDTH))

    xdt = xs * dt_exp
    xdt_b = xdt.astype(BF16)
    xw = (xdt * jnp.exp2(-_rows8(sub, a_exp, a_last))).astype(BF16)
    xw_pad = jnp.concatenate([xw, jnp.zeros_like(xw)], axis=0)
    chunk_decay = jnp.exp2(a_last)
    grow = jnp.exp2(a_exp)

    lane = lax.broadcasted_iota(jnp.int32, (Q, LANES), 1)
    rowq = lax.broadcasted_iota(jnp.int32, (Q, LANES), 0)
    causal2 = (lane % Q) <= rowq
    lo_half = lane < SSD_HEAD_DIM

    y_parts = []
    for g in range(SSD_GROUPS):
        bg = b_all[:, g * SSD_STATE:(g + 1) * SSD_STATE]
        cg = c_all[:, g * SSD_STATE:(g + 1) * SSD_STATE].astype(BF16)
        bg_b = bg.astype(BF16)
        cb2 = _dot_nt(cg, jnp.concatenate([bg_b, bg_b], axis=0))
        gs = slice(g * GROUP_W, (g + 1) * GROUP_W)
        st_g = st_ref[:, gs]
        y_off = _dot(cg, st_g.astype(BF16)) * grow[:, gs]
        bg_t = jnp.concatenate([bg, jnp.zeros_like(bg)], axis=0).T.astype(BF16)
        st_ref[:, gs] = _rows8(mul, st_g, chunk_decay[:, gs]) + _dot(bg_t, xw_pad[:, gs])
        pair_out = []
        for k in range(GROUP_W // LANES):
            i = g * (GROUP_W // LANES) + k
            ps = slice(i * LANES, (i + 1) * LANES)
            a_row = jnp.broadcast_to(a_cum_t[i:i + 1, :], (8, LANES))
            decay = jnp.exp2(jnp.minimum(_rows8(sub, a_exp[:, ps], a_row), 0.0))
            m2 = jnp.where(causal2, cb2 * decay, 0.0).astype(BF16)
            xp = xdt_b[:, ps]
            zero = jnp.zeros_like(xp)
            xbd = jnp.concatenate([jnp.where(lo_half, xp, zero), jnp.where(lo_half, zero, xp)], axis=0)
            pair_out.append(_dot(m2, xbd))
        y_diag = jnp.concatenate(pair_out, axis=1)
        y = y_diag + y_off + _rows8(mul, xs[:, gs], dskip_ref[:, gs])
        zg = pad_rows(z_ref[0, :, gs].astype(F32))
        y = y * (zg * _sigmoid(zg))
        ms = jnp.mean(y * y, axis=-1, keepdims=True)
        y_parts.append(_rows8(mul, y * lax.rsqrt(ms + EPS), gn_ref[:, gs]))
    y_all = jnp.concatenate(y_parts, axis=1)
    y_ref[0] = y_all[:rows].astype(y_ref.dtype)

    @pl.when(c == n_chunks - 1)
    def _():
        for j in range(n_blk):
            hfin_ref[0, j * LANES:(j + 1) * LANES, :] = st_ref[:, j * LANES:(j + 1) * LANES].T


def _ssd_constants():
    Q = SSD_CHUNK
    e = np.zeros((LANES, SSD_WIDTH), np.float32)
    for h in range(SSD_HEADS):
        e[h, h * SSD_HEAD_DIM:(h + 1) * SSD_HEAD_DIM] = 1.0
    e3 = np.concatenate([e, e, e], axis=0)
    t = np.arange(Q)
    ltri = (t[:, None] >= t[None, :]).astype(np.float32)
    ltri3 = np.concatenate([ltri, ltri, ltri], axis=1)
    ublk = np.zeros((LANES, LANES), np.float32)
    ublk[:Q, :Q] = ltri.T
    ublk[Q:, Q:] = ltri.T
    ublk3 = np.concatenate([ublk, ublk, ublk], axis=0)
    shift = np.zeros((SSD_CONV * Q, 2 * Q), np.float32)
    for i in range(SSD_CONV):
        for r in range(Q):
            src = r - (SSD_CONV - 1 - i)
            if src >= 0:
                shift[i * Q + r, src] = 1.0
            else:
                shift[i * Q + r, Q + 16 + src] = 1.0
                shift[i * Q + r, Q + 32 + src] = 1.0
    return jnp.asarray(e3, BF16), jnp.asarray(ltri3, BF16), jnp.asarray(ublk3, BF16), jnp.asarray(shift, BF16)


def ssd_mixer(big, xbc_blk, z_blk, dt_raw, conv_prev, h0, p):
    b, s, _ = big.shape
    Q = SSD_CHUNK
    rows = min(s, Q)
    n_chunks = -(-s // Q)
    dtp = dt_raw[:, :, :SSD_HEADS]
    if s < Q:
        dtp = jnp.pad(dtp, ((0, 0), (0, Q - s), (0, 0)))
    dt_t = dtp.reshape(b, n_chunks, Q, SSD_HEADS).transpose(0, 1, 3, 2).reshape(b, n_chunks, SSD_HEADS // 2, 2 * Q)
    e3, ltri3, ublk3, shift = _ssd_constants()
    prev16 = jnp.pad(conv_prev, ((0, 0), (16 - (SSD_CONV - 1), 0), (0, 0)))
    rep8 = lambda v: jnp.repeat(v, 8, axis=0)
    const = lambda shape: pl.BlockSpec(shape, lambda bi, ci: (0,) * len(shape))
    in_specs = [
        pl.BlockSpec((1, rows, SSD_CONV_CH), lambda bi, ci: (bi, ci, xbc_blk)),
        pl.BlockSpec((1, 16, SSD_CONV_CH), lambda bi, ci: (bi, jnp.maximum(ci * (Q // 16) - 1, 0), xbc_blk)),
        pl.BlockSpec((1, rows, SSD_WIDTH), lambda bi, ci: (bi, ci, z_blk)),
        pl.BlockSpec((1, rows, LANES), lambda bi, ci: (bi, ci, 0)),
        pl.BlockSpec((1, 1, SSD_HEADS // 2, 2 * Q), lambda bi, ci: (bi, ci, 0, 0)),
        pl.BlockSpec((1, 16, SSD_CONV_CH), lambda bi, ci: (bi, 0, 0)),
        pl.BlockSpec((1, SSD_WIDTH, SSD_STATE), lambda bi, ci: (bi, 0, 0)),
        const((8 * SSD_CONV, SSD_CONV_CH)), const((8, SSD_CONV_CH)),
        const((1, LANES)), const((SSD_HEADS // 2, 2 * Q)),
        const((1, LANES)), const((SSD_HEADS // 2, 2 * Q)),
        const((8, SSD_WIDTH)), const((8, SSD_WIDTH)),
        const(e3.shape), const(ltri3.shape), const(ublk3.shape), const(shift.shape),
    ]
    y, h_fin = pl.pallas_call(
        functools.partial(_ssd_kernel, rows=rows, n_chunks=n_chunks),
        grid=(b, n_chunks),
        in_specs=in_specs,
        out_specs=[pl.BlockSpec((1, rows, SSD_WIDTH), lambda bi, ci: (bi, ci, 0)),
                   pl.BlockSpec((1, SSD_WIDTH, SSD_STATE), lambda bi, ci: (bi, 0, 0))],
        out_shape=[jax.ShapeDtypeStruct((b, s, SSD_WIDTH), BF16),
                   jax.ShapeDtypeStruct((b, SSD_WIDTH, SSD_STATE), F32)],
        scratch_shapes=[pltpu.VMEM((SSD_STATE, SSD_WIDTH), F32)],
        compiler_params=_cparams(("parallel", "arbitrary")),
        name="ssd_mixer",
    )(big, big, big, dt_raw, dt_t, prev16, h0, rep8(p['conv_w_ssd']), rep8(p['conv_b_ssd']), p['dt_bias'], p['dt_bias_t'],
      p['a_log'], p['a_log_t'], rep8(p['d_skip']), rep8(p['g_ssd_norm']), e3, ltri3, ublk3, shift)
    return y, h_fin


def _merge_kernel(att_ref, ssd_ref, ga_ref, gs_ref, h_ref, wa_ref, ws_ref, wo_ref, o_ref, wa_b, ws_b, wo_b):
    @pl.when(pl.program_id(0) == 0)
    def _():
        wa_b[...] = wa_ref[...].astype(BF16)
        ws_b[...] = ws_ref[...].astype(BF16)
        wo_b[...] = wo_ref[...].astype(BF16)

    a = _dot(att_ref[...], wa_b[...])
    s = _dot(ssd_ref[...], ws_b[...])
    merged = _sigmoid(ga_ref[...].astype(F32)) * a + _sigmoid(gs_ref[...].astype(F32)) * s
    o_ref[...] = h_ref[...] + _dot(merged.astype(BF16), wo_b[...])


def _resident(a):
    return pl.BlockSpec(a.shape, lambda i: (0,) * a.ndim, pipeline_mode=pl.Buffered(1))


def merge(att, ssd, big, h, p):
    m = h.shape[0]
    tm = _pick(m, (512, 256, 128))
    row = lambda w: pl.BlockSpec((tm, w), lambda i: (i, 0))
    gate_blk = BIG_GATE // D_MODEL
    ws = [p['w_br_att'], p['w_br_ssd'], p['w_out']]
    return pl.pallas_call(
        _merge_kernel,
        grid=(m // tm,),
        in_specs=[row(SB_WIDTH), row(SSD_WIDTH),
                  pl.BlockSpec((tm, D_MODEL), lambda i: (i, gate_blk)),
                  pl.BlockSpec((tm, D_MODEL), lambda i: (i, gate_blk + 1)),
                  row(D_MODEL)] + [_resident(w) for w in ws],
        out_specs=row(D_MODEL),
        out_shape=jax.ShapeDtypeStruct((m, D_MODEL), F32),
        scratch_shapes=[pltpu.VMEM(w.shape, BF16) for w in ws],
        compiler_params=_cparams(("arbitrary",)),
        name="merge_out_proj",
    )(att, ssd, big, big, h, *ws)


def _ffn_act_kernel(ug_ref, uv_ref, pg_ref, pv_ref, wg_ref, wv_ref, bg_ref, bv_ref, o_ref, gbuf_ref, vbuf_ref,
                    *, ts):
    t = pl.program_id(1)
    k1 = FFN_CONV - 1

    def conv(u_ref, p_ref, w_ref, b_ref, buf_ref):
        @pl.when(t == 0)
        def _():
            buf_ref[8 - k1:8, :] = p_ref[0]

        buf_ref[8:8 + ts, :] = u_ref[0].astype(F32)
        y = b_ref[...]
        for i in range(FFN_CONV):
            y = y + buf_ref[8 - k1 + i:8 - k1 + i + ts, :] * w_ref[i:i + 1, :]
        buf_ref[8 - k1:8, :] = buf_ref[8 + ts - k1:8 + ts, :]
        return y

    gate = conv(ug_ref, pg_ref, wg_ref, bg_ref, gbuf_ref)
    val = conv(uv_ref, pv_ref, wv_ref, bv_ref, vbuf_ref)
    o_ref[0] = (gate * _sigmoid(gate) * val).astype(o_ref.dtype)


FFN_SUB = 64


def _ffn_act_long_kernel(ug_ref, uv_ref, g16_ref, v16_ref, pg_ref, pv_ref, wg_ref, wv_ref, bg_ref, bv_ref,
                         s_ref, o_ref, *, ts):
    t = pl.program_id(1)
    zeros = jnp.zeros((FFN_SUB - 16, D_FF), BF16)

    def conv(u_ref, b16_ref, p_ref, w_ref, b_ref, j):
        if j == 0:
            before = jnp.where(t == 0, p_ref[0], b16_ref[0])
            window = jnp.concatenate([zeros, before, u_ref[0, 0:FFN_SUB, :]], axis=0)
        else:
            window = u_ref[0, (j - 1) * FFN_SUB:(j + 1) * FFN_SUB, :]
        sh = _dot(s_ref[...], window)
        y = _rows8(jnp.multiply, sh[0:FFN_SUB], w_ref[0:8, :])
        for i in range(1, FFN_CONV):
            y = y + _rows8(jnp.multiply, sh[i * FFN_SUB:(i + 1) * FFN_SUB], w_ref[8 * i:8 * i + 8, :])
        return _rows8(jnp.add, y, b_ref[...])

    for j in range(ts // FFN_SUB):
        gate = conv(ug_ref, g16_ref, pg_ref, wg_ref, bg_ref, j)
        val = conv(uv_ref, v16_ref, pv_ref, wv_ref, bv_ref, j)
        o_ref[0, j * FFN_SUB:(j + 1) * FFN_SUB, :] = (gate * _sigmoid(gate) * val).astype(o_ref.dtype)


def ffn_act_long(up, prev, p):
    b, s, _ = up.shape
    ts = 512
    prev16 = jnp.pad(prev, ((0, 0), (16 - (FFN_CONV - 1), 0), (0, 0)))
    sel = np.zeros((FFN_CONV * FFN_SUB, 2 * FFN_SUB), np.float32)
    for i in range(FFN_CONV):
        for r in range(FFN_SUB):
            sel[i * FFN_SUB + r, FFN_SUB + r - (FFN_CONV - 1 - i)] = 1.0
    rep8 = lambda v: jnp.repeat(v, 8, axis=0)
    half = lambda blk: pl.BlockSpec((1, ts, D_FF), lambda bi, ti: (bi, ti, blk))
    before = lambda blk: pl.BlockSpec((1, 16, D_FF), lambda bi, ti: (bi, jnp.maximum(ti * (ts // 16) - 1, 0), blk))
    prevs = lambda blk: pl.BlockSpec((1, 16, D_FF), lambda bi, ti: (bi, 0, blk))
    wspec = lambda blk: pl.BlockSpec((8 * FFN_CONV, D_FF), lambda bi, ti: (0, blk))
    bspec = lambda blk: pl.BlockSpec((8, D_FF), lambda bi, ti: (0, blk))
    w8, b8 = rep8(p['conv_w_ffn']), rep8(p['conv_b_ffn'])
    return pl.pallas_call(
        functools.partial(_ffn_act_long_kernel, ts=ts),
        grid=(b, s // ts),
        in_specs=[half(0), half(1), before(0), before(1), prevs(0), prevs(1), wspec(0), wspec(1), bspec(0),
                  bspec(1), pl.BlockSpec(sel.shape, lambda bi, ti: (0, 0))],
        out_specs=pl.BlockSpec((1, ts, D_FF), lambda bi, ti: (bi, ti, 0)),
        out_shape=jax.ShapeDtypeStruct((b, s, D_FF), BF16),
        compiler_params=_cparams(("parallel", "parallel")),
        name="ffn_conv_act_long",
    )(up, up, up, up, prev16, prev16, w8, w8, b8, b8, jnp.asarray(sel, BF16))


def ffn_act(up, prev, p):
    b, s, _ = up.shape
    ts = _pick(s, (512, 256, 128))
    half = lambda blk: pl.BlockSpec((1, ts, D_FF), lambda bi, ti: (bi, ti, blk))
    prevs = lambda blk: pl.BlockSpec((1, FFN_CONV - 1, D_FF), lambda bi, ti: (bi, 0, blk))
    wspec = lambda blk: pl.BlockSpec((FFN_CONV, D_FF), lambda bi, ti: (0, blk))
    bspec = lambda blk: pl.BlockSpec((1, D_FF), lambda bi, ti: (0, blk))
    return pl.pallas_call(
        functools.partial(_ffn_act_kernel, ts=ts),
        grid=(b, s // ts),
        in_specs=[half(0), half(1), prevs(0), prevs(1), wspec(0), wspec(1), bspec(0), bspec(1)],
        out_specs=pl.BlockSpec((1, ts, D_FF), lambda bi, ti: (bi, ti, 0)),
        out_shape=jax.ShapeDtypeStruct((b, s, D_FF), BF16),
        scratch_shapes=[pltpu.VMEM((8 + ts, D_FF), F32)] * 2,
        compiler_params=_cparams(("parallel", "arbitrary")),
        name="ffn_conv_act",
    )(up, up, prev, prev, p['conv_w_ffn'], p['conv_w_ffn'], p['conv_b_ffn'], p['conv_b_ffn'])


def _down_kernel(a_ref, w_ref, h_ref, g_ref, o_ref, w_b):
    @pl.when(pl.program_id(0) == 0)
    def _():
        w_b[...] = w_ref[...].astype(BF16)

    h = h_ref[...] + _dot(a_ref[...], w_b[...])
    ms = jnp.mean(h * h, axis=-1, keepdims=True)
    o_ref[...] = h * lax.rsqrt(ms + EPS) * g_ref[...]


def down_norm(act, h, p):
    m = h.shape[0]
    tm = _pick(m, (512, 256, 128))
    return pl.pallas_call(
        _down_kernel,
        grid=(m // tm,),
        in_specs=[pl.BlockSpec((tm, D_FF), lambda i: (i, 0)),
                  _resident(p['w_down']),
                  pl.BlockSpec((tm, D_MODEL), lambda i: (i, 0)),
                  pl.BlockSpec((1, D_MODEL), lambda i: (0, 0))],
        out_specs=pl.BlockSpec((tm, D_MODEL), lambda i: (i, 0)),
        out_shape=jax.ShapeDtypeStruct((m, D_MODEL), F32),
        scratch_shapes=[pltpu.VMEM(p['w_down'].shape, BF16)],
        compiler_params=_cparams(("arbitrary",)),
        name="down_proj_norm",
    )(act, p['w_down'], h, p['g_final'])


def mix_and_ffn(big3, att, dt_raw, hf, ssd_h0, ssd_conv_prev, ffn_conv_prev, p):
    b, s, _ = big3.shape
    m = b * s
    ssd, ssd_state = ssd_mixer(big3, BIG_XBC // SSD_CONV_CH, BIG_Z // SSD_WIDTH, dt_raw, ssd_conv_prev, ssd_h0, p)
    h1 = merge(att.reshape(m, SB_WIDTH), ssd.reshape(m, SSD_WIDTH), big3.reshape(m, BIG_W), hf, p)
    up = norm_mm(h1, p['g_ffn'], p['w_up'], BF16).reshape(b, s, 2 * D_FF)
    if s % 512 == 0:
        act = ffn_act_long(up, ffn_conv_prev.astype(BF16), p)
    else:
        act = ffn_act(up, ffn_conv_prev, p)
    y = down_norm(act.reshape(m, D_FF), h1, p).reshape(b, s, D_MODEL)
    xbc_rows = jnp.concatenate([ssd_conv_prev, big3[:, :, BIG_XBC:BIG_XBC + SSD_CONV_CH][:, -(SSD_CONV - 1):].astype(F32)],
                               axis=1)[:, -(SSD_CONV - 1):]
    up_rows = jnp.concatenate([ffn_conv_prev, up[:, -(FFN_CONV - 1):].astype(F32)], axis=1)[:, -(FFN_CONV - 1):]
    return y, ssd_state, xbc_rows, up_rows


def _prep_params(g_mix, w_in, conv_w_ssd, conv_b_ssd, dt_bias, a_log, d_skip, g_ssd_norm, w_br_att, w_br_ssd,
                 w_out, g_ffn, w_up, conv_w_ffn, conv_b_ffn, w_down, g_final):
    w_t = w_in.T
    dt0 = 3 * SB_WIDTH + SSD_WIDTH + SSD_CONV_CH
    w_tail_t = jnp.concatenate([w_t[dt0 + SSD_HEADS:], w_t[dt0:dt0 + SSD_HEADS],
                                jnp.zeros((LANES - SSD_HEADS, D_MODEL), F32)], axis=0)
    Q = SSD_CHUNK
    lanes_t = lambda v: jnp.repeat(v.reshape(SSD_HEADS // 2, 2, 1), Q, axis=2).reshape(SSD_HEADS // 2, 2 * Q)
    pad_l = lambda v: jnp.pad(v, (0, LANES - SSD_HEADS)).reshape(1, LANES)
    return {
        'g_mix': g_mix.reshape(1, -1), 'g_ffn': g_ffn.reshape(1, -1), 'g_final': g_final.reshape(1, -1),
        'w_t': w_t, 'w_tail_t': w_tail_t,
        'conv_w_ssd': conv_w_ssd, 'conv_b_ssd': conv_b_ssd.reshape(1, -1),
        'dt_bias': pad_l(dt_bias), 'dt_bias_t': lanes_t(dt_bias),
        'a_log': pad_l(a_log), 'a_log_t': lanes_t(a_log),
        'd_skip': jnp.repeat(d_skip, SSD_HEAD_DIM).reshape(1, -1), 'g_ssd_norm': g_ssd_norm.reshape(1, -1),
        'w_br_att': w_br_att, 'w_br_ssd': w_br_ssd, 'w_out': w_out,
        'w_up': w_up, 'conv_w_ffn': conv_w_ffn, 'conv_b_ffn': conv_b_ffn.reshape(1, -1),
        'w_down': w_down,
    }


def kernel(x_prompt, x_sample, cache_k, cache_v, state_ssm, state_ssm_conv, state_ffn_conv, meta_tokens, g_mix, w_in, conv_w_ssd, conv_b_ssd, dt_bias, a_log, d_skip, g_ssd_norm, w_br_att, w_br_ssd, w_out, g_ffn, w_up, conv_w_ffn, conv_b_ffn, w_down, g_final):
    bp, seq, _ = x_prompt.shape
    bs, dec = x_sample.shape[:2]
    assert dec == N_META and seq % KVT_TS == 0
    p = _prep_params(g_mix[0], w_in[0], conv_w_ssd[0], conv_b_ssd[0], dt_bias[0], a_log[0], d_skip[0],
                     g_ssd_norm[0], w_br_att[0], w_br_ssd[0], w_out[0], g_ffn[0], w_up[0], conv_w_ffn[0],
                     conv_b_ffn[0], w_down[0], g_final)
    proj = functools.partial(in_proj, g=p['g_mix'], w_t=p['w_t'], w_tail_t=p['w_tail_t'])
    state_shape = (SSD_WIDTH, SSD_STATE)
    rows = lambda a, b: a.reshape(b, -1, a.shape[-1])

    k_m, v_m, big_m, dt_m = proj(meta_tokens, with_kv=True)
    big_m, k_m, v_m = big_m[None], k_m[None], v_m[None]
    att_m = attention_step(big_m, 0, k_m, v_m, None, None, 0, SB_WIDTH)
    _, ssm_m, conv_m, ffn_m = mix_and_ffn(
        big_m, att_m, dt_m[None], meta_tokens, jnp.zeros((1,) + state_shape, F32),
        jnp.zeros((1, SSD_CONV - 1, SSD_CONV_CH), F32), jnp.zeros((1, FFN_CONV - 1, 2 * D_FF), F32), p)

    xf = x_prompt.reshape(bp * seq, D_MODEL)
    big_x, dt_x = proj(xf, with_kv=False)
    big_x = rows(big_x, bp)
    k_t, v_t, k_end, v_end = kv_transposed(x_prompt, meta_tokens, p['g_mix'], p['w_t'])
    att_x = attention_prompt(big_x, big_m, k_t, v_t)
    att_end = attention_step(big_x, seq // N_META - 1, k_end, v_end, k_t, v_t, seq // KEY_TILE, ATTN_PROMPT_WIDTH)
    att_x = lax.dynamic_update_slice(att_x, att_end, (0, seq - N_META, 0))
    rep = lambda a: jnp.broadcast_to(a, (bp,) + a.shape[1:])
    y_prompt, ssm_p, conv_p, ffn_p = mix_and_ffn(
        big_x, att_x, rows(dt_x, bp), xf, rep(ssm_m), rep(conv_m), rep(ffn_m), p)

    n_rows = cache_k.shape[2]
    n_tiles = (n_rows - N_META) // KEY_TILE
    assert n_tiles * KEY_TILE + N_META == n_rows
    k_s, v_s, big_s, dt_s = proj(x_sample.reshape(bs * dec, D_MODEL), with_kv=True)
    big_s, k_s, v_s = rows(big_s, bs), rows(k_s, bs), rows(v_s, bs)
    cache_t = lambda c: c[0].transpose(0, 2, 3, 1).reshape(bs, SB_WIDTH, n_rows)
    nearest = lambda c, new: jnp.concatenate([c[0][:, n_tiles * KEY_TILE:].reshape(bs, N_META, SB_WIDTH), new], axis=1)
    att_s = attention_step(big_s, 0, nearest(cache_k, k_s), nearest(cache_v, v_s), cache_t(cache_k), cache_t(cache_v),
                           n_tiles, SB_WIDTH)
    y_sample, ssm_s, conv_s, ffn_s = mix_and_ffn(
        big_s, att_s, rows(dt_s, bs), x_sample.reshape(bs * dec, D_MODEL),
        state_ssm[0].reshape((bs,) + state_shape), state_ssm_conv[0], state_ffn_conv[0], p)

    heads = lambda a: a.reshape(a.shape[0], a.shape[1], SB_HEADS, SB_HEAD_DIM)[None]
    heads_t = lambda a: a.reshape(a.shape[0], SB_HEADS, SB_HEAD_DIM, a.shape[2]).transpose(0, 3, 1, 2)[None]
    state5 = lambda a: a.reshape(a.shape[0], SSD_HEADS, SSD_HEAD_DIM, SSD_STATE)[None]
    return (y_prompt, y_sample, heads_t(k_t), heads_t(v_t), state5(ssm_p), conv_p[None], ffn_p[None],
            heads(k_s), heads(v_s), state5(ssm_s), conv_s[None], ffn_s[None])
```

```python
import functools

import numpy as np
import jax
import jax.numpy as jnp
from jax import lax
from jax.experimental import pallas as pl
from jax.experimental.pallas import tpu as pltpu

F32 = jnp.float32
BF16 = jnp.bfloat16

D_MODEL = 1024
N_META = 16
SB_HEADS = 16
SB_HEAD_DIM = 64
SB_WIDTH = SB_HEADS * SB_HEAD_DIM
SSD_WIDTH = 2 * D_MODEL
SSD_HEAD_DIM = 64
SSD_HEADS = SSD_WIDTH // SSD_HEAD_DIM
SSD_GROUPS = 4
SSD_STATE = 128
SSD_CONV = 4
SSD_CONV_CH = SSD_WIDTH + 2 * SSD_GROUPS * SSD_STATE
D_FF = 2816
FFN_CONV = 3
EPS = 1e-6

LANES = 128
SSD_CHUNK = 64
SSD_STEP_CHUNKS = 2
KEY_TILE = 128
GROUP_W = SSD_WIDTH // SSD_GROUPS
DEAD_LOG = -104.0
LOG2E = 1.4426950408889634
ATTN_STATIC_TILES = 2
VMEM_LIMIT = 56 * 1024 * 1024


def _cparams(sem):
    return pltpu.CompilerParams(dimension_semantics=sem, vmem_limit_bytes=VMEM_LIMIT)


def _pick(n, cands):
    for c in cands:
        if n % c == 0:
            return c
    return n


def _split2(x):
    hi = x.astype(BF16)
    lo = (x - hi.astype(F32)).astype(BF16)
    return hi, lo


def _split3(x):
    hi = x.astype(BF16)
    r = x - hi.astype(F32)
    mid = r.astype(BF16)
    lo = (r - mid.astype(F32)).astype(BF16)
    return hi, mid, lo


def _dot(a, b):
    return jnp.dot(a, b, preferred_element_type=F32)


def _dot_nt(a, b):
    return lax.dot_general(a, b, (((1,), (1,)), ((), ())), preferred_element_type=F32)


def _softplus(x):
    return jnp.maximum(x, 0.0) + jnp.log(1.0 + jnp.exp(-jnp.abs(x)))


def _sigmoid(x):
    return 1.0 / (1.0 + jnp.exp(-x))


def _rms_norm_bf16(x, g):
    ms = jnp.mean(x * x, axis=-1, keepdims=True)
    return (x * lax.rsqrt(ms + EPS) * g).astype(BF16)


def _norm_mm_kernel(x_ref, g_ref, w_ref, o_ref, u_ref):
    @pl.when(pl.program_id(1) == 0)
    def _():
        u_ref[...] = _rms_norm_bf16(x_ref[...], g_ref[...])

    o_ref[...] = _dot(u_ref[...], w_ref[...].astype(BF16)).astype(o_ref.dtype)


PROJ_TN = 1024
UP_TN = 1408


def norm_mm(x, g, w, out_dtype):
    m, d = x.shape
    n = w.shape[1]
    tm = _pick(m, (2048, 1024, 512, 256, 128))
    tn = UP_TN
    return pl.pallas_call(
        _norm_mm_kernel,
        grid=(m // tm, n // tn),
        in_specs=[pl.BlockSpec((tm, d), lambda i, j: (i, 0)),
                  pl.BlockSpec((1, d), lambda i, j: (0, 0)),
                  pl.BlockSpec((d, tn), lambda i, j: (0, j))],
        out_specs=pl.BlockSpec((tm, tn), lambda i, j: (i, j)),
        out_shape=jax.ShapeDtypeStruct((m, n), out_dtype),
        scratch_shapes=[pltpu.VMEM((tm, d), BF16)],
        compiler_params=_cparams(("parallel", "arbitrary")),
        name="norm_up_proj",
    )(x, g, w)


_NQ = SB_WIDTH // PROJ_TN
_J_K, _J_V, _J_Z = _NQ, 2 * _NQ, 3 * _NQ
_J_X = _J_Z + SSD_WIDTH // PROJ_TN
_J_G = _J_X + SSD_CONV_CH // PROJ_TN
_J_END = _J_G + 2 * D_MODEL // PROJ_TN
BIG_Z, BIG_Q, BIG_XBC, BIG_GATE = 0, SSD_WIDTH, SSD_WIDTH + SB_WIDTH, SSD_WIDTH + SB_WIDTH + SSD_CONV_CH
BIG_W = BIG_GATE + 2 * D_MODEL


def _big_block(j):
    return jnp.where(j < _J_K, BIG_Q // PROJ_TN + j,
                     jnp.where(j < _J_Z, BIG_Q // PROJ_TN + _NQ - 1,
                               jnp.where(j < _J_X, j - _J_Z + BIG_Z // PROJ_TN,
                                         jnp.where(j < _J_G, j - _J_X + BIG_XBC // PROJ_TN,
                                                   j - _J_G + BIG_GATE // PROJ_TN))))


def _sweep_step(j, with_kv):
    return j if with_kv else jnp.where(j >= _J_K, j + (_J_Z - _J_K), j)


def _in_proj_kernel(*refs, with_kv):
    if with_kv:
        x_ref, g_ref, win_ref, wgate_ref, wdt_ref, k_ref, v_ref, big_ref, dt_ref, u_ref = refs
    else:
        x_ref, g_ref, win_ref, wgate_ref, wdt_ref, big_ref, dt_ref, u_ref = refs
    j = _sweep_step(pl.program_id(1), with_kv)

    def proj(w_ref):
        return _dot_nt(u_ref[...], w_ref[...].astype(BF16))

    @pl.when(pl.program_id(1) == 0)
    def _():
        u_ref[...] = _rms_norm_bf16(x_ref[...], g_ref[...])
        dt_ref[...] = proj(wdt_ref)

    if with_kv:
        @pl.when(jnp.logical_and(j >= _J_K, j < _J_V))
        def _():
            k_ref[...] = proj(win_ref)

        @pl.when(jnp.logical_and(j >= _J_V, j < _J_Z))
        def _():
            v_ref[...] = proj(win_ref)

    @pl.when(jnp.logical_or(j < _J_K, jnp.logical_and(j >= _J_Z, j < _J_G)))
    def _():
        big_ref[...] = proj(win_ref).astype(BF16)

    @pl.when(j >= _J_G)
    def _():
        big_ref[...] = proj(wgate_ref).astype(BF16)


def in_proj(x, g, w_t, w_tail_t, with_kv):
    m, d = x.shape
    tm = _pick(m, (2048, 1024, 512, 256, 128))
    tn = PROJ_TN
    step = functools.partial(_sweep_step, with_kv=with_kv)
    clip = lambda j, lo, n: jnp.clip(step(j) - lo, 0, n - 1)
    kv_specs = [pl.BlockSpec((tm, tn), lambda i, j: (i, clip(j, _J_K, _NQ))),
                pl.BlockSpec((tm, tn), lambda i, j: (i, clip(j, _J_V, _NQ)))]
    kv_shapes = [jax.ShapeDtypeStruct((m, SB_WIDTH), F32)] * 2
    return pl.pallas_call(
        functools.partial(_in_proj_kernel, with_kv=with_kv),
        grid=(m // tm, _J_END if with_kv else _J_END - (_J_Z - _J_K)),
        in_specs=[pl.BlockSpec((tm, d), lambda i, j: (i, 0), pipeline_mode=pl.Buffered(1)),
                  pl.BlockSpec((1, d), lambda i, j: (0, 0)),
                  pl.BlockSpec((tn, d), lambda i, j: (jnp.minimum(step(j), _J_G - 1), 0)),
                  pl.BlockSpec((tn, d), lambda i, j: (clip(j, _J_G, _J_END - _J_G), 0)),
                  pl.BlockSpec((LANES, d), lambda i, j: (2 * D_MODEL // LANES, 0))],
        out_specs=(kv_specs if with_kv else []) + [
            pl.BlockSpec((tm, tn), lambda i, j: (i, _big_block(step(j)))),
            pl.BlockSpec((tm, LANES), lambda i, j: (i, 0))],
        out_shape=(kv_shapes if with_kv else []) + [
            jax.ShapeDtypeStruct((m, BIG_W), BF16), jax.ShapeDtypeStruct((m, LANES), F32)],
        scratch_shapes=[pltpu.VMEM((tm, d), BF16)],
        compiler_params=_cparams(("parallel", "arbitrary")),
        name="in_proj",
    )(x, g, w_t, w_tail_t, w_tail_t)


KVT_TS = 1024


def _kv_t_kernel(x16_ref, xt_ref, meta_ref, g_ref, wk_ref, wv_ref, kt_ref, vt_ref, kl_ref, vl_ref, wk_b, wv_b):
    j = pl.program_id(1)

    @pl.when(jnp.logical_and(pl.program_id(0) == 0, j == 0))
    def _():
        wk_b[...] = wk_ref[...].astype(BF16)
        wv_b[...] = wv_ref[...].astype(BF16)

    first = jnp.where(j == 0, meta_ref[...], x16_ref[0])
    x = jnp.concatenate([first, xt_ref[0, :KVT_TS - N_META, :]], axis=0)
    u = _rms_norm_bf16(x, g_ref[...])
    kt_ref[0] = _dot_nt(wk_b[...], u)
    vt_ref[0] = _dot_nt(wv_b[...], u)

    @pl.when(j == pl.num_programs(1) - 1)
    def _():
        kl_ref[0] = _dot_nt(u[0:N_META, :], wk_b[...])
        vl_ref[0] = _dot_nt(u[0:N_META, :], wv_b[...])


def kv_transposed(x, meta, g, w_t):
    b, s, d = x.shape
    n_pos = N_META + s
    n_tiles = -(-n_pos // KVT_TS)
    assert n_pos - (n_tiles - 1) * KVT_TS == N_META
    per_tile = KVT_TS // N_META
    out_spec = pl.BlockSpec((1, SB_WIDTH, KVT_TS), lambda bi, j: (bi, 0, j))
    last_spec = pl.BlockSpec((1, N_META, SB_WIDTH), lambda bi, j: (bi, 0, 0))
    w_spec = lambda blk: pl.BlockSpec((SB_WIDTH, d), lambda bi, j: (blk, 0), pipeline_mode=pl.Buffered(1))
    return pl.pallas_call(
        _kv_t_kernel,
        grid=(b, n_tiles),
        in_specs=[pl.BlockSpec((1, N_META, d), lambda bi, j: (bi, jnp.maximum(j * per_tile - 1, 0), 0)),
                  pl.BlockSpec((1, KVT_TS, d), lambda bi, j: (bi, jnp.minimum(j, s // KVT_TS - 1), 0)),
                  pl.BlockSpec((N_META, d), lambda bi, j: (0, 0)),
                  pl.BlockSpec((1, d), lambda bi, j: (0, 0)),
                  w_spec(1), w_spec(2)],
        out_specs=[out_spec, out_spec, last_spec, last_spec],
        out_shape=[jax.ShapeDtypeStruct((b, SB_WIDTH, n_pos), F32)] * 2
        + [jax.ShapeDtypeStruct((b, N_META, SB_WIDTH), F32)] * 2,
        scratch_shapes=[pltpu.VMEM((SB_WIDTH, d), BF16)] * 2,
        compiler_params=_cparams(("arbitrary", "arbitrary")),
        name="kv_transposed",
    )(x, x, meta, g, w_t, w_t)


def _stick_breaking_tiles(qs, u2, tiles, carries, accs):
    n_heads = len(qs)
    units = [(t, h) for t in range(len(tiles)) for h in range(n_heads)]
    ps = lambda h: slice((h // 2) * LANES, (h // 2 + 1) * LANES)

    def scores(t, h):
        kind, keys = tiles[t][0], tiles[t][1]
        return _dot_nt(qs[h], keys[:, ps(h)]) if kind == 'rows' else _dot(qs[h], keys[ps(h), :])

    z = {u: scores(*u) for u in units}
    log_beta, cat = {}, {}
    for u in units:
        mask = tiles[u[0]][3]
        soft = jnp.log(1.0 + jnp.exp(-jnp.abs(z[u])))
        log_beta[u] = jnp.minimum(z[u], 0.0) - soft
        log_keep = log_beta[u] - z[u]
        if mask is not None:
            log_keep = jnp.where(mask, log_keep, 0.0)
        cat[u] = jnp.concatenate(_split2(log_keep), axis=1)
    r = {u: _dot(cat[u], u2) for u in units}
    carries, accs = list(carries), list(accs)
    w = {}
    for t, h in units:
        mask = tiles[t][3]
        wt = jnp.exp(log_beta[(t, h)] + r[(t, h)][:, :LANES] + carries[h])
        if mask is not None:
            wt = jnp.where(mask, wt, 0.0)
        w[(t, h)] = wt.astype(BF16)
        carries[h] = carries[h] + r[(t, h)][:, LANES:]
    for t, h in units:
        kind, vals = tiles[t][0], tiles[t][2]
        pv = _dot(w[(t, h)], vals[:, ps(h)]) if kind == 'rows' else _dot_nt(w[(t, h)], vals[ps(h), :])
        accs[h] = accs[h] + pv
    return carries, accs


def _head_queries(q, lo_half):
    qs = []
    for h in range(q.shape[1] // SB_HEAD_DIM):
        qp = q[:, (h // 2) * LANES:(h // 2 + 1) * LANES]
        keep = lo_half if h % 2 == 0 else jnp.logical_not(lo_half)
        qs.append(jnp.where(keep, qp, jnp.zeros_like(qp)) * jnp.asarray(SB_HEAD_DIM ** -0.5, BF16))
    return qs


def _any_alive(carries):
    m = carries[0]
    for c in carries[1:]:
        m = jnp.maximum(m, c)
    return (jnp.max(m) > DEAD_LOG).astype(jnp.int32)


def _attend(qs, u2, first_tiles, kt_ref, vt_ref, j0, lane):
    tq = qs[0].shape[0]

    def cache_tile(j):
        if isinstance(j, int):
            off, mask = max(j, 0) * KEY_TILE, (None if j >= 0 else lane < 0)
        else:
            off, mask = pl.multiple_of(jnp.maximum(j, 0) * KEY_TILE, KEY_TILE), j >= 0
            mask = jnp.logical_and(mask, lane >= 0)
        kt = kt_ref[0, :, pl.ds(off, KEY_TILE)].astype(BF16)
        vt = vt_ref[0, :, pl.ds(off, KEY_TILE)].astype(BF16)
        return 'cols', kt, vt, mask

    tiles = list(first_tiles)
    if kt_ref is not None:
        for _ in range(ATTN_STATIC_TILES):
            tiles.append(cache_tile(j0))
            j0 = j0 - 1
    zeros = jnp.zeros((tq, LANES), F32)
    carries, accs = _stick_breaking_tiles(qs, u2, tiles, [zeros] * len(qs), [zeros] * len(qs))
    if kt_ref is not None:
        def cond(s):
            return jnp.logical_and(s[0] >= 0, s[1] > 0)

        def body(s):
            j, _, carries, accs = s
            carries, accs = _stick_breaking_tiles(qs, u2, [cache_tile(j)], carries, accs)
            return j - 1, _any_alive(carries), tuple(carries), tuple(accs)

        state = (jnp.asarray(j0, jnp.int32), _any_alive(carries), tuple(carries), tuple(accs))
        _, _, carries, accs = lax.while_loop(cond, body, state)
    return accs


def _pair_outputs(accs, lo_half, dtype):
    return [jnp.where(lo_half, accs[2 * p], accs[2 * p + 1]).astype(dtype) for p in range(len(accs) // 2)]


def _attn_prompt_kernel(q16_ref, qt_ref, qm_ref, u2_ref, kt_ref, vt_ref, o_ref):
    m = pl.program_id(2)
    tq = KEY_TILE
    lane = lax.broadcasted_iota(jnp.int32, (tq, LANES), 1)
    row = lax.broadcasted_iota(jnp.int32, (tq, LANES), 0)
    lo_half = lane < SB_HEAD_DIM
    first = jnp.where(m == 0, qm_ref[0], q16_ref[0])
    q = jnp.concatenate([first, qt_ref[0, :tq - N_META, :]], axis=0)
    qs = _head_queries(q, lo_half)
    off = pl.multiple_of(m * KEY_TILE, KEY_TILE)
    diag = ('cols', kt_ref[0, :, pl.ds(off, KEY_TILE)].astype(BF16),
            vt_ref[0, :, pl.ds(off, KEY_TILE)].astype(BF16), lane < row)
    accs = _attend(qs, u2_ref[...], [diag], kt_ref, vt_ref, m - 1, lane)
    out = jnp.concatenate(_pair_outputs(accs, lo_half, o_ref.dtype), axis=1)

    @pl.when(m == 0)
    def _():
        o_ref[0, 0:tq - N_META, :] = out[N_META:, :]
        o_ref[0, o_ref.shape[1] - N_META:, :] = jnp.zeros((N_META, o_ref.shape[2]), o_ref.dtype)

    @pl.when(m > 0)
    def _():
        o_ref[0, pl.ds(pl.multiple_of(m * KEY_TILE - N_META, N_META), tq), :] = out


def _attn_step_kernel(*refs, n_free, n_cache_tiles):
    if n_cache_tiles:
        q_ref, kd_ref, vd_ref, u2_ref, kt_ref, vt_ref, o_ref, kd_scr, vd_scr = refs
    else:
        q_ref, kd_ref, vd_ref, u2_ref, o_ref, kd_scr, vd_scr = refs
        kt_ref = vt_ref = None
    tq = q_ref.shape[1]
    n_rows = kd_ref.shape[1]
    lane = lax.broadcasted_iota(jnp.int32, (tq, LANES), 1)
    row = lax.broadcasted_iota(jnp.int32, (tq, LANES), 0)
    lo_half = lane < SB_HEAD_DIM
    kd_scr[...] = jnp.zeros_like(kd_scr)
    vd_scr[...] = jnp.zeros_like(vd_scr)
    kd_scr[0:n_rows, :] = kd_ref[0].astype(BF16)
    vd_scr[0:n_rows, :] = vd_ref[0].astype(BF16)
    diag = ('rows', kd_scr[...], vd_scr[...], lane < row + n_free)
    qs = _head_queries(q_ref[0], lo_half)
    accs = _attend(qs, u2_ref[...], [diag], kt_ref, vt_ref, n_cache_tiles - 1, lane)
    for p, out in enumerate(_pair_outputs(accs, lo_half, o_ref.dtype)):
        o_ref[0, :, p * LANES:(p + 1) * LANES] = out


def _cumsum_rhs():
    s = np.arange(KEY_TILE)
    strict = (s[:, None] > s[None, :]).astype(np.float32)
    half = np.concatenate([strict, np.ones((KEY_TILE, KEY_TILE), np.float32)], axis=1)
    return jnp.asarray(np.concatenate([half, half], axis=0), BF16)


ATTN_PROMPT_WIDTH = 8 * SB_HEAD_DIM


def attention_prompt(big, big_meta, k_t, v_t):
    b, s, _ = big.shape
    width = ATTN_PROMPT_WIDTH
    qb0 = BIG_Q // width
    per_tile = KEY_TILE // N_META
    n_pos = k_t.shape[2]
    cache = pl.BlockSpec((1, width, n_pos), lambda bi, hp, m: (bi, hp, 0))
    return pl.pallas_call(
        _attn_prompt_kernel,
        grid=(b, SB_WIDTH // width, s // KEY_TILE),
        in_specs=[pl.BlockSpec((1, N_META, width), lambda bi, hp, m: (bi, jnp.maximum(m * per_tile - 1, 0), qb0 + hp)),
                  pl.BlockSpec((1, KEY_TILE, width), lambda bi, hp, m: (bi, m, qb0 + hp)),
                  pl.BlockSpec((1, N_META, width), lambda bi, hp, m: (0, 0, qb0 + hp)),
                  pl.BlockSpec((2 * KEY_TILE, 2 * KEY_TILE), lambda bi, hp, m: (0, 0)),
                  cache, cache],
        out_specs=pl.BlockSpec((1, s, width), lambda bi, hp, m: (bi, 0, hp)),
        out_shape=jax.ShapeDtypeStruct((b, s, SB_WIDTH), BF16),
        compiler_params=_cparams(("parallel", "parallel", "arbitrary")),
        name="stick_breaking_prompt",
    )(big, big, big_meta, _cumsum_rhs(), k_t, v_t)


def attention_step(big, q_row_blk, k_rows, v_rows, k_t, v_t, n_cache_tiles, width):
    b = big.shape[0]
    n_rows = k_rows.shape[1]
    qb0 = BIG_Q // width
    rows = pl.BlockSpec((1, n_rows, width), lambda bi, hp: (bi, 0, hp))
    in_specs = [pl.BlockSpec((1, N_META, width), lambda bi, hp: (bi, q_row_blk, qb0 + hp)), rows, rows,
                pl.BlockSpec((2 * KEY_TILE, 2 * KEY_TILE), lambda bi, hp: (0, 0))]
    args = [big, k_rows, v_rows, _cumsum_rhs()]
    if n_cache_tiles:
        in_specs += [pl.BlockSpec((1, width, k_t.shape[2]), lambda bi, hp: (bi, hp, 0))] * 2
        args += [k_t, v_t]
    return pl.pallas_call(
        functools.partial(_attn_step_kernel, n_free=n_rows - N_META, n_cache_tiles=n_cache_tiles),
        grid=(b, SB_WIDTH // width),
        in_specs=in_specs,
        out_specs=pl.BlockSpec((1, N_META, width), lambda bi, hp: (bi, 0, hp)),
        out_shape=jax.ShapeDtypeStruct((b, N_META, SB_WIDTH), BF16),
        scratch_shapes=[pltpu.VMEM((KEY_TILE, width), BF16)] * 2,
        compiler_params=_cparams(("parallel", "parallel")),
        name="stick_breaking_step",
    )(*args)


def _rows8(op, x, r8):
    return op(x.reshape(x.shape[0] // 8, 8, x.shape[1]), r8[None]).reshape(x.shape)


def _ssd_kernel(xbc_ref, x16_ref, z_ref, dt_ref, dtT_ref, prev_ref, h0_ref, cw_ref, cb_ref, dtb_ref, dtbT_ref,
                alog_ref, alogT_ref, dskip_ref, gn_ref, e3_ref, ltri_ref, ublk_ref, shift_ref,
                y_ref, hfin_ref, st_ref, *, rows, n_sub, n_chunks):
    Q = SSD_CHUNK
    c = pl.program_id(1)
    n_blk = SSD_WIDTH // LANES
    mul, sub = jnp.multiply, jnp.subtract

    @pl.when(c == 0)
    def _():
        for j in range(n_blk):
            st_ref[:, j * LANES:(j + 1) * LANES] = h0_ref[0, j * LANES:(j + 1) * LANES, :].T

    def pad_rows(v):
        if rows == Q:
            return v
        return jnp.concatenate([v, jnp.zeros((Q - rows, v.shape[1]), v.dtype)], axis=0)

    lane = lax.broadcasted_iota(jnp.int32, (Q, LANES), 1)
    rowq = lax.broadcasted_iota(jnp.int32, (Q, LANES), 0)
    causal2 = (lane % Q) <= rowq
    lo_half = lane < SSD_HEAD_DIM
    decay_rate = -LOG2E * jnp.exp(alog_ref[...])
    decay_rate_t = -LOG2E * jnp.exp(alogT_ref[...])

    for k in range(n_sub):
        r0 = k * Q
        hi, lo = _split2(prev_ref[0])
        if k == 0:
            hi = jnp.where(c == 0, hi, x16_ref[0])
            lo = jnp.where(c == 0, lo, jnp.zeros_like(lo))
        else:
            hi, lo = xbc_ref[0, r0 - 16:r0, :], jnp.zeros_like(lo)
        window = jnp.concatenate([pad_rows(xbc_ref[0, r0:r0 + rows, :]), hi, lo,
                                  jnp.zeros((Q - 32, SSD_CONV_CH), BF16)], axis=0)
        shifted = _dot(shift_ref[...], window)
        conv = _rows8(mul, shifted[0:Q], cw_ref[0:8, :])
        for i in range(1, SSD_CONV):
            conv = conv + _rows8(mul, shifted[i * Q:(i + 1) * Q], cw_ref[8 * i:8 * i + 8, :])
        conv = _rows8(jnp.add, conv, cb_ref[...])
        xc = conv * _sigmoid(conv)
        xs = xc[:, :SSD_WIDTH]
        b_all = xc[:, SSD_WIDTH:SSD_WIDTH + SSD_GROUPS * SSD_STATE]
        c_all = xc[:, SSD_WIDTH + SSD_GROUPS * SSD_STATE:]

        dt = _softplus(pad_rows(dt_ref[0, r0:r0 + rows, :]) + dtb_ref[...])
        dt_t = _softplus(dtT_ref[0, k] + dtbT_ref[...])
        if rows < Q:
            rowi = lax.broadcasted_iota(jnp.int32, (Q, 1), 0)
            dt = jnp.where(rowi < rows, dt, 0.0)
            xs = jnp.where(rowi < rows, xs, 0.0)
            lane_t = lax.broadcasted_iota(jnp.int32, dt_t.shape, 1)
            dt_t = jnp.where(lane_t % Q < rows, dt_t, 0.0)
        da = dt * decay_rate
        da_t = dt_t * decay_rate_t

        a_cum = _dot(ltri_ref[...], jnp.concatenate(_split3(da), axis=0))
        a_cum_t = _dot(jnp.concatenate(_split3(da_t), axis=1), ublk_ref[...])
        dt_exp = _dot(jnp.concatenate(_split2(dt), axis=1), e3_ref[0:2 * LANES, :])
        a_exp = _dot(jnp.concatenate(_split3(a_cum), axis=1), e3_ref[...])
        a_last = jnp.broadcast_to(a_exp[Q - 1:Q, :], (8, SSD_WIDTH))

        xdt = xs * dt_exp
        xdt_b = xdt.astype(BF16)
        xw = (xdt * jnp.exp2(-_rows8(sub, a_exp, a_last))).astype(BF16)
        xw_pad = jnp.concatenate([xw, jnp.zeros_like(xw)], axis=0)
        chunk_decay = jnp.exp2(a_last)
        grow = jnp.exp2(a_exp)

        y_parts = []
        for g in range(SSD_GROUPS):
            bg = b_all[:, g * SSD_STATE:(g + 1) * SSD_STATE]
            cg = c_all[:, g * SSD_STATE:(g + 1) * SSD_STATE].astype(BF16)
            bg_b = bg.astype(BF16)
            cb2 = _dot_nt(cg, jnp.concatenate([bg_b, bg_b], axis=0))
            gs = slice(g * GROUP_W, (g + 1) * GROUP_W)
            st_g = st_ref[:, gs]
            y_off = _dot(cg, st_g.astype(BF16)) * grow[:, gs]
            bg_t = jnp.concatenate([bg, jnp.zeros_like(bg)], axis=0).T.astype(BF16)
            st_ref[:, gs] = _rows8(mul, st_g, chunk_decay[:, gs]) + _dot(bg_t, xw_pad[:, gs])
            pair_out = []
            for kk in range(GROUP_W // LANES):
                i = g * (GROUP_W // LANES) + kk
                ps = slice(i * LANES, (i + 1) * LANES)
                a_row = jnp.broadcast_to(a_cum_t[i:i + 1, :], (8, LANES))
                decay = jnp.exp2(jnp.minimum(_rows8(sub, a_exp[:, ps], a_row), 0.0))
                m2 = jnp.where(causal2, cb2 * decay, 0.0).astype(BF16)
                xp = xdt_b[:, ps]
                zero = jnp.zeros_like(xp)
                xbd = jnp.concatenate([jnp.where(lo_half, xp, zero), jnp.where(lo_half, zero, xp)], axis=0)
                pair_out.append(_dot(m2, xbd))
            y_diag = jnp.concatenate(pair_out, axis=1)
            y = y_diag + y_off + _rows8(mul, xs[:, gs], dskip_ref[:, gs])
            zg = pad_rows(z_ref[0, r0:r0 + rows, gs].astype(F32))
            y = y * (zg * _sigmoid(zg))
            ms = jnp.mean(y * y, axis=-1, keepdims=True)
            y_parts.append(_rows8(mul, y * lax.rsqrt(ms + EPS), gn_ref[:, gs]))
        y_all = jnp.concatenate(y_parts, axis=1)
        y_ref[0, r0:r0 + rows, :] = y_all[:rows].astype(y_ref.dtype)

    @pl.when(c == n_chunks - 1)
    def _():
        for j in range(n_blk):
            hfin_ref[0, j * LANES:(j + 1) * LANES, :] = st_ref[:, j * LANES:(j + 1) * LANES].T


def _ssd_constants():
    Q = SSD_CHUNK
    e = np.zeros((LANES, SSD_WIDTH), np.float32)
    for h in range(SSD_HEADS):
        e[h, h * SSD_HEAD_DIM:(h + 1) * SSD_HEAD_DIM] = 1.0
    e3 = np.concatenate([e, e, e], axis=0)
    t = np.arange(Q)
    ltri = (t[:, None] >= t[None, :]).astype(np.float32)
    ltri3 = np.concatenate([ltri, ltri, ltri], axis=1)
    ublk = np.zeros((LANES, LANES), np.float32)
    ublk[:Q, :Q] = ltri.T
    ublk[Q:, Q:] = ltri.T
    ublk3 = np.concatenate([ublk, ublk, ublk], axis=0)
    shift = np.zeros((SSD_CONV * Q, 2 * Q), np.float32)
    for i in range(SSD_CONV):
        for r in range(Q):
            src = r - (SSD_CONV - 1 - i)
            if src >= 0:
                shift[i * Q + r, src] = 1.0
            else:
                shift[i * Q + r, Q + 16 + src] = 1.0
                shift[i * Q + r, Q + 32 + src] = 1.0
    return jnp.asarray(e3, BF16), jnp.asarray(ltri3, BF16), jnp.asarray(ublk3, BF16), jnp.asarray(shift, BF16)


def ssd_mixer(big, xbc_blk, z_blk, dt_raw, conv_prev, h0, p):
    b, s, _ = big.shape
    Q = SSD_CHUNK
    rows = min(s, Q)
    n_sub = SSD_STEP_CHUNKS if s % (SSD_STEP_CHUNKS * Q) == 0 else 1
    blk = rows * n_sub
    n_chunks = -(-s // blk)
    dtp = dt_raw[:, :, :SSD_HEADS]
    if s < Q:
        dtp = jnp.pad(dtp, ((0, 0), (0, Q - s), (0, 0)))
    dt_t = dtp.reshape(b, n_chunks * n_sub, Q, SSD_HEADS).transpose(0, 1, 3, 2).reshape(
        b, n_chunks * n_sub, SSD_HEADS // 2, 2 * Q)
    e3, ltri3, ublk3, shift = _ssd_constants()
    prev16 = jnp.pad(conv_prev, ((0, 0), (16 - (SSD_CONV - 1), 0), (0, 0)))
    rep8 = lambda v: jnp.repeat(v, 8, axis=0)
    const = lambda shape: pl.BlockSpec(shape, lambda bi, ci: (0,) * len(shape))
    in_specs = [
        pl.BlockSpec((1, blk, SSD_CONV_CH), lambda bi, ci: (bi, ci, xbc_blk)),
        pl.BlockSpec((1, 16, SSD_CONV_CH), lambda bi, ci: (bi, jnp.maximum(ci * (blk // 16) - 1, 0), xbc_blk)),
        pl.BlockSpec((1, blk, SSD_WIDTH), lambda bi, ci: (bi, ci, z_blk)),
        pl.BlockSpec((1, blk, LANES), lambda bi, ci: (bi, ci, 0)),
        pl.BlockSpec((1, n_sub, SSD_HEADS // 2, 2 * Q), lambda bi, ci: (bi, ci, 0, 0)),
        pl.BlockSpec((1, 16, SSD_CONV_CH), lambda bi, ci: (bi, 0, 0)),
        pl.BlockSpec((1, SSD_WIDTH, SSD_STATE), lambda bi, ci: (bi, 0, 0)),
        const((8 * SSD_CONV, SSD_CONV_CH)), const((8, SSD_CONV_CH)),
        const((1, LANES)), const((SSD_HEADS // 2, 2 * Q)),
        const((1, LANES)), const((SSD_HEADS // 2, 2 * Q)),
        const((8, SSD_WIDTH)), const((8, SSD_WIDTH)),
        const(e3.shape), const(ltri3.shape), const(ublk3.shape), const(shift.shape),
    ]
    y, h_fin = pl.pallas_call(
        functools.partial(_ssd_kernel, rows=rows, n_sub=n_sub, n_chunks=n_chunks),
        grid=(b, n_chunks),
        in_specs=in_specs,
        out_specs=[pl.BlockSpec((1, blk, SSD_WIDTH), lambda bi, ci: (bi, ci, 0)),
                   pl.BlockSpec((1, SSD_WIDTH, SSD_STATE), lambda bi, ci: (bi, 0, 0))],
        out_shape=[jax.ShapeDtypeStruct((b, s, SSD_WIDTH), BF16),
                   jax.ShapeDtypeStruct((b, SSD_WIDTH, SSD_STATE), F32)],
        scratch_shapes=[pltpu.VMEM((SSD_STATE, SSD_WIDTH), F32)],
        compiler_params=_cparams(("parallel", "arbitrary")),
        name="ssd_mixer",
    )(big, big, big, dt_raw, dt_t, prev16, h0, rep8(p['conv_w_ssd']), rep8(p['conv_b_ssd']), p['dt_bias'], p['dt_bias_t'],
      p['a_log'], p['a_log_t'], rep8(p['d_skip']), rep8(p['g_ssd_norm']), e3, ltri3, ublk3, shift)
    return y, h_fin


def _merge_kernel(att_ref, ssd_ref, ga_ref, gs_ref, h_ref, wa_ref, ws_ref, wo_ref, o_ref, wa_b, ws_b, wo_b):
    @pl.when(pl.program_id(0) == 0)
    def _():
        wa_b[...] = wa_ref[...].astype(BF16)
        ws_b[...] = ws_ref[...].astype(BF16)
        wo_b[...] = wo_ref[...].astype(BF16)

    a = _dot(att_ref[...], wa_b[...])
    s = _dot(ssd_ref[...], ws_b[...])
    merged = _sigmoid(ga_ref[...].astype(F32)) * a + _sigmoid(gs_ref[...].astype(F32)) * s
    o_ref[...] = h_ref[...] + _dot(merged.astype(BF16), wo_b[...])


def _resident(a):
    return pl.BlockSpec(a.shape, lambda i: (0,) * a.ndim, pipeline_mode=pl.Buffered(1))


def merge(att, ssd, big, h, p):
    m = h.shape[0]
    tm = _pick(m, (512, 256, 128))
    row = lambda w: pl.BlockSpec((tm, w), lambda i: (i, 0))
    gate_blk = BIG_GATE // D_MODEL
    ws = [p['w_br_att'], p['w_br_ssd'], p['w_out']]
    return pl.pallas_call(
        _merge_kernel,
        grid=(m // tm,),
        in_specs=[row(SB_WIDTH), row(SSD_WIDTH),
                  pl.BlockSpec((tm, D_MODEL), lambda i: (i, gate_blk)),
                  pl.BlockSpec((tm, D_MODEL), lambda i: (i, gate_blk + 1)),
                  row(D_MODEL)] + [_resident(w) for w in ws],
        out_specs=row(D_MODEL),
        out_shape=jax.ShapeDtypeStruct((m, D_MODEL), F32),
        scratch_shapes=[pltpu.VMEM(w.shape, BF16) for w in ws],
        compiler_params=_cparams(("arbitrary",)),
        name="merge_out_proj",
    )(att, ssd, big, big, h, *ws)


def _ffn_act_kernel(ug_ref, uv_ref, pg_ref, pv_ref, wg_ref, wv_ref, bg_ref, bv_ref, o_ref, gbuf_ref, vbuf_ref,
                    *, ts):
    t = pl.program_id(1)
    k1 = FFN_CONV - 1

    def conv(u_ref, p_ref, w_ref, b_ref, buf_ref):
        @pl.when(t == 0)
        def _():
            buf_ref[8 - k1:8, :] = p_ref[0]

        buf_ref[8:8 + ts, :] = u_ref[0].astype(F32)
        y = b_ref[...]
        for i in range(FFN_CONV):
            y = y + buf_ref[8 - k1 + i:8 - k1 + i + ts, :] * w_ref[i:i + 1, :]
        buf_ref[8 - k1:8, :] = buf_ref[8 + ts - k1:8 + ts, :]
        return y

    gate = conv(ug_ref, pg_ref, wg_ref, bg_ref, gbuf_ref)
    val = conv(uv_ref, pv_ref, wv_ref, bv_ref, vbuf_ref)
    o_ref[0] = (gate * _sigmoid(gate) * val).astype(o_ref.dtype)


FFN_SUB = 64


def _ffn_act_long_kernel(ug_ref, uv_ref, g16_ref, v16_ref, pg_ref, pv_ref, wg_ref, wv_ref, bg_ref, bv_ref,
                         s_ref, o_ref, *, ts):
    t = pl.program_id(1)
    zeros = jnp.zeros((FFN_SUB - 16, D_FF), BF16)

    def conv(u_ref, b16_ref, p_ref, w_ref, b_ref, j):
        if j == 0:
            before = jnp.where(t == 0, p_ref[0], b16_ref[0])
            window = jnp.concatenate([zeros, before, u_ref[0, 0:FFN_SUB, :]], axis=0)
        else:
            window = u_ref[0, (j - 1) * FFN_SUB:(j + 1) * FFN_SUB, :]
        sh = _dot(s_ref[...], window)
        y = _rows8(jnp.multiply, sh[0:FFN_SUB], w_ref[0:8, :])
        for i in range(1, FFN_CONV):
            y = y + _rows8(jnp.multiply, sh[i * FFN_SUB:(i + 1) * FFN_SUB], w_ref[8 * i:8 * i + 8, :])
        return _rows8(jnp.add, y, b_ref[...])

    for j in range(ts // FFN_SUB):
        gate = conv(ug_ref, g16_ref, pg_ref, wg_ref, bg_ref, j)
        val = conv(uv_ref, v16_ref, pv_ref, wv_ref, bv_ref, j)
        o_ref[0, j * FFN_SUB:(j + 1) * FFN_SUB, :] = (gate * _sigmoid(gate) * val).astype(o_ref.dtype)


def ffn_act_long(up, prev, p):
    b, s, _ = up.shape
    ts = 512
    prev16 = jnp.pad(prev, ((0, 0), (16 - (FFN_CONV - 1), 0), (0, 0)))
    sel = np.zeros((FFN_CONV * FFN_SUB, 2 * FFN_SUB), np.float32)
    for i in range(FFN_CONV):
        for r in range(FFN_SUB):
            sel[i * FFN_SUB + r, FFN_SUB + r - (FFN_CONV - 1 - i)] = 1.0
    rep8 = lambda v: jnp.repeat(v, 8, axis=0)
    half = lambda blk: pl.BlockSpec((1, ts, D_FF), lambda bi, ti: (bi, ti, blk))
    before = lambda blk: pl.BlockSpec((1, 16, D_FF), lambda bi, ti: (bi, jnp.maximum(ti * (ts // 16) - 1, 0), blk))
    prevs = lambda blk: pl.BlockSpec((1, 16, D_FF), lambda bi, ti: (bi, 0, blk))
    wspec = lambda blk: pl.BlockSpec((8 * FFN_CONV, D_FF), lambda bi, ti: (0, blk))
    bspec = lambda blk: pl.BlockSpec((8, D_FF), lambda bi, ti: (0, blk))
    w8, b8 = rep8(p['conv_w_ffn']), rep8(p['conv_b_ffn'])
    return pl.pallas_call(
        functools.partial(_ffn_act_long_kernel, ts=ts),
        grid=(b, s // ts),
        in_specs=[half(0), half(1), before(0), before(1), prevs(0), prevs(1), wspec(0), wspec(1), bspec(0),
                  bspec(1), pl.BlockSpec(sel.shape, lambda bi, ti: (0, 0))],
        out_specs=pl.BlockSpec((1, ts, D_FF), lambda bi, ti: (bi, ti, 0)),
        out_shape=jax.ShapeDtypeStruct((b, s, D_FF), BF16),
        compiler_params=_cparams(("parallel", "parallel")),
        name="ffn_conv_act_long",
    )(up, up, up, up, prev16, prev16, w8, w8, b8, b8, jnp.asarray(sel, BF16))


def ffn_act(up, prev, p):
    b, s, _ = up.shape
    ts = _pick(s, (512, 256, 128))
    half = lambda blk: pl.BlockSpec((1, ts, D_FF), lambda bi, ti: (bi, ti, blk))
    prevs = lambda blk: pl.BlockSpec((1, FFN_CONV - 1, D_FF), lambda bi, ti: (bi, 0, blk))
    wspec = lambda blk: pl.BlockSpec((FFN_CONV, D_FF), lambda bi, ti: (0, blk))
    bspec = lambda blk: pl.BlockSpec((1, D_FF), lambda bi, ti: (0, blk))
    return pl.pallas_call(
        functools.partial(_ffn_act_kernel, ts=ts),
        grid=(b, s // ts),
        in_specs=[half(0), half(1), prevs(0), prevs(1), wspec(0), wspec(1), bspec(0), bspec(1)],
        out_specs=pl.BlockSpec((1, ts, D_FF), lambda bi, ti: (bi, ti, 0)),
        out_shape=jax.ShapeDtypeStruct((b, s, D_FF), BF16),
        scratch_shapes=[pltpu.VMEM((8 + ts, D_FF), F32)] * 2,
        compiler_params=_cparams(("parallel", "arbitrary")),
        name="ffn_conv_act",
    )(up, up, prev, prev, p['conv_w_ffn'], p['conv_w_ffn'], p['conv_b_ffn'], p['conv_b_ffn'])


def _down_kernel(a_ref, w_ref, h_ref, g_ref, o_ref, w_b):
    @pl.when(pl.program_id(0) == 0)
    def _():
        w_b[...] = w_ref[...].astype(BF16)

    h = h_ref[...] + _dot(a_ref[...], w_b[...])
    ms = jnp.mean(h * h, axis=-1, keepdims=True)
    o_ref[...] = h * lax.rsqrt(ms + EPS) * g_ref[...]


def down_norm(act, h, p):
    m = h.shape[0]
    tm = _pick(m, (512, 256, 128))
    return pl.pallas_call(
        _down_kernel,
        grid=(m // tm,),
        in_specs=[pl.BlockSpec((tm, D_FF), lambda i: (i, 0)),
                  _resident(p['w_down']),
                  pl.BlockSpec((tm, D_MODEL), lambda i: (i, 0)),
                  pl.BlockSpec((1, D_MODEL), lambda i: (0, 0))],
        out_specs=pl.BlockSpec((tm, D_MODEL), lambda i: (i, 0)),
        out_shape=jax.ShapeDtypeStruct((m, D_MODEL), F32),
        scratch_shapes=[pltpu.VMEM(p['w_down'].shape, BF16)],
        compiler_params=_cparams(("arbitrary",)),
        name="down_proj_norm",
    )(act, p['w_down'], h, p['g_final'])


def mix_and_ffn(big3, att, dt_raw, hf, ssd_h0, ssd_conv_prev, ffn_conv_prev, p):
    b, s, _ = big3.shape
    m = b * s
    ssd, ssd_state = ssd_mixer(big3, BIG_XBC // SSD_CONV_CH, BIG_Z // SSD_WIDTH, dt_raw, ssd_conv_prev, ssd_h0, p)
    h1 = merge(att.reshape(m, SB_WIDTH), ssd.reshape(m, SSD_WIDTH), big3.reshape(m, BIG_W), hf, p)
    up = norm_mm(h1, p['g_ffn'], p['w_up'], BF16).reshape(b, s, 2 * D_FF)
    if s % 512 == 0:
        act = ffn_act_long(up, ffn_conv_prev.astype(BF16), p)
    else:
        act = ffn_act(up, ffn_conv_prev, p)
    y = down_norm(act.reshape(m, D_FF), h1, p).reshape(b, s, D_MODEL)
    xbc_rows = jnp.concatenate([ssd_conv_prev, big3[:, :, BIG_XBC:BIG_XBC + SSD_CONV_CH][:, -(SSD_CONV - 1):].astype(F32)],
                               axis=1)[:, -(SSD_CONV - 1):]
    up_rows = jnp.concatenate([ffn_conv_prev, up[:, -(FFN_CONV - 1):].astype(F32)], axis=1)[:, -(FFN_CONV - 1):]
    return y, ssd_state, xbc_rows, up_rows


def _prep_params(g_mix, w_in, conv_w_ssd, conv_b_ssd, dt_bias, a_log, d_skip, g_ssd_norm, w_br_att, w_br_ssd,
                 w_out, g_ffn, w_up, conv_w_ffn, conv_b_ffn, w_down, g_final):
    w_t = w_in.T
    dt0 = 3 * SB_WIDTH + SSD_WIDTH + SSD_CONV_CH
    w_tail_t = jnp.concatenate([w_t[dt0 + SSD_HEADS:], w_t[dt0:dt0 + SSD_HEADS],
                                jnp.zeros((LANES - SSD_HEADS, D_MODEL), F32)], axis=0)
    Q = SSD_CHUNK
    lanes_t = lambda v: jnp.repeat(v.reshape(SSD_HEADS // 2, 2, 1), Q, axis=2).reshape(SSD_HEADS // 2, 2 * Q)
    pad_l = lambda v: jnp.pad(v, (0, LANES - SSD_HEADS)).reshape(1, LANES)
    return {
        'g_mix': g_mix.reshape(1, -1), 'g_ffn': g_ffn.reshape(1, -1), 'g_final': g_final.reshape(1, -1),
        'w_t': w_t, 'w_tail_t': w_tail_t,
        'conv_w_ssd': conv_w_ssd, 'conv_b_ssd': conv_b_ssd.reshape(1, -1),
        'dt_bias': pad_l(dt_bias), 'dt_bias_t': lanes_t(dt_bias),
        'a_log': pad_l(a_log), 'a_log_t': lanes_t(a_log),
        'd_skip': jnp.repeat(d_skip, SSD_HEAD_DIM).reshape(1, -1), 'g_ssd_norm': g_ssd_norm.reshape(1, -1),
        'w_br_att': w_br_att, 'w_br_ssd': w_br_ssd, 'w_out': w_out,
        'w_up': w_up, 'conv_w_ffn': conv_w_ffn, 'conv_b_ffn': conv_b_ffn.reshape(1, -1),
        'w_down': w_down,
    }


def kernel(x_prompt, x_sample, cache_k, cache_v, state_ssm, state_ssm_conv, state_ffn_conv, meta_tokens, g_mix, w_in, conv_w_ssd, conv_b_ssd, dt_bias, a_log, d_skip, g_ssd_norm, w_br_att, w_br_ssd, w_out, g_ffn, w_up, conv_w_ffn, conv_b_ffn, w_down, g_final):
    bp, seq, _ = x_prompt.shape
    bs, dec = x_sample.shape[:2]
    assert dec == N_META and seq % KVT_TS == 0
    p = _prep_params(g_mix[0], w_in[0], conv_w_ssd[0], conv_b_ssd[0], dt_bias[0], a_log[0], d_skip[0],
                     g_ssd_norm[0], w_br_att[0], w_br_ssd[0], w_out[0], g_ffn[0], w_up[0], conv_w_ffn[0],
                     conv_b_ffn[0], w_down[0], g_final)
    proj = functools.partial(in_proj, g=p['g_mix'], w_t=p['w_t'], w_tail_t=p['w_tail_t'])
    state_shape = (SSD_WIDTH, SSD_STATE)
    rows = lambda a, b: a.reshape(b, -1, a.shape[-1])

    k_m, v_m, big_m, dt_m = proj(meta_tokens, with_kv=True)
    big_m, k_m, v_m = big_m[None], k_m[None], v_m[None]
    att_m = attention_step(big_m, 0, k_m, v_m, None, None, 0, SB_WIDTH)
    _, ssm_m, conv_m, ffn_m = mix_and_ffn(
        big_m, att_m, dt_m[None], meta_tokens, jnp.zeros((1,) + state_shape, F32),
        jnp.zeros((1, SSD_CONV - 1, SSD_CONV_CH), F32), jnp.zeros((1, FFN_CONV - 1, 2 * D_FF), F32), p)

    xf = x_prompt.reshape(bp * seq, D_MODEL)
    big_x, dt_x = proj(xf, with_kv=False)
    big_x = rows(big_x, bp)
    k_t, v_t, k_end, v_end = kv_transposed(x_prompt, meta_tokens, p['g_mix'], p['w_t'])
    att_x = attention_prompt(big_x, big_m, k_t, v_t)
    att_end = attention_step(big_x, seq // N_META - 1, k_end, v_end, k_t, v_t, seq // KEY_TILE, ATTN_PROMPT_WIDTH)
    att_x = lax.dynamic_update_slice(att_x, att_end, (0, seq - N_META, 0))
    rep = lambda a: jnp.broadcast_to(a, (bp,) + a.shape[1:])
    y_prompt, ssm_p, conv_p, ffn_p = mix_and_ffn(
        big_x, att_x, rows(dt_x, bp), xf, rep(ssm_m), rep(conv_m), rep(ffn_m), p)

    n_rows = cache_k.shape[2]
    n_tiles = (n_rows - N_META) // KEY_TILE
    assert n_tiles * KEY_TILE + N_META == n_rows
    k_s, v_s, big_s, dt_s = proj(x_sample.reshape(bs * dec, D_MODEL), with_kv=True)
    big_s, k_s, v_s = rows(big_s, bs), rows(k_s, bs), rows(v_s, bs)
    cache_t = lambda c: c[0].transpose(0, 2, 3, 1).reshape(bs, SB_WIDTH, n_rows)
    nearest = lambda c, new: jnp.concatenate([c[0][:, n_tiles * KEY_TILE:].reshape(bs, N_META, SB_WIDTH), new], axis=1)
    att_s = attention_step(big_s, 0, nearest(cache_k, k_s), nearest(cache_v, v_s), cache_t(cache_k), cache_t(cache_v),
                           n_tiles, SB_WIDTH)
    y_sample, ssm_s, conv_s, ffn_s = mix_and_ffn(
        big_s, att_s, rows(dt_s, bs), x_sample.reshape(bs * dec, D_MODEL),
        state_ssm[0].reshape((bs,) + state_shape), state_ssm_conv[0], state_ffn_conv[0], p)

    heads = lambda a: a.reshape(a.shape[0], a.shape[1], SB_HEADS, SB_HEAD_DIM)[None]
    heads_t = lambda a: a.reshape(a.shape[0], SB_HEADS, SB_HEAD_DIM, a.shape[2]).transpose(0, 3, 1, 2)[None]
    state5 = lambda a: a.reshape(a.shape[0], SSD_HEADS, SSD_HEAD_DIM, SSD_STATE)[None]
    return (y_prompt, y_sample, heads_t(k_t), heads_t(v_t), state5(ssm_p), conv_p[None], ffn_p[None],
            heads(k_s), heads(v_s), state5(ssm_s), conv_s[None], ffn_s[None])
```

```python
import functools

import numpy as np
import jax
import jax.numpy as jnp
from jax import lax
from jax.experimental import pallas as pl
from jax.experimental.pallas import tpu as pltpu

F32 = jnp.float32
BF16 = jnp.bfloat16

D_MODEL = 1024
N_META = 16
SB_HEADS = 16
SB_HEAD_DIM = 64
SB_WIDTH = SB_HEADS * SB_HEAD_DIM
SSD_WIDTH = 2 * D_MODEL
SSD_HEAD_DIM = 64
SSD_HEADS = SSD_WIDTH // SSD_HEAD_DIM
SSD_GROUPS = 4
SSD_STATE = 128
SSD_CONV = 4
SSD_CONV_CH = SSD_WIDTH + 2 * SSD_GROUPS * SSD_STATE
D_FF = 2816
FFN_CONV = 3
EPS = 1e-6

LANES = 128
SSD_CHUNK = 64
SSD_STEP_CHUNKS = 2
KEY_TILE = 128
GROUP_W = SSD_WIDTH // SSD_GROUPS
DEAD_LOG = -104.0
LOG2E = 1.4426950408889634
ATTN_STATIC_TILES = 2
VMEM_LIMIT = 56 * 1024 * 1024


def _cparams(sem):
    return pltpu.CompilerParams(dimension_semantics=sem, vmem_limit_bytes=VMEM_LIMIT)


def _pick(n, cands):
    for c in cands:
        if n % c == 0:
            return c
    return n


def _split2(x):
    hi = x.astype(BF16)
    lo = (x - hi.astype(F32)).astype(BF16)
    return hi, lo


def _split3(x):
    hi = x.astype(BF16)
    r = x - hi.astype(F32)
    mid = r.astype(BF16)
    lo = (r - mid.astype(F32)).astype(BF16)
    return hi, mid, lo


def _dot(a, b):
    return jnp.dot(a, b, preferred_element_type=F32)


def _dot_nt(a, b):
    return lax.dot_general(a, b, (((1,), (1,)), ((), ())), preferred_element_type=F32)


def _softplus(x):
    return jnp.maximum(x, 0.0) + jnp.log(1.0 + jnp.exp(-jnp.abs(x)))


def _sigmoid(x):
    return 1.0 / (1.0 + jnp.exp(-x))


def _rms_norm_bf16(x, g):
    ms = jnp.mean(x * x, axis=-1, keepdims=True)
    return (x * lax.rsqrt(ms + EPS) * g).astype(BF16)


def _norm_mm_kernel(x_ref, g_ref, w_ref, o_ref, u_ref):
    @pl.when(pl.program_id(1) == 0)
    def _():
        u_ref[...] = _rms_norm_bf16(x_ref[...], g_ref[...])

    o_ref[...] = _dot(u_ref[...], w_ref[...].astype(BF16)).astype(o_ref.dtype)


PROJ_TN = 1024
UP_TN = 1408


def norm_mm(x, g, w, out_dtype):
    m, d = x.shape
    n = w.shape[1]
    tm = _pick(m, (2048, 1024, 512, 256, 128))
    tn = UP_TN
    return pl.pallas_call(
        _norm_mm_kernel,
        grid=(m // tm, n // tn),
        in_specs=[pl.BlockSpec((tm, d), lambda i, j: (i, 0)),
                  pl.BlockSpec((1, d), lambda i, j: (0, 0)),
                  pl.BlockSpec((d, tn), lambda i, j: (0, j))],
        out_specs=pl.BlockSpec((tm, tn), lambda i, j: (i, j)),
        out_shape=jax.ShapeDtypeStruct((m, n), out_dtype),
        scratch_shapes=[pltpu.VMEM((tm, d), BF16)],
        compiler_params=_cparams(("parallel", "arbitrary")),
        name="norm_up_proj",
    )(x, g, w)


_NQ = SB_WIDTH // PROJ_TN
_J_K, _J_V, _J_Z = _NQ, 2 * _NQ, 3 * _NQ
_J_X = _J_Z + SSD_WIDTH // PROJ_TN
_J_G = _J_X + SSD_CONV_CH // PROJ_TN
_J_END = _J_G + 2 * D_MODEL // PROJ_TN
BIG_Z, BIG_Q, BIG_XBC, BIG_GATE = 0, SSD_WIDTH, SSD_WIDTH + SB_WIDTH, SSD_WIDTH + SB_WIDTH + SSD_CONV_CH
BIG_W = BIG_GATE + 2 * D_MODEL


def _big_block(j):
    return jnp.where(j < _J_K, BIG_Q // PROJ_TN + j,
                     jnp.where(j < _J_Z, BIG_Q // PROJ_TN + _NQ - 1,
                               jnp.where(j < _J_X, j - _J_Z + BIG_Z // PROJ_TN,
                                         jnp.where(j < _J_G, j - _J_X + BIG_XBC // PROJ_TN,
                                                   j - _J_G + BIG_GATE // PROJ_TN))))


def _sweep_step(j, with_kv):
    return j if with_kv else jnp.where(j >= _J_K, j + (_J_Z - _J_K), j)


def _in_proj_kernel(*refs, with_kv):
    if with_kv:
        x_ref, g_ref, win_ref, wgate_ref, wdt_ref, k_ref, v_ref, big_ref, dt_ref, u_ref = refs
    else:
        x_ref, g_ref, win_ref, wgate_ref, wdt_ref, big_ref, dt_ref, u_ref = refs
    j = _sweep_step(pl.program_id(1), with_kv)

    def proj(w_ref):
        return _dot_nt(u_ref[...], w_ref[...].astype(BF16))

    @pl.when(pl.program_id(1) == 0)
    def _():
        u_ref[...] = _rms_norm_bf16(x_ref[...], g_ref[...])
        dt_ref[...] = proj(wdt_ref)

    if with_kv:
        @pl.when(jnp.logical_and(j >= _J_K, j < _J_V))
        def _():
            k_ref[...] = proj(win_ref)

        @pl.when(jnp.logical_and(j >= _J_V, j < _J_Z))
        def _():
            v_ref[...] = proj(win_ref)

    @pl.when(jnp.logical_or(j < _J_K, jnp.logical_and(j >= _J_Z, j < _J_G)))
    def _():
        big_ref[...] = proj(win_ref).astype(BF16)

    @pl.when(j >= _J_G)
    def _():
        big_ref[...] = proj(wgate_ref).astype(BF16)


def in_proj(x, g, w_t, w_tail_t, with_kv):
    m, d = x.shape
    tm = _pick(m, (2048, 1024, 512, 256, 128))
    tn = PROJ_TN
    step = functools.partial(_sweep_step, with_kv=with_kv)
    clip = lambda j, lo, n: jnp.clip(step(j) - lo, 0, n - 1)
    kv_specs = [pl.BlockSpec((tm, tn), lambda i, j: (i, clip(j, _J_K, _NQ))),
                pl.BlockSpec((tm, tn), lambda i, j: (i, clip(j, _J_V, _NQ)))]
    kv_shapes = [jax.ShapeDtypeStruct((m, SB_WIDTH), F32)] * 2
    return pl.pallas_call(
        functools.partial(_in_proj_kernel, with_kv=with_kv),
        grid=(m // tm, _J_END if with_kv else _J_END - (_J_Z - _J_K)),
        in_specs=[pl.BlockSpec((tm, d), lambda i, j: (i, 0), pipeline_mode=pl.Buffered(1)),
                  pl.BlockSpec((1, d), lambda i, j: (0, 0)),
                  pl.BlockSpec((tn, d), lambda i, j: (jnp.minimum(step(j), _J_G - 1), 0)),
                  pl.BlockSpec((tn, d), lambda i, j: (clip(j, _J_G, _J_END - _J_G), 0)),
                  pl.BlockSpec((LANES, d), lambda i, j: (2 * D_MODEL // LANES, 0))],
        out_specs=(kv_specs if with_kv else []) + [
            pl.BlockSpec((tm, tn), lambda i, j: (i, _big_block(step(j)))),
            pl.BlockSpec((tm, LANES), lambda i, j: (i, 0))],
        out_shape=(kv_shapes if with_kv else []) + [
            jax.ShapeDtypeStruct((m, BIG_W), BF16), jax.ShapeDtypeStruct((m, LANES), F32)],
        scratch_shapes=[pltpu.VMEM((tm, d), BF16)],
        compiler_params=_cparams(("parallel", "arbitrary")),
        name="in_proj",
    )(x, g, w_t, w_tail_t, w_tail_t)


KVT_TS = 1024


def _kv_t_kernel(x16_ref, xt_ref, meta_ref, g_ref, wk_ref, wv_ref, kt_ref, vt_ref, kl_ref, vl_ref, wk_b, wv_b):
    j = pl.program_id(1)

    @pl.when(jnp.logical_and(pl.program_id(0) == 0, j == 0))
    def _():
        wk_b[...] = wk_ref[...].astype(BF16)
        wv_b[...] = wv_ref[...].astype(BF16)

    first = jnp.where(j == 0, meta_ref[...], x16_ref[0])
    x = jnp.concatenate([first, xt_ref[0, :KVT_TS - N_META, :]], axis=0)
    u = _rms_norm_bf16(x, g_ref[...])
    kt_ref[0] = _dot_nt(wk_b[...], u)
    vt_ref[0] = _dot_nt(wv_b[...], u)

    @pl.when(j == pl.num_programs(1) - 1)
    def _():
        kl_ref[0] = _dot_nt(u[0:N_META, :], wk_b[...])
        vl_ref[0] = _dot_nt(u[0:N_META, :], wv_b[...])


def kv_transposed(x, meta, g, w_t):
    b, s, d = x.shape
    n_pos = N_META + s
    n_tiles = -(-n_pos // KVT_TS)
    assert n_pos - (n_tiles - 1) * KVT_TS == N_META
    per_tile = KVT_TS // N_META
    out_spec = pl.BlockSpec((1, SB_WIDTH, KVT_TS), lambda bi, j: (bi, 0, j))
    last_spec = pl.BlockSpec((1, N_META, SB_WIDTH), lambda bi, j: (bi, 0, 0))
    w_spec = lambda blk: pl.BlockSpec((SB_WIDTH, d), lambda bi, j: (blk, 0), pipeline_mode=pl.Buffered(1))
    return pl.pallas_call(
        _kv_t_kernel,
        grid=(b, n_tiles),
        in_specs=[pl.BlockSpec((1, N_META, d), lambda bi, j: (bi, jnp.maximum(j * per_tile - 1, 0), 0)),
                  pl.BlockSpec((1, KVT_TS, d), lambda bi, j: (bi, jnp.minimum(j, s // KVT_TS - 1), 0)),
                  pl.BlockSpec((N_META, d), lambda bi, j: (0, 0)),
                  pl.BlockSpec((1, d), lambda bi, j: (0, 0)),
                  w_spec(1), w_spec(2)],
        out_specs=[out_spec, out_spec, last_spec, last_spec],
        out_shape=[jax.ShapeDtypeStruct((b, SB_WIDTH, n_pos), F32)] * 2
        + [jax.ShapeDtypeStruct((b, N_META, SB_WIDTH), F32)] * 2,
        scratch_shapes=[pltpu.VMEM((SB_WIDTH, d), BF16)] * 2,
        compiler_params=_cparams(("arbitrary", "arbitrary")),
        name="kv_transposed",
    )(x, x, meta, g, w_t, w_t)


def _stick_breaking_tiles(qs, u2, tiles, carries, accs):
    n_heads = len(qs)
    units = [(t, h) for t in range(len(tiles)) for h in range(n_heads)]
    ps = lambda h: slice((h // 2) * LANES, (h // 2 + 1) * LANES)

    def scores(t, h):
        kind, keys = tiles[t][0], tiles[t][1]
        return _dot_nt(qs[h], keys[:, ps(h)]) if kind == 'rows' else _dot(qs[h], keys[ps(h), :])

    z = {u: scores(*u) for u in units}
    log_beta, cat = {}, {}
    for u in units:
        mask = tiles[u[0]][3]
        soft = jnp.log(1.0 + jnp.exp(-jnp.abs(z[u])))
        log_beta[u] = jnp.minimum(z[u], 0.0) - soft
        log_keep = log_beta[u] - z[u]
        if mask is not None:
            log_keep = jnp.where(mask, log_keep, 0.0)
        cat[u] = jnp.concatenate(_split2(log_keep), axis=1)
    r = {u: _dot(cat[u], u2) for u in units}
    carries, accs = list(carries), list(accs)
    w = {}
    for t, h in units:
        mask = tiles[t][3]
        wt = jnp.exp(log_beta[(t, h)] + r[(t, h)][:, :LANES] + carries[h])
        if mask is not None:
            wt = jnp.where(mask, wt, 0.0)
        w[(t, h)] = wt.astype(BF16)
        carries[h] = carries[h] + r[(t, h)][:, LANES:]
    for t, h in units:
        kind, vals = tiles[t][0], tiles[t][2]
        pv = _dot(w[(t, h)], vals[:, ps(h)]) if kind == 'rows' else _dot_nt(w[(t, h)], vals[ps(h), :])
        accs[h] = accs[h] + pv
    return carries, accs


def _head_queries(q, lo_half):
    qs = []
    for h in range(q.shape[1] // SB_HEAD_DIM):
        qp = q[:, (h // 2) * LANES:(h // 2 + 1) * LANES]
        keep = lo_half if h % 2 == 0 else jnp.logical_not(lo_half)
        qs.append(jnp.where(keep, qp, jnp.zeros_like(qp)) * jnp.asarray(SB_HEAD_DIM ** -0.5, BF16))
    return qs


def _any_alive(carries):
    m = carries[0]
    for c in carries[1:]:
        m = jnp.maximum(m, c)
    return (jnp.max(m) > DEAD_LOG).astype(jnp.int32)


def _attend(qs, u2, first_tiles, load_tile, j0, lane):
    tq = qs[0].shape[0]

    def cache_tile(j):
        if isinstance(j, int):
            mask = None if j >= 0 else lane < 0
            kt, vt = load_tile(max(j, 0))
        else:
            mask = jnp.logical_and(j >= 0, lane >= 0)
            kt, vt = load_tile(jnp.maximum(j, 0))
        return 'cols', kt.astype(BF16), vt.astype(BF16), mask

    tiles = list(first_tiles)
    if load_tile is not None:
        for _ in range(ATTN_STATIC_TILES):
            tiles.append(cache_tile(j0))
            j0 = j0 - 1
    zeros = jnp.zeros((tq, LANES), F32)
    carries, accs = _stick_breaking_tiles(qs, u2, tiles, [zeros] * len(qs), [zeros] * len(qs))
    if load_tile is not None:
        def cond(s):
            return jnp.logical_and(s[0] >= 0, s[1] > 0)

        def body(s):
            j, _, carries, accs = s
            carries, accs = _stick_breaking_tiles(qs, u2, [cache_tile(j)], carries, accs)
            return j - 1, _any_alive(carries), tuple(carries), tuple(accs)

        state = (jnp.asarray(j0, jnp.int32), _any_alive(carries), tuple(carries), tuple(accs))
        _, _, carries, accs = lax.while_loop(cond, body, state)
    return accs


def _pair_outputs(accs, lo_half, dtype):
    return [jnp.where(lo_half, accs[2 * p], accs[2 * p + 1]).astype(dtype) for p in range(len(accs) // 2)]


def _attn_prompt_kernel(q16_ref, qt_ref, qm_ref, u2_ref, kt_ref, vt_ref, o_ref):
    m = pl.program_id(2)
    tq = KEY_TILE
    lane = lax.broadcasted_iota(jnp.int32, (tq, LANES), 1)
    row = lax.broadcasted_iota(jnp.int32, (tq, LANES), 0)
    lo_half = lane < SB_HEAD_DIM
    first = jnp.where(m == 0, qm_ref[0], q16_ref[0])
    q = jnp.concatenate([first, qt_ref[0, :tq - N_META, :]], axis=0)
    qs = _head_queries(q, lo_half)
    def load_tile(j):
        off = pl.multiple_of(j * KEY_TILE, KEY_TILE)
        return kt_ref[0, :, pl.ds(off, KEY_TILE)], vt_ref[0, :, pl.ds(off, KEY_TILE)]

    kd, vd = load_tile(m)
    diag = ('cols', kd.astype(BF16), vd.astype(BF16), lane < row)
    accs = _attend(qs, u2_ref[...], [diag], load_tile, m - 1, lane)
    out = jnp.concatenate(_pair_outputs(accs, lo_half, o_ref.dtype), axis=1)

    @pl.when(m == 0)
    def _():
        o_ref[0, 0:tq - N_META, :] = out[N_META:, :]
        o_ref[0, o_ref.shape[1] - N_META:, :] = jnp.zeros((N_META, o_ref.shape[2]), o_ref.dtype)

    @pl.when(m > 0)
    def _():
        o_ref[0, pl.ds(pl.multiple_of(m * KEY_TILE - N_META, N_META), tq), :] = out


def _attn_step_kernel(*refs, n_free, n_cache_tiles):
    if n_cache_tiles:
        (q_ref, kd_ref, vd_ref, u2_ref, kn_ref, vn_ref, kt_any, vt_any, o_ref,
         kd_scr, vd_scr, kbuf, vbuf) = refs
        width = q_ref.shape[2]
        n_near = kn_ref.shape[2] // KEY_TILE

        def load_tile(j):
            if isinstance(j, int) and j >= n_cache_tiles - n_near:
                ls = slice((j - (n_cache_tiles - n_near)) * KEY_TILE, (j - (n_cache_tiles - n_near) + 1) * KEY_TILE)
                return kn_ref[0, :, ls], vn_ref[0, :, ls]
            off = j * KEY_TILE if isinstance(j, int) else pl.multiple_of(j * KEY_TILE, KEY_TILE)
            src = (pl.program_id(0), pl.ds(pl.program_id(1) * width, width), pl.ds(off, KEY_TILE))
            pltpu.sync_copy(kt_any.at[src], kbuf)
            pltpu.sync_copy(vt_any.at[src], vbuf)
            return kbuf[...], vbuf[...]
    else:
        q_ref, kd_ref, vd_ref, u2_ref, o_ref, kd_scr, vd_scr = refs
        load_tile = None
    tq = q_ref.shape[1]
    n_rows = kd_ref.shape[1]
    lane = lax.broadcasted_iota(jnp.int32, (tq, LANES), 1)
    row = lax.broadcasted_iota(jnp.int32, (tq, LANES), 0)
    lo_half = lane < SB_HEAD_DIM
    kd_scr[...] = jnp.zeros_like(kd_scr)
    vd_scr[...] = jnp.zeros_like(vd_scr)
    kd_scr[0:n_rows, :] = kd_ref[0].astype(BF16)
    vd_scr[0:n_rows, :] = vd_ref[0].astype(BF16)
    diag = ('rows', kd_scr[...], vd_scr[...], lane < row + n_free)
    qs = _head_queries(q_ref[0], lo_half)
    accs = _attend(qs, u2_ref[...], [diag], load_tile, n_cache_tiles - 1, lane)
    for p, out in enumerate(_pair_outputs(accs, lo_half, o_ref.dtype)):
        o_ref[0, :, p * LANES:(p + 1) * LANES] = out


def _cumsum_rhs():
    s = np.arange(KEY_TILE)
    strict = (s[:, None] > s[None, :]).astype(np.float32)
    half = np.concatenate([strict, np.ones((KEY_TILE, KEY_TILE), np.float32)], axis=1)
    return jnp.asarray(np.concatenate([half, half], axis=0), BF16)


ATTN_PROMPT_WIDTH = 8 * SB_HEAD_DIM


def attention_prompt(big, big_meta, k_t, v_t):
    b, s, _ = big.shape
    width = ATTN_PROMPT_WIDTH
    qb0 = BIG_Q // width
    per_tile = KEY_TILE // N_META
    n_pos = k_t.shape[2]
    cache = pl.BlockSpec((1, width, n_pos), lambda bi, hp, m: (bi, hp, 0))
    return pl.pallas_call(
        _attn_prompt_kernel,
        grid=(b, SB_WIDTH // width, s // KEY_TILE),
        in_specs=[pl.BlockSpec((1, N_META, width), lambda bi, hp, m: (bi, jnp.maximum(m * per_tile - 1, 0), qb0 + hp)),
                  pl.BlockSpec((1, KEY_TILE, width), lambda bi, hp, m: (bi, m, qb0 + hp)),
                  pl.BlockSpec((1, N_META, width), lambda bi, hp, m: (0, 0, qb0 + hp)),
                  pl.BlockSpec((2 * KEY_TILE, 2 * KEY_TILE), lambda bi, hp, m: (0, 0)),
                  cache, cache],
        out_specs=pl.BlockSpec((1, s, width), lambda bi, hp, m: (bi, 0, hp)),
        out_shape=jax.ShapeDtypeStruct((b, s, SB_WIDTH), BF16),
        compiler_params=_cparams(("parallel", "parallel", "arbitrary")),
        name="stick_breaking_prompt",
    )(big, big, big_meta, _cumsum_rhs(), k_t, v_t)


def attention_step(big, q_row_blk, k_rows, v_rows, k_t, v_t, n_cache_tiles, width):
    b = big.shape[0]
    n_rows = k_rows.shape[1]
    qb0 = BIG_Q // width
    rows = pl.BlockSpec((1, n_rows, width), lambda bi, hp: (bi, 0, hp))
    in_specs = [pl.BlockSpec((1, N_META, width), lambda bi, hp: (bi, q_row_blk, qb0 + hp)), rows, rows,
                pl.BlockSpec((2 * KEY_TILE, 2 * KEY_TILE), lambda bi, hp: (0, 0))]
    args = [big, k_rows, v_rows, _cumsum_rhs()]
    scratch = [pltpu.VMEM((KEY_TILE, width), BF16)] * 2
    if n_cache_tiles:
        near = ATTN_STATIC_TILES * KEY_TILE
        assert n_cache_tiles % ATTN_STATIC_TILES == 0
        near_blk = n_cache_tiles // ATTN_STATIC_TILES - 1
        in_specs += [pl.BlockSpec((1, width, near), lambda bi, hp: (bi, hp, near_blk))] * 2
        in_specs += [pl.BlockSpec(memory_space=pl.ANY)] * 2
        args += [k_t, v_t, k_t, v_t]
        scratch += [pltpu.VMEM((width, KEY_TILE), F32)] * 2
    return pl.pallas_call(
        functools.partial(_attn_step_kernel, n_free=n_rows - N_META, n_cache_tiles=n_cache_tiles),
        grid=(b, SB_WIDTH // width),
        in_specs=in_specs,
        out_specs=pl.BlockSpec((1, N_META, width), lambda bi, hp: (bi, 0, hp)),
        out_shape=jax.ShapeDtypeStruct((b, N_META, SB_WIDTH), BF16),
        scratch_shapes=scratch,
        compiler_params=_cparams(("parallel", "parallel")),
        name="stick_breaking_step",
    )(*args)


def _rows8(op, x, r8):
    return op(x.reshape(x.shape[0] // 8, 8, x.shape[1]), r8[None]).reshape(x.shape)


def _ssd_kernel(xbc_ref, x16_ref, z_ref, dt_ref, dtT_ref, prev_ref, h0_ref, cw_ref, cb_ref, dtb_ref, dtbT_ref,
                alog_ref, alogT_ref, dskip_ref, gn_ref, e3_ref, ltri_ref, ublk_ref, shift_ref,
                y_ref, hfin_ref, st_ref, *, rows, n_sub, n_chunks):
    Q = SSD_CHUNK
    c = pl.program_id(1)
    n_blk = SSD_WIDTH // LANES
    mul, sub = jnp.multiply, jnp.subtract

    @pl.when(c == 0)
    def _():
        for j in range(n_blk):
            st_ref[:, j * LANES:(j + 1) * LANES] = h0_ref[0, j * LANES:(j + 1) * LANES, :].T

    def pad_rows(v):
        if rows == Q:
            return v
        return jnp.concatenate([v, jnp.zeros((Q - rows, v.shape[1]), v.dtype)], axis=0)

    lane = lax.broadcasted_iota(jnp.int32, (Q, LANES), 1)
    rowq = lax.broadcasted_iota(jnp.int32, (Q, LANES), 0)
    causal2 = (lane % Q) <= rowq
    lo_half = lane < SSD_HEAD_DIM
    decay_rate = -LOG2E * jnp.exp(alog_ref[...])
    decay_rate_t = -LOG2E * jnp.exp(alogT_ref[...])

    for k in range(n_sub):
        r0 = k * Q
        hi, lo = _split2(prev_ref[0])
        if k == 0:
            hi = jnp.where(c == 0, hi, x16_ref[0])
            lo = jnp.where(c == 0, lo, jnp.zeros_like(lo))
        else:
            hi, lo = xbc_ref[0, r0 - 16:r0, :], jnp.zeros_like(lo)
        window = jnp.concatenate([pad_rows(xbc_ref[0, r0:r0 + rows, :]), hi, lo,
                                  jnp.zeros((Q - 32, SSD_CONV_CH), BF16)], axis=0)
        shifted = _dot(shift_ref[...], window)
        conv = _rows8(mul, shifted[0:Q], cw_ref[0:8, :])
        for i in range(1, SSD_CONV):
            conv = conv + _rows8(mul, shifted[i * Q:(i + 1) * Q], cw_ref[8 * i:8 * i + 8, :])
        conv = _rows8(jnp.add, conv, cb_ref[...])
        xc = conv * _sigmoid(conv)
        xs = xc[:, :SSD_WIDTH]
        b_all = xc[:, SSD_WIDTH:SSD_WIDTH + SSD_GROUPS * SSD_STATE]
        c_all = xc[:, SSD_WIDTH + SSD_GROUPS * SSD_STATE:]

        dt = _softplus(pad_rows(dt_ref[0, r0:r0 + rows, :]) + dtb_ref[...])
        dt_t = _softplus(dtT_ref[0, k] + dtbT_ref[...])
        if rows < Q:
            rowi = lax.broadcasted_iota(jnp.int32, (Q, 1), 0)
            dt = jnp.where(rowi < rows, dt, 0.0)
            xs = jnp.where(rowi < rows, xs, 0.0)
            lane_t = lax.broadcasted_iota(jnp.int32, dt_t.shape, 1)
            dt_t = jnp.where(lane_t % Q < rows, dt_t, 0.0)
        da = dt * decay_rate
        da_t = dt_t * decay_rate_t

        a_cum = _dot(ltri_ref[...], jnp.concatenate(_split3(da), axis=0))
        a_cum_t = _dot(jnp.concatenate(_split3(da_t), axis=1), ublk_ref[...])
        dt_exp = _dot(jnp.concatenate(_split2(dt), axis=1), e3_ref[0:2 * LANES, :])
        a_exp = _dot(jnp.concatenate(_split3(a_cum), axis=1), e3_ref[...])
        a_last = jnp.broadcast_to(a_exp[Q - 1:Q, :], (8, SSD_WIDTH))

        xdt = xs * dt_exp
        xdt_b = xdt.astype(BF16)
        xw = (xdt * jnp.exp2(-_rows8(sub, a_exp, a_last))).astype(BF16)
        xw_pad = jnp.concatenate([xw, jnp.zeros_like(xw)], axis=0)
        chunk_decay = jnp.exp2(a_last)
        grow = jnp.exp2(a_exp)

        y_parts = []
        for g in range(SSD_GROUPS):
            bg = b_all[:, g * SSD_STATE:(g + 1) * SSD_STATE]
            cg = c_all[:, g * SSD_STATE:(g + 1) * SSD_STATE].astype(BF16)
            bg_b = bg.astype(BF16)
            cb2 = _dot_nt(cg, jnp.concatenate([bg_b, bg_b], axis=0))
            gs = slice(g * GROUP_W, (g + 1) * GROUP_W)
            st_g = st_ref[:, gs]
            y_off = _dot(cg, st_g.astype(BF16)) * grow[:, gs]
            bg_t = jnp.concatenate([bg, jnp.zeros_like(bg)], axis=0).T.astype(BF16)
            st_ref[:, gs] = _rows8(mul, st_g, chunk_decay[:, gs]) + _dot(bg_t, xw_pad[:, gs])
            pair_out = []
            for kk in range(GROUP_W // LANES):
                i = g * (GROUP_W // LANES) + kk
                ps = slice(i * LANES, (i + 1) * LANES)
                a_row = jnp.broadcast_to(a_cum_t[i:i + 1, :], (8, LANES))
                decay = jnp.exp2(jnp.minimum(_rows8(sub, a_exp[:, ps], a_row), 0.0))
                m2 = jnp.where(causal2, cb2 * decay, 0.0).astype(BF16)
                xp = xdt_b[:, ps]
                zero = jnp.zeros_like(xp)
                xbd = jnp.concatenate([jnp.where(lo_half, xp, zero), jnp.where(lo_half, zero, xp)], axis=0)
                pair_out.append(_dot(m2, xbd))
            y_diag = jnp.concatenate(pair_out, axis=1)
            y = y_diag + y_off + _rows8(mul, xs[:, gs], dskip_ref[:, gs])
            zg = pad_rows(z_ref[0, r0:r0 + rows, gs].astype(F32))
            y = y * (zg * _sigmoid(zg))
            ms = jnp.mean(y * y, axis=-1, keepdims=True)
            y_parts.append(_rows8(mul, y * lax.rsqrt(ms + EPS), gn_ref[:, gs]))
        y_all = jnp.concatenate(y_parts, axis=1)
        y_ref[0, r0:r0 + rows, :] = y_all[:rows].astype(y_ref.dtype)

    @pl.when(c == n_chunks - 1)
    def _():
        for j in range(n_blk):
            hfin_ref[0, j * LANES:(j + 1) * LANES, :] = st_ref[:, j * LANES:(j + 1) * LANES].T


def _ssd_constants():
    Q = SSD_CHUNK
    e = np.zeros((LANES, SSD_WIDTH), np.float32)
    for h in range(SSD_HEADS):
        e[h, h * SSD_HEAD_DIM:(h + 1) * SSD_HEAD_DIM] = 1.0
    e3 = np.concatenate([e, e, e], axis=0)
    t = np.arange(Q)
    ltri = (t[:, None] >= t[None, :]).astype(np.float32)
    ltri3 = np.concatenate([ltri, ltri, ltri], axis=1)
    ublk = np.zeros((LANES, LANES), np.float32)
    ublk[:Q, :Q] = ltri.T
    ublk[Q:, Q:] = ltri.T
    ublk3 = np.concatenate([ublk, ublk, ublk], axis=0)
    shift = np.zeros((SSD_CONV * Q, 2 * Q), np.float32)
    for i in range(SSD_CONV):
        for r in range(Q):
            src = r - (SSD_CONV - 1 - i)
            if src >= 0:
                shift[i * Q + r, src] = 1.0
            else:
                shift[i * Q + r, Q + 16 + src] = 1.0
                shift[i * Q + r, Q + 32 + src] = 1.0
    return jnp.asarray(e3, BF16), jnp.asarray(ltri3, BF16), jnp.asarray(ublk3, BF16), jnp.asarray(shift, BF16)


def ssd_mixer(big, xbc_blk, z_blk, dt_raw, conv_prev, h0, p):
    b, s, _ = big.shape
    Q = SSD_CHUNK
    rows = min(s, Q)
    n_sub = SSD_STEP_CHUNKS if s % (SSD_STEP_CHUNKS * Q) == 0 else 1
    blk = rows * n_sub
    n_chunks = -(-s // blk)
    dtp = dt_raw[:, :, :SSD_HEADS]
    if s < Q:
        dtp = jnp.pad(dtp, ((0, 0), (0, Q - s), (0, 0)))
    dt_t = dtp.reshape(b, n_chunks * n_sub, Q, SSD_HEADS).transpose(0, 1, 3, 2).reshape(
        b, n_chunks * n_sub, SSD_HEADS // 2, 2 * Q)
    e3, ltri3, ublk3, shift = _ssd_constants()
    prev16 = jnp.pad(conv_prev, ((0, 0), (16 - (SSD_CONV - 1), 0), (0, 0)))
    rep8 = lambda v: jnp.repeat(v, 8, axis=0)
    const = lambda shape: pl.BlockSpec(shape, lambda bi, ci: (0,) * len(shape))
    in_specs = [
        pl.BlockSpec((1, blk, SSD_CONV_CH), lambda bi, ci: (bi, ci, xbc_blk)),
        pl.BlockSpec((1, 16, SSD_CONV_CH), lambda bi, ci: (bi, jnp.maximum(ci * (blk // 16) - 1, 0), xbc_blk)),
        pl.BlockSpec((1, blk, SSD_WIDTH), lambda bi, ci: (bi, ci, z_blk)),
        pl.BlockSpec((1, blk, LANES), lambda bi, ci: (bi, ci, 0)),
        pl.BlockSpec((1, n_sub, SSD_HEADS // 2, 2 * Q), lambda bi, ci: (bi, ci, 0, 0)),
        pl.BlockSpec((1, 16, SSD_CONV_CH), lambda bi, ci: (bi, 0, 0)),
        pl.BlockSpec((1, SSD_WIDTH, SSD_STATE), lambda bi, ci: (bi, 0, 0)),
        const((8 * SSD_CONV, SSD_CONV_CH)), const((8, SSD_CONV_CH)),
        const((1, LANES)), const((SSD_HEADS // 2, 2 * Q)),
        const((1, LANES)), const((SSD_HEADS // 2, 2 * Q)),
        const((8, SSD_WIDTH)), const((8, SSD_WIDTH)),
        const(e3.shape), const(ltri3.shape), const(ublk3.shape), const(shift.shape),
    ]
    y, h_fin = pl.pallas_call(
        functools.partial(_ssd_kernel, rows=rows, n_sub=n_sub, n_chunks=n_chunks),
        grid=(b, n_chunks),
        in_specs=in_specs,
        out_specs=[pl.BlockSpec((1, blk, SSD_WIDTH), lambda bi, ci: (bi, ci, 0)),
                   pl.BlockSpec((1, SSD_WIDTH, SSD_STATE), lambda bi, ci: (bi, 0, 0))],
        out_shape=[jax.ShapeDtypeStruct((b, s, SSD_WIDTH), BF16),
                   jax.ShapeDtypeStruct((b, SSD_WIDTH, SSD_STATE), F32)],
        scratch_shapes=[pltpu.VMEM((SSD_STATE, SSD_WIDTH), F32)],
        compiler_params=_cparams(("parallel", "arbitrary")),
        name="ssd_mixer",
    )(big, big, big, dt_raw, dt_t, prev16, h0, rep8(p['conv_w_ssd']), rep8(p['conv_b_ssd']), p['dt_bias'], p['dt_bias_t'],
      p['a_log'], p['a_log_t'], rep8(p['d_skip']), rep8(p['g_ssd_norm']), e3, ltri3, ublk3, shift)
    return y, h_fin


def _merge_kernel(att_ref, ssd_ref, ga_ref, gs_ref, h_ref, wa_ref, ws_ref, wo_ref, o_ref, wa_b, ws_b, wo_b):
    @pl.when(pl.program_id(0) == 0)
    def _():
        wa_b[...] = wa_ref[...].astype(BF16)
        ws_b[...] = ws_ref[...].astype(BF16)
        wo_b[...] = wo_ref[...].astype(BF16)

    a = _dot(att_ref[...], wa_b[...])
    s = _dot(ssd_ref[...], ws_b[...])
    merged = _sigmoid(ga_ref[...].astype(F32)) * a + _sigmoid(gs_ref[...].astype(F32)) * s
    o_ref[...] = h_ref[...] + _dot(merged.astype(BF16), wo_b[...])


def _resident(a):
    return pl.BlockSpec(a.shape, lambda i: (0,) * a.ndim, pipeline_mode=pl.Buffered(1))


def merge(att, ssd, big, h, p):
    m = h.shape[0]
    tm = _pick(m, (512, 256, 128))
    row = lambda w: pl.BlockSpec((tm, w), lambda i: (i, 0))
    gate_blk = BIG_GATE // D_MODEL
    ws = [p['w_br_att'], p['w_br_ssd'], p['w_out']]
    return pl.pallas_call(
        _merge_kernel,
        grid=(m // tm,),
        in_specs=[row(SB_WIDTH), row(SSD_WIDTH),
                  pl.BlockSpec((tm, D_MODEL), lambda i: (i, gate_blk)),
                  pl.BlockSpec((tm, D_MODEL), lambda i: (i, gate_blk + 1)),
                  row(D_MODEL)] + [_resident(w) for w in ws],
        out_specs=row(D_MODEL),
        out_shape=jax.ShapeDtypeStruct((m, D_MODEL), F32),
        scratch_shapes=[pltpu.VMEM(w.shape, BF16) for w in ws],
        compiler_params=_cparams(("arbitrary",)),
        name="merge_out_proj",
    )(att, ssd, big, big, h, *ws)


def _ffn_act_kernel(ug_ref, uv_ref, pg_ref, pv_ref, wg_ref, wv_ref, bg_ref, bv_ref, o_ref, gbuf_ref, vbuf_ref,
                    *, ts):
    t = pl.program_id(1)
    k1 = FFN_CONV - 1

    def conv(u_ref, p_ref, w_ref, b_ref, buf_ref):
        @pl.when(t == 0)
        def _():
            buf_ref[8 - k1:8, :] = p_ref[0]

        buf_ref[8:8 + ts, :] = u_ref[0].astype(F32)
        y = b_ref[...]
        for i in range(FFN_CONV):
            y = y + buf_ref[8 - k1 + i:8 - k1 + i + ts, :] * w_ref[i:i + 1, :]
        buf_ref[8 - k1:8, :] = buf_ref[8 + ts - k1:8 + ts, :]
        return y

    gate = conv(ug_ref, pg_ref, wg_ref, bg_ref, gbuf_ref)
    val = conv(uv_ref, pv_ref, wv_ref, bv_ref, vbuf_ref)
    o_ref[0] = (gate * _sigmoid(gate) * val).astype(o_ref.dtype)


FFN_SUB = 64


def _ffn_act_long_kernel(ug_ref, uv_ref, g16_ref, v16_ref, pg_ref, pv_ref, wg_ref, wv_ref, bg_ref, bv_ref,
                         s_ref, o_ref, *, ts):
    t = pl.program_id(1)
    zeros = jnp.zeros((FFN_SUB - 16, D_FF), BF16)

    def conv(u_ref, b16_ref, p_ref, w_ref, b_ref, j):
        if j == 0:
            before = jnp.where(t == 0, p_ref[0], b16_ref[0])
            window = jnp.concatenate([zeros, before, u_ref[0, 0:FFN_SUB, :]], axis=0)
        else:
            window = u_ref[0, (j - 1) * FFN_SUB:(j + 1) * FFN_SUB, :]
        sh = _dot(s_ref[...], window)
        y = _rows8(jnp.multiply, sh[0:FFN_SUB], w_ref[0:8, :])
        for i in range(1, FFN_CONV):
            y = y + _rows8(jnp.multiply, sh[i * FFN_SUB:(i + 1) * FFN_SUB], w_ref[8 * i:8 * i + 8, :])
        return _rows8(jnp.add, y, b_ref[...])

    for j in range(ts // FFN_SUB):
        gate = conv(ug_ref, g16_ref, pg_ref, wg_ref, bg_ref, j)
        val = conv(uv_ref, v16_ref, pv_ref, wv_ref, bv_ref, j)
        o_ref[0, j * FFN_SUB:(j + 1) * FFN_SUB, :] = (gate * _sigmoid(gate) * val).astype(o_ref.dtype)


def ffn_act_long(up, prev, p):
    b, s, _ = up.shape
    ts = 512
    prev16 = jnp.pad(prev, ((0, 0), (16 - (FFN_CONV - 1), 0), (0, 0)))
    sel = np.zeros((FFN_CONV * FFN_SUB, 2 * FFN_SUB), np.float32)
    for i in range(FFN_CONV):
        for r in range(FFN_SUB):
            sel[i * FFN_SUB + r, FFN_SUB + r - (FFN_CONV - 1 - i)] = 1.0
    rep8 = lambda v: jnp.repeat(v, 8, axis=0)
    half = lambda blk: pl.BlockSpec((1, ts, D_FF), lambda bi, ti: (bi, ti, blk))
    before = lambda blk: pl.BlockSpec((1, 16, D_FF), lambda bi, ti: (bi, jnp.maximum(ti * (ts // 16) - 1, 0), blk))
    prevs = lambda blk: pl.BlockSpec((1, 16, D_FF), lambda bi, ti: (bi, 0, blk))
    wspec = lambda blk: pl.BlockSpec((8 * FFN_CONV, D_FF), lambda bi, ti: (0, blk))
    bspec = lambda blk: pl.BlockSpec((8, D_FF), lambda bi, ti: (0, blk))
    w8, b8 = rep8(p['conv_w_ffn']), rep8(p['conv_b_ffn'])
    return pl.pallas_call(
        functools.partial(_ffn_act_long_kernel, ts=ts),
        grid=(b, s // ts),
        in_specs=[half(0), half(1), before(0), before(1), prevs(0), prevs(1), wspec(0), wspec(1), bspec(0),
                  bspec(1), pl.BlockSpec(sel.shape, lambda bi, ti: (0, 0))],
        out_specs=pl.BlockSpec((1, ts, D_FF), lambda bi, ti: (bi, ti, 0)),
        out_shape=jax.ShapeDtypeStruct((b, s, D_FF), BF16),
        compiler_params=_cparams(("parallel", "parallel")),
        name="ffn_conv_act_long",
    )(up, up, up, up, prev16, prev16, w8, w8, b8, b8, jnp.asarray(sel, BF16))


def ffn_act(up, prev, p):
    b, s, _ = up.shape
    ts = _pick(s, (512, 256, 128))
    half = lambda blk: pl.BlockSpec((1, ts, D_FF), lambda bi, ti: (bi, ti, blk))
    prevs = lambda blk: pl.BlockSpec((1, FFN_CONV - 1, D_FF), lambda bi, ti: (bi, 0, blk))
    wspec = lambda blk: pl.BlockSpec((FFN_CONV, D_FF), lambda bi, ti: (0, blk))
    bspec = lambda blk: pl.BlockSpec((1, D_FF), lambda bi, ti: (0, blk))
    return pl.pallas_call(
        functools.partial(_ffn_act_kernel, ts=ts),
        grid=(b, s // ts),
        in_specs=[half(0), half(1), prevs(0), prevs(1), wspec(0), wspec(1), bspec(0), bspec(1)],
        out_specs=pl.BlockSpec((1, ts, D_FF), lambda bi, ti: (bi, ti, 0)),
        out_shape=jax.ShapeDtypeStruct((b, s, D_FF), BF16),
        scratch_shapes=[pltpu.VMEM((8 + ts, D_FF), F32)] * 2,
        compiler_params=_cparams(("parallel", "arbitrary")),
        name="ffn_conv_act",
    )(up, up, prev, prev, p['conv_w_ffn'], p['conv_w_ffn'], p['conv_b_ffn'], p['conv_b_ffn'])


def _down_kernel(a_ref, w_ref, h_ref, g_ref, o_ref, w_b):
    @pl.when(pl.program_id(0) == 0)
    def _():
        w_b[...] = w_ref[...].astype(BF16)

    h = h_ref[...] + _dot(a_ref[...], w_b[...])
    ms = jnp.mean(h * h, axis=-1, keepdims=True)
    o_ref[...] = h * lax.rsqrt(ms + EPS) * g_ref[...]


def down_norm(act, h, p):
    m = h.shape[0]
    tm = _pick(m, (512, 256, 128))
    return pl.pallas_call(
        _down_kernel,
        grid=(m // tm,),
        in_specs=[pl.BlockSpec((tm, D_FF), lambda i: (i, 0)),
                  _resident(p['w_down']),
                  pl.BlockSpec((tm, D_MODEL), lambda i: (i, 0)),
                  pl.BlockSpec((1, D_MODEL), lambda i: (0, 0))],
        out_specs=pl.BlockSpec((tm, D_MODEL), lambda i: (i, 0)),
        out_shape=jax.ShapeDtypeStruct((m, D_MODEL), F32),
        scratch_shapes=[pltpu.VMEM(p['w_down'].shape, BF16)],
        compiler_params=_cparams(("arbitrary",)),
        name="down_proj_norm",
    )(act, p['w_down'], h, p['g_final'])


def mix_and_ffn(big3, att, dt_raw, hf, ssd_h0, ssd_conv_prev, ffn_conv_prev, p):
    b, s, _ = big3.shape
    m = b * s
    ssd, ssd_state = ssd_mixer(big3, BIG_XBC // SSD_CONV_CH, BIG_Z // SSD_WIDTH, dt_raw, ssd_conv_prev, ssd_h0, p)
    h1 = merge(att.reshape(m, SB_WIDTH), ssd.reshape(m, SSD_WIDTH), big3.reshape(m, BIG_W), hf, p)
    up = norm_mm(h1, p['g_ffn'], p['w_up'], BF16).reshape(b, s, 2 * D_FF)
    if s % 512 == 0:
        act = ffn_act_long(up, ffn_conv_prev.astype(BF16), p)
    else:
        act = ffn_act(up, ffn_conv_prev, p)
    y = down_norm(act.reshape(m, D_FF), h1, p).reshape(b, s, D_MODEL)
    xbc_rows = jnp.concatenate([ssd_conv_prev, big3[:, :, BIG_XBC:BIG_XBC + SSD_CONV_CH][:, -(SSD_CONV - 1):].astype(F32)],
                               axis=1)[:, -(SSD_CONV - 1):]
    up_rows = jnp.concatenate([ffn_conv_prev, up[:, -(FFN_CONV - 1):].astype(F32)], axis=1)[:, -(FFN_CONV - 1):]
    return y, ssd_state, xbc_rows, up_rows


def _prep_params(g_mix, w_in, conv_w_ssd, conv_b_ssd, dt_bias, a_log, d_skip, g_ssd_norm, w_br_att, w_br_ssd,
                 w_out, g_ffn, w_up, conv_w_ffn, conv_b_ffn, w_down, g_final):
    w_t = w_in.T
    dt0 = 3 * SB_WIDTH + SSD_WIDTH + SSD_CONV_CH
    w_tail_t = jnp.concatenate([w_t[dt0 + SSD_HEADS:], w_t[dt0:dt0 + SSD_HEADS],
                                jnp.zeros((LANES - SSD_HEADS, D_MODEL), F32)], axis=0)
    Q = SSD_CHUNK
    lanes_t = lambda v: jnp.repeat(v.reshape(SSD_HEADS // 2, 2, 1), Q, axis=2).reshape(SSD_HEADS // 2, 2 * Q)
    pad_l = lambda v: jnp.pad(v, (0, LANES - SSD_HEADS)).reshape(1, LANES)
    return {
        'g_mix': g_mix.reshape(1, -1), 'g_ffn': g_ffn.reshape(1, -1), 'g_final': g_final.reshape(1, -1),
        'w_t': w_t, 'w_tail_t': w_tail_t,
        'conv_w_ssd': conv_w_ssd, 'conv_b_ssd': conv_b_ssd.reshape(1, -1),
        'dt_bias': pad_l(dt_bias), 'dt_bias_t': lanes_t(dt_bias),
        'a_log': pad_l(a_log), 'a_log_t': lanes_t(a_log),
        'd_skip': jnp.repeat(d_skip, SSD_HEAD_DIM).reshape(1, -1), 'g_ssd_norm': g_ssd_norm.reshape(1, -1),
        'w_br_att': w_br_att, 'w_br_ssd': w_br_ssd, 'w_out': w_out,
        'w_up': w_up, 'conv_w_ffn': conv_w_ffn, 'conv_b_ffn': conv_b_ffn.reshape(1, -1),
        'w_down': w_down,
    }


def kernel(x_prompt, x_sample, cache_k, cache_v, state_ssm, state_ssm_conv, state_ffn_conv, meta_tokens, g_mix, w_in, conv_w_ssd, conv_b_ssd, dt_bias, a_log, d_skip, g_ssd_norm, w_br_att, w_br_ssd, w_out, g_ffn, w_up, conv_w_ffn, conv_b_ffn, w_down, g_final):
    bp, seq, _ = x_prompt.shape
    bs, dec = x_sample.shape[:2]
    assert dec == N_META and seq % KVT_TS == 0
    p = _prep_params(g_mix[0], w_in[0], conv_w_ssd[0], conv_b_ssd[0], dt_bias[0], a_log[0], d_skip[0],
                     g_ssd_norm[0], w_br_att[0], w_br_ssd[0], w_out[0], g_ffn[0], w_up[0], conv_w_ffn[0],
                     conv_b_ffn[0], w_down[0], g_final)
    proj = functools.partial(in_proj, g=p['g_mix'], w_t=p['w_t'], w_tail_t=p['w_tail_t'])
    state_shape = (SSD_WIDTH, SSD_STATE)
    rows = lambda a, b: a.reshape(b, -1, a.shape[-1])

    k_m, v_m, big_m, dt_m = proj(meta_tokens, with_kv=True)
    big_m, k_m, v_m = big_m[None], k_m[None], v_m[None]
    att_m = attention_step(big_m, 0, k_m, v_m, None, None, 0, SB_WIDTH)
    _, ssm_m, conv_m, ffn_m = mix_and_ffn(
        big_m, att_m, dt_m[None], meta_tokens, jnp.zeros((1,) + state_shape, F32),
        jnp.zeros((1, SSD_CONV - 1, SSD_CONV_CH), F32), jnp.zeros((1, FFN_CONV - 1, 2 * D_FF), F32), p)

    xf = x_prompt.reshape(bp * seq, D_MODEL)
    big_x, dt_x = proj(xf, with_kv=False)
    big_x = rows(big_x, bp)
    k_t, v_t, k_end, v_end = kv_transposed(x_prompt, meta_tokens, p['g_mix'], p['w_t'])
    att_x = attention_prompt(big_x, big_m, k_t, v_t)
    att_end = attention_step(big_x, seq // N_META - 1, k_end, v_end, k_t, v_t, seq // KEY_TILE, ATTN_PROMPT_WIDTH)
    att_x = lax.dynamic_update_slice(att_x, att_end, (0, seq - N_META, 0))
    rep = lambda a: jnp.broadcast_to(a, (bp,) + a.shape[1:])
    y_prompt, ssm_p, conv_p, ffn_p = mix_and_ffn(
        big_x, att_x, rows(dt_x, bp), xf, rep(ssm_m), rep(conv_m), rep(ffn_m), p)

    n_rows = cache_k.shape[2]
    n_tiles = (n_rows - N_META) // KEY_TILE
    assert n_tiles * KEY_TILE + N_META == n_rows
    k_s, v_s, big_s, dt_s = proj(x_sample.reshape(bs * dec, D_MODEL), with_kv=True)
    big_s, k_s, v_s = rows(big_s, bs), rows(k_s, bs), rows(v_s, bs)
    cache_t = lambda c: c[0].transpose(0, 2, 3, 1).reshape(bs, SB_WIDTH, n_rows)
    nearest = lambda c, new: jnp.concatenate([c[0][:, n_tiles * KEY_TILE:].reshape(bs, N_META, SB_WIDTH), new], axis=1)
    att_s = attention_step(big_s, 0, nearest(cache_k, k_s), nearest(cache_v, v_s), cache_t(cache_k), cache_t(cache_v),
                           n_tiles, SB_WIDTH)
    y_sample, ssm_s, conv_s, ffn_s = mix_and_ffn(
        big_s, att_s, rows(dt_s, bs), x_sample.reshape(bs * dec, D_MODEL),
        state_ssm[0].reshape((bs,) + state_shape), state_ssm_conv[0], state_ffn_conv[0], p)

    heads = lambda a: a.reshape(a.shape[0], a.shape[1], SB_HEADS, SB_HEAD_DIM)[None]
    heads_t = lambda a: a.reshape(a.shape[0], SB_HEADS, SB_HEAD_DIM, a.shape[2]).transpose(0, 3, 1, 2)[None]
    state5 = lambda a: a.reshape(a.shape[0], SSD_HEADS, SSD_HEAD_DIM, SSD_STATE)[None]
    return (y_prompt, y_sample, heads_t(k_t), heads_t(v_t), state5(ssm_p), conv_p[None], ffn_p[None],
            heads(k_s), heads(v_s), state5(ssm_s), conv_s[None], ffn_s[None])
```

```python
import functools

import numpy as np
import jax
import jax.numpy as jnp
from jax import lax
from jax.experimental import pallas as pl
from jax.experimental.pallas import tpu as pltpu

F32 = jnp.float32
BF16 = jnp.bfloat16

D_MODEL = 1024
N_META = 16
SB_HEADS = 16
SB_HEAD_DIM = 64
SB_WIDTH = SB_HEADS * SB_HEAD_DIM
SSD_WIDTH = 2 * D_MODEL
SSD_HEAD_DIM = 64
SSD_HEADS = SSD_WIDTH // SSD_HEAD_DIM
SSD_GROUPS = 4
SSD_STATE = 128
SSD_CONV = 4
SSD_CONV_CH = SSD_WIDTH + 2 * SSD_GROUPS * SSD_STATE
D_FF = 2816
FFN_CONV = 3
EPS = 1e-6

LANES = 128
SSD_CHUNK = 64
SSD_STEP_CHUNKS = 2
KEY_TILE = 128
GROUP_W = SSD_WIDTH // SSD_GROUPS
DEAD_LOG = -104.0
LOG2E = 1.4426950408889634
ATTN_STATIC_TILES = 2
VMEM_LIMIT = 56 * 1024 * 1024


def _cparams(sem):
    return pltpu.CompilerParams(dimension_semantics=sem, vmem_limit_bytes=VMEM_LIMIT)


def _pick(n, cands):
    for c in cands:
        if n % c == 0:
            return c
    return n


def _split2(x):
    hi = x.astype(BF16)
    lo = (x - hi.astype(F32)).astype(BF16)
    return hi, lo


def _split3(x):
    hi = x.astype(BF16)
    r = x - hi.astype(F32)
    mid = r.astype(BF16)
    lo = (r - mid.astype(F32)).astype(BF16)
    return hi, mid, lo


def _dot(a, b):
    return jnp.dot(a, b, preferred_element_type=F32)


def _dot_nt(a, b):
    return lax.dot_general(a, b, (((1,), (1,)), ((), ())), preferred_element_type=F32)


def _softplus(x):
    return jnp.maximum(x, 0.0) + jnp.log(1.0 + jnp.exp(-jnp.abs(x)))


def _sigmoid(x):
    return 1.0 / (1.0 + jnp.exp(-x))


def _rms_norm_bf16(x, g):
    ms = jnp.mean(x * x, axis=-1, keepdims=True)
    return (x * lax.rsqrt(ms + EPS) * g).astype(BF16)


def _norm_mm_kernel(x_ref, g_ref, w_ref, o_ref, u_ref):
    @pl.when(pl.program_id(1) == 0)
    def _():
        u_ref[...] = _rms_norm_bf16(x_ref[...], g_ref[...])

    o_ref[...] = _dot(u_ref[...], w_ref[...].astype(BF16)).astype(o_ref.dtype)


PROJ_TN = 1024
UP_TN = 1408


def norm_mm(x, g, w, out_dtype):
    m, d = x.shape
    n = w.shape[1]
    tm = _pick(m, (2048, 1024, 512, 256, 128))
    tn = UP_TN
    return pl.pallas_call(
        _norm_mm_kernel,
        grid=(m // tm, n // tn),
        in_specs=[pl.BlockSpec((tm, d), lambda i, j: (i, 0)),
                  pl.BlockSpec((1, d), lambda i, j: (0, 0)),
                  pl.BlockSpec((d, tn), lambda i, j: (0, j))],
        out_specs=pl.BlockSpec((tm, tn), lambda i, j: (i, j)),
        out_shape=jax.ShapeDtypeStruct((m, n), out_dtype),
        scratch_shapes=[pltpu.VMEM((tm, d), BF16)],
        compiler_params=_cparams(("parallel", "arbitrary")),
        name="norm_up_proj",
    )(x, g, w)


_NQ = SB_WIDTH // PROJ_TN
_J_K, _J_V, _J_Z = _NQ, 2 * _NQ, 3 * _NQ
_J_X = _J_Z + SSD_WIDTH // PROJ_TN
_J_G = _J_X + SSD_CONV_CH // PROJ_TN
_J_END = _J_G + 2 * D_MODEL // PROJ_TN
BIG_Z, BIG_Q, BIG_XBC, BIG_GATE = 0, SSD_WIDTH, SSD_WIDTH + SB_WIDTH, SSD_WIDTH + SB_WIDTH + SSD_CONV_CH
BIG_W = BIG_GATE + 2 * D_MODEL


def _big_block(j):
    return jnp.where(j < _J_K, BIG_Q // PROJ_TN + j,
                     jnp.where(j < _J_Z, BIG_Q // PROJ_TN + _NQ - 1,
                               jnp.where(j < _J_X, j - _J_Z + BIG_Z // PROJ_TN,
                                         jnp.where(j < _J_G, j - _J_X + BIG_XBC // PROJ_TN,
                                                   j - _J_G + BIG_GATE // PROJ_TN))))


def _sweep_step(j, with_kv):
    return j if with_kv else jnp.where(j >= _J_K, j + (_J_Z - _J_K), j)


def _in_proj_kernel(*refs, with_kv):
    if with_kv:
        x_ref, g_ref, win_ref, wgate_ref, wdt_ref, k_ref, v_ref, big_ref, dt_ref, u_ref = refs
    else:
        x_ref, g_ref, win_ref, wgate_ref, wdt_ref, big_ref, dt_ref, u_ref = refs
    j = _sweep_step(pl.program_id(1), with_kv)

    def proj(w_ref):
        return _dot_nt(u_ref[...], w_ref[...].astype(BF16))

    @pl.when(pl.program_id(1) == 0)
    def _():
        u_ref[...] = _rms_norm_bf16(x_ref[...], g_ref[...])
        dt_ref[...] = proj(wdt_ref)

    if with_kv:
        @pl.when(jnp.logical_and(j >= _J_K, j < _J_V))
        def _():
            k_ref[...] = proj(win_ref)

        @pl.when(jnp.logical_and(j >= _J_V, j < _J_Z))
        def _():
            v_ref[...] = proj(win_ref)

    @pl.when(jnp.logical_or(j < _J_K, jnp.logical_and(j >= _J_Z, j < _J_G)))
    def _():
        big_ref[...] = proj(win_ref).astype(BF16)

    @pl.when(j >= _J_G)
    def _():
        big_ref[...] = proj(wgate_ref).astype(BF16)


def in_proj(x, g, w_t, w_tail_t, with_kv):
    m, d = x.shape
    tm = _pick(m, (2048, 1024, 512, 256, 128))
    tn = PROJ_TN
    step = functools.partial(_sweep_step, with_kv=with_kv)
    clip = lambda j, lo, n: jnp.clip(step(j) - lo, 0, n - 1)
    kv_specs = [pl.BlockSpec((tm, tn), lambda i, j: (i, clip(j, _J_K, _NQ))),
                pl.BlockSpec((tm, tn), lambda i, j: (i, clip(j, _J_V, _NQ)))]
    kv_shapes = [jax.ShapeDtypeStruct((m, SB_WIDTH), F32)] * 2
    return pl.pallas_call(
        functools.partial(_in_proj_kernel, with_kv=with_kv),
        grid=(m // tm, _J_END if with_kv else _J_END - (_J_Z - _J_K)),
        in_specs=[pl.BlockSpec((tm, d), lambda i, j: (i, 0), pipeline_mode=pl.Buffered(1)),
                  pl.BlockSpec((1, d), lambda i, j: (0, 0)),
                  pl.BlockSpec((tn, d), lambda i, j: (jnp.minimum(step(j), _J_G - 1), 0)),
                  pl.BlockSpec((tn, d), lambda i, j: (clip(j, _J_G, _J_END - _J_G), 0)),
                  pl.BlockSpec((LANES, d), lambda i, j: (2 * D_MODEL // LANES, 0))],
        out_specs=(kv_specs if with_kv else []) + [
            pl.BlockSpec((tm, tn), lambda i, j: (i, _big_block(step(j)))),
            pl.BlockSpec((tm, LANES), lambda i, j: (i, 0))],
        out_shape=(kv_shapes if with_kv else []) + [
            jax.ShapeDtypeStruct((m, BIG_W), BF16), jax.ShapeDtypeStruct((m, LANES), F32)],
        scratch_shapes=[pltpu.VMEM((tm, d), BF16)],
        compiler_params=_cparams(("parallel", "arbitrary")),
        name="in_proj",
    )(x, g, w_t, w_tail_t, w_tail_t)


KVT_TS = 1024


def _kv_t_kernel(x16_ref, xt_ref, meta_ref, g_ref, wk_ref, wv_ref, kt_ref, vt_ref, kl_ref, vl_ref, wk_b, wv_b):
    j = pl.program_id(1)

    @pl.when(jnp.logical_and(pl.program_id(0) == 0, j == 0))
    def _():
        wk_b[...] = wk_ref[...].astype(BF16)
        wv_b[...] = wv_ref[...].astype(BF16)

    first = jnp.where(j == 0, meta_ref[...], x16_ref[0])
    x = jnp.concatenate([first, xt_ref[0, :KVT_TS - N_META, :]], axis=0)
    u = _rms_norm_bf16(x, g_ref[...])
    kt_ref[0] = _dot_nt(wk_b[...], u)
    vt_ref[0] = _dot_nt(wv_b[...], u)

    @pl.when(j == pl.num_programs(1) - 1)
    def _():
        kl_ref[0] = _dot_nt(u[0:N_META, :], wk_b[...])
        vl_ref[0] = _dot_nt(u[0:N_META, :], wv_b[...])


def kv_transposed(x, meta, g, w_t):
    b, s, d = x.shape
    n_pos = N_META + s
    n_tiles = -(-n_pos // KVT_TS)
    assert n_pos - (n_tiles - 1) * KVT_TS == N_META
    per_tile = KVT_TS // N_META
    out_spec = pl.BlockSpec((1, SB_WIDTH, KVT_TS), lambda bi, j: (bi, 0, j))
    last_spec = pl.BlockSpec((1, N_META, SB_WIDTH), lambda bi, j: (bi, 0, 0))
    w_spec = lambda blk: pl.BlockSpec((SB_WIDTH, d), lambda bi, j: (blk, 0), pipeline_mode=pl.Buffered(1))
    return pl.pallas_call(
        _kv_t_kernel,
        grid=(b, n_tiles),
        in_specs=[pl.BlockSpec((1, N_META, d), lambda bi, j: (bi, jnp.maximum(j * per_tile - 1, 0), 0)),
                  pl.BlockSpec((1, KVT_TS, d), lambda bi, j: (bi, jnp.minimum(j, s // KVT_TS - 1), 0)),
                  pl.BlockSpec((N_META, d), lambda bi, j: (0, 0)),
                  pl.BlockSpec((1, d), lambda bi, j: (0, 0)),
                  w_spec(1), w_spec(2)],
        out_specs=[out_spec, out_spec, last_spec, last_spec],
        out_shape=[jax.ShapeDtypeStruct((b, SB_WIDTH, n_pos), F32)] * 2
        + [jax.ShapeDtypeStruct((b, N_META, SB_WIDTH), F32)] * 2,
        scratch_shapes=[pltpu.VMEM((SB_WIDTH, d), BF16)] * 2,
        compiler_params=_cparams(("arbitrary", "arbitrary")),
        name="kv_transposed",
    )(x, x, meta, g, w_t, w_t)


def _stick_breaking_tiles(qs, u2, tiles, carries, accs):
    n_heads = len(qs)
    units = [(t, h) for t in range(len(tiles)) for h in range(n_heads)]
    ps = lambda h: slice((h // 2) * LANES, (h // 2 + 1) * LANES)

    def scores(t, h):
        kind, keys = tiles[t][0], tiles[t][1]
        return _dot_nt(qs[h], keys[:, ps(h)]) if kind == 'rows' else _dot(qs[h], keys[ps(h), :])

    z = {u: scores(*u) for u in units}
    log_beta, cat = {}, {}
    for u in units:
        mask = tiles[u[0]][3]
        soft = jnp.log(1.0 + jnp.exp(-jnp.abs(z[u])))
        log_beta[u] = jnp.minimum(z[u], 0.0) - soft
        log_keep = log_beta[u] - z[u]
        if mask is not None:
            log_keep = jnp.where(mask, log_keep, 0.0)
        cat[u] = jnp.concatenate(_split2(log_keep), axis=1)
    r = {u: _dot(cat[u], u2) for u in units}
    carries, accs = list(carries), list(accs)
    w = {}
    for t, h in units:
        mask = tiles[t][3]
        wt = jnp.exp(log_beta[(t, h)] + r[(t, h)][:, :LANES] + carries[h])
        if mask is not None:
            wt = jnp.where(mask, wt, 0.0)
        w[(t, h)] = wt.astype(BF16)
        carries[h] = carries[h] + r[(t, h)][:, LANES:]
    for t, h in units:
        kind, vals = tiles[t][0], tiles[t][2]
        pv = _dot(w[(t, h)], vals[:, ps(h)]) if kind == 'rows' else _dot_nt(w[(t, h)], vals[ps(h), :])
        accs[h] = accs[h] + pv
    return carries, accs


def _head_queries(q, lo_half):
    qs = []
    for h in range(q.shape[1] // SB_HEAD_DIM):
        qp = q[:, (h // 2) * LANES:(h // 2 + 1) * LANES]
        keep = lo_half if h % 2 == 0 else jnp.logical_not(lo_half)
        qs.append(jnp.where(keep, qp, jnp.zeros_like(qp)) * jnp.asarray(SB_HEAD_DIM ** -0.5, BF16))
    return qs


def _any_alive(carries):
    m = carries[0]
    for c in carries[1:]:
        m = jnp.maximum(m, c)
    return (jnp.max(m) > DEAD_LOG).astype(jnp.int32)


def _attend(qs, u2, first_tiles, load_tile, j0, lane):
    tq = qs[0].shape[0]

    def cache_tile(j):
        if isinstance(j, int):
            mask = None if j >= 0 else lane < 0
            kt, vt = load_tile(max(j, 0))
        else:
            mask = jnp.logical_and(j >= 0, lane >= 0)
            kt, vt = load_tile(jnp.maximum(j, 0))
        return 'cols', kt.astype(BF16), vt.astype(BF16), mask

    tiles = list(first_tiles)
    if load_tile is not None:
        for _ in range(ATTN_STATIC_TILES):
            tiles.append(cache_tile(j0))
            j0 = j0 - 1
    zeros = jnp.zeros((tq, LANES), F32)
    carries, accs = _stick_breaking_tiles(qs, u2, tiles, [zeros] * len(qs), [zeros] * len(qs))
    if load_tile is not None:
        def cond(s):
            return jnp.logical_and(s[0] >= 0, s[1] > 0)

        def body(s):
            j, _, carries, accs = s
            carries, accs = _stick_breaking_tiles(qs, u2, [cache_tile(j)], carries, accs)
            return j - 1, _any_alive(carries), tuple(carries), tuple(accs)

        state = (jnp.asarray(j0, jnp.int32), _any_alive(carries), tuple(carries), tuple(accs))
        _, _, carries, accs = lax.while_loop(cond, body, state)
    return accs


def _pair_outputs(accs, lo_half, dtype):
    return [jnp.where(lo_half, accs[2 * p], accs[2 * p + 1]).astype(dtype) for p in range(len(accs) // 2)]


def _attn_prompt_kernel(q16_ref, qt_ref, qm_ref, u2_ref, kt_ref, vt_ref, o_ref):
    m = pl.program_id(2)
    tq = KEY_TILE
    lane = lax.broadcasted_iota(jnp.int32, (tq, LANES), 1)
    row = lax.broadcasted_iota(jnp.int32, (tq, LANES), 0)
    lo_half = lane < SB_HEAD_DIM
    first = jnp.where(m == 0, qm_ref[0], q16_ref[0])
    q = jnp.concatenate([first, qt_ref[0, :tq - N_META, :]], axis=0)
    qs = _head_queries(q, lo_half)
    def load_tile(j):
        off = pl.multiple_of(j * KEY_TILE, KEY_TILE)
        return kt_ref[0, :, pl.ds(off, KEY_TILE)], vt_ref[0, :, pl.ds(off, KEY_TILE)]

    kd, vd = load_tile(m)
    diag = ('cols', kd.astype(BF16), vd.astype(BF16), lane < row)
    accs = _attend(qs, u2_ref[...], [diag], load_tile, m - 1, lane)
    out = jnp.concatenate(_pair_outputs(accs, lo_half, o_ref.dtype), axis=1)

    @pl.when(m == 0)
    def _():
        o_ref[0, 0:tq - N_META, :] = out[N_META:, :]
        o_ref[0, o_ref.shape[1] - N_META:, :] = jnp.zeros((N_META, o_ref.shape[2]), o_ref.dtype)

    @pl.when(m > 0)
    def _():
        o_ref[0, pl.ds(pl.multiple_of(m * KEY_TILE - N_META, N_META), tq), :] = out


def _attn_step_kernel(*refs, n_free, n_cache_tiles):
    if n_cache_tiles:
        (q_ref, kd_ref, vd_ref, u2_ref, kn_ref, vn_ref, kt_any, vt_any, o_ref,
         kd_scr, vd_scr, kbuf, vbuf) = refs
        width = q_ref.shape[2]
        n_near = kn_ref.shape[2] // KEY_TILE

        def load_tile(j):
            if isinstance(j, int) and j >= n_cache_tiles - n_near:
                ls = slice((j - (n_cache_tiles - n_near)) * KEY_TILE, (j - (n_cache_tiles - n_near) + 1) * KEY_TILE)
                return kn_ref[0, :, ls], vn_ref[0, :, ls]
            off = j * KEY_TILE if isinstance(j, int) else pl.multiple_of(j * KEY_TILE, KEY_TILE)
            src = (pl.program_id(0), pl.ds(pl.program_id(1) * width, width), pl.ds(off, KEY_TILE))
            pltpu.sync_copy(kt_any.at[src], kbuf)
            pltpu.sync_copy(vt_any.at[src], vbuf)
            return kbuf[...], vbuf[...]
    else:
        q_ref, kd_ref, vd_ref, u2_ref, o_ref, kd_scr, vd_scr = refs
        load_tile = None
    tq = q_ref.shape[1]
    n_rows = kd_ref.shape[1]
    lane = lax.broadcasted_iota(jnp.int32, (tq, LANES), 1)
    row = lax.broadcasted_iota(jnp.int32, (tq, LANES), 0)
    lo_half = lane < SB_HEAD_DIM
    kd_scr[...] = jnp.zeros_like(kd_scr)
    vd_scr[...] = jnp.zeros_like(vd_scr)
    kd_scr[0:n_rows, :] = kd_ref[0].astype(BF16)
    vd_scr[0:n_rows, :] = vd_ref[0].astype(BF16)
    diag = ('rows', kd_scr[...], vd_scr[...], lane < row + n_free)
    qs = _head_queries(q_ref[0], lo_half)
    accs = _attend(qs, u2_ref[...], [diag], load_tile, n_cache_tiles - 1, lane)
    for p, out in enumerate(_pair_outputs(accs, lo_half, o_ref.dtype)):
        o_ref[0, :, p * LANES:(p + 1) * LANES] = out


def _cumsum_rhs():
    s = np.arange(KEY_TILE)
    strict = (s[:, None] > s[None, :]).astype(np.float32)
    half = np.concatenate([strict, np.ones((KEY_TILE, KEY_TILE), np.float32)], axis=1)
    return jnp.asarray(np.concatenate([half, half], axis=0), BF16)


ATTN_PROMPT_WIDTH = 8 * SB_HEAD_DIM


def attention_prompt(big, big_meta, k_t, v_t):
    b, s, _ = big.shape
    width = ATTN_PROMPT_WIDTH
    qb0 = BIG_Q // width
    per_tile = KEY_TILE // N_META
    n_pos = k_t.shape[2]
    cache = pl.BlockSpec((1, width, n_pos), lambda bi, hp, m: (bi, hp, 0))
    return pl.pallas_call(
        _attn_prompt_kernel,
        grid=(b, SB_WIDTH // width, s // KEY_TILE),
        in_specs=[pl.BlockSpec((1, N_META, width), lambda bi, hp, m: (bi, jnp.maximum(m * per_tile - 1, 0), qb0 + hp)),
                  pl.BlockSpec((1, KEY_TILE, width), lambda bi, hp, m: (bi, m, qb0 + hp)),
                  pl.BlockSpec((1, N_META, width), lambda bi, hp, m: (0, 0, qb0 + hp)),
                  pl.BlockSpec((2 * KEY_TILE, 2 * KEY_TILE), lambda bi, hp, m: (0, 0)),
                  cache, cache],
        out_specs=pl.BlockSpec((1, s, width), lambda bi, hp, m: (bi, 0, hp)),
        out_shape=jax.ShapeDtypeStruct((b, s, SB_WIDTH), BF16),
        compiler_params=_cparams(("parallel", "parallel", "arbitrary")),
        name="stick_breaking_prompt",
    )(big, big, big_meta, _cumsum_rhs(), k_t, v_t)


def attention_step(big, q_row_blk, k_rows, v_rows, k_t, v_t, n_cache_tiles, width):
    b = big.shape[0]
    n_rows = k_rows.shape[1]
    qb0 = BIG_Q // width
    rows = pl.BlockSpec((1, n_rows, width), lambda bi, hp: (bi, 0, hp))
    in_specs = [pl.BlockSpec((1, N_META, width), lambda bi, hp: (bi, q_row_blk, qb0 + hp)), rows, rows,
                pl.BlockSpec((2 * KEY_TILE, 2 * KEY_TILE), lambda bi, hp: (0, 0))]
    args = [big, k_rows, v_rows, _cumsum_rhs()]
    scratch = [pltpu.VMEM((KEY_TILE, width), BF16)] * 2
    if n_cache_tiles:
        near = ATTN_STATIC_TILES * KEY_TILE
        assert n_cache_tiles % ATTN_STATIC_TILES == 0
        near_blk = n_cache_tiles // ATTN_STATIC_TILES - 1
        in_specs += [pl.BlockSpec((1, width, near), lambda bi, hp: (bi, hp, near_blk))] * 2
        in_specs += [pl.BlockSpec(memory_space=pl.ANY)] * 2
        args += [k_t, v_t, k_t, v_t]
        scratch += [pltpu.VMEM((width, KEY_TILE), F32)] * 2
    return pl.pallas_call(
        functools.partial(_attn_step_kernel, n_free=n_rows - N_META, n_cache_tiles=n_cache_tiles),
        grid=(b, SB_WIDTH // width),
        in_specs=in_specs,
        out_specs=pl.BlockSpec((1, N_META, width), lambda bi, hp: (bi, 0, hp)),
        out_shape=jax.ShapeDtypeStruct((b, N_META, SB_WIDTH), BF16),
        scratch_shapes=scratch,
        compiler_params=_cparams(("parallel", "parallel")),
        name="stick_breaking_step",
    )(*args)


def _rows8(op, x, r8):
    return op(x.reshape(x.shape[0] // 8, 8, x.shape[1]), r8[None]).reshape(x.shape)


def _ssd_kernel(xbc_ref, x16_ref, z_ref, dt_ref, dtT_ref, prev_ref, h0_ref, cw_ref, cb_ref, dtb_ref, dtbT_ref,
                alog_ref, alogT_ref, dskip_ref, gn_ref, e3_ref, ltri_ref, ublk_ref, shift_ref,
                y_ref, hfin_ref, st_ref, *, rows, n_sub, n_chunks):
    Q = SSD_CHUNK
    c = pl.program_id(1)
    n_blk = SSD_WIDTH // LANES
    mul, sub = jnp.multiply, jnp.subtract

    @pl.when(c == 0)
    def _():
        for j in range(n_blk):
            st_ref[:, j * LANES:(j + 1) * LANES] = h0_ref[0, j * LANES:(j + 1) * LANES, :].T

    def pad_rows(v):
        if rows == Q:
            return v
        return jnp.concatenate([v, jnp.zeros((Q - rows, v.shape[1]), v.dtype)], axis=0)

    lane = lax.broadcasted_iota(jnp.int32, (Q, LANES), 1)
    rowq = lax.broadcasted_iota(jnp.int32, (Q, LANES), 0)
    causal2 = (lane % Q) <= rowq
    lo_half = lane < SSD_HEAD_DIM
    decay_rate = -LOG2E * jnp.exp(alog_ref[...])
    decay_rate_t = -LOG2E * jnp.exp(alogT_ref[...])

    for k in range(n_sub):
        r0 = k * Q
        hi, lo = _split2(prev_ref[0])
        if k == 0:
            hi = jnp.where(c == 0, hi, x16_ref[0])
            lo = jnp.where(c == 0, lo, jnp.zeros_like(lo))
        else:
            hi, lo = xbc_ref[0, r0 - 16:r0, :], jnp.zeros_like(lo)
        window = jnp.concatenate([pad_rows(xbc_ref[0, r0:r0 + rows, :]), hi, lo,
                                  jnp.zeros((Q - 32, SSD_CONV_CH), BF16)], axis=0)
        shifted = _dot(shift_ref[...], window)
        conv = _rows8(mul, shifted[0:Q], cw_ref[0:8, :])
        for i in range(1, SSD_CONV):
            conv = conv + _rows8(mul, shifted[i * Q:(i + 1) * Q], cw_ref[8 * i:8 * i + 8, :])
        conv = _rows8(jnp.add, conv, cb_ref[...])
        xc = conv * _sigmoid(conv)
        xs = xc[:, :SSD_WIDTH]
        b_all = xc[:, SSD_WIDTH:SSD_WIDTH + SSD_GROUPS * SSD_STATE]
        c_all = xc[:, SSD_WIDTH + SSD_GROUPS * SSD_STATE:]

        dt = _softplus(pad_rows(dt_ref[0, r0:r0 + rows, :]) + dtb_ref[...])
        dt_t = _softplus(dtT_ref[0, k] + dtbT_ref[...])
        if rows < Q:
            rowi = lax.broadcasted_iota(jnp.int32, (Q, 1), 0)
            dt = jnp.where(rowi < rows, dt, 0.0)
            xs = jnp.where(rowi < rows, xs, 0.0)
            lane_t = lax.broadcasted_iota(jnp.int32, dt_t.shape, 1)
            dt_t = jnp.where(lane_t % Q < rows, dt_t, 0.0)
        da = dt * decay_rate
        da_t = dt_t * decay_rate_t

        a_cum = _dot(ltri_ref[...], jnp.concatenate(_split3(da), axis=0))
        a_cum_t = _dot(jnp.concatenate(_split3(da_t), axis=1), ublk_ref[...])
        dt_exp = _dot(jnp.concatenate(_split2(dt), axis=1), e3_ref[0:2 * LANES, :])
        a_exp = _dot(jnp.concatenate(_split3(a_cum), axis=1), e3_ref[...])
        a_last = jnp.broadcast_to(a_exp[Q - 1:Q, :], (8, SSD_WIDTH))

        xdt = xs * dt_exp
        xdt_b = xdt.astype(BF16)
        xw = (xdt * jnp.exp2(-_rows8(sub, a_exp, a_last))).astype(BF16)
        xw_pad = jnp.concatenate([xw, jnp.zeros_like(xw)], axis=0)
        chunk_decay = jnp.exp2(a_last)
        grow = jnp.exp2(a_exp)

        y_parts = []
        for g in range(SSD_GROUPS):
            bg = b_all[:, g * SSD_STATE:(g + 1) * SSD_STATE]
            cg = c_all[:, g * SSD_STATE:(g + 1) * SSD_STATE].astype(BF16)
            bg_b = bg.astype(BF16)
            cb2 = _dot_nt(cg, jnp.concatenate([bg_b, bg_b], axis=0))
            gs = slice(g * GROUP_W, (g + 1) * GROUP_W)
            st_g = st_ref[:, gs]
            y_off = _dot(cg, st_g.astype(BF16)) * grow[:, gs]
            bg_t = jnp.concatenate([bg, jnp.zeros_like(bg)], axis=0).T.astype(BF16)
            st_ref[:, gs] = _rows8(mul, st_g, chunk_decay[:, gs]) + _dot(bg_t, xw_pad[:, gs])
            pair_out = []
            for kk in range(GROUP_W // LANES):
                i = g * (GROUP_W // LANES) + kk
                ps = slice(i * LANES, (i + 1) * LANES)
                a_row = jnp.broadcast_to(a_cum_t[i:i + 1, :], (8, LANES))
                decay = jnp.exp2(jnp.minimum(_rows8(sub, a_exp[:, ps], a_row), 0.0))
                m2 = jnp.where(causal2, cb2 * decay, 0.0).astype(BF16)
                xp = xdt_b[:, ps]
                zero = jnp.zeros_like(xp)
                xbd = jnp.concatenate([jnp.where(lo_half, xp, zero), jnp.where(lo_half, zero, xp)], axis=0)
                pair_out.append(_dot(m2, xbd))
            y_diag = jnp.concatenate(pair_out, axis=1)
            y = y_diag + y_off + _rows8(mul, xs[:, gs], dskip_ref[:, gs])
            zg = pad_rows(z_ref[0, r0:r0 + rows, gs].astype(F32))
            y = y * (zg * _sigmoid(zg))
            ms = jnp.mean(y * y, axis=-1, keepdims=True)
            y_parts.append(_rows8(mul, y * lax.rsqrt(ms + EPS), gn_ref[:, gs]))
        y_all = jnp.concatenate(y_parts, axis=1)
        y_ref[0, r0:r0 + rows, :] = y_all[:rows].astype(y_ref.dtype)

    @pl.when(c == n_chunks - 1)
    def _():
        for j in range(n_blk):
            hfin_ref[0, j * LANES:(j + 1) * LANES, :] = st_ref[:, j * LANES:(j + 1) * LANES].T


def _ssd_constants():
    Q = SSD_CHUNK
    e = np.zeros((LANES, SSD_WIDTH), np.float32)
    for h in range(SSD_HEADS):
        e[h, h * SSD_HEAD_DIM:(h + 1) * SSD_HEAD_DIM] = 1.0
    e3 = np.concatenate([e, e, e], axis=0)
    t = np.arange(Q)
    ltri = (t[:, None] >= t[None, :]).astype(np.float32)
    ltri3 = np.concatenate([ltri, ltri, ltri], axis=1)
    ublk = np.zeros((LANES, LANES), np.float32)
    ublk[:Q, :Q] = ltri.T
    ublk[Q:, Q:] = ltri.T
    ublk3 = np.concatenate([ublk, ublk, ublk], axis=0)
    shift = np.zeros((SSD_CONV * Q, 2 * Q), np.float32)
    for i in range(SSD_CONV):
        for r in range(Q):
            src = r - (SSD_CONV - 1 - i)
            if src >= 0:
                shift[i * Q + r, src] = 1.0
            else:
                shift[i * Q + r, Q + 16 + src] = 1.0
                shift[i * Q + r, Q + 32 + src] = 1.0
    return jnp.asarray(e3, BF16), jnp.asarray(ltri3, BF16), jnp.asarray(ublk3, BF16), jnp.asarray(shift, BF16)


def ssd_mixer(big, xbc_blk, z_blk, dt_raw, conv_prev, h0, p):
    b, s, _ = big.shape
    Q = SSD_CHUNK
    rows = min(s, Q)
    n_sub = SSD_STEP_CHUNKS if s % (SSD_STEP_CHUNKS * Q) == 0 else 1
    blk = rows * n_sub
    n_chunks = -(-s // blk)
    dtp = dt_raw[:, :, :SSD_HEADS]
    if s < Q:
        dtp = jnp.pad(dtp, ((0, 0), (0, Q - s), (0, 0)))
    dt_t = dtp.reshape(b, n_chunks * n_sub, Q, SSD_HEADS).transpose(0, 1, 3, 2).reshape(
        b, n_chunks * n_sub, SSD_HEADS // 2, 2 * Q)
    e3, ltri3, ublk3, shift = _ssd_constants()
    prev16 = jnp.pad(conv_prev, ((0, 0), (16 - (SSD_CONV - 1), 0), (0, 0)))
    rep8 = lambda v: jnp.repeat(v, 8, axis=0)
    const = lambda shape: pl.BlockSpec(shape, lambda bi, ci: (0,) * len(shape))
    in_specs = [
        pl.BlockSpec((1, blk, SSD_CONV_CH), lambda bi, ci: (bi, ci, xbc_blk)),
        pl.BlockSpec((1, 16, SSD_CONV_CH), lambda bi, ci: (bi, jnp.maximum(ci * (blk // 16) - 1, 0), xbc_blk)),
        pl.BlockSpec((1, blk, SSD_WIDTH), lambda bi, ci: (bi, ci, z_blk)),
        pl.BlockSpec((1, blk, LANES), lambda bi, ci: (bi, ci, 0)),
        pl.BlockSpec((1, n_sub, SSD_HEADS // 2, 2 * Q), lambda bi, ci: (bi, ci, 0, 0)),
        pl.BlockSpec((1, 16, SSD_CONV_CH), lambda bi, ci: (bi, 0, 0)),
        pl.BlockSpec((1, SSD_WIDTH, SSD_STATE), lambda bi, ci: (bi, 0, 0)),
        const((8 * SSD_CONV, SSD_CONV_CH)), const((8, SSD_CONV_CH)),
        const((1, LANES)), const((SSD_HEADS // 2, 2 * Q)),
        const((1, LANES)), const((SSD_HEADS // 2, 2 * Q)),
        const((8, SSD_WIDTH)), const((8, SSD_WIDTH)),
        const(e3.shape), const(ltri3.shape), const(ublk3.shape), const(shift.shape),
    ]
    y, h_fin = pl.pallas_call(
        functools.partial(_ssd_kernel, rows=rows, n_sub=n_sub, n_chunks=n_chunks),
        grid=(b, n_chunks),
        in_specs=in_specs,
        out_specs=[pl.BlockSpec((1, blk, SSD_WIDTH), lambda bi, ci: (bi, ci, 0)),
                   pl.BlockSpec((1, SSD_WIDTH, SSD_STATE), lambda bi, ci: (bi, 0, 0))],
        out_shape=[jax.ShapeDtypeStruct((b, s, SSD_WIDTH), BF16),
                   jax.ShapeDtypeStruct((b, SSD_WIDTH, SSD_STATE), F32)],
        scratch_shapes=[pltpu.VMEM((SSD_STATE, SSD_WIDTH), F32)],
        compiler_params=_cparams(("parallel", "arbitrary")),
        name="ssd_mixer",
    )(big, big, big, dt_raw, dt_t, prev16, h0, rep8(p['conv_w_ssd']), rep8(p['conv_b_ssd']), p['dt_bias'], p['dt_bias_t'],
      p['a_log'], p['a_log_t'], rep8(p['d_skip']), rep8(p['g_ssd_norm']), e3, ltri3, ublk3, shift)
    return y, h_fin


def _merge_kernel(att_ref, ssd_ref, ga_ref, gs_ref, h_ref, wa_ref, ws_ref, wo_ref, o_ref, wa_b, ws_b, wo_b):
    @pl.when(pl.program_id(0) == 0)
    def _():
        wa_b[...] = wa_ref[...].astype(BF16)
        ws_b[...] = ws_ref[...].astype(BF16)
        wo_b[...] = wo_ref[...].astype(BF16)

    a = _dot(att_ref[...], wa_b[...])
    s = _dot(ssd_ref[...], ws_b[...])
    merged = _sigmoid(ga_ref[...].astype(F32)) * a + _sigmoid(gs_ref[...].astype(F32)) * s
    o_ref[...] = h_ref[...] + _dot(merged.astype(BF16), wo_b[...])


def _resident(a):
    return pl.BlockSpec(a.shape, lambda i: (0,) * a.ndim, pipeline_mode=pl.Buffered(1))


def merge(att, ssd, big, h, p):
    m = h.shape[0]
    tm = _pick(m, (512, 256, 128))
    row = lambda w: pl.BlockSpec((tm, w), lambda i: (i, 0))
    gate_blk = BIG_GATE // D_MODEL
    ws = [p['w_br_att'], p['w_br_ssd'], p['w_out']]
    return pl.pallas_call(
        _merge_kernel,
        grid=(m // tm,),
        in_specs=[row(SB_WIDTH), row(SSD_WIDTH),
                  pl.BlockSpec((tm, D_MODEL), lambda i: (i, gate_blk)),
                  pl.BlockSpec((tm, D_MODEL), lambda i: (i, gate_blk + 1)),
                  row(D_MODEL)] + [_resident(w) for w in ws],
        out_specs=row(D_MODEL),
        out_shape=jax.ShapeDtypeStruct((m, D_MODEL), F32),
        scratch_shapes=[pltpu.VMEM(w.shape, BF16) for w in ws],
        compiler_params=_cparams(("arbitrary",)),
        name="merge_out_proj",
    )(att, ssd, big, big, h, *ws)


def _ffn_act_kernel(ug_ref, uv_ref, pg_ref, pv_ref, wg_ref, wv_ref, bg_ref, bv_ref, o_ref, gbuf_ref, vbuf_ref,
                    *, ts):
    t = pl.program_id(1)
    k1 = FFN_CONV - 1

    def conv(u_ref, p_ref, w_ref, b_ref, buf_ref):
        @pl.when(t == 0)
        def _():
            buf_ref[8 - k1:8, :] = p_ref[0]

        buf_ref[8:8 + ts, :] = u_ref[0].astype(F32)
        y = b_ref[...]
        for i in range(FFN_CONV):
            y = y + buf_ref[8 - k1 + i:8 - k1 + i + ts, :] * w_ref[i:i + 1, :]
        buf_ref[8 - k1:8, :] = buf_ref[8 + ts - k1:8 + ts, :]
        return y

    gate = conv(ug_ref, pg_ref, wg_ref, bg_ref, gbuf_ref)
    val = conv(uv_ref, pv_ref, wv_ref, bv_ref, vbuf_ref)
    o_ref[0] = (gate * _sigmoid(gate) * val).astype(o_ref.dtype)


FFN_SUB = 64


UPF_TM = 512
UPF_TN = D_FF // 2


def _up_ffn_kernel(h_ref, h16_ref, g_ref, wg_ref, wv_ref, pg_ref, pv_ref, cwg_ref, cwv_ref, cbg_ref, cbv_ref,
                   o_ref, gl_ref, vl_ref, wg_b, wv_b, gbuf_ref, vbuf_ref):
    i = pl.program_id(2)
    tm = h_ref.shape[1]

    @pl.when(jnp.logical_and(pl.program_id(1) == 0, i == 0))
    def _():
        wg_b[...] = wg_ref[...].astype(BF16)
        wv_b[...] = wv_ref[...].astype(BF16)

    u = _rms_norm_bf16(h_ref[0], g_ref[...])
    u16 = _rms_norm_bf16(h16_ref[0], g_ref[...])

    def conv(w_b, p_ref, last_ref, buf_ref, cw_ref, cb_ref):
        up = _dot(u, w_b[...])
        buf_ref[0:16, :] = jnp.where(i == 0, p_ref[0], _dot(u16, w_b[...]))
        buf_ref[16:16 + tm, :] = up
        last_ref[0] = up[tm - 16:tm, :]
        y = _rows8(jnp.multiply, up, cw_ref[8 * (FFN_CONV - 1):8 * FFN_CONV, :])
        for t in range(FFN_CONV - 1):
            r0 = 16 - (FFN_CONV - 1) + t
            y = y + _rows8(jnp.multiply, buf_ref[r0:r0 + tm, :], cw_ref[8 * t:8 * t + 8, :])
        return _rows8(jnp.add, y, cb_ref[...])

    gate = conv(wg_b, pg_ref, gl_ref, gbuf_ref, cwg_ref, cbg_ref)
    val = conv(wv_b, pv_ref, vl_ref, vbuf_ref, cwv_ref, cbv_ref)
    o_ref[0] = (gate * _sigmoid(gate) * val).astype(o_ref.dtype)


def up_ffn_long(h, prev, p):
    b, s, d = h.shape
    tm, tn = UPF_TM, UPF_TN
    nj = D_FF // tn
    prev16 = jnp.pad(prev, ((0, 0), (16 - (FFN_CONV - 1), 0), (0, 0)))
    rep8 = lambda v: jnp.repeat(v, 8, axis=0)
    w8, b8 = rep8(p['conv_w_ffn']), rep8(p['conv_b_ffn'])
    cols = lambda rows, half: pl.BlockSpec((rows, tn), lambda j, bi, i: (0, half * nj + j))
    w_spec = lambda half: pl.BlockSpec((d, tn), lambda j, bi, i: (0, half * nj + j), pipeline_mode=pl.Buffered(1))
    prevs = lambda half: pl.BlockSpec((1, 16, tn), lambda j, bi, i: (bi, 0, half * nj + j))
    last_spec = pl.BlockSpec((1, 16, tn), lambda j, bi, i: (bi, 0, j))
    return pl.pallas_call(
        _up_ffn_kernel,
        grid=(nj, b, s // tm),
        in_specs=[pl.BlockSpec((1, tm, d), lambda j, bi, i: (bi, i, 0)),
                  pl.BlockSpec((1, 16, d), lambda j, bi, i: (bi, jnp.maximum(i * (tm // 16) - 1, 0), 0)),
                  pl.BlockSpec((1, d), lambda j, bi, i: (0, 0)),
                  w_spec(0), w_spec(1), prevs(0), prevs(1),
                  cols(8 * FFN_CONV, 0), cols(8 * FFN_CONV, 1), cols(8, 0), cols(8, 1)],
        out_specs=[pl.BlockSpec((1, tm, tn), lambda j, bi, i: (bi, i, j)), last_spec, last_spec],
        out_shape=[jax.ShapeDtypeStruct((b, s, D_FF), BF16)] + [jax.ShapeDtypeStruct((b, 16, D_FF), F32)] * 2,
        scratch_shapes=[pltpu.VMEM((d, tn), BF16)] * 2 + [pltpu.VMEM((16 + tm, tn), F32)] * 2,
        compiler_params=_cparams(("arbitrary", "arbitrary", "arbitrary")),
        name="up_proj_conv_ffn",
    )(h, h, p['g_ffn'], p['w_up'], p['w_up'], prev16, prev16, w8, w8, b8, b8)


def ffn_act(up, prev, p):
    b, s, _ = up.shape
    ts = _pick(s, (512, 256, 128))
    half = lambda blk: pl.BlockSpec((1, ts, D_FF), lambda bi, ti: (bi, ti, blk))
    prevs = lambda blk: pl.BlockSpec((1, FFN_CONV - 1, D_FF), lambda bi, ti: (bi, 0, blk))
    wspec = lambda blk: pl.BlockSpec((FFN_CONV, D_FF), lambda bi, ti: (0, blk))
    bspec = lambda blk: pl.BlockSpec((1, D_FF), lambda bi, ti: (0, blk))
    return pl.pallas_call(
        functools.partial(_ffn_act_kernel, ts=ts),
        grid=(b, s // ts),
        in_specs=[half(0), half(1), prevs(0), prevs(1), wspec(0), wspec(1), bspec(0), bspec(1)],
        out_specs=pl.BlockSpec((1, ts, D_FF), lambda bi, ti: (bi, ti, 0)),
        out_shape=jax.ShapeDtypeStruct((b, s, D_FF), BF16),
        scratch_shapes=[pltpu.VMEM((8 + ts, D_FF), F32)] * 2,
        compiler_params=_cparams(("parallel", "arbitrary")),
        name="ffn_conv_act",
    )(up, up, prev, prev, p['conv_w_ffn'], p['conv_w_ffn'], p['conv_b_ffn'], p['conv_b_ffn'])


def _down_kernel(a_ref, w_ref, h_ref, g_ref, o_ref, w_b):
    @pl.when(pl.program_id(0) == 0)
    def _():
        w_b[...] = w_ref[...].astype(BF16)

    h = h_ref[...] + _dot(a_ref[...], w_b[...])
    ms = jnp.mean(h * h, axis=-1, keepdims=True)
    o_ref[...] = h * lax.rsqrt(ms + EPS) * g_ref[...]


def down_norm(act, h, p):
    m = h.shape[0]
    tm = _pick(m, (512, 256, 128))
    return pl.pallas_call(
        _down_kernel,
        grid=(m // tm,),
        in_specs=[pl.BlockSpec((tm, D_FF), lambda i: (i, 0)),
                  _resident(p['w_down']),
                  pl.BlockSpec((tm, D_MODEL), lambda i: (i, 0)),
                  pl.BlockSpec((1, D_MODEL), lambda i: (0, 0))],
        out_specs=pl.BlockSpec((tm, D_MODEL), lambda i: (i, 0)),
        out_shape=jax.ShapeDtypeStruct((m, D_MODEL), F32),
        scratch_shapes=[pltpu.VMEM(p['w_down'].shape, BF16)],
        compiler_params=_cparams(("arbitrary",)),
        name="down_proj_norm",
    )(act, p['w_down'], h, p['g_final'])


def mix_and_ffn(big3, att, dt_raw, hf, ssd_h0, ssd_conv_prev, ffn_conv_prev, p):
    b, s, _ = big3.shape
    m = b * s
    ssd, ssd_state = ssd_mixer(big3, BIG_XBC // SSD_CONV_CH, BIG_Z // SSD_WIDTH, dt_raw, ssd_conv_prev, ssd_h0, p)
    h1 = merge(att.reshape(m, SB_WIDTH), ssd.reshape(m, SSD_WIDTH), big3.reshape(m, BIG_W), hf, p)
    if s % UPF_TM == 0:
        act, up_g, up_v = up_ffn_long(h1.reshape(b, s, D_MODEL), ffn_conv_prev, p)
        up_last = jnp.concatenate([up_g, up_v], axis=2)
    else:
        up_last = norm_mm(h1, p['g_ffn'], p['w_up'], BF16).reshape(b, s, 2 * D_FF)
        act = ffn_act(up_last, ffn_conv_prev, p)
    y = down_norm(act.reshape(m, D_FF), h1, p).reshape(b, s, D_MODEL)
    xbc_rows = jnp.concatenate([ssd_conv_prev, big3[:, :, BIG_XBC:BIG_XBC + SSD_CONV_CH][:, -(SSD_CONV - 1):].astype(F32)],
                               axis=1)[:, -(SSD_CONV - 1):]
    up_rows = jnp.concatenate([ffn_conv_prev, up_last[:, -(FFN_CONV - 1):].astype(F32)], axis=1)[:, -(FFN_CONV - 1):]
    return y, ssd_state, xbc_rows, up_rows


def _prep_params(g_mix, w_in, conv_w_ssd, conv_b_ssd, dt_bias, a_log, d_skip, g_ssd_norm, w_br_att, w_br_ssd,
                 w_out, g_ffn, w_up, conv_w_ffn, conv_b_ffn, w_down, g_final):
    w_t = w_in.T
    dt0 = 3 * SB_WIDTH + SSD_WIDTH + SSD_CONV_CH
    w_tail_t = jnp.concatenate([w_t[dt0 + SSD_HEADS:], w_t[dt0:dt0 + SSD_HEADS],
                                jnp.zeros((LANES - SSD_HEADS, D_MODEL), F32)], axis=0)
    Q = SSD_CHUNK
    lanes_t = lambda v: jnp.repeat(v.reshape(SSD_HEADS // 2, 2, 1), Q, axis=2).reshape(SSD_HEADS // 2, 2 * Q)
    pad_l = lambda v: jnp.pad(v, (0, LANES - SSD_HEADS)).reshape(1, LANES)
    return {
        'g_mix': g_mix.reshape(1, -1), 'g_ffn': g_ffn.reshape(1, -1), 'g_final': g_final.reshape(1, -1),
        'w_t': w_t, 'w_tail_t': w_tail_t,
        'conv_w_ssd': conv_w_ssd, 'conv_b_ssd': conv_b_ssd.reshape(1, -1),
        'dt_bias': pad_l(dt_bias), 'dt_bias_t': lanes_t(dt_bias),
        'a_log': pad_l(a_log), 'a_log_t': lanes_t(a_log),
        'd_skip': jnp.repeat(d_skip, SSD_HEAD_DIM).reshape(1, -1), 'g_ssd_norm': g_ssd_norm.reshape(1, -1),
        'w_br_att': w_br_att, 'w_br_ssd': w_br_ssd, 'w_out': w_out,
        'w_up': w_up, 'conv_w_ffn': conv_w_ffn, 'conv_b_ffn': conv_b_ffn.reshape(1, -1),
        'w_down': w_down,
    }


def kernel(x_prompt, x_sample, cache_k, cache_v, state_ssm, state_ssm_conv, state_ffn_conv, meta_tokens, g_mix, w_in, conv_w_ssd, conv_b_ssd, dt_bias, a_log, d_skip, g_ssd_norm, w_br_att, w_br_ssd, w_out, g_ffn, w_up, conv_w_ffn, conv_b_ffn, w_down, g_final):
    bp, seq, _ = x_prompt.shape
    bs, dec = x_sample.shape[:2]
    assert dec == N_META and seq % KVT_TS == 0
    p = _prep_params(g_mix[0], w_in[0], conv_w_ssd[0], conv_b_ssd[0], dt_bias[0], a_log[0], d_skip[0],
                     g_ssd_norm[0], w_br_att[0], w_br_ssd[0], w_out[0], g_ffn[0], w_up[0], conv_w_ffn[0],
                     conv_b_ffn[0], w_down[0], g_final)
    proj = functools.partial(in_proj, g=p['g_mix'], w_t=p['w_t'], w_tail_t=p['w_tail_t'])
    state_shape = (SSD_WIDTH, SSD_STATE)
    rows = lambda a, b: a.reshape(b, -1, a.shape[-1])

    k_m, v_m, big_m, dt_m = proj(meta_tokens, with_kv=True)
    big_m, k_m, v_m = big_m[None], k_m[None], v_m[None]
    att_m = attention_step(big_m, 0, k_m, v_m, None, None, 0, SB_WIDTH)
    _, ssm_m, conv_m, ffn_m = mix_and_ffn(
        big_m, att_m, dt_m[None], meta_tokens, jnp.zeros((1,) + state_shape, F32),
        jnp.zeros((1, SSD_CONV - 1, SSD_CONV_CH), F32), jnp.zeros((1, FFN_CONV - 1, 2 * D_FF), F32), p)

    xf = x_prompt.reshape(bp * seq, D_MODEL)
    big_x, dt_x = proj(xf, with_kv=False)
    big_x = rows(big_x, bp)
    k_t, v_t, k_end, v_end = kv_transposed(x_prompt, meta_tokens, p['g_mix'], p['w_t'])
    att_x = attention_prompt(big_x, big_m, k_t, v_t)
    att_end = attention_step(big_x, seq // N_META - 1, k_end, v_end, k_t, v_t, seq // KEY_TILE, ATTN_PROMPT_WIDTH)
    att_x = lax.dynamic_update_slice(att_x, att_end, (0, seq - N_META, 0))
    rep = lambda a: jnp.broadcast_to(a, (bp,) + a.shape[1:])
    y_prompt, ssm_p, conv_p, ffn_p = mix_and_ffn(
        big_x, att_x, rows(dt_x, bp), xf, rep(ssm_m), rep(conv_m), rep(ffn_m), p)

    n_rows = cache_k.shape[2]
    n_tiles = (n_rows - N_META) // KEY_TILE
    assert n_tiles * KEY_TILE + N_META == n_rows
    k_s, v_s, big_s, dt_s = proj(x_sample.reshape(bs * dec, D_MODEL), with_kv=True)
    big_s, k_s, v_s = rows(big_s, bs), rows(k_s, bs), rows(v_s, bs)
    cache_t = lambda c: c[0].transpose(0, 2, 3, 1).reshape(bs, SB_WIDTH, n_rows)
    nearest = lambda c, new: jnp.concatenate([c[0][:, n_tiles * KEY_TILE:].reshape(bs, N_META, SB_WIDTH), new], axis=1)
    att_s = attention_step(big_s, 0, nearest(cache_k, k_s), nearest(cache_v, v_s), cache_t(cache_k), cache_t(cache_v),
                           n_tiles, SB_WIDTH)
    y_sample, ssm_s, conv_s, ffn_s = mix_and_ffn(
        big_s, att_s, rows(dt_s, bs), x_sample.reshape(bs * dec, D_MODEL),
        state_ssm[0].reshape((bs,) + state_shape), state_ssm_conv[0], state_ffn_conv[0], p)

    heads = lambda a: a.reshape(a.shape[0], a.shape[1], SB_HEADS, SB_HEAD_DIM)[None]
    heads_t = lambda a: a.reshape(a.shape[0], SB_HEADS, SB_HEAD_DIM, a.shape[2]).transpose(0, 3, 1, 2)[None]
    state5 = lambda a: a.reshape(a.shape[0], SSD_HEADS, SSD_HEAD_DIM, SSD_STATE)[None]
    return (y_prompt, y_sample, heads_t(k_t), heads_t(v_t), state5(ssm_p), conv_p[None], ffn_p[None],
            heads(k_s), heads(v_s), state5(ssm_s), conv_s[None], ffn_s[None])
```

```python
import functools

import numpy as np
import jax
import jax.numpy as jnp
from jax import lax
from jax.experimental import pallas as pl
from jax.experimental.pallas import tpu as pltpu

F32 = jnp.float32
BF16 = jnp.bfloat16

D_MODEL = 1024
N_META = 16
SB_HEADS = 16
SB_HEAD_DIM = 64
SB_WIDTH = SB_HEADS * SB_HEAD_DIM
SSD_WIDTH = 2 * D_MODEL
SSD_HEAD_DIM = 64
SSD_HEADS = SSD_WIDTH // SSD_HEAD_DIM
SSD_GROUPS = 4
SSD_STATE = 128
SSD_CONV = 4
SSD_CONV_CH = SSD_WIDTH + 2 * SSD_GROUPS * SSD_STATE
D_FF = 2816
FFN_CONV = 3
EPS = 1e-6

LANES = 128
SSD_CHUNK = 64
SSD_STEP_CHUNKS = 4
KEY_TILE = 128
GROUP_W = SSD_WIDTH // SSD_GROUPS
DEAD_LOG = -104.0
LOG2E = 1.4426950408889634
ATTN_STATIC_TILES = 2
VMEM_LIMIT = 56 * 1024 * 1024


def _cparams(sem):
    return pltpu.CompilerParams(dimension_semantics=sem, vmem_limit_bytes=VMEM_LIMIT)


def _pick(n, cands):
    for c in cands:
        if n % c == 0:
            return c
    return n


def _split2(x):
    hi = x.astype(BF16)
    lo = (x - hi.astype(F32)).astype(BF16)
    return hi, lo


def _split3(x):
    hi = x.astype(BF16)
    r = x - hi.astype(F32)
    mid = r.astype(BF16)
    lo = (r - mid.astype(F32)).astype(BF16)
    return hi, mid, lo


def _dot(a, b):
    return jnp.dot(a, b, preferred_element_type=F32)


def _dot_nt(a, b):
    return lax.dot_general(a, b, (((1,), (1,)), ((), ())), preferred_element_type=F32)


def _softplus(x):
    return jnp.maximum(x, 0.0) + jnp.log(1.0 + jnp.exp(-jnp.abs(x)))


def _sigmoid(x):
    return 1.0 / (1.0 + jnp.exp(-x))


def _rms_norm_bf16(x, g):
    ms = jnp.mean(x * x, axis=-1, keepdims=True)
    return (x * lax.rsqrt(ms + EPS) * g).astype(BF16)


def _norm_mm_kernel(x_ref, g_ref, w_ref, o_ref, u_ref):
    @pl.when(pl.program_id(1) == 0)
    def _():
        u_ref[...] = _rms_norm_bf16(x_ref[...], g_ref[...])

    o_ref[...] = _dot(u_ref[...], w_ref[...].astype(BF16)).astype(o_ref.dtype)


PROJ_TN = 1024
UP_TN = 1408


def norm_mm(x, g, w, out_dtype):
    m, d = x.shape
    n = w.shape[1]
    tm = _pick(m, (2048, 1024, 512, 256, 128))
    tn = UP_TN
    return pl.pallas_call(
        _norm_mm_kernel,
        grid=(m // tm, n // tn),
        in_specs=[pl.BlockSpec((tm, d), lambda i, j: (i, 0)),
                  pl.BlockSpec((1, d), lambda i, j: (0, 0)),
                  pl.BlockSpec((d, tn), lambda i, j: (0, j))],
        out_specs=pl.BlockSpec((tm, tn), lambda i, j: (i, j)),
        out_shape=jax.ShapeDtypeStruct((m, n), out_dtype),
        scratch_shapes=[pltpu.VMEM((tm, d), BF16)],
        compiler_params=_cparams(("parallel", "arbitrary")),
        name="norm_up_proj",
    )(x, g, w)


_NQ = SB_WIDTH // PROJ_TN
_J_K, _J_V, _J_Z = _NQ, 2 * _NQ, 3 * _NQ
_J_X = _J_Z + SSD_WIDTH // PROJ_TN
_J_G = _J_X + SSD_CONV_CH // PROJ_TN
_J_END = _J_G + 2 * D_MODEL // PROJ_TN
BIG_Z, BIG_Q, BIG_XBC, BIG_GATE = 0, SSD_WIDTH, SSD_WIDTH + SB_WIDTH, SSD_WIDTH + SB_WIDTH + SSD_CONV_CH
BIG_W = BIG_GATE + 2 * D_MODEL


def _big_block(j):
    return jnp.where(j < _J_K, BIG_Q // PROJ_TN + j,
                     jnp.where(j < _J_Z, BIG_Q // PROJ_TN + _NQ - 1,
                               jnp.where(j < _J_X, j - _J_Z + BIG_Z // PROJ_TN,
                                         jnp.where(j < _J_G, j - _J_X + BIG_XBC // PROJ_TN,
                                                   j - _J_G + BIG_GATE // PROJ_TN))))


def _sweep_step(j, with_kv):
    return j if with_kv else jnp.where(j >= _J_K, j + (_J_Z - _J_K), j)


def _in_proj_kernel(*refs, with_kv):
    if with_kv:
        x_ref, g_ref, win_ref, wgate_ref, wdt_ref, k_ref, v_ref, big_ref, dt_ref, u_ref = refs
    else:
        x_ref, g_ref, win_ref, wgate_ref, wdt_ref, big_ref, dt_ref, u_ref = refs
    j = _sweep_step(pl.program_id(1), with_kv)

    def proj(w_ref):
        return _dot_nt(u_ref[...], w_ref[...].astype(BF16))

    @pl.when(pl.program_id(1) == 0)
    def _():
        u_ref[...] = _rms_norm_bf16(x_ref[...], g_ref[...])
        dt_ref[...] = proj(wdt_ref)

    if with_kv:
        @pl.when(jnp.logical_and(j >= _J_K, j < _J_V))
        def _():
            k_ref[...] = proj(win_ref)

        @pl.when(jnp.logical_and(j >= _J_V, j < _J_Z))
        def _():
            v_ref[...] = proj(win_ref)

    @pl.when(jnp.logical_or(j < _J_K, jnp.logical_and(j >= _J_Z, j < _J_G)))
    def _():
        big_ref[...] = proj(win_ref).astype(BF16)

    @pl.when(j >= _J_G)
    def _():
        big_ref[...] = proj(wgate_ref).astype(BF16)


def in_proj(x, g, w_t, w_tail_t, with_kv):
    m, d = x.shape
    tm = _pick(m, (2048, 1024, 512, 256, 128))
    tn = PROJ_TN
    step = functools.partial(_sweep_step, with_kv=with_kv)
    clip = lambda j, lo, n: jnp.clip(step(j) - lo, 0, n - 1)
    kv_specs = [pl.BlockSpec((tm, tn), lambda i, j: (i, clip(j, _J_K, _NQ))),
                pl.BlockSpec((tm, tn), lambda i, j: (i, clip(j, _J_V, _NQ)))]
    kv_shapes = [jax.ShapeDtypeStruct((m, SB_WIDTH), F32)] * 2
    return pl.pallas_call(
        functools.partial(_in_proj_kernel, with_kv=with_kv),
        grid=(m // tm, _J_END if with_kv else _J_END - (_J_Z - _J_K)),
        in_specs=[pl.BlockSpec((tm, d), lambda i, j: (i, 0), pipeline_mode=pl.Buffered(1)),
                  pl.BlockSpec((1, d), lambda i, j: (0, 0)),
                  pl.BlockSpec((tn, d), lambda i, j: (jnp.minimum(step(j), _J_G - 1), 0)),
                  pl.BlockSpec((tn, d), lambda i, j: (clip(j, _J_G, _J_END - _J_G), 0)),
                  pl.BlockSpec((LANES, d), lambda i, j: (2 * D_MODEL // LANES, 0))],
        out_specs=(kv_specs if with_kv else []) + [
            pl.BlockSpec((tm, tn), lambda i, j: (i, _big_block(step(j)))),
            pl.BlockSpec((tm, LANES), lambda i, j: (i, 0))],
        out_shape=(kv_shapes if with_kv else []) + [
            jax.ShapeDtypeStruct((m, BIG_W), BF16), jax.ShapeDtypeStruct((m, LANES), F32)],
        scratch_shapes=[pltpu.VMEM((tm, d), BF16)],
        compiler_params=_cparams(("parallel", "arbitrary")),
        name="in_proj",
    )(x, g, w_t, w_tail_t, w_tail_t)


KVT_TS = 1024


def _kv_t_kernel(x16_ref, xt_ref, meta_ref, g_ref, wk_ref, wv_ref, kt_ref, vt_ref, kl_ref, vl_ref, wk_b, wv_b):
    j = pl.program_id(1)

    @pl.when(jnp.logical_and(pl.program_id(0) == 0, j == 0))
    def _():
        wk_b[...] = wk_ref[...].astype(BF16)
        wv_b[...] = wv_ref[...].astype(BF16)

    first = jnp.where(j == 0, meta_ref[...], x16_ref[0])
    x = jnp.concatenate([first, xt_ref[0, :KVT_TS - N_META, :]], axis=0)
    u = _rms_norm_bf16(x, g_ref[...])
    kt_ref[0] = _dot_nt(wk_b[...], u)
    vt_ref[0] = _dot_nt(wv_b[...], u)

    @pl.when(j == pl.num_programs(1) - 1)
    def _():
        kl_ref[0] = _dot_nt(u[0:N_META, :], wk_b[...])
        vl_ref[0] = _dot_nt(u[0:N_META, :], wv_b[...])


def kv_transposed(x, meta, g, w_t):
    b, s, d = x.shape
    n_pos = N_META + s
    n_tiles = -(-n_pos // KVT_TS)
    assert n_pos - (n_tiles - 1) * KVT_TS == N_META
    per_tile = KVT_TS // N_META
    out_spec = pl.BlockSpec((1, SB_WIDTH, KVT_TS), lambda bi, j: (bi, 0, j))
    last_spec = pl.BlockSpec((1, N_META, SB_WIDTH), lambda bi, j: (bi, 0, 0))
    w_spec = lambda blk: pl.BlockSpec((SB_WIDTH, d), lambda bi, j: (blk, 0), pipeline_mode=pl.Buffered(1))
    return pl.pallas_call(
        _kv_t_kernel,
        grid=(b, n_tiles),
        in_specs=[pl.BlockSpec((1, N_META, d), lambda bi, j: (bi, jnp.maximum(j * per_tile - 1, 0), 0)),
                  pl.BlockSpec((1, KVT_TS, d), lambda bi, j: (bi, jnp.minimum(j, s // KVT_TS - 1), 0)),
                  pl.BlockSpec((N_META, d), lambda bi, j: (0, 0)),
                  pl.BlockSpec((1, d), lambda bi, j: (0, 0)),
                  w_spec(1), w_spec(2)],
        out_specs=[out_spec, out_spec, last_spec, last_spec],
        out_shape=[jax.ShapeDtypeStruct((b, SB_WIDTH, n_pos), F32)] * 2
        + [jax.ShapeDtypeStruct((b, N_META, SB_WIDTH), F32)] * 2,
        scratch_shapes=[pltpu.VMEM((SB_WIDTH, d), BF16)] * 2,
        compiler_params=_cparams(("arbitrary", "arbitrary")),
        name="kv_transposed",
    )(x, x, meta, g, w_t, w_t)


def _stick_breaking_tiles(qs, u2, tiles, carries, accs):
    n_heads = len(qs)
    units = [(t, h) for t in range(len(tiles)) for h in range(n_heads)]
    ps = lambda h: slice((h // 2) * LANES, (h // 2 + 1) * LANES)

    def scores(t, h):
        kind, keys = tiles[t][0], tiles[t][1]
        return _dot_nt(qs[h], keys[:, ps(h)]) if kind == 'rows' else _dot(qs[h], keys[ps(h), :])

    z = {u: scores(*u) for u in units}
    log_beta, cat = {}, {}
    for u in units:
        mask = tiles[u[0]][3]
        soft = jnp.log(1.0 + jnp.exp(-jnp.abs(z[u])))
        log_beta[u] = jnp.minimum(z[u], 0.0) - soft
        log_keep = log_beta[u] - z[u]
        if mask is not None:
            log_keep = jnp.where(mask, log_keep, 0.0)
        cat[u] = jnp.concatenate(_split2(log_keep), axis=1)
    r = {u: _dot(cat[u], u2) for u in units}
    carries, accs = list(carries), list(accs)
    w = {}
    for t, h in units:
        mask = tiles[t][3]
        wt = jnp.exp(log_beta[(t, h)] + r[(t, h)][:, :LANES] + carries[h])
        if mask is not None:
            wt = jnp.where(mask, wt, 0.0)
        w[(t, h)] = wt.astype(BF16)
        carries[h] = carries[h] + r[(t, h)][:, LANES:]
    for t, h in units:
        kind, vals = tiles[t][0], tiles[t][2]
        pv = _dot(w[(t, h)], vals[:, ps(h)]) if kind == 'rows' else _dot_nt(w[(t, h)], vals[ps(h), :])
        accs[h] = accs[h] + pv
    return carries, accs


def _head_queries(q, lo_half):
    qs = []
    for h in range(q.shape[1] // SB_HEAD_DIM):
        qp = q[:, (h // 2) * LANES:(h // 2 + 1) * LANES]
        keep = lo_half if h % 2 == 0 else jnp.logical_not(lo_half)
        qs.append(jnp.where(keep, qp, jnp.zeros_like(qp)) * jnp.asarray(SB_HEAD_DIM ** -0.5, BF16))
    return qs


def _any_alive(carries):
    m = carries[0]
    for c in carries[1:]:
        m = jnp.maximum(m, c)
    return (jnp.max(m) > DEAD_LOG).astype(jnp.int32)


def _attend(qs, u2, first_tiles, load_tile, j0, lane):
    tq = qs[0].shape[0]

    def cache_tile(j):
        if isinstance(j, int):
            mask = None if j >= 0 else lane < 0
            kt, vt = load_tile(max(j, 0))
        else:
            mask = jnp.logical_and(j >= 0, lane >= 0)
            kt, vt = load_tile(jnp.maximum(j, 0))
        return 'cols', kt.astype(BF16), vt.astype(BF16), mask

    tiles = list(first_tiles)
    if load_tile is not None:
        for _ in range(ATTN_STATIC_TILES):
            tiles.append(cache_tile(j0))
            j0 = j0 - 1
    zeros = jnp.zeros((tq, LANES), F32)
    carries, accs = _stick_breaking_tiles(qs, u2, tiles, [zeros] * len(qs), [zeros] * len(qs))
    if load_tile is not None:
        def cond(s):
            return jnp.logical_and(s[0] >= 0, s[1] > 0)

        def body(s):
            j, _, carries, accs = s
            carries, accs = _stick_breaking_tiles(qs, u2, [cache_tile(j)], carries, accs)
            return j - 1, _any_alive(carries), tuple(carries), tuple(accs)

        state = (jnp.asarray(j0, jnp.int32), _any_alive(carries), tuple(carries), tuple(accs))
        _, _, carries, accs = lax.while_loop(cond, body, state)
    return accs


def _pair_outputs(accs, lo_half, dtype):
    return [jnp.where(lo_half, accs[2 * p], accs[2 * p + 1]).astype(dtype) for p in range(len(accs) // 2)]


def _attn_prompt_kernel(q16_ref, qt_ref, qm_ref, u2_ref, kt_ref, vt_ref, o_ref):
    step = pl.program_id(2)
    tq = KEY_TILE
    lane = lax.broadcasted_iota(jnp.int32, (tq, LANES), 1)
    row = lax.broadcasted_iota(jnp.int32, (tq, LANES), 0)
    lo_half = lane < SB_HEAD_DIM

    def load_tile(j):
        off = pl.multiple_of(j * KEY_TILE, KEY_TILE)
        return kt_ref[0, :, pl.ds(off, KEY_TILE)], vt_ref[0, :, pl.ds(off, KEY_TILE)]

    for sub in range(ATTN_STEP_TILES):
        m = step * ATTN_STEP_TILES + sub
        if sub == 0:
            first = jnp.where(step == 0, qm_ref[0], q16_ref[0])
            q = jnp.concatenate([first, qt_ref[0, :tq - N_META, :]], axis=0)
        else:
            q = qt_ref[0, sub * tq - N_META:(sub + 1) * tq - N_META, :]
        qs = _head_queries(q, lo_half)
        kd, vd = load_tile(m)
        diag = ('cols', kd.astype(BF16), vd.astype(BF16), lane < row)
        accs = _attend(qs, u2_ref[...], [diag], load_tile, m - 1, lane)
        out = jnp.concatenate(_pair_outputs(accs, lo_half, o_ref.dtype), axis=1)
        if sub == 0:
            @pl.when(step == 0)
            def _(out=out):
                o_ref[0, 0:tq - N_META, :] = out[N_META:, :]
                o_ref[0, o_ref.shape[1] - N_META:, :] = jnp.zeros((N_META, o_ref.shape[2]), o_ref.dtype)

            @pl.when(step > 0)
            def _(out=out, m=m):
                o_ref[0, pl.ds(pl.multiple_of(m * KEY_TILE - N_META, N_META), tq), :] = out
        else:
            o_ref[0, pl.ds(pl.multiple_of(m * KEY_TILE - N_META, N_META), tq), :] = out


def _attn_step_kernel(*refs, n_free, n_cache_tiles):
    if n_cache_tiles:
        (q_ref, kd_ref, vd_ref, u2_ref, kn_ref, vn_ref, kt_any, vt_any, o_ref,
         kd_scr, vd_scr, kbuf, vbuf) = refs
        width = q_ref.shape[2]
        n_near = kn_ref.shape[2] // KEY_TILE

        def load_tile(j):
            if isinstance(j, int) and j >= n_cache_tiles - n_near:
                ls = slice((j - (n_cache_tiles - n_near)) * KEY_TILE, (j - (n_cache_tiles - n_near) + 1) * KEY_TILE)
                return kn_ref[0, :, ls], vn_ref[0, :, ls]
            off = j * KEY_TILE if isinstance(j, int) else pl.multiple_of(j * KEY_TILE, KEY_TILE)
            src = (pl.program_id(0), pl.ds(pl.program_id(1) * width, width), pl.ds(off, KEY_TILE))
            pltpu.sync_copy(kt_any.at[src], kbuf)
            pltpu.sync_copy(vt_any.at[src], vbuf)
            return kbuf[...], vbuf[...]
    else:
        q_ref, kd_ref, vd_ref, u2_ref, o_ref, kd_scr, vd_scr = refs
        load_tile = None
    tq = q_ref.shape[1]
    n_rows = kd_ref.shape[1]
    lane = lax.broadcasted_iota(jnp.int32, (tq, LANES), 1)
    row = lax.broadcasted_iota(jnp.int32, (tq, LANES), 0)
    lo_half = lane < SB_HEAD_DIM
    kd_scr[...] = jnp.zeros_like(kd_scr)
    vd_scr[...] = jnp.zeros_like(vd_scr)
    kd_scr[0:n_rows, :] = kd_ref[0].astype(BF16)
    vd_scr[0:n_rows, :] = vd_ref[0].astype(BF16)
    diag = ('rows', kd_scr[...], vd_scr[...], lane < row + n_free)
    qs = _head_queries(q_ref[0], lo_half)
    accs = _attend(qs, u2_ref[...], [diag], load_tile, n_cache_tiles - 1, lane)
    for p, out in enumerate(_pair_outputs(accs, lo_half, o_ref.dtype)):
        o_ref[0, :, p * LANES:(p + 1) * LANES] = out


def _cumsum_rhs():
    s = np.arange(KEY_TILE)
    strict = (s[:, None] > s[None, :]).astype(np.float32)
    half = np.concatenate([strict, np.ones((KEY_TILE, KEY_TILE), np.float32)], axis=1)
    return jnp.asarray(np.concatenate([half, half], axis=0), BF16)


ATTN_PROMPT_WIDTH = 8 * SB_HEAD_DIM
ATTN_STEP_TILES = 2


def attention_prompt(big, big_meta, k_t, v_t):
    b, s, _ = big.shape
    width = ATTN_PROMPT_WIDTH
    qb0 = BIG_Q // width
    rows = ATTN_STEP_TILES * KEY_TILE
    per_step = rows // N_META
    n_pos = k_t.shape[2]
    cache = pl.BlockSpec((1, width, n_pos), lambda bi, hp, m: (bi, hp, 0))
    return pl.pallas_call(
        _attn_prompt_kernel,
        grid=(b, SB_WIDTH // width, s // rows),
        in_specs=[pl.BlockSpec((1, N_META, width), lambda bi, hp, m: (bi, jnp.maximum(m * per_step - 1, 0), qb0 + hp)),
                  pl.BlockSpec((1, rows, width), lambda bi, hp, m: (bi, m, qb0 + hp)),
                  pl.BlockSpec((1, N_META, width), lambda bi, hp, m: (0, 0, qb0 + hp)),
                  pl.BlockSpec((2 * KEY_TILE, 2 * KEY_TILE), lambda bi, hp, m: (0, 0)),
                  cache, cache],
        out_specs=pl.BlockSpec((1, s, width), lambda bi, hp, m: (bi, 0, hp)),
        out_shape=jax.ShapeDtypeStruct((b, s, SB_WIDTH), BF16),
        compiler_params=_cparams(("parallel", "parallel", "arbitrary")),
        name="stick_breaking_prompt",
    )(big, big, big_meta, _cumsum_rhs(), k_t, v_t)


def attention_step(big, q_row_blk, k_rows, v_rows, k_t, v_t, n_cache_tiles, width):
    b = big.shape[0]
    n_rows = k_rows.shape[1]
    qb0 = BIG_Q // width
    rows = pl.BlockSpec((1, n_rows, width), lambda bi, hp: (bi, 0, hp))
    in_specs = [pl.BlockSpec((1, N_META, width), lambda bi, hp: (bi, q_row_blk, qb0 + hp)), rows, rows,
                pl.BlockSpec((2 * KEY_TILE, 2 * KEY_TILE), lambda bi, hp: (0, 0))]
    args = [big, k_rows, v_rows, _cumsum_rhs()]
    scratch = [pltpu.VMEM((KEY_TILE, width), BF16)] * 2
    if n_cache_tiles:
        near = ATTN_STATIC_TILES * KEY_TILE
        assert n_cache_tiles % ATTN_STATIC_TILES == 0
        near_blk = n_cache_tiles // ATTN_STATIC_TILES - 1
        in_specs += [pl.BlockSpec((1, width, near), lambda bi, hp: (bi, hp, near_blk))] * 2
        in_specs += [pl.BlockSpec(memory_space=pl.ANY)] * 2
        args += [k_t, v_t, k_t, v_t]
        scratch += [pltpu.VMEM((width, KEY_TILE), F32)] * 2
    return pl.pallas_call(
        functools.partial(_attn_step_kernel, n_free=n_rows - N_META, n_cache_tiles=n_cache_tiles),
        grid=(b, SB_WIDTH // width),
        in_specs=in_specs,
        out_specs=pl.BlockSpec((1, N_META, width), lambda bi, hp: (bi, 0, hp)),
        out_shape=jax.ShapeDtypeStruct((b, N_META, SB_WIDTH), BF16),
        scratch_shapes=scratch,
        compiler_params=_cparams(("parallel", "parallel")),
        name="stick_breaking_step",
    )(*args)


def _rows8(op, x, r8):
    return op(x.reshape(x.shape[0] // 8, 8, x.shape[1]), r8[None]).reshape(x.shape)


def _ssd_kernel(xbc_ref, x16_ref, z_ref, dt_ref, dtT_ref, prev_ref, h0_ref, cw_ref, cb_ref, dtb_ref, dtbT_ref,
                alog_ref, alogT_ref, dskip_ref, gn_ref, e3_ref, ltri_ref, ublk_ref, shift_ref,
                y_ref, hfin_ref, st_ref, *, rows, n_sub, n_chunks):
    Q = SSD_CHUNK
    c = pl.program_id(1)
    n_blk = SSD_WIDTH // LANES
    mul, sub = jnp.multiply, jnp.subtract

    @pl.when(c == 0)
    def _():
        for j in range(n_blk):
            st_ref[:, j * LANES:(j + 1) * LANES] = h0_ref[0, j * LANES:(j + 1) * LANES, :].T

    def pad_rows(v):
        if rows == Q:
            return v
        return jnp.concatenate([v, jnp.zeros((Q - rows, v.shape[1]), v.dtype)], axis=0)

    lane = lax.broadcasted_iota(jnp.int32, (Q, LANES), 1)
    rowq = lax.broadcasted_iota(jnp.int32, (Q, LANES), 0)
    causal2 = (lane % Q) <= rowq
    lo_half = lane < SSD_HEAD_DIM
    decay_rate = -LOG2E * jnp.exp(alog_ref[...])
    decay_rate_t = -LOG2E * jnp.exp(alogT_ref[...])

    for k in range(n_sub):
        r0 = k * Q
        hi, lo = _split2(prev_ref[0])
        if k == 0:
            hi = jnp.where(c == 0, hi, x16_ref[0])
            lo = jnp.where(c == 0, lo, jnp.zeros_like(lo))
        else:
            hi, lo = xbc_ref[0, r0 - 16:r0, :], jnp.zeros_like(lo)
        window = jnp.concatenate([pad_rows(xbc_ref[0, r0:r0 + rows, :]), hi, lo,
                                  jnp.zeros((Q - 32, SSD_CONV_CH), BF16)], axis=0)
        shifted = _dot(shift_ref[...], window)
        conv = _rows8(mul, shifted[0:Q], cw_ref[0:8, :])
        for i in range(1, SSD_CONV):
            conv = conv + _rows8(mul, shifted[i * Q:(i + 1) * Q], cw_ref[8 * i:8 * i + 8, :])
        conv = _rows8(jnp.add, conv, cb_ref[...])
        xc = conv * _sigmoid(conv)
        xs = xc[:, :SSD_WIDTH]
        b_all = xc[:, SSD_WIDTH:SSD_WIDTH + SSD_GROUPS * SSD_STATE]
        c_all = xc[:, SSD_WIDTH + SSD_GROUPS * SSD_STATE:]

        dt = _softplus(pad_rows(dt_ref[0, r0:r0 + rows, :]) + dtb_ref[...])
        dt_t = _softplus(dtT_ref[0, k] + dtbT_ref[...])
        if rows < Q:
            rowi = lax.broadcasted_iota(jnp.int32, (Q, 1), 0)
            dt = jnp.where(rowi < rows, dt, 0.0)
            xs = jnp.where(rowi < rows, xs, 0.0)
            lane_t = lax.broadcasted_iota(jnp.int32, dt_t.shape, 1)
            dt_t = jnp.where(lane_t % Q < rows, dt_t, 0.0)
        da = dt * decay_rate
        da_t = dt_t * decay_rate_t

        a_cum = _dot(ltri_ref[...], jnp.concatenate(_split3(da), axis=0))
        a_cum_t = _dot(jnp.concatenate(_split3(da_t), axis=1), ublk_ref[...])
        dt_exp = _dot(jnp.concatenate(_split2(dt), axis=1), e3_ref[0:2 * LANES, :])
        a_exp = _dot(jnp.concatenate(_split3(a_cum), axis=1), e3_ref[...])
        a_last = jnp.broadcast_to(a_exp[Q - 1:Q, :], (8, SSD_WIDTH))

        xdt = xs * dt_exp
        xdt_b = xdt.astype(BF16)
        xw = (xdt * jnp.exp2(-_rows8(sub, a_exp, a_last))).astype(BF16)
        xw_pad = jnp.concatenate([xw, jnp.zeros_like(xw)], axis=0)
        chunk_decay = jnp.exp2(a_last)
        grow = jnp.exp2(a_exp)

        y_parts = []
        for g in range(SSD_GROUPS):
            bg = b_all[:, g * SSD_STATE:(g + 1) * SSD_STATE]
            cg = c_all[:, g * SSD_STATE:(g + 1) * SSD_STATE].astype(BF16)
            bg_b = bg.astype(BF16)
            cb2 = _dot_nt(cg, jnp.concatenate([bg_b, bg_b], axis=0))
            gs = slice(g * GROUP_W, (g + 1) * GROUP_W)
            st_g = st_ref[:, gs]
            y_off = _dot(cg, st_g.astype(BF16)) * grow[:, gs]
            bg_t = jnp.concatenate([bg, jnp.zeros_like(bg)], axis=0).T.astype(BF16)
            st_ref[:, gs] = _rows8(mul, st_g, chunk_decay[:, gs]) + _dot(bg_t, xw_pad[:, gs])
            pair_out = []
            for kk in range(GROUP_W // LANES):
                i = g * (GROUP_W // LANES) + kk
                ps = slice(i * LANES, (i + 1) * LANES)
                a_row = jnp.broadcast_to(a_cum_t[i:i + 1, :], (8, LANES))
                decay = jnp.exp2(jnp.minimum(_rows8(sub, a_exp[:, ps], a_row), 0.0))
                m2 = jnp.where(causal2, cb2 * decay, 0.0).astype(BF16)
                xp = xdt_b[:, ps]
                zero = jnp.zeros_like(xp)
                xbd = jnp.concatenate([jnp.where(lo_half, xp, zero), jnp.where(lo_half, zero, xp)], axis=0)
                pair_out.append(_dot(m2, xbd))
            y_diag = jnp.concatenate(pair_out, axis=1)
            y = y_diag + y_off + _rows8(mul, xs[:, gs], dskip_ref[:, gs])
            zg = pad_rows(z_ref[0, r0:r0 + rows, gs].astype(F32))
            y = y * (zg * _sigmoid(zg))
            ms = jnp.mean(y * y, axis=-1, keepdims=True)
            y_parts.append(_rows8(mul, y * lax.rsqrt(ms + EPS), gn_ref[:, gs]))
        y_all = jnp.concatenate(y_parts, axis=1)
        y_ref[0, r0:r0 + rows, :] = y_all[:rows].astype(y_ref.dtype)

    @pl.when(c == n_chunks - 1)
    def _():
        for j in range(n_blk):
            hfin_ref[0, j * LANES:(j + 1) * LANES, :] = st_ref[:, j * LANES:(j + 1) * LANES].T


def _ssd_constants():
    Q = SSD_CHUNK
    e = np.zeros((LANES, SSD_WIDTH), np.float32)
    for h in range(SSD_HEADS):
        e[h, h * SSD_HEAD_DIM:(h + 1) * SSD_HEAD_DIM] = 1.0
    e3 = np.concatenate([e, e, e], axis=0)
    t = np.arange(Q)
    ltri = (t[:, None] >= t[None, :]).astype(np.float32)
    ltri3 = np.concatenate([ltri, ltri, ltri], axis=1)
    ublk = np.zeros((LANES, LANES), np.float32)
    ublk[:Q, :Q] = ltri.T
    ublk[Q:, Q:] = ltri.T
    ublk3 = np.concatenate([ublk, ublk, ublk], axis=0)
    shift = np.zeros((SSD_CONV * Q, 2 * Q), np.float32)
    for i in range(SSD_CONV):
        for r in range(Q):
            src = r - (SSD_CONV - 1 - i)
            if src >= 0:
                shift[i * Q + r, src] = 1.0
            else:
                shift[i * Q + r, Q + 16 + src] = 1.0
                shift[i * Q + r, Q + 32 + src] = 1.0
    return jnp.asarray(e3, BF16), jnp.asarray(ltri3, BF16), jnp.asarray(ublk3, BF16), jnp.asarray(shift, BF16)


def ssd_mixer(big, xbc_blk, z_blk, dt_raw, conv_prev, h0, p):
    b, s, _ = big.shape
    Q = SSD_CHUNK
    rows = min(s, Q)
    n_sub = SSD_STEP_CHUNKS if s % (SSD_STEP_CHUNKS * Q) == 0 else 1
    blk = rows * n_sub
    n_chunks = -(-s // blk)
    dtp = dt_raw[:, :, :SSD_HEADS]
    if s < Q:
        dtp = jnp.pad(dtp, ((0, 0), (0, Q - s), (0, 0)))
    dt_t = dtp.reshape(b, n_chunks * n_sub, Q, SSD_HEADS).transpose(0, 1, 3, 2).reshape(
        b, n_chunks * n_sub, SSD_HEADS // 2, 2 * Q)
    e3, ltri3, ublk3, shift = _ssd_constants()
    prev16 = jnp.pad(conv_prev, ((0, 0), (16 - (SSD_CONV - 1), 0), (0, 0)))
    rep8 = lambda v: jnp.repeat(v, 8, axis=0)
    const = lambda shape: pl.BlockSpec(shape, lambda bi, ci: (0,) * len(shape))
    in_specs = [
        pl.BlockSpec((1, blk, SSD_CONV_CH), lambda bi, ci: (bi, ci, xbc_blk)),
        pl.BlockSpec((1, 16, SSD_CONV_CH), lambda bi, ci: (bi, jnp.maximum(ci * (blk // 16) - 1, 0), xbc_blk)),
        pl.BlockSpec((1, blk, SSD_WIDTH), lambda bi, ci: (bi, ci, z_blk)),
        pl.BlockSpec((1, blk, LANES), lambda bi, ci: (bi, ci, 0)),
        pl.BlockSpec((1, n_sub, SSD_HEADS // 2, 2 * Q), lambda bi, ci: (bi, ci, 0, 0)),
        pl.BlockSpec((1, 16, SSD_CONV_CH), lambda bi, ci: (bi, 0, 0)),
        pl.BlockSpec((1, SSD_WIDTH, SSD_STATE), lambda bi, ci: (bi, 0, 0)),
        const((8 * SSD_CONV, SSD_CONV_CH)), const((8, SSD_CONV_CH)),
        const((1, LANES)), const((SSD_HEADS // 2, 2 * Q)),
        const((1, LANES)), const((SSD_HEADS // 2, 2 * Q)),
        const((8, SSD_WIDTH)), const((8, SSD_WIDTH)),
        const(e3.shape), const(ltri3.shape), const(ublk3.shape), const(shift.shape),
    ]
    y, h_fin = pl.pallas_call(
        functools.partial(_ssd_kernel, rows=rows, n_sub=n_sub, n_chunks=n_chunks),
        grid=(b, n_chunks),
        in_specs=in_specs,
        out_specs=[pl.BlockSpec((1, blk, SSD_WIDTH), lambda bi, ci: (bi, ci, 0)),
                   pl.BlockSpec((1, SSD_WIDTH, SSD_STATE), lambda bi, ci: (bi, 0, 0))],
        out_shape=[jax.ShapeDtypeStruct((b, s, SSD_WIDTH), BF16),
                   jax.ShapeDtypeStruct((b, SSD_WIDTH, SSD_STATE), F32)],
        scratch_shapes=[pltpu.VMEM((SSD_STATE, SSD_WIDTH), F32)],
        compiler_params=_cparams(("parallel", "arbitrary")),
        name="ssd_mixer",
    )(big, big, big, dt_raw, dt_t, prev16, h0, rep8(p['conv_w_ssd']), rep8(p['conv_b_ssd']), p['dt_bias'], p['dt_bias_t'],
      p['a_log'], p['a_log_t'], rep8(p['d_skip']), rep8(p['g_ssd_norm']), e3, ltri3, ublk3, shift)
    return y, h_fin


def _merge_kernel(att_ref, ssd_ref, ga_ref, gs_ref, h_ref, wa_ref, ws_ref, wo_ref, o_ref, wa_b, ws_b, wo_b):
    @pl.when(pl.program_id(0) == 0)
    def _():
        wa_b[...] = wa_ref[...].astype(BF16)
        ws_b[...] = ws_ref[...].astype(BF16)
        wo_b[...] = wo_ref[...].astype(BF16)

    a = _dot(att_ref[...], wa_b[...])
    s = _dot(ssd_ref[...], ws_b[...])
    merged = _sigmoid(ga_ref[...].astype(F32)) * a + _sigmoid(gs_ref[...].astype(F32)) * s
    o_ref[...] = h_ref[...] + _dot(merged.astype(BF16), wo_b[...])


def _resident(a):
    return pl.BlockSpec(a.shape, lambda i: (0,) * a.ndim, pipeline_mode=pl.Buffered(1))


def merge(att, ssd, big, h, p):
    m = h.shape[0]
    tm = _pick(m, (512, 256, 128))
    row = lambda w: pl.BlockSpec((tm, w), lambda i: (i, 0))
    gate_blk = BIG_GATE // D_MODEL
    ws = [p['w_br_att'], p['w_br_ssd'], p['w_out']]
    return pl.pallas_call(
        _merge_kernel,
        grid=(m // tm,),
        in_specs=[row(SB_WIDTH), row(SSD_WIDTH),
                  pl.BlockSpec((tm, D_MODEL), lambda i: (i, gate_blk)),
                  pl.BlockSpec((tm, D_MODEL), lambda i: (i, gate_blk + 1)),
                  row(D_MODEL)] + [_resident(w) for w in ws],
        out_specs=row(D_MODEL),
        out_shape=jax.ShapeDtypeStruct((m, D_MODEL), F32),
        scratch_shapes=[pltpu.VMEM(w.shape, BF16) for w in ws],
        compiler_params=_cparams(("arbitrary",)),
        name="merge_out_proj",
    )(att, ssd, big, big, h, *ws)


def _ffn_act_kernel(ug_ref, uv_ref, pg_ref, pv_ref, wg_ref, wv_ref, bg_ref, bv_ref, o_ref, gbuf_ref, vbuf_ref,
                    *, ts):
    t = pl.program_id(1)
    k1 = FFN_CONV - 1

    def conv(u_ref, p_ref, w_ref, b_ref, buf_ref):
        @pl.when(t == 0)
        def _():
            buf_ref[8 - k1:8, :] = p_ref[0]

        buf_ref[8:8 + ts, :] = u_ref[0].astype(F32)
        y = b_ref[...]
        for i in range(FFN_CONV):
            y = y + buf_ref[8 - k1 + i:8 - k1 + i + ts, :] * w_ref[i:i + 1, :]
        buf_ref[8 - k1:8, :] = buf_ref[8 + ts - k1:8 + ts, :]
        return y

    gate = conv(ug_ref, pg_ref, wg_ref, bg_ref, gbuf_ref)
    val = conv(uv_ref, pv_ref, wv_ref, bv_ref, vbuf_ref)
    o_ref[0] = (gate * _sigmoid(gate) * val).astype(o_ref.dtype)


FFN_SUB = 64


UPF_TM = 512
UPF_TN = D_FF // 2


def _up_ffn_kernel(h_ref, h16_ref, g_ref, wg_ref, wv_ref, pg_ref, pv_ref, cwg_ref, cwv_ref, cbg_ref, cbv_ref,
                   o_ref, gl_ref, vl_ref, wg_b, wv_b, gbuf_ref, vbuf_ref):
    i = pl.program_id(2)
    tm = h_ref.shape[1]

    @pl.when(jnp.logical_and(pl.program_id(1) == 0, i == 0))
    def _():
        wg_b[...] = wg_ref[...].astype(BF16)
        wv_b[...] = wv_ref[...].astype(BF16)

    u = _rms_norm_bf16(h_ref[0], g_ref[...])
    u16 = _rms_norm_bf16(h16_ref[0], g_ref[...])

    def conv(w_b, p_ref, last_ref, buf_ref, cw_ref, cb_ref):
        up = _dot(u, w_b[...])
        buf_ref[0:16, :] = jnp.where(i == 0, p_ref[0], _dot(u16, w_b[...]))
        buf_ref[16:16 + tm, :] = up
        last_ref[0] = up[tm - 16:tm, :]
        y = _rows8(jnp.multiply, up, cw_ref[8 * (FFN_CONV - 1):8 * FFN_CONV, :])
        for t in range(FFN_CONV - 1):
            r0 = 16 - (FFN_CONV - 1) + t
            y = y + _rows8(jnp.multiply, buf_ref[r0:r0 + tm, :], cw_ref[8 * t:8 * t + 8, :])
        return _rows8(jnp.add, y, cb_ref[...])

    gate = conv(wg_b, pg_ref, gl_ref, gbuf_ref, cwg_ref, cbg_ref)
    val = conv(wv_b, pv_ref, vl_ref, vbuf_ref, cwv_ref, cbv_ref)
    o_ref[0] = (gate * _sigmoid(gate) * val).astype(o_ref.dtype)


def up_ffn_long(h, prev, p):
    b, s, d = h.shape
    tm, tn = UPF_TM, UPF_TN
    nj = D_FF // tn
    prev16 = jnp.pad(prev, ((0, 0), (16 - (FFN_CONV - 1), 0), (0, 0)))
    rep8 = lambda v: jnp.repeat(v, 8, axis=0)
    w8, b8 = rep8(p['conv_w_ffn']), rep8(p['conv_b_ffn'])
    cols = lambda rows, half: pl.BlockSpec((rows, tn), lambda j, bi, i: (0, half * nj + j))
    w_spec = lambda half: pl.BlockSpec((d, tn), lambda j, bi, i: (0, half * nj + j), pipeline_mode=pl.Buffered(1))
    prevs = lambda half: pl.BlockSpec((1, 16, tn), lambda j, bi, i: (bi, 0, half * nj + j))
    last_spec = pl.BlockSpec((1, 16, tn), lambda j, bi, i: (bi, 0, j))
    return pl.pallas_call(
        _up_ffn_kernel,
        grid=(nj, b, s // tm),
        in_specs=[pl.BlockSpec((1, tm, d), lambda j, bi, i: (bi, i, 0)),
                  pl.BlockSpec((1, 16, d), lambda j, bi, i: (bi, jnp.maximum(i * (tm // 16) - 1, 0), 0)),
                  pl.BlockSpec((1, d), lambda j, bi, i: (0, 0)),
                  w_spec(0), w_spec(1), prevs(0), prevs(1),
                  cols(8 * FFN_CONV, 0), cols(8 * FFN_CONV, 1), cols(8, 0), cols(8, 1)],
        out_specs=[pl.BlockSpec((1, tm, tn), lambda j, bi, i: (bi, i, j)), last_spec, last_spec],
        out_shape=[jax.ShapeDtypeStruct((b, s, D_FF), BF16)] + [jax.ShapeDtypeStruct((b, 16, D_FF), F32)] * 2,
        scratch_shapes=[pltpu.VMEM((d, tn), BF16)] * 2 + [pltpu.VMEM((16 + tm, tn), F32)] * 2,
        compiler_params=_cparams(("arbitrary", "arbitrary", "arbitrary")),
        name="up_proj_conv_ffn",
    )(h, h, p['g_ffn'], p['w_up'], p['w_up'], prev16, prev16, w8, w8, b8, b8)


def ffn_act(up, prev, p):
    b, s, _ = up.shape
    ts = _pick(s, (512, 256, 128))
    half = lambda blk: pl.BlockSpec((1, ts, D_FF), lambda bi, ti: (bi, ti, blk))
    prevs = lambda blk: pl.BlockSpec((1, FFN_CONV - 1, D_FF), lambda bi, ti: (bi, 0, blk))
    wspec = lambda blk: pl.BlockSpec((FFN_CONV, D_FF), lambda bi, ti: (0, blk))
    bspec = lambda blk: pl.BlockSpec((1, D_FF), lambda bi, ti: (0, blk))
    return pl.pallas_call(
        functools.partial(_ffn_act_kernel, ts=ts),
        grid=(b, s // ts),
        in_specs=[half(0), half(1), prevs(0), prevs(1), wspec(0), wspec(1), bspec(0), bspec(1)],
        out_specs=pl.BlockSpec((1, ts, D_FF), lambda bi, ti: (bi, ti, 0)),
        out_shape=jax.ShapeDtypeStruct((b, s, D_FF), BF16),
        scratch_shapes=[pltpu.VMEM((8 + ts, D_FF), F32)] * 2,
        compiler_params=_cparams(("parallel", "arbitrary")),
        name="ffn_conv_act",
    )(up, up, prev, prev, p['conv_w_ffn'], p['conv_w_ffn'], p['conv_b_ffn'], p['conv_b_ffn'])


def _down_kernel(a_ref, w_ref, h_ref, g_ref, o_ref, w_b):
    @pl.when(pl.program_id(0) == 0)
    def _():
        w_b[...] = w_ref[...].astype(BF16)

    h = h_ref[...] + _dot(a_ref[...], w_b[...])
    ms = jnp.mean(h * h, axis=-1, keepdims=True)
    o_ref[...] = h * lax.rsqrt(ms + EPS) * g_ref[...]


def down_norm(act, h, p):
    m = h.shape[0]
    tm = _pick(m, (512, 256, 128))
    return pl.pallas_call(
        _down_kernel,
        grid=(m // tm,),
        in_specs=[pl.BlockSpec((tm, D_FF), lambda i: (i, 0)),
                  _resident(p['w_down']),
                  pl.BlockSpec((tm, D_MODEL), lambda i: (i, 0)),
                  pl.BlockSpec((1, D_MODEL), lambda i: (0, 0))],
        out_specs=pl.BlockSpec((tm, D_MODEL), lambda i: (i, 0)),
        out_shape=jax.ShapeDtypeStruct((m, D_MODEL), F32),
        scratch_shapes=[pltpu.VMEM(p['w_down'].shape, BF16)],
        compiler_params=_cparams(("arbitrary",)),
        name="down_proj_norm",
    )(act, p['w_down'], h, p['g_final'])


def mix_and_ffn(big3, att, dt_raw, hf, ssd_h0, ssd_conv_prev, ffn_conv_prev, p):
    b, s, _ = big3.shape
    m = b * s
    ssd, ssd_state = ssd_mixer(big3, BIG_XBC // SSD_CONV_CH, BIG_Z // SSD_WIDTH, dt_raw, ssd_conv_prev, ssd_h0, p)
    h1 = merge(att.reshape(m, SB_WIDTH), ssd.reshape(m, SSD_WIDTH), big3.reshape(m, BIG_W), hf, p)
    if s % UPF_TM == 0:
        act, up_g, up_v = up_ffn_long(h1.reshape(b, s, D_MODEL), ffn_conv_prev, p)
        up_last = jnp.concatenate([up_g, up_v], axis=2)
    else:
        up_last = norm_mm(h1, p['g_ffn'], p['w_up'], BF16).reshape(b, s, 2 * D_FF)
        act = ffn_act(up_last, ffn_conv_prev, p)
    y = down_norm(act.reshape(m, D_FF), h1, p).reshape(b, s, D_MODEL)
    xbc_rows = jnp.concatenate([ssd_conv_prev, big3[:, :, BIG_XBC:BIG_XBC + SSD_CONV_CH][:, -(SSD_CONV - 1):].astype(F32)],
                               axis=1)[:, -(SSD_CONV - 1):]
    up_rows = jnp.concatenate([ffn_conv_prev, up_last[:, -(FFN_CONV - 1):].astype(F32)], axis=1)[:, -(FFN_CONV - 1):]
    return y, ssd_state, xbc_rows, up_rows


def _prep_params(g_mix, w_in, conv_w_ssd, conv_b_ssd, dt_bias, a_log, d_skip, g_ssd_norm, w_br_att, w_br_ssd,
                 w_out, g_ffn, w_up, conv_w_ffn, conv_b_ffn, w_down, g_final):
    w_t = w_in.T
    dt0 = 3 * SB_WIDTH + SSD_WIDTH + SSD_CONV_CH
    w_tail_t = jnp.concatenate([w_t[dt0 + SSD_HEADS:], w_t[dt0:dt0 + SSD_HEADS],
                                jnp.zeros((LANES - SSD_HEADS, D_MODEL), F32)], axis=0)
    Q = SSD_CHUNK
    lanes_t = lambda v: jnp.repeat(v.reshape(SSD_HEADS // 2, 2, 1), Q, axis=2).reshape(SSD_HEADS // 2, 2 * Q)
    pad_l = lambda v: jnp.pad(v, (0, LANES - SSD_HEADS)).reshape(1, LANES)
    return {
        'g_mix': g_mix.reshape(1, -1), 'g_ffn': g_ffn.reshape(1, -1), 'g_final': g_final.reshape(1, -1),
        'w_t': w_t, 'w_tail_t': w_tail_t,
        'conv_w_ssd': conv_w_ssd, 'conv_b_ssd': conv_b_ssd.reshape(1, -1),
        'dt_bias': pad_l(dt_bias), 'dt_bias_t': lanes_t(dt_bias),
        'a_log': pad_l(a_log), 'a_log_t': lanes_t(a_log),
        'd_skip': jnp.repeat(d_skip, SSD_HEAD_DIM).reshape(1, -1), 'g_ssd_norm': g_ssd_norm.reshape(1, -1),
        'w_br_att': w_br_att, 'w_br_ssd': w_br_ssd, 'w_out': w_out,
        'w_up': w_up, 'conv_w_ffn': conv_w_ffn, 'conv_b_ffn': conv_b_ffn.reshape(1, -1),
        'w_down': w_down,
    }


def kernel(x_prompt, x_sample, cache_k, cache_v, state_ssm, state_ssm_conv, state_ffn_conv, meta_tokens, g_mix, w_in, conv_w_ssd, conv_b_ssd, dt_bias, a_log, d_skip, g_ssd_norm, w_br_att, w_br_ssd, w_out, g_ffn, w_up, conv_w_ffn, conv_b_ffn, w_down, g_final):
    bp, seq, _ = x_prompt.shape
    bs, dec = x_sample.shape[:2]
    assert dec == N_META and seq % KVT_TS == 0
    p = _prep_params(g_mix[0], w_in[0], conv_w_ssd[0], conv_b_ssd[0], dt_bias[0], a_log[0], d_skip[0],
                     g_ssd_norm[0], w_br_att[0], w_br_ssd[0], w_out[0], g_ffn[0], w_up[0], conv_w_ffn[0],
                     conv_b_ffn[0], w_down[0], g_final)
    proj = functools.partial(in_proj, g=p['g_mix'], w_t=p['w_t'], w_tail_t=p['w_tail_t'])
    state_shape = (SSD_WIDTH, SSD_STATE)
    rows = lambda a, b: a.reshape(b, -1, a.shape[-1])

    k_m, v_m, big_m, dt_m = proj(meta_tokens, with_kv=True)
    big_m, k_m, v_m = big_m[None], k_m[None], v_m[None]
    att_m = attention_step(big_m, 0, k_m, v_m, None, None, 0, SB_WIDTH)
    _, ssm_m, conv_m, ffn_m = mix_and_ffn(
        big_m, att_m, dt_m[None], meta_tokens, jnp.zeros((1,) + state_shape, F32),
        jnp.zeros((1, SSD_CONV - 1, SSD_CONV_CH), F32), jnp.zeros((1, FFN_CONV - 1, 2 * D_FF), F32), p)

    xf = x_prompt.reshape(bp * seq, D_MODEL)
    big_x, dt_x = proj(xf, with_kv=False)
    big_x = rows(big_x, bp)
    k_t, v_t, k_end, v_end = kv_transposed(x_prompt, meta_tokens, p['g_mix'], p['w_t'])
    att_x = attention_prompt(big_x, big_m, k_t, v_t)
    att_end = attention_step(big_x, seq // N_META - 1, k_end, v_end, k_t, v_t, seq // KEY_TILE, ATTN_PROMPT_WIDTH)
    att_x = lax.dynamic_update_slice(att_x, att_end, (0, seq - N_META, 0))
    rep = lambda a: jnp.broadcast_to(a, (bp,) + a.shape[1:])
    y_prompt, ssm_p, conv_p, ffn_p = mix_and_ffn(
        big_x, att_x, rows(dt_x, bp), xf, rep(ssm_m), rep(conv_m), rep(ffn_m), p)

    n_rows = cache_k.shape[2]
    n_tiles = (n_rows - N_META) // KEY_TILE
    assert n_tiles * KEY_TILE + N_META == n_rows
    k_s, v_s, big_s, dt_s = proj(x_sample.reshape(bs * dec, D_MODEL), with_kv=True)
    big_s, k_s, v_s = rows(big_s, bs), rows(k_s, bs), rows(v_s, bs)
    cache_t = lambda c: c[0].transpose(0, 2, 3, 1).reshape(bs, SB_WIDTH, n_rows)
    nearest = lambda c, new: jnp.concatenate([c[0][:, n_tiles * KEY_TILE:].reshape(bs, N_META, SB_WIDTH), new], axis=1)
    att_s = attention_step(big_s, 0, nearest(cache_k, k_s), nearest(cache_v, v_s), cache_t(cache_k), cache_t(cache_v),
                           n_tiles, SB_WIDTH)
    y_sample, ssm_s, conv_s, ffn_s = mix_and_ffn(
        big_s, att_s, rows(dt_s, bs), x_sample.reshape(bs * dec, D_MODEL),
        state_ssm[0].reshape((bs,) + state_shape), state_ssm_conv[0], state_ffn_conv[0], p)

    heads = lambda a: a.reshape(a.shape[0], a.shape[1], SB_HEADS, SB_HEAD_DIM)[None]
    heads_t = lambda a: a.reshape(a.shape[0], SB_HEADS, SB_HEAD_DIM, a.shape[2]).transpose(0, 3, 1, 2)[None]
    state5 = lambda a: a.reshape(a.shape[0], SSD_HEADS, SSD_HEAD_DIM, SSD_STATE)[None]
    return (y_prompt, y_sample, heads_t(k_t), heads_t(v_t), state5(ssm_p), conv_p[None], ffn_p[None],
            heads(k_s), heads(v_s), state5(ssm_s), conv_s[None], ffn_s[None])
```

```python
import functools

import numpy as np
import jax
import jax.numpy as jnp
from jax import lax
from jax.experimental import pallas as pl
from jax.experimental.pallas import tpu as pltpu

F32 = jnp.float32
BF16 = jnp.bfloat16

D_MODEL = 1024
N_META = 16
SB_HEADS = 16
SB_HEAD_DIM = 64
SB_WIDTH = SB_HEADS * SB_HEAD_DIM
SSD_WIDTH = 2 * D_MODEL
SSD_HEAD_DIM = 64
SSD_HEADS = SSD_WIDTH // SSD_HEAD_DIM
SSD_GROUPS = 4
SSD_STATE = 128
SSD_CONV = 4
SSD_CONV_CH = SSD_WIDTH + 2 * SSD_GROUPS * SSD_STATE
D_FF = 2816
FFN_CONV = 3
EPS = 1e-6

LANES = 128
SSD_CHUNK = 64
SSD_STEP_CHUNKS = 4
KEY_TILE = 128
GROUP_W = SSD_WIDTH // SSD_GROUPS
DEAD_LOG = -104.0
LOG2E = 1.4426950408889634
ATTN_STATIC_TILES = 2
ATTN_GROUP_TILES = 3
VMEM_LIMIT = 56 * 1024 * 1024


def _cparams(sem):
    return pltpu.CompilerParams(dimension_semantics=sem, vmem_limit_bytes=VMEM_LIMIT)


def _pick(n, cands):
    for c in cands:
        if n % c == 0:
            return c
    return n


def _split2(x):
    hi = x.astype(BF16)
    lo = (x - hi.astype(F32)).astype(BF16)
    return hi, lo


def _split3(x):
    hi = x.astype(BF16)
    r = x - hi.astype(F32)
    mid = r.astype(BF16)
    lo = (r - mid.astype(F32)).astype(BF16)
    return hi, mid, lo


def _dot(a, b):
    return jnp.dot(a, b, preferred_element_type=F32)


def _dot_nt(a, b):
    return lax.dot_general(a, b, (((1,), (1,)), ((), ())), preferred_element_type=F32)


def _softplus(x):
    return jnp.maximum(x, 0.0) + jnp.log(1.0 + jnp.exp(-jnp.abs(x)))


def _sigmoid(x):
    return 1.0 / (1.0 + jnp.exp(-x))


def _rms_norm_bf16(x, g):
    ms = jnp.mean(x * x, axis=-1, keepdims=True)
    return (x * lax.rsqrt(ms + EPS) * g).astype(BF16)


def _norm_mm_kernel(x_ref, g_ref, w_ref, o_ref, u_ref):
    @pl.when(pl.program_id(1) == 0)
    def _():
        u_ref[...] = _rms_norm_bf16(x_ref[...], g_ref[...])

    o_ref[...] = _dot(u_ref[...], w_ref[...].astype(BF16)).astype(o_ref.dtype)


PROJ_TN = 1024
UP_TN = 1408


def norm_mm(x, g, w, out_dtype):
    m, d = x.shape
    n = w.shape[1]
    tm = _pick(m, (2048, 1024, 512, 256, 128))
    tn = UP_TN
    return pl.pallas_call(
        _norm_mm_kernel,
        grid=(m // tm, n // tn),
        in_specs=[pl.BlockSpec((tm, d), lambda i, j: (i, 0)),
                  pl.BlockSpec((1, d), lambda i, j: (0, 0)),
                  pl.BlockSpec((d, tn), lambda i, j: (0, j))],
        out_specs=pl.BlockSpec((tm, tn), lambda i, j: (i, j)),
        out_shape=jax.ShapeDtypeStruct((m, n), out_dtype),
        scratch_shapes=[pltpu.VMEM((tm, d), BF16)],
        compiler_params=_cparams(("parallel", "arbitrary")),
        name="norm_up_proj",
    )(x, g, w)


_NQ = SB_WIDTH // PROJ_TN
_J_K, _J_V, _J_Z = _NQ, 2 * _NQ, 3 * _NQ
_J_X = _J_Z + SSD_WIDTH // PROJ_TN
_J_G = _J_X + SSD_CONV_CH // PROJ_TN
_J_END = _J_G + 2 * D_MODEL // PROJ_TN
BIG_Z, BIG_Q, BIG_XBC, BIG_GATE = 0, SSD_WIDTH, SSD_WIDTH + SB_WIDTH, SSD_WIDTH + SB_WIDTH + SSD_CONV_CH
BIG_W = BIG_GATE + 2 * D_MODEL


def _big_block(j):
    return jnp.where(j < _J_K, BIG_Q // PROJ_TN + j,
                     jnp.where(j < _J_Z, BIG_Q // PROJ_TN + _NQ - 1,
                               jnp.where(j < _J_X, j - _J_Z + BIG_Z // PROJ_TN,
                                         jnp.where(j < _J_G, j - _J_X + BIG_XBC // PROJ_TN,
                                                   j - _J_G + BIG_GATE // PROJ_TN))))


def _sweep_step(j, with_kv):
    return j if with_kv else jnp.where(j >= _J_K, j + (_J_Z - _J_K), j)


def _in_proj_kernel(*refs, with_kv):
    if with_kv:
        x_ref, g_ref, win_ref, wgate_ref, wdt_ref, k_ref, v_ref, big_ref, dt_ref, u_ref = refs
    else:
        x_ref, g_ref, win_ref, wgate_ref, wdt_ref, big_ref, dt_ref, u_ref = refs
    j = _sweep_step(pl.program_id(1), with_kv)

    def proj(w_ref):
        return _dot(u_ref[...], w_ref[...].T.astype(BF16))

    @pl.when(pl.program_id(1) == 0)
    def _():
        u_ref[...] = _rms_norm_bf16(x_ref[...], g_ref[...])
        dt_ref[...] = proj(wdt_ref)

    if with_kv:
        @pl.when(jnp.logical_and(j >= _J_K, j < _J_V))
        def _():
            k_ref[...] = proj(win_ref)

        @pl.when(jnp.logical_and(j >= _J_V, j < _J_Z))
        def _():
            v_ref[...] = proj(win_ref)

    @pl.when(jnp.logical_or(j < _J_K, jnp.logical_and(j >= _J_Z, j < _J_G)))
    def _():
        big_ref[...] = proj(win_ref).astype(BF16)

    @pl.when(j >= _J_G)
    def _():
        big_ref[...] = proj(wgate_ref).astype(BF16)


def in_proj(x, g, w_t, w_tail_t, with_kv):
    m, d = x.shape
    tm = _pick(m, (2048, 1024, 512, 256, 128))
    tn = PROJ_TN
    step = functools.partial(_sweep_step, with_kv=with_kv)
    clip = lambda j, lo, n: jnp.clip(step(j) - lo, 0, n - 1)
    kv_specs = [pl.BlockSpec((tm, tn), lambda i, j: (i, clip(j, _J_K, _NQ))),
                pl.BlockSpec((tm, tn), lambda i, j: (i, clip(j, _J_V, _NQ)))]
    kv_shapes = [jax.ShapeDtypeStruct((m, SB_WIDTH), F32)] * 2
    return pl.pallas_call(
        functools.partial(_in_proj_kernel, with_kv=with_kv),
        grid=(m // tm, _J_END if with_kv else _J_END - (_J_Z - _J_K)),
        in_specs=[pl.BlockSpec((tm, d), lambda i, j: (i, 0), pipeline_mode=pl.Buffered(1)),
                  pl.BlockSpec((1, d), lambda i, j: (0, 0)),
                  pl.BlockSpec((tn, d), lambda i, j: (jnp.minimum(step(j), _J_G - 1), 0)),
                  pl.BlockSpec((tn, d), lambda i, j: (clip(j, _J_G, _J_END - _J_G), 0)),
                  pl.BlockSpec((LANES, d), lambda i, j: (2 * D_MODEL // LANES, 0))],
        out_specs=(kv_specs if with_kv else []) + [
            pl.BlockSpec((tm, tn), lambda i, j: (i, _big_block(step(j)))),
            pl.BlockSpec((tm, LANES), lambda i, j: (i, 0))],
        out_shape=(kv_shapes if with_kv else []) + [
            jax.ShapeDtypeStruct((m, BIG_W), BF16), jax.ShapeDtypeStruct((m, LANES), F32)],
        scratch_shapes=[pltpu.VMEM((tm, d), BF16)],
        compiler_params=_cparams(("parallel", "arbitrary")),
        name="in_proj",
    )(x, g, w_t, w_tail_t, w_tail_t)


KVT_TS = 1024


def _kv_t_kernel(x16_ref, xt_ref, meta_ref, g_ref, wk_ref, wv_ref, kt_ref, vt_ref, kl_ref, vl_ref, wk_b, wv_b):
    j = pl.program_id(1)

    @pl.when(jnp.logical_and(pl.program_id(0) == 0, j == 0))
    def _():
        wk_b[...] = wk_ref[...].astype(BF16)
        wv_b[...] = wv_ref[...].astype(BF16)

    first = jnp.where(j == 0, meta_ref[...], x16_ref[0])
    x = jnp.concatenate([first, xt_ref[0, :KVT_TS - N_META, :]], axis=0)
    u = _rms_norm_bf16(x, g_ref[...])
    kt_ref[0] = _dot_nt(wk_b[...], u)
    vt_ref[0] = _dot_nt(wv_b[...], u)

    @pl.when(j == pl.num_programs(1) - 1)
    def _():
        kl_ref[0] = _dot_nt(u[0:N_META, :], wk_b[...])
        vl_ref[0] = _dot_nt(u[0:N_META, :], wv_b[...])


def kv_transposed(x, meta, g, w_t):
    b, s, d = x.shape
    n_pos = N_META + s
    n_tiles = -(-n_pos // KVT_TS)
    assert n_pos - (n_tiles - 1) * KVT_TS == N_META
    per_tile = KVT_TS // N_META
    out_spec = pl.BlockSpec((1, SB_WIDTH, KVT_TS), lambda bi, j: (bi, 0, j))
    last_spec = pl.BlockSpec((1, N_META, SB_WIDTH), lambda bi, j: (bi, 0, 0))
    w_spec = lambda blk: pl.BlockSpec((SB_WIDTH, d), lambda bi, j: (blk, 0), pipeline_mode=pl.Buffered(1))
    return pl.pallas_call(
        _kv_t_kernel,
        grid=(b, n_tiles),
        in_specs=[pl.BlockSpec((1, N_META, d), lambda bi, j: (bi, jnp.maximum(j * per_tile - 1, 0), 0)),
                  pl.BlockSpec((1, KVT_TS, d), lambda bi, j: (bi, jnp.minimum(j, s // KVT_TS - 1), 0)),
                  pl.BlockSpec((N_META, d), lambda bi, j: (0, 0)),
                  pl.BlockSpec((1, d), lambda bi, j: (0, 0)),
                  w_spec(1), w_spec(2)],
        out_specs=[out_spec, out_spec, last_spec, last_spec],
        out_shape=[jax.ShapeDtypeStruct((b, SB_WIDTH, n_pos), F32)] * 2
        + [jax.ShapeDtypeStruct((b, N_META, SB_WIDTH), F32)] * 2,
        scratch_shapes=[pltpu.VMEM((SB_WIDTH, d), BF16)] * 2,
        compiler_params=_cparams(("arbitrary", "arbitrary")),
        name="kv_transposed",
    )(x, x, meta, g, w_t, w_t)


def _stick_breaking_tiles(qs, u2, tiles, carries, accs):
    n_heads = len(qs)
    units = [(t, h) for t in range(len(tiles)) for h in range(n_heads)]
    ps = lambda h: slice((h // 2) * LANES, (h // 2 + 1) * LANES)

    def scores(t, h):
        kind, keys = tiles[t][0], tiles[t][1]
        return _dot_nt(qs[h], keys[:, ps(h)]) if kind == 'rows' else _dot(qs[h], keys[ps(h), :])

    z = {u: scores(*u) for u in units}
    log_beta, cat = {}, {}
    for u in units:
        mask = tiles[u[0]][3]
        soft = jnp.log(1.0 + jnp.exp(-jnp.abs(z[u])))
        log_beta[u] = jnp.minimum(z[u], 0.0) - soft
        log_keep = log_beta[u] - z[u]
        if mask is not None:
            log_keep = jnp.where(mask, log_keep, 0.0)
        cat[u] = jnp.concatenate(_split2(log_keep), axis=1)
    r = {u: _dot(cat[u], u2) for u in units}
    carries, accs = list(carries), list(accs)
    w = {}
    for t, h in units:
        mask = tiles[t][3]
        wt = jnp.exp(log_beta[(t, h)] + r[(t, h)][:, :LANES] + carries[h])
        if mask is not None:
            wt = jnp.where(mask, wt, 0.0)
        w[(t, h)] = wt.astype(BF16)
        carries[h] = carries[h] + r[(t, h)][:, LANES:]
    for t, h in units:
        kind, vals = tiles[t][0], tiles[t][2]
        pv = _dot(w[(t, h)], vals[:, ps(h)]) if kind == 'rows' else _dot_nt(w[(t, h)], vals[ps(h), :])
        accs[h] = accs[h] + pv
    return carries, accs


def _head_queries(q, lo_half):
    qs = []
    for h in range(q.shape[1] // SB_HEAD_DIM):
        qp = q[:, (h // 2) * LANES:(h // 2 + 1) * LANES]
        keep = lo_half if h % 2 == 0 else jnp.logical_not(lo_half)
        qs.append(jnp.where(keep, qp, jnp.zeros_like(qp)) * jnp.asarray(SB_HEAD_DIM ** -0.5, BF16))
    return qs


def _any_alive(carries):
    m = carries[0]
    for c in carries[1:]:
        m = jnp.maximum(m, c)
    return (jnp.max(m) > DEAD_LOG).astype(jnp.int32)


def _attend(qs, u2, first_tiles, load_tile, j0, lane):
    tq = qs[0].shape[0]

    def cache_tile(j):
        if isinstance(j, int):
            mask = None if j >= 0 else lane < 0
            kt, vt = load_tile(max(j, 0))
        else:
            mask = jnp.logical_and(j >= 0, lane >= 0)
            kt, vt = load_tile(jnp.maximum(j, 0))
        return 'cols', kt.astype(BF16), vt.astype(BF16), mask

    tiles = list(first_tiles)
    if load_tile is not None:
        for _ in range(ATTN_STATIC_TILES):
            tiles.append(cache_tile(j0))
            j0 = j0 - 1
    zeros = jnp.zeros((tq, LANES), F32)
    carries, accs = [zeros] * len(qs), [zeros] * len(qs)
    for g in range(0, len(tiles), ATTN_GROUP_TILES):
        carries, accs = _stick_breaking_tiles(qs, u2, tiles[g:g + ATTN_GROUP_TILES], carries, accs)
    if load_tile is not None:
        def cond(s):
            return jnp.logical_and(s[0] >= 0, s[1] > 0)

        def body(s):
            j, _, carries, accs = s
            carries, accs = _stick_breaking_tiles(qs, u2, [cache_tile(j)], carries, accs)
            return j - 1, _any_alive(carries), tuple(carries), tuple(accs)

        state = (jnp.asarray(j0, jnp.int32), _any_alive(carries), tuple(carries), tuple(accs))
        _, _, carries, accs = lax.while_loop(cond, body, state)
    return accs


def _pair_outputs(accs, lo_half, dtype):
    return [jnp.where(lo_half, accs[2 * p], accs[2 * p + 1]).astype(dtype) for p in range(len(accs) // 2)]


def _attn_prompt_kernel(q16_ref, qt_ref, qm_ref, u2_ref, kt_ref, vt_ref, o_ref):
    step = pl.program_id(2)
    tq = KEY_TILE
    lane = lax.broadcasted_iota(jnp.int32, (tq, LANES), 1)
    row = lax.broadcasted_iota(jnp.int32, (tq, LANES), 0)
    lo_half = lane < SB_HEAD_DIM

    def load_tile(j):
        off = pl.multiple_of(j * KEY_TILE, KEY_TILE)
        return kt_ref[0, :, pl.ds(off, KEY_TILE)], vt_ref[0, :, pl.ds(off, KEY_TILE)]

    for sub in range(ATTN_STEP_TILES):
        m = step * ATTN_STEP_TILES + sub
        if sub == 0:
            first = jnp.where(step == 0, qm_ref[0], q16_ref[0])
            q = jnp.concatenate([first, qt_ref[0, :tq - N_META, :]], axis=0)
        else:
            q = qt_ref[0, sub * tq - N_META:(sub + 1) * tq - N_META, :]
        qs = _head_queries(q, lo_half)
        kd, vd = load_tile(m)
        diag = ('cols', kd.astype(BF16), vd.astype(BF16), lane < row)
        accs = _attend(qs, u2_ref[...], [diag], load_tile, m - 1, lane)
        out = jnp.concatenate(_pair_outputs(accs, lo_half, o_ref.dtype), axis=1)
        if sub == 0:
            @pl.when(step == 0)
            def _(out=out):
                o_ref[0, 0:tq - N_META, :] = out[N_META:, :]
                o_ref[0, o_ref.shape[1] - N_META:, :] = jnp.zeros((N_META, o_ref.shape[2]), o_ref.dtype)

            @pl.when(step > 0)
            def _(out=out, m=m):
                o_ref[0, pl.ds(pl.multiple_of(m * KEY_TILE - N_META, N_META), tq), :] = out
        else:
            o_ref[0, pl.ds(pl.multiple_of(m * KEY_TILE - N_META, N_META), tq), :] = out


def _attn_step_kernel(*refs, n_free, n_cache_tiles):
    if n_cache_tiles:
        (q_ref, kd_ref, vd_ref, u2_ref, kn_ref, vn_ref, kt_any, vt_any, o_ref,
         kd_scr, vd_scr, kbuf, vbuf) = refs
        width = q_ref.shape[2]
        n_near = kn_ref.shape[2] // KEY_TILE

        def load_tile(j):
            if isinstance(j, int) and j >= n_cache_tiles - n_near:
                ls = slice((j - (n_cache_tiles - n_near)) * KEY_TILE, (j - (n_cache_tiles - n_near) + 1) * KEY_TILE)
                return kn_ref[0, :, ls], vn_ref[0, :, ls]
            off = j * KEY_TILE if isinstance(j, int) else pl.multiple_of(j * KEY_TILE, KEY_TILE)
            src = (pl.program_id(0), pl.ds(pl.program_id(1) * width, width), pl.ds(off, KEY_TILE))
            pltpu.sync_copy(kt_any.at[src], kbuf)
            pltpu.sync_copy(vt_any.at[src], vbuf)
            return kbuf[...], vbuf[...]
    else:
        q_ref, kd_ref, vd_ref, u2_ref, o_ref, kd_scr, vd_scr = refs
        load_tile = None
    tq = q_ref.shape[1]
    n_rows = kd_ref.shape[1]
    lane = lax.broadcasted_iota(jnp.int32, (tq, LANES), 1)
    row = lax.broadcasted_iota(jnp.int32, (tq, LANES), 0)
    lo_half = lane < SB_HEAD_DIM
    kd_scr[...] = jnp.zeros_like(kd_scr)
    vd_scr[...] = jnp.zeros_like(vd_scr)
    kd_scr[0:n_rows, :] = kd_ref[0].astype(BF16)
    vd_scr[0:n_rows, :] = vd_ref[0].astype(BF16)
    diag = ('rows', kd_scr[...], vd_scr[...], lane < row + n_free)
    qs = _head_queries(q_ref[0], lo_half)
    accs = _attend(qs, u2_ref[...], [diag], load_tile, n_cache_tiles - 1, lane)
    for p, out in enumerate(_pair_outputs(accs, lo_half, o_ref.dtype)):
        o_ref[0, :, p * LANES:(p + 1) * LANES] = out


def _cumsum_rhs():
    s = np.arange(KEY_TILE)
    strict = (s[:, None] > s[None, :]).astype(np.float32)
    half = np.concatenate([strict, np.ones((KEY_TILE, KEY_TILE), np.float32)], axis=1)
    return jnp.asarray(np.concatenate([half, half], axis=0), BF16)


ATTN_PROMPT_WIDTH = 8 * SB_HEAD_DIM
ATTN_STEP_TILES = 2


def attention_prompt(big, big_meta, k_t, v_t):
    b, s, _ = big.shape
    width = ATTN_PROMPT_WIDTH
    qb0 = BIG_Q // width
    rows = ATTN_STEP_TILES * KEY_TILE
    per_step = rows // N_META
    n_pos = k_t.shape[2]
    cache = pl.BlockSpec((1, width, n_pos), lambda bi, hp, m: (bi, hp, 0))
    return pl.pallas_call(
        _attn_prompt_kernel,
        grid=(b, SB_WIDTH // width, s // rows),
        in_specs=[pl.BlockSpec((1, N_META, width), lambda bi, hp, m: (bi, jnp.maximum(m * per_step - 1, 0), qb0 + hp)),
                  pl.BlockSpec((1, rows, width), lambda bi, hp, m: (bi, m, qb0 + hp)),
                  pl.BlockSpec((1, N_META, width), lambda bi, hp, m: (0, 0, qb0 + hp)),
                  pl.BlockSpec((2 * KEY_TILE, 2 * KEY_TILE), lambda bi, hp, m: (0, 0)),
                  cache, cache],
        out_specs=pl.BlockSpec((1, s, width), lambda bi, hp, m: (bi, 0, hp)),
        out_shape=jax.ShapeDtypeStruct((b, s, SB_WIDTH), BF16),
        compiler_params=_cparams(("parallel", "parallel", "arbitrary")),
        name="stick_breaking_prompt",
    )(big, big, big_meta, _cumsum_rhs(), k_t, v_t)


def attention_step(big, q_row_blk, k_rows, v_rows, k_t, v_t, n_cache_tiles, width):
    b = big.shape[0]
    n_rows = k_rows.shape[1]
    qb0 = BIG_Q // width
    rows = pl.BlockSpec((1, n_rows, width), lambda bi, hp: (bi, 0, hp))
    in_specs = [pl.BlockSpec((1, N_META, width), lambda bi, hp: (bi, q_row_blk, qb0 + hp)), rows, rows,
                pl.BlockSpec((2 * KEY_TILE, 2 * KEY_TILE), lambda bi, hp: (0, 0))]
    args = [big, k_rows, v_rows, _cumsum_rhs()]
    scratch = [pltpu.VMEM((KEY_TILE, width), BF16)] * 2
    if n_cache_tiles:
        near = ATTN_STATIC_TILES * KEY_TILE
        assert n_cache_tiles % ATTN_STATIC_TILES == 0
        near_blk = n_cache_tiles // ATTN_STATIC_TILES - 1
        in_specs += [pl.BlockSpec((1, width, near), lambda bi, hp: (bi, hp, near_blk))] * 2
        in_specs += [pl.BlockSpec(memory_space=pl.ANY)] * 2
        args += [k_t, v_t, k_t, v_t]
        scratch += [pltpu.VMEM((width, KEY_TILE), F32)] * 2
    return pl.pallas_call(
        functools.partial(_attn_step_kernel, n_free=n_rows - N_META, n_cache_tiles=n_cache_tiles),
        grid=(b, SB_WIDTH // width),
        in_specs=in_specs,
        out_specs=pl.BlockSpec((1, N_META, width), lambda bi, hp: (bi, 0, hp)),
        out_shape=jax.ShapeDtypeStruct((b, N_META, SB_WIDTH), BF16),
        scratch_shapes=scratch,
        compiler_params=_cparams(("parallel", "parallel")),
        name="stick_breaking_step",
    )(*args)


def _rows8(op, x, r8):
    return op(x.reshape(x.shape[0] // 8, 8, x.shape[1]), r8[None]).reshape(x.shape)


def _ssd_kernel(xbc_ref, x16_ref, z_ref, dt_ref, dtT_ref, prev_ref, h0_ref, cw_ref, cb_ref, dtb_ref, dtbT_ref,
                alog_ref, alogT_ref, dskip_ref, gn_ref, e3_ref, ltri_ref, ublk_ref, shift_ref,
                y_ref, hfin_ref, st_ref, *, rows, n_sub, n_chunks):
    Q = SSD_CHUNK
    c = pl.program_id(1)
    n_blk = SSD_WIDTH // LANES
    mul, sub = jnp.multiply, jnp.subtract

    @pl.when(c == 0)
    def _():
        for j in range(n_blk):
            st_ref[:, j * LANES:(j + 1) * LANES] = h0_ref[0, j * LANES:(j + 1) * LANES, :].T

    def pad_rows(v):
        if rows == Q:
            return v
        return jnp.concatenate([v, jnp.zeros((Q - rows, v.shape[1]), v.dtype)], axis=0)

    lane = lax.broadcasted_iota(jnp.int32, (Q, LANES), 1)
    rowq = lax.broadcasted_iota(jnp.int32, (Q, LANES), 0)
    causal2 = (lane % Q) <= rowq
    lo_half = lane < SSD_HEAD_DIM
    decay_rate = -LOG2E * jnp.exp(alog_ref[...])
    decay_rate_t = -LOG2E * jnp.exp(alogT_ref[...])

    for k in range(n_sub):
        r0 = k * Q
        hi, lo = _split2(prev_ref[0])
        if k == 0:
            hi = jnp.where(c == 0, hi, x16_ref[0])
            lo = jnp.where(c == 0, lo, jnp.zeros_like(lo))
        else:
            hi, lo = xbc_ref[0, r0 - 16:r0, :], jnp.zeros_like(lo)
        window = jnp.concatenate([pad_rows(xbc_ref[0, r0:r0 + rows, :]), hi, lo,
                                  jnp.zeros((Q - 32, SSD_CONV_CH), BF16)], axis=0)
        shifted = _dot(shift_ref[...], window)
        conv = _rows8(mul, shifted[0:Q], cw_ref[0:8, :])
        for i in range(1, SSD_CONV):
            conv = conv + _rows8(mul, shifted[i * Q:(i + 1) * Q], cw_ref[8 * i:8 * i + 8, :])
        conv = _rows8(jnp.add, conv, cb_ref[...])
        xc = conv * _sigmoid(conv)
        xs = xc[:, :SSD_WIDTH]
        b_all = xc[:, SSD_WIDTH:SSD_WIDTH + SSD_GROUPS * SSD_STATE]
        c_all = xc[:, SSD_WIDTH + SSD_GROUPS * SSD_STATE:]

        dt = _softplus(pad_rows(dt_ref[0, r0:r0 + rows, :]) + dtb_ref[...])
        dt_t = _softplus(dtT_ref[0, k] + dtbT_ref[...])
        if rows < Q:
            rowi = lax.broadcasted_iota(jnp.int32, (Q, 1), 0)
            dt = jnp.where(rowi < rows, dt, 0.0)
            xs = jnp.where(rowi < rows, xs, 0.0)
            lane_t = lax.broadcasted_iota(jnp.int32, dt_t.shape, 1)
            dt_t = jnp.where(lane_t % Q < rows, dt_t, 0.0)
        da = dt * decay_rate
        da_t = dt_t * decay_rate_t

        a_cum = _dot(ltri_ref[...], jnp.concatenate(_split3(da), axis=0))
        a_cum_t = _dot(jnp.concatenate(_split3(da_t), axis=1), ublk_ref[...])
        dt_exp = _dot(jnp.concatenate(_split2(dt), axis=1), e3_ref[0:2 * LANES, :])
        a_exp = _dot(jnp.concatenate(_split3(a_cum), axis=1), e3_ref[...])
        a_last = jnp.broadcast_to(a_exp[Q - 1:Q, :], (8, SSD_WIDTH))

        xdt = xs * dt_exp
        xdt_b = xdt.astype(BF16)
        xw = (xdt * jnp.exp2(-_rows8(sub, a_exp, a_last))).astype(BF16)
        xw_pad = jnp.concatenate([xw, jnp.zeros_like(xw)], axis=0)
        chunk_decay = jnp.exp2(a_last)
        grow = jnp.exp2(a_exp)

        y_parts = []
        for g in range(SSD_GROUPS):
            bg = b_all[:, g * SSD_STATE:(g + 1) * SSD_STATE]
            cg = c_all[:, g * SSD_STATE:(g + 1) * SSD_STATE].astype(BF16)
            bg_b = bg.astype(BF16)
            cb2 = _dot_nt(cg, jnp.concatenate([bg_b, bg_b], axis=0))
            gs = slice(g * GROUP_W, (g + 1) * GROUP_W)
            st_g = st_ref[:, gs]
            y_off = _dot(cg, st_g.astype(BF16)) * grow[:, gs]
            bg_t = jnp.concatenate([bg, jnp.zeros_like(bg)], axis=0).T.astype(BF16)
            st_ref[:, gs] = _rows8(mul, st_g, chunk_decay[:, gs]) + _dot(bg_t, xw_pad[:, gs])
            pair_out = []
            for kk in range(GROUP_W // LANES):
                i = g * (GROUP_W // LANES) + kk
                ps = slice(i * LANES, (i + 1) * LANES)
                a_row = jnp.broadcast_to(a_cum_t[i:i + 1, :], (8, LANES))
                decay = jnp.exp2(jnp.minimum(_rows8(sub, a_exp[:, ps], a_row), 0.0))
                m2 = jnp.where(causal2, cb2 * decay, 0.0).astype(BF16)
                xp = xdt_b[:, ps]
                zero = jnp.zeros_like(xp)
                xbd = jnp.concatenate([jnp.where(lo_half, xp, zero), jnp.where(lo_half, zero, xp)], axis=0)
                pair_out.append(_dot(m2, xbd))
            y_diag = jnp.concatenate(pair_out, axis=1)
            y = y_diag + y_off + _rows8(mul, xs[:, gs], dskip_ref[:, gs])
            zg = pad_rows(z_ref[0, r0:r0 + rows, gs].astype(F32))
            y = y * (zg * _sigmoid(zg))
            ms = jnp.mean(y * y, axis=-1, keepdims=True)
            y_parts.append(_rows8(mul, y * lax.rsqrt(ms + EPS), gn_ref[:, gs]))
        y_all = jnp.concatenate(y_parts, axis=1)
        y_ref[0, r0:r0 + rows, :] = y_all[:rows].astype(y_ref.dtype)

    @pl.when(c == n_chunks - 1)
    def _():
        for j in range(n_blk):
            hfin_ref[0, j * LANES:(j + 1) * LANES, :] = st_ref[:, j * LANES:(j + 1) * LANES].T


def _ssd_constants():
    Q = SSD_CHUNK
    e = np.zeros((LANES, SSD_WIDTH), np.float32)
    for h in range(SSD_HEADS):
        e[h, h * SSD_HEAD_DIM:(h + 1) * SSD_HEAD_DIM] = 1.0
    e3 = np.concatenate([e, e, e], axis=0)
    t = np.arange(Q)
    ltri = (t[:, None] >= t[None, :]).astype(np.float32)
    ltri3 = np.concatenate([ltri, ltri, ltri], axis=1)
    ublk = np.zeros((LANES, LANES), np.float32)
    ublk[:Q, :Q] = ltri.T
    ublk[Q:, Q:] = ltri.T
    ublk3 = np.concatenate([ublk, ublk, ublk], axis=0)
    shift = np.zeros((SSD_CONV * Q, 2 * Q), np.float32)
    for i in range(SSD_CONV):
        for r in range(Q):
            src = r - (SSD_CONV - 1 - i)
            if src >= 0:
                shift[i * Q + r, src] = 1.0
            else:
                shift[i * Q + r, Q + 16 + src] = 1.0
                shift[i * Q + r, Q + 32 + src] = 1.0
    return jnp.asarray(e3, BF16), jnp.asarray(ltri3, BF16), jnp.asarray(ublk3, BF16), jnp.asarray(shift, BF16)


def ssd_mixer(big, xbc_blk, z_blk, dt_raw, conv_prev, h0, p):
    b, s, _ = big.shape
    Q = SSD_CHUNK
    rows = min(s, Q)
    n_sub = SSD_STEP_CHUNKS if s % (SSD_STEP_CHUNKS * Q) == 0 else 1
    blk = rows * n_sub
    n_chunks = -(-s // blk)
    dtp = dt_raw[:, :, :SSD_HEADS]
    if s < Q:
        dtp = jnp.pad(dtp, ((0, 0), (0, Q - s), (0, 0)))
    dt_t = dtp.reshape(b, n_chunks * n_sub, Q, SSD_HEADS).transpose(0, 1, 3, 2).reshape(
        b, n_chunks * n_sub, SSD_HEADS // 2, 2 * Q)
    e3, ltri3, ublk3, shift = _ssd_constants()
    prev16 = jnp.pad(conv_prev, ((0, 0), (16 - (SSD_CONV - 1), 0), (0, 0)))
    rep8 = lambda v: jnp.repeat(v, 8, axis=0)
    const = lambda shape: pl.BlockSpec(shape, lambda bi, ci: (0,) * len(shape))
    in_specs = [
        pl.BlockSpec((1, blk, SSD_CONV_CH), lambda bi, ci: (bi, ci, xbc_blk)),
        pl.BlockSpec((1, 16, SSD_CONV_CH), lambda bi, ci: (bi, jnp.maximum(ci * (blk // 16) - 1, 0), xbc_blk)),
        pl.BlockSpec((1, blk, SSD_WIDTH), lambda bi, ci: (bi, ci, z_blk)),
        pl.BlockSpec((1, blk, LANES), lambda bi, ci: (bi, ci, 0)),
        pl.BlockSpec((1, n_sub, SSD_HEADS // 2, 2 * Q), lambda bi, ci: (bi, ci, 0, 0)),
        pl.BlockSpec((1, 16, SSD_CONV_CH), lambda bi, ci: (bi, 0, 0)),
        pl.BlockSpec((1, SSD_WIDTH, SSD_STATE), lambda bi, ci: (bi, 0, 0)),
        const((8 * SSD_CONV, SSD_CONV_CH)), const((8, SSD_CONV_CH)),
        const((1, LANES)), const((SSD_HEADS // 2, 2 * Q)),
        const((1, LANES)), const((SSD_HEADS // 2, 2 * Q)),
        const((8, SSD_WIDTH)), const((8, SSD_WIDTH)),
        const(e3.shape), const(ltri3.shape), const(ublk3.shape), const(shift.shape),
    ]
    y, h_fin = pl.pallas_call(
        functools.partial(_ssd_kernel, rows=rows, n_sub=n_sub, n_chunks=n_chunks),
        grid=(b, n_chunks),
        in_specs=in_specs,
        out_specs=[pl.BlockSpec((1, blk, SSD_WIDTH), lambda bi, ci: (bi, ci, 0)),
                   pl.BlockSpec((1, SSD_WIDTH, SSD_STATE), lambda bi, ci: (bi, 0, 0))],
        out_shape=[jax.ShapeDtypeStruct((b, s, SSD_WIDTH), BF16),
                   jax.ShapeDtypeStruct((b, SSD_WIDTH, SSD_STATE), F32)],
        scratch_shapes=[pltpu.VMEM((SSD_STATE, SSD_WIDTH), F32)],
        compiler_params=_cparams(("parallel", "arbitrary")),
        name="ssd_mixer",
    )(big, big, big, dt_raw, dt_t, prev16, h0, rep8(p['conv_w_ssd']), rep8(p['conv_b_ssd']), p['dt_bias'], p['dt_bias_t'],
      p['a_log'], p['a_log_t'], rep8(p['d_skip']), rep8(p['g_ssd_norm']), e3, ltri3, ublk3, shift)
    return y, h_fin


def _merge_kernel(att_ref, ssd_ref, ga_ref, gs_ref, h_ref, wa_ref, ws_ref, wo_ref, o_ref, wa_b, ws_b, wo_b):
    @pl.when(pl.program_id(0) == 0)
    def _():
        wa_b[...] = wa_ref[...].astype(BF16)
        ws_b[...] = ws_ref[...].astype(BF16)
        wo_b[...] = wo_ref[...].astype(BF16)

    a = _dot(att_ref[...], wa_b[...])
    s = _dot(ssd_ref[...], ws_b[...])
    merged = _sigmoid(ga_ref[...].astype(F32)) * a + _sigmoid(gs_ref[...].astype(F32)) * s
    o_ref[...] = h_ref[...] + _dot(merged.astype(BF16), wo_b[...])


def _resident(a):
    return pl.BlockSpec(a.shape, lambda i: (0,) * a.ndim, pipeline_mode=pl.Buffered(1))


def merge(att, ssd, big, h, p):
    m = h.shape[0]
    tm = _pick(m, (512, 256, 128))
    row = lambda w: pl.BlockSpec((tm, w), lambda i: (i, 0))
    gate_blk = BIG_GATE // D_MODEL
    ws = [p['w_br_att'], p['w_br_ssd'], p['w_out']]
    return pl.pallas_call(
        _merge_kernel,
        grid=(m // tm,),
        in_specs=[row(SB_WIDTH), row(SSD_WIDTH),
                  pl.BlockSpec((tm, D_MODEL), lambda i: (i, gate_blk)),
                  pl.BlockSpec((tm, D_MODEL), lambda i: (i, gate_blk + 1)),
                  row(D_MODEL)] + [_resident(w) for w in ws],
        out_specs=row(D_MODEL),
        out_shape=jax.ShapeDtypeStruct((m, D_MODEL), F32),
        scratch_shapes=[pltpu.VMEM(w.shape, BF16) for w in ws],
        compiler_params=_cparams(("arbitrary",)),
        name="merge_out_proj",
    )(att, ssd, big, big, h, *ws)


def _ffn_act_kernel(ug_ref, uv_ref, pg_ref, pv_ref, wg_ref, wv_ref, bg_ref, bv_ref, o_ref, gbuf_ref, vbuf_ref,
                    *, ts):
    t = pl.program_id(1)
    k1 = FFN_CONV - 1

    def conv(u_ref, p_ref, w_ref, b_ref, buf_ref):
        @pl.when(t == 0)
        def _():
            buf_ref[8 - k1:8, :] = p_ref[0]

        buf_ref[8:8 + ts, :] = u_ref[0].astype(F32)
        y = b_ref[...]
        for i in range(FFN_CONV):
            y = y + buf_ref[8 - k1 + i:8 - k1 + i + ts, :] * w_ref[i:i + 1, :]
        buf_ref[8 - k1:8, :] = buf_ref[8 + ts - k1:8 + ts, :]
        return y

    gate = conv(ug_ref, pg_ref, wg_ref, bg_ref, gbuf_ref)
    val = conv(uv_ref, pv_ref, wv_ref, bv_ref, vbuf_ref)
    o_ref[0] = (gate * _sigmoid(gate) * val).astype(o_ref.dtype)


FFN_SUB = 64


UPF_TM = 512
UPF_TN = D_FF // 2


def _up_ffn_kernel(h_ref, h16_ref, g_ref, wg_ref, wv_ref, pg_ref, pv_ref, cwg_ref, cwv_ref, cbg_ref, cbv_ref,
                   o_ref, gl_ref, vl_ref, wg_b, wv_b, gbuf_ref, vbuf_ref):
    i = pl.program_id(2)
    tm = h_ref.shape[1]

    @pl.when(jnp.logical_and(pl.program_id(1) == 0, i == 0))
    def _():
        wg_b[...] = wg_ref[...].astype(BF16)
        wv_b[...] = wv_ref[...].astype(BF16)

    u = _rms_norm_bf16(h_ref[0], g_ref[...])
    u16 = _rms_norm_bf16(h16_ref[0], g_ref[...])

    def conv(w_b, p_ref, last_ref, buf_ref, cw_ref, cb_ref):
        up = _dot(u, w_b[...])
        buf_ref[0:16, :] = jnp.where(i == 0, p_ref[0], _dot(u16, w_b[...]))
        buf_ref[16:16 + tm, :] = up
        last_ref[0] = up[tm - 16:tm, :]
        y = _rows8(jnp.multiply, up, cw_ref[8 * (FFN_CONV - 1):8 * FFN_CONV, :])
        for t in range(FFN_CONV - 1):
            r0 = 16 - (FFN_CONV - 1) + t
            y = y + _rows8(jnp.multiply, buf_ref[r0:r0 + tm, :], cw_ref[8 * t:8 * t + 8, :])
        return _rows8(jnp.add, y, cb_ref[...])

    gate = conv(wg_b, pg_ref, gl_ref, gbuf_ref, cwg_ref, cbg_ref)
    val = conv(wv_b, pv_ref, vl_ref, vbuf_ref, cwv_ref, cbv_ref)
    o_ref[0] = (gate * _sigmoid(gate) * val).astype(o_ref.dtype)


def up_ffn_long(h, prev, p):
    b, s, d = h.shape
    tm, tn = UPF_TM, UPF_TN
    nj = D_FF // tn
    prev16 = jnp.pad(prev, ((0, 0), (16 - (FFN_CONV - 1), 0), (0, 0)))
    rep8 = lambda v: jnp.repeat(v, 8, axis=0)
    w8, b8 = rep8(p['conv_w_ffn']), rep8(p['conv_b_ffn'])
    cols = lambda rows, half: pl.BlockSpec((rows, tn), lambda j, bi, i: (0, half * nj + j))
    w_spec = lambda half: pl.BlockSpec((d, tn), lambda j, bi, i: (0, half * nj + j), pipeline_mode=pl.Buffered(1))
    prevs = lambda half: pl.BlockSpec((1, 16, tn), lambda j, bi, i: (bi, 0, half * nj + j))
    last_spec = pl.BlockSpec((1, 16, tn), lambda j, bi, i: (bi, 0, j))
    return pl.pallas_call(
        _up_ffn_kernel,
        grid=(nj, b, s // tm),
        in_specs=[pl.BlockSpec((1, tm, d), lambda j, bi, i: (bi, i, 0)),
                  pl.BlockSpec((1, 16, d), lambda j, bi, i: (bi, jnp.maximum(i * (tm // 16) - 1, 0), 0)),
                  pl.BlockSpec((1, d), lambda j, bi, i: (0, 0)),
                  w_spec(0), w_spec(1), prevs(0), prevs(1),
                  cols(8 * FFN_CONV, 0), cols(8 * FFN_CONV, 1), cols(8, 0), cols(8, 1)],
        out_specs=[pl.BlockSpec((1, tm, tn), lambda j, bi, i: (bi, i, j)), last_spec, last_spec],
        out_shape=[jax.ShapeDtypeStruct((b, s, D_FF), BF16)] + [jax.ShapeDtypeStruct((b, 16, D_FF), F32)] * 2,
        scratch_shapes=[pltpu.VMEM((d, tn), BF16)] * 2 + [pltpu.VMEM((16 + tm, tn), F32)] * 2,
        compiler_params=_cparams(("arbitrary", "arbitrary", "arbitrary")),
        name="up_proj_conv_ffn",
    )(h, h, p['g_ffn'], p['w_up'], p['w_up'], prev16, prev16, w8, w8, b8, b8)


def ffn_act(up, prev, p):
    b, s, _ = up.shape
    ts = _pick(s, (512, 256, 128))
    half = lambda blk: pl.BlockSpec((1, ts, D_FF), lambda bi, ti: (bi, ti, blk))
    prevs = lambda blk: pl.BlockSpec((1, FFN_CONV - 1, D_FF), lambda bi, ti: (bi, 0, blk))
    wspec = lambda blk: pl.BlockSpec((FFN_CONV, D_FF), lambda bi, ti: (0, blk))
    bspec = lambda blk: pl.BlockSpec((1, D_FF), lambda bi, ti: (0, blk))
    return pl.pallas_call(
        functools.partial(_ffn_act_kernel, ts=ts),
        grid=(b, s // ts),
        in_specs=[half(0), half(1), prevs(0), prevs(1), wspec(0), wspec(1), bspec(0), bspec(1)],
        out_specs=pl.BlockSpec((1, ts, D_FF), lambda bi, ti: (bi, ti, 0)),
        out_shape=jax.ShapeDtypeStruct((b, s, D_FF), BF16),
        scratch_shapes=[pltpu.VMEM((8 + ts, D_FF), F32)] * 2,
        compiler_params=_cparams(("parallel", "arbitrary")),
        name="ffn_conv_act",
    )(up, up, prev, prev, p['conv_w_ffn'], p['conv_w_ffn'], p['conv_b_ffn'], p['conv_b_ffn'])


def _down_kernel(a_ref, w_ref, h_ref, g_ref, o_ref, w_b):
    @pl.when(pl.program_id(0) == 0)
    def _():
        w_b[...] = w_ref[...].astype(BF16)

    h = h_ref[...] + _dot(a_ref[...], w_b[...])
    ms = jnp.mean(h * h, axis=-1, keepdims=True)
    o_ref[...] = h * lax.rsqrt(ms + EPS) * g_ref[...]


def down_norm(act, h, p):
    m = h.shape[0]
    tm = _pick(m, (512, 256, 128))
    return pl.pallas_call(
        _down_kernel,
        grid=(m // tm,),
        in_specs=[pl.BlockSpec((tm, D_FF), lambda i: (i, 0)),
                  _resident(p['w_down']),
                  pl.BlockSpec((tm, D_MODEL), lambda i: (i, 0)),
                  pl.BlockSpec((1, D_MODEL), lambda i: (0, 0))],
        out_specs=pl.BlockSpec((tm, D_MODEL), lambda i: (i, 0)),
        out_shape=jax.ShapeDtypeStruct((m, D_MODEL), F32),
        scratch_shapes=[pltpu.VMEM(p['w_down'].shape, BF16)],
        compiler_params=_cparams(("arbitrary",)),
        name="down_proj_norm",
    )(act, p['w_down'], h, p['g_final'])


def mix_and_ffn(big3, att, dt_raw, hf, ssd_h0, ssd_conv_prev, ffn_conv_prev, p):
    b, s, _ = big3.shape
    m = b * s
    ssd, ssd_state = ssd_mixer(big3, BIG_XBC // SSD_CONV_CH, BIG_Z // SSD_WIDTH, dt_raw, ssd_conv_prev, ssd_h0, p)
    h1 = merge(att.reshape(m, SB_WIDTH), ssd.reshape(m, SSD_WIDTH), big3.reshape(m, BIG_W), hf, p)
    if s % UPF_TM == 0:
        act, up_g, up_v = up_ffn_long(h1.reshape(b, s, D_MODEL), ffn_conv_prev, p)
        up_last = jnp.concatenate([up_g, up_v], axis=2)
    else:
        up_last = norm_mm(h1, p['g_ffn'], p['w_up'], BF16).reshape(b, s, 2 * D_FF)
        act = ffn_act(up_last, ffn_conv_prev, p)
    y = down_norm(act.reshape(m, D_FF), h1, p).reshape(b, s, D_MODEL)
    xbc_rows = jnp.concatenate([ssd_conv_prev, big3[:, :, BIG_XBC:BIG_XBC + SSD_CONV_CH][:, -(SSD_CONV - 1):].astype(F32)],
                               axis=1)[:, -(SSD_CONV - 1):]
    up_rows = jnp.concatenate([ffn_conv_prev, up_last[:, -(FFN_CONV - 1):].astype(F32)], axis=1)[:, -(FFN_CONV - 1):]
    return y, ssd_state, xbc_rows, up_rows


def _prep_params(g_mix, w_in, conv_w_ssd, conv_b_ssd, dt_bias, a_log, d_skip, g_ssd_norm, w_br_att, w_br_ssd,
                 w_out, g_ffn, w_up, conv_w_ffn, conv_b_ffn, w_down, g_final):
    w_t = w_in.T
    dt0 = 3 * SB_WIDTH + SSD_WIDTH + SSD_CONV_CH
    w_tail_t = jnp.concatenate([w_t[dt0 + SSD_HEADS:], w_t[dt0:dt0 + SSD_HEADS],
                                jnp.zeros((LANES - SSD_HEADS, D_MODEL), F32)], axis=0)
    Q = SSD_CHUNK
    lanes_t = lambda v: jnp.repeat(v.reshape(SSD_HEADS // 2, 2, 1), Q, axis=2).reshape(SSD_HEADS // 2, 2 * Q)
    pad_l = lambda v: jnp.pad(v, (0, LANES - SSD_HEADS)).reshape(1, LANES)
    return {
        'g_mix': g_mix.reshape(1, -1), 'g_ffn': g_ffn.reshape(1, -1), 'g_final': g_final.reshape(1, -1),
        'w_t': w_t, 'w_tail_t': w_tail_t,
        'conv_w_ssd': conv_w_ssd, 'conv_b_ssd': conv_b_ssd.reshape(1, -1),
        'dt_bias': pad_l(dt_bias), 'dt_bias_t': lanes_t(dt_bias),
        'a_log': pad_l(a_log), 'a_log_t': lanes_t(a_log),
        'd_skip': jnp.repeat(d_skip, SSD_HEAD_DIM).reshape(1, -1), 'g_ssd_norm': g_ssd_norm.reshape(1, -1),
        'w_br_att': w_br_att, 'w_br_ssd': w_br_ssd, 'w_out': w_out,
        'w_up': w_up, 'conv_w_ffn': conv_w_ffn, 'conv_b_ffn': conv_b_ffn.reshape(1, -1),
        'w_down': w_down,
    }


def kernel(x_prompt, x_sample, cache_k, cache_v, state_ssm, state_ssm_conv, state_ffn_conv, meta_tokens, g_mix, w_in, conv_w_ssd, conv_b_ssd, dt_bias, a_log, d_skip, g_ssd_norm, w_br_att, w_br_ssd, w_out, g_ffn, w_up, conv_w_ffn, conv_b_ffn, w_down, g_final):
    bp, seq, _ = x_prompt.shape
    bs, dec = x_sample.shape[:2]
    assert dec == N_META and seq % KVT_TS == 0
    p = _prep_params(g_mix[0], w_in[0], conv_w_ssd[0], conv_b_ssd[0], dt_bias[0], a_log[0], d_skip[0],
                     g_ssd_norm[0], w_br_att[0], w_br_ssd[0], w_out[0], g_ffn[0], w_up[0], conv_w_ffn[0],
                     conv_b_ffn[0], w_down[0], g_final)
    proj = functools.partial(in_proj, g=p['g_mix'], w_t=p['w_t'], w_tail_t=p['w_tail_t'])
    state_shape = (SSD_WIDTH, SSD_STATE)
    rows = lambda a, b: a.reshape(b, -1, a.shape[-1])

    k_m, v_m, big_m, dt_m = proj(meta_tokens, with_kv=True)
    big_m, k_m, v_m = big_m[None], k_m[None], v_m[None]
    att_m = attention_step(big_m, 0, k_m, v_m, None, None, 0, SB_WIDTH)
    _, ssm_m, conv_m, ffn_m = mix_and_ffn(
        big_m, att_m, dt_m[None], meta_tokens, jnp.zeros((1,) + state_shape, F32),
        jnp.zeros((1, SSD_CONV - 1, SSD_CONV_CH), F32), jnp.zeros((1, FFN_CONV - 1, 2 * D_FF), F32), p)

    xf = x_prompt.reshape(bp * seq, D_MODEL)
    big_x, dt_x = proj(xf, with_kv=False)
    big_x = rows(big_x, bp)
    k_t, v_t, k_end, v_end = kv_transposed(x_prompt, meta_tokens, p['g_mix'], p['w_t'])
    att_x = attention_prompt(big_x, big_m, k_t, v_t)
    att_end = attention_step(big_x, seq // N_META - 1, k_end, v_end, k_t, v_t, seq // KEY_TILE, ATTN_PROMPT_WIDTH)
    att_x = lax.dynamic_update_slice(att_x, att_end, (0, seq - N_META, 0))
    rep = lambda a: jnp.broadcast_to(a, (bp,) + a.shape[1:])
    y_prompt, ssm_p, conv_p, ffn_p = mix_and_ffn(
        big_x, att_x, rows(dt_x, bp), xf, rep(ssm_m), rep(conv_m), rep(ffn_m), p)

    n_rows = cache_k.shape[2]
    n_tiles = (n_rows - N_META) // KEY_TILE
    assert n_tiles * KEY_TILE + N_META == n_rows
    k_s, v_s, big_s, dt_s = proj(x_sample.reshape(bs * dec, D_MODEL), with_kv=True)
    big_s, k_s, v_s = rows(big_s, bs), rows(k_s, bs), rows(v_s, bs)
    cache_t = lambda c: c[0].transpose(0, 2, 3, 1).reshape(bs, SB_WIDTH, n_rows)
    nearest = lambda c, new: jnp.concatenate([c[0][:, n_tiles * KEY_TILE:].reshape(bs, N_META, SB_WIDTH), new], axis=1)
    att_s = attention_step(big_s, 0, nearest(cache_k, k_s), nearest(cache_v, v_s), cache_t(cache_k), cache_t(cache_v),
                           n_tiles, SB_WIDTH)
    y_sample, ssm_s, conv_s, ffn_s = mix_and_ffn(
        big_s, att_s, rows(dt_s, bs), x_sample.reshape(bs * dec, D_MODEL),
        state_ssm[0].reshape((bs,) + state_shape), state_ssm_conv[0], state_ffn_conv[0], p)

    heads = lambda a: a.reshape(a.shape[0], a.shape[1], SB_HEADS, SB_HEAD_DIM)[None]
    heads_t = lambda a: a.reshape(a.shape[0], SB_HEADS, SB_HEAD_DIM, a.shape[2]).transpose(0, 3, 1, 2)[None]
    state5 = lambda a: a.reshape(a.shape[0], SSD_HEADS, SSD_HEAD_DIM, SSD_STATE)[None]
    return (y_prompt, y_sample, heads_t(k_t), heads_t(v_t), state5(ssm_p), conv_p[None], ffn_p[None],
            heads(k_s), heads(v_s), state5(ssm_s), conv_s[None], ffn_s[None])
```

```python
import functools

import numpy as np
import jax
import jax.numpy as jnp
from jax import lax
from jax.experimental import pallas as pl
from jax.experimental.pallas import tpu as pltpu

F32 = jnp.float32
BF16 = jnp.bfloat16

D_MODEL = 1024
N_META = 16
SB_HEADS = 16
SB_HEAD_DIM = 64
SB_WIDTH = SB_HEADS * SB_HEAD_DIM
SSD_WIDTH = 2 * D_MODEL
SSD_HEAD_DIM = 64
SSD_HEADS = SSD_WIDTH // SSD_HEAD_DIM
SSD_GROUPS = 4
SSD_STATE = 128
SSD_CONV = 4
SSD_CONV_CH = SSD_WIDTH + 2 * SSD_GROUPS * SSD_STATE
D_FF = 2816
FFN_CONV = 3
EPS = 1e-6

LANES = 128
SSD_CHUNK = 64
SSD_STEP_CHUNKS = 4
KEY_TILE = 128
GROUP_W = SSD_WIDTH // SSD_GROUPS
DEAD_LOG = -104.0
LOG2E = 1.4426950408889634
ATTN_STATIC_TILES = 2
ATTN_GROUP_TILES = 3
VMEM_LIMIT = 56 * 1024 * 1024


def _cparams(sem):
    return pltpu.CompilerParams(dimension_semantics=sem, vmem_limit_bytes=VMEM_LIMIT)


def _pick(n, cands):
    for c in cands:
        if n % c == 0:
            return c
    return n


def _split2(x):
    hi = x.astype(BF16)
    lo = (x - hi.astype(F32)).astype(BF16)
    return hi, lo


def _split3(x):
    hi = x.astype(BF16)
    r = x - hi.astype(F32)
    mid = r.astype(BF16)
    lo = (r - mid.astype(F32)).astype(BF16)
    return hi, mid, lo


def _dot(a, b):
    return jnp.dot(a, b, preferred_element_type=F32)


def _dot_nt(a, b):
    return lax.dot_general(a, b, (((1,), (1,)), ((), ())), preferred_element_type=F32)


def _softplus(x):
    return jnp.maximum(x, 0.0) + jnp.log(1.0 + jnp.exp(-jnp.abs(x)))


def _sigmoid(x):
    return 1.0 / (1.0 + jnp.exp(-x))


def _rms_norm_bf16(x, g):
    ms = jnp.mean(x * x, axis=-1, keepdims=True)
    return (x * lax.rsqrt(ms + EPS) * g).astype(BF16)


def _norm_mm_kernel(x_ref, g_ref, w_ref, o_ref, u_ref):
    @pl.when(pl.program_id(1) == 0)
    def _():
        u_ref[...] = _rms_norm_bf16(x_ref[...], g_ref[...])

    o_ref[...] = _dot(u_ref[...], w_ref[...].astype(BF16)).astype(o_ref.dtype)


PROJ_TN = 1024
UP_TN = 1408


def norm_mm(x, g, w, out_dtype):
    m, d = x.shape
    n = w.shape[1]
    tm = _pick(m, (2048, 1024, 512, 256, 128))
    tn = UP_TN
    return pl.pallas_call(
        _norm_mm_kernel,
        grid=(m // tm, n // tn),
        in_specs=[pl.BlockSpec((tm, d), lambda i, j: (i, 0)),
                  pl.BlockSpec((1, d), lambda i, j: (0, 0)),
                  pl.BlockSpec((d, tn), lambda i, j: (0, j))],
        out_specs=pl.BlockSpec((tm, tn), lambda i, j: (i, j)),
        out_shape=jax.ShapeDtypeStruct((m, n), out_dtype),
        scratch_shapes=[pltpu.VMEM((tm, d), BF16)],
        compiler_params=_cparams(("parallel", "arbitrary")),
        name="norm_up_proj",
    )(x, g, w)


_NQ = SB_WIDTH // PROJ_TN
_J_K, _J_V, _J_Z = _NQ, 2 * _NQ, 3 * _NQ
_J_X = _J_Z + SSD_WIDTH // PROJ_TN
_J_G = _J_X + SSD_CONV_CH // PROJ_TN
_J_END = _J_G + 2 * D_MODEL // PROJ_TN
BIG_Z, BIG_Q, BIG_XBC, BIG_GATE = 0, SSD_WIDTH, SSD_WIDTH + SB_WIDTH, SSD_WIDTH + SB_WIDTH + SSD_CONV_CH
BIG_W = BIG_GATE + 2 * D_MODEL


def _big_block(j):
    return jnp.where(j < _J_K, BIG_Q // PROJ_TN + j,
                     jnp.where(j < _J_Z, BIG_Q // PROJ_TN + _NQ - 1,
                               jnp.where(j < _J_X, j - _J_Z + BIG_Z // PROJ_TN,
                                         jnp.where(j < _J_G, j - _J_X + BIG_XBC // PROJ_TN,
                                                   j - _J_G + BIG_GATE // PROJ_TN))))


def _sweep_step(j, with_kv):
    return j if with_kv else jnp.where(j >= _J_K, j + (_J_Z - _J_K), j)


def _in_proj_kernel(*refs, with_kv):
    if with_kv:
        x_ref, g_ref, win_ref, wgate_ref, wdt_ref, k_ref, v_ref, big_ref, dt_ref, u_ref, w_b = refs
    else:
        x_ref, g_ref, win_ref, wgate_ref, wdt_ref, big_ref, dt_ref, u_ref, w_b = refs
    step = pl.program_id(1)
    j = _sweep_step(step, with_kv)
    tn = win_ref.shape[0]
    rows = pl.ds(pl.multiple_of(step * tn, tn), tn)

    @pl.when(jnp.logical_and(pl.program_id(0) == 0, j < _J_G))
    def _():
        w_b[rows, :] = win_ref[...].astype(BF16)

    @pl.when(jnp.logical_and(pl.program_id(0) == 0, j >= _J_G))
    def _():
        w_b[rows, :] = wgate_ref[...].astype(BF16)

    @pl.when(step == 0)
    def _():
        u_ref[...] = _rms_norm_bf16(x_ref[...], g_ref[...])
        dt_ref[...] = _dot_nt(u_ref[...], wdt_ref[...].astype(BF16))

    def proj():
        return _dot_nt(u_ref[...], w_b[rows, :])

    if with_kv:
        @pl.when(jnp.logical_and(j >= _J_K, j < _J_V))
        def _():
            k_ref[...] = proj()

        @pl.when(jnp.logical_and(j >= _J_V, j < _J_Z))
        def _():
            v_ref[...] = proj()

    @pl.when(jnp.logical_or(j < _J_K, j >= _J_Z))
    def _():
        big_ref[...] = proj().astype(BF16)


def in_proj(x, g, w_t, w_tail_t, with_kv):
    m, d = x.shape
    tm = _pick(m, (1024, 512, 256, 128))
    tn = PROJ_TN
    n_steps = _J_END if with_kv else _J_END - (_J_Z - _J_K)
    step = functools.partial(_sweep_step, with_kv=with_kv)
    clip = lambda j, lo, n: jnp.clip(step(j) - lo, 0, n - 1)
    first = lambda i, idx, last: jnp.where(i == 0, idx, last)
    once = pl.Buffered(1)
    kv_specs = [pl.BlockSpec((tm, tn), lambda i, j: (i, clip(j, _J_K, _NQ))),
                pl.BlockSpec((tm, tn), lambda i, j: (i, clip(j, _J_V, _NQ)))]
    kv_shapes = [jax.ShapeDtypeStruct((m, SB_WIDTH), F32)] * 2
    return pl.pallas_call(
        functools.partial(_in_proj_kernel, with_kv=with_kv),
        grid=(m // tm, n_steps),
        in_specs=[pl.BlockSpec((tm, d), lambda i, j: (i, 0), pipeline_mode=once),
                  pl.BlockSpec((1, d), lambda i, j: (0, 0)),
                  pl.BlockSpec((tn, d), lambda i, j: (first(i, jnp.minimum(step(j), _J_G - 1), _J_G - 1), 0),
                               pipeline_mode=once),
                  pl.BlockSpec((tn, d), lambda i, j: (first(i, clip(j, _J_G, _J_END - _J_G), _J_END - _J_G - 1), 0),
                               pipeline_mode=once),
                  pl.BlockSpec((LANES, d), lambda i, j: (2 * D_MODEL // LANES, 0))],
        out_specs=(kv_specs if with_kv else []) + [
            pl.BlockSpec((tm, tn), lambda i, j: (i, _big_block(step(j)))),
            pl.BlockSpec((tm, LANES), lambda i, j: (i, 0))],
        out_shape=(kv_shapes if with_kv else []) + [
            jax.ShapeDtypeStruct((m, BIG_W), BF16), jax.ShapeDtypeStruct((m, LANES), F32)],
        scratch_shapes=[pltpu.VMEM((tm, d), BF16), pltpu.VMEM((n_steps * tn, d), BF16)],
        compiler_params=_cparams(("arbitrary", "arbitrary")),
        name="in_proj",
    )(x, g, w_t, w_tail_t, w_tail_t)


KVT_TS = 1024


def _kv_t_kernel(x16_ref, xt_ref, meta_ref, g_ref, wk_ref, wv_ref, kt_ref, vt_ref, kl_ref, vl_ref, wk_b, wv_b):
    j = pl.program_id(1)

    @pl.when(jnp.logical_and(pl.program_id(0) == 0, j == 0))
    def _():
        wk_b[...] = wk_ref[...].astype(BF16)
        wv_b[...] = wv_ref[...].astype(BF16)

    first = jnp.where(j == 0, meta_ref[...], x16_ref[0])
    x = jnp.concatenate([first, xt_ref[0, :KVT_TS - N_META, :]], axis=0)
    u = _rms_norm_bf16(x, g_ref[...])
    kt_ref[0] = _dot_nt(wk_b[...], u)
    vt_ref[0] = _dot_nt(wv_b[...], u)

    @pl.when(j == pl.num_programs(1) - 1)
    def _():
        kl_ref[0] = _dot_nt(u[0:N_META, :], wk_b[...])
        vl_ref[0] = _dot_nt(u[0:N_META, :], wv_b[...])


def kv_transposed(x, meta, g, w_t):
    b, s, d = x.shape
    n_pos = N_META + s
    n_tiles = -(-n_pos // KVT_TS)
    assert n_pos - (n_tiles - 1) * KVT_TS == N_META
    per_tile = KVT_TS // N_META
    out_spec = pl.BlockSpec((1, SB_WIDTH, KVT_TS), lambda bi, j: (bi, 0, j))
    last_spec = pl.BlockSpec((1, N_META, SB_WIDTH), lambda bi, j: (bi, 0, 0))
    w_spec = lambda blk: pl.BlockSpec((SB_WIDTH, d), lambda bi, j: (blk, 0), pipeline_mode=pl.Buffered(1))
    return pl.pallas_call(
        _kv_t_kernel,
        grid=(b, n_tiles),
        in_specs=[pl.BlockSpec((1, N_META, d), lambda bi, j: (bi, jnp.maximum(j * per_tile - 1, 0), 0)),
                  pl.BlockSpec((1, KVT_TS, d), lambda bi, j: (bi, jnp.minimum(j, s // KVT_TS - 1), 0)),
                  pl.BlockSpec((N_META, d), lambda bi, j: (0, 0)),
                  pl.BlockSpec((1, d), lambda bi, j: (0, 0)),
                  w_spec(1), w_spec(2)],
        out_specs=[out_spec, out_spec, last_spec, last_spec],
        out_shape=[jax.ShapeDtypeStruct((b, SB_WIDTH, n_pos), F32)] * 2
        + [jax.ShapeDtypeStruct((b, N_META, SB_WIDTH), F32)] * 2,
        scratch_shapes=[pltpu.VMEM((SB_WIDTH, d), BF16)] * 2,
        compiler_params=_cparams(("arbitrary", "arbitrary")),
        name="kv_transposed",
    )(x, x, meta, g, w_t, w_t)


def _stick_breaking_tiles(qs, u2, tiles, carries, accs):
    n_heads = len(qs)
    units = [(t, h) for t in range(len(tiles)) for h in range(n_heads)]
    ps = lambda h: slice((h // 2) * LANES, (h // 2 + 1) * LANES)

    def scores(t, h):
        kind, keys = tiles[t][0], tiles[t][1]
        return _dot_nt(qs[h], keys[:, ps(h)]) if kind == 'rows' else _dot(qs[h], keys[ps(h), :])

    z = {u: scores(*u) for u in units}
    log_beta, cat = {}, {}
    for u in units:
        mask = tiles[u[0]][3]
        soft = jnp.log(1.0 + jnp.exp(-jnp.abs(z[u])))
        log_beta[u] = jnp.minimum(z[u], 0.0) - soft
        log_keep = log_beta[u] - z[u]
        if mask is not None:
            log_keep = jnp.where(mask, log_keep, 0.0)
        cat[u] = jnp.concatenate(_split2(log_keep), axis=1)
    r = {u: _dot(cat[u], u2) for u in units}
    carries, accs = list(carries), list(accs)
    w = {}
    for t, h in units:
        mask = tiles[t][3]
        wt = jnp.exp(log_beta[(t, h)] + r[(t, h)][:, :LANES] + carries[h])
        if mask is not None:
            wt = jnp.where(mask, wt, 0.0)
        w[(t, h)] = wt.astype(BF16)
        carries[h] = carries[h] + r[(t, h)][:, LANES:]
    for t, h in units:
        kind, vals = tiles[t][0], tiles[t][2]
        pv = _dot(w[(t, h)], vals[:, ps(h)]) if kind == 'rows' else _dot_nt(w[(t, h)], vals[ps(h), :])
        accs[h] = accs[h] + pv
    return carries, accs


def _head_queries(q, lo_half):
    qs = []
    for h in range(q.shape[1] // SB_HEAD_DIM):
        qp = q[:, (h // 2) * LANES:(h // 2 + 1) * LANES]
        keep = lo_half if h % 2 == 0 else jnp.logical_not(lo_half)
        qs.append(jnp.where(keep, qp, jnp.zeros_like(qp)) * jnp.asarray(SB_HEAD_DIM ** -0.5, BF16))
    return qs


def _any_alive(carries):
    m = carries[0]
    for c in carries[1:]:
        m = jnp.maximum(m, c)
    return (jnp.max(m) > DEAD_LOG).astype(jnp.int32)


def _attend(qs, u2, first_tiles, load_tile, j0, lane):
    tq = qs[0].shape[0]

    def cache_tile(j):
        if isinstance(j, int):
            mask = None if j >= 0 else lane < 0
            kt, vt = load_tile(max(j, 0))
        else:
            mask = jnp.logical_and(j >= 0, lane >= 0)
            kt, vt = load_tile(jnp.maximum(j, 0))
        return 'cols', kt.astype(BF16), vt.astype(BF16), mask

    tiles = list(first_tiles)
    if load_tile is not None:
        for _ in range(ATTN_STATIC_TILES):
            tiles.append(cache_tile(j0))
            j0 = j0 - 1
    zeros = jnp.zeros((tq, LANES), F32)
    carries, accs = [zeros] * len(qs), [zeros] * len(qs)
    for g in range(0, len(tiles), ATTN_GROUP_TILES):
        carries, accs = _stick_breaking_tiles(qs, u2, tiles[g:g + ATTN_GROUP_TILES], carries, accs)
    if load_tile is not None:
        def cond(s):
            return jnp.logical_and(s[0] >= 0, s[1] > 0)

        def body(s):
            j, _, carries, accs = s
            carries, accs = _stick_breaking_tiles(qs, u2, [cache_tile(j)], carries, accs)
            return j - 1, _any_alive(carries), tuple(carries), tuple(accs)

        state = (jnp.asarray(j0, jnp.int32), _any_alive(carries), tuple(carries), tuple(accs))
        _, _, carries, accs = lax.while_loop(cond, body, state)
    return accs


def _pair_outputs(accs, lo_half, dtype):
    return [jnp.where(lo_half, accs[2 * p], accs[2 * p + 1]).astype(dtype) for p in range(len(accs) // 2)]


def _attn_prompt_kernel(q16_ref, qt_ref, qm_ref, u2_ref, kt_ref, vt_ref, o_ref):
    step = pl.program_id(2)
    tq = KEY_TILE
    lane = lax.broadcasted_iota(jnp.int32, (tq, LANES), 1)
    row = lax.broadcasted_iota(jnp.int32, (tq, LANES), 0)
    lo_half = lane < SB_HEAD_DIM

    def load_tile(j):
        off = pl.multiple_of(j * KEY_TILE, KEY_TILE)
        return kt_ref[0, :, pl.ds(off, KEY_TILE)], vt_ref[0, :, pl.ds(off, KEY_TILE)]

    for sub in range(ATTN_STEP_TILES):
        m = step * ATTN_STEP_TILES + sub
        if sub == 0:
            first = jnp.where(step == 0, qm_ref[0], q16_ref[0])
            q = jnp.concatenate([first, qt_ref[0, :tq - N_META, :]], axis=0)
        else:
            q = qt_ref[0, sub * tq - N_META:(sub + 1) * tq - N_META, :]
        qs = _head_queries(q, lo_half)
        kd, vd = load_tile(m)
        diag = ('cols', kd.astype(BF16), vd.astype(BF16), lane < row)
        accs = _attend(qs, u2_ref[...], [diag], load_tile, m - 1, lane)
        out = jnp.concatenate(_pair_outputs(accs, lo_half, o_ref.dtype), axis=1)
        if sub == 0:
            @pl.when(step == 0)
            def _(out=out):
                o_ref[0, 0:tq - N_META, :] = out[N_META:, :]
                o_ref[0, o_ref.shape[1] - N_META:, :] = jnp.zeros((N_META, o_ref.shape[2]), o_ref.dtype)

            @pl.when(step > 0)
            def _(out=out, m=m):
                o_ref[0, pl.ds(pl.multiple_of(m * KEY_TILE - N_META, N_META), tq), :] = out
        else:
            o_ref[0, pl.ds(pl.multiple_of(m * KEY_TILE - N_META, N_META), tq), :] = out


def _attn_step_kernel(*refs, n_free, n_cache_tiles):
    if n_cache_tiles:
        (q_ref, kd_ref, vd_ref, u2_ref, kn_ref, vn_ref, kt_any, vt_any, o_ref,
         kd_scr, vd_scr, kbuf, vbuf) = refs
        width = q_ref.shape[2]
        n_near = kn_ref.shape[2] // KEY_TILE

        def load_tile(j):
            if isinstance(j, int) and j >= n_cache_tiles - n_near:
                ls = slice((j - (n_cache_tiles - n_near)) * KEY_TILE, (j - (n_cache_tiles - n_near) + 1) * KEY_TILE)
                return kn_ref[0, :, ls], vn_ref[0, :, ls]
            off = j * KEY_TILE if isinstance(j, int) else pl.multiple_of(j * KEY_TILE, KEY_TILE)
            src = (pl.program_id(0), pl.ds(pl.program_id(1) * width, width), pl.ds(off, KEY_TILE))
            pltpu.sync_copy(kt_any.at[src], kbuf)
            pltpu.sync_copy(vt_any.at[src], vbuf)
            return kbuf[...], vbuf[...]
    else:
        q_ref, kd_ref, vd_ref, u2_ref, o_ref, kd_scr, vd_scr = refs
        load_tile = None
    tq = q_ref.shape[1]
    n_rows = kd_ref.shape[1]
    lane = lax.broadcasted_iota(jnp.int32, (tq, LANES), 1)
    row = lax.broadcasted_iota(jnp.int32, (tq, LANES), 0)
    lo_half = lane < SB_HEAD_DIM
    kd_scr[...] = jnp.zeros_like(kd_scr)
    vd_scr[...] = jnp.zeros_like(vd_scr)
    kd_scr[0:n_rows, :] = kd_ref[0].astype(BF16)
    vd_scr[0:n_rows, :] = vd_ref[0].astype(BF16)
    diag = ('rows', kd_scr[...], vd_scr[...], lane < row + n_free)
    qs = _head_queries(q_ref[0], lo_half)
    accs = _attend(qs, u2_ref[...], [diag], load_tile, n_cache_tiles - 1, lane)
    for p, out in enumerate(_pair_outputs(accs, lo_half, o_ref.dtype)):
        o_ref[0, :, p * LANES:(p + 1) * LANES] = out


def _cumsum_rhs():
    s = np.arange(KEY_TILE)
    strict = (s[:, None] > s[None, :]).astype(np.float32)
    half = np.concatenate([strict, np.ones((KEY_TILE, KEY_TILE), np.float32)], axis=1)
    return jnp.asarray(np.concatenate([half, half], axis=0), BF16)


ATTN_PROMPT_WIDTH = 8 * SB_HEAD_DIM
ATTN_STEP_TILES = 2


def attention_prompt(big, big_meta, k_t, v_t):
    b, s, _ = big.shape
    width = ATTN_PROMPT_WIDTH
    qb0 = BIG_Q // width
    rows = ATTN_STEP_TILES * KEY_TILE
    per_step = rows // N_META
    n_pos = k_t.shape[2]
    cache = pl.BlockSpec((1, width, n_pos), lambda bi, hp, m: (bi, hp, 0))
    return pl.pallas_call(
        _attn_prompt_kernel,
        grid=(b, SB_WIDTH // width, s // rows),
        in_specs=[pl.BlockSpec((1, N_META, width), lambda bi, hp, m: (bi, jnp.maximum(m * per_step - 1, 0), qb0 + hp)),
                  pl.BlockSpec((1, rows, width), lambda bi, hp, m: (bi, m, qb0 + hp)),
                  pl.BlockSpec((1, N_META, width), lambda bi, hp, m: (0, 0, qb0 + hp)),
                  pl.BlockSpec((2 * KEY_TILE, 2 * KEY_TILE), lambda bi, hp, m: (0, 0)),
                  cache, cache],
        out_specs=pl.BlockSpec((1, s, width), lambda bi, hp, m: (bi, 0, hp)),
        out_shape=jax.ShapeDtypeStruct((b, s, SB_WIDTH), BF16),
        compiler_params=_cparams(("parallel", "parallel", "arbitrary")),
        name="stick_breaking_prompt",
    )(big, big, big_meta, _cumsum_rhs(), k_t, v_t)


def attention_step(big, q_row_blk, k_rows, v_rows, k_t, v_t, n_cache_tiles, width):
    b = big.shape[0]
    n_rows = k_rows.shape[1]
    qb0 = BIG_Q // width
    rows = pl.BlockSpec((1, n_rows, width), lambda bi, hp: (bi, 0, hp))
    in_specs = [pl.BlockSpec((1, N_META, width), lambda bi, hp: (bi, q_row_blk, qb0 + hp)), rows, rows,
                pl.BlockSpec((2 * KEY_TILE, 2 * KEY_TILE), lambda bi, hp: (0, 0))]
    args = [big, k_rows, v_rows, _cumsum_rhs()]
    scratch = [pltpu.VMEM((KEY_TILE, width), BF16)] * 2
    if n_cache_tiles:
        near = ATTN_STATIC_TILES * KEY_TILE
        assert n_cache_tiles % ATTN_STATIC_TILES == 0
        near_blk = n_cache_tiles // ATTN_STATIC_TILES - 1
        in_specs += [pl.BlockSpec((1, width, near), lambda bi, hp: (bi, hp, near_blk))] * 2
        in_specs += [pl.BlockSpec(memory_space=pl.ANY)] * 2
        args += [k_t, v_t, k_t, v_t]
        scratch += [pltpu.VMEM((width, KEY_TILE), F32)] * 2
    return pl.pallas_call(
        functools.partial(_attn_step_kernel, n_free=n_rows - N_META, n_cache_tiles=n_cache_tiles),
        grid=(b, SB_WIDTH // width),
        in_specs=in_specs,
        out_specs=pl.BlockSpec((1, N_META, width), lambda bi, hp: (bi, 0, hp)),
        out_shape=jax.ShapeDtypeStruct((b, N_META, SB_WIDTH), BF16),
        scratch_shapes=scratch,
        compiler_params=_cparams(("parallel", "parallel")),
        name="stick_breaking_step",
    )(*args)


def _rows8(op, x, r8):
    return op(x.reshape(x.shape[0] // 8, 8, x.shape[1]), r8[None]).reshape(x.shape)


def _ssd_kernel(xbc_ref, x16_ref, z_ref, dt_ref, dtT_ref, prev_ref, h0_ref, cw_ref, cb_ref, dtb_ref, dtbT_ref,
                alog_ref, alogT_ref, dskip_ref, gn_ref, e3_ref, ltri_ref, ublk_ref, shift_ref,
                y_ref, hfin_ref, st_ref, *, rows, n_sub, n_chunks):
    Q = SSD_CHUNK
    c = pl.program_id(1)
    n_blk = SSD_WIDTH // LANES
    mul, sub = jnp.multiply, jnp.subtract

    @pl.when(c == 0)
    def _():
        for j in range(n_blk):
            st_ref[:, j * LANES:(j + 1) * LANES] = h0_ref[0, j * LANES:(j + 1) * LANES, :].T

    def pad_rows(v):
        if rows == Q:
            return v
        return jnp.concatenate([v, jnp.zeros((Q - rows, v.shape[1]), v.dtype)], axis=0)

    lane = lax.broadcasted_iota(jnp.int32, (Q, LANES), 1)
    rowq = lax.broadcasted_iota(jnp.int32, (Q, LANES), 0)
    causal2 = (lane % Q) <= rowq
    lo_half = lane < SSD_HEAD_DIM
    decay_rate = -LOG2E * jnp.exp(alog_ref[...])
    decay_rate_t = -LOG2E * jnp.exp(alogT_ref[...])

    for k in range(n_sub):
        r0 = k * Q
        hi, lo = _split2(prev_ref[0])
        if k == 0:
            hi = jnp.where(c == 0, hi, x16_ref[0])
            lo = jnp.where(c == 0, lo, jnp.zeros_like(lo))
        else:
            hi, lo = xbc_ref[0, r0 - 16:r0, :], jnp.zeros_like(lo)
        window = jnp.concatenate([pad_rows(xbc_ref[0, r0:r0 + rows, :]), hi, lo,
                                  jnp.zeros((Q - 32, SSD_CONV_CH), BF16)], axis=0)
        shifted = _dot(shift_ref[...], window)
        conv = _rows8(mul, shifted[0:Q], cw_ref[0:8, :])
        for i in range(1, SSD_CONV):
            conv = conv + _rows8(mul, shifted[i * Q:(i + 1) * Q], cw_ref[8 * i:8 * i + 8, :])
        conv = _rows8(jnp.add, conv, cb_ref[...])
        xc = conv * _sigmoid(conv)
        xs = xc[:, :SSD_WIDTH]
        b_all = xc[:, SSD_WIDTH:SSD_WIDTH + SSD_GROUPS * SSD_STATE]
        c_all = xc[:, SSD_WIDTH + SSD_GROUPS * SSD_STATE:]

        dt = _softplus(pad_rows(dt_ref[0, r0:r0 + rows, :]) + dtb_ref[...])
        dt_t = _softplus(dtT_ref[0, k] + dtbT_ref[...])
        if rows < Q:
            rowi = lax.broadcasted_iota(jnp.int32, (Q, 1), 0)
            dt = jnp.where(rowi < rows, dt, 0.0)
            xs = jnp.where(rowi < rows, xs, 0.0)
            lane_t = lax.broadcasted_iota(jnp.int32, dt_t.shape, 1)
            dt_t = jnp.where(lane_t % Q < rows, dt_t, 0.0)
        da = dt * decay_rate
        da_t = dt_t * decay_rate_t

        a_cum = _dot(ltri_ref[...], jnp.concatenate(_split3(da), axis=0))
        a_cum_t = _dot(jnp.concatenate(_split3(da_t), axis=1), ublk_ref[...])
        dt_exp = _dot(jnp.concatenate(_split2(dt), axis=1), e3_ref[0:2 * LANES, :])
        a_exp = _dot(jnp.concatenate(_split3(a_cum), axis=1), e3_ref[...])
        a_last = jnp.broadcast_to(a_exp[Q - 1:Q, :], (8, SSD_WIDTH))

        xdt = xs * dt_exp
        xdt_b = xdt.astype(BF16)
        xw = (xdt * jnp.exp2(-_rows8(sub, a_exp, a_last))).astype(BF16)
        xw_pad = jnp.concatenate([xw, jnp.zeros_like(xw)], axis=0)
        chunk_decay = jnp.exp2(a_last)
        grow = jnp.exp2(a_exp)

        y_parts = []
        for g in range(SSD_GROUPS):
            bg = b_all[:, g * SSD_STATE:(g + 1) * SSD_STATE]
            cg = c_all[:, g * SSD_STATE:(g + 1) * SSD_STATE].astype(BF16)
            bg_b = bg.astype(BF16)
            cb2 = _dot_nt(cg, jnp.concatenate([bg_b, bg_b], axis=0))
            gs = slice(g * GROUP_W, (g + 1) * GROUP_W)
            st_g = st_ref[:, gs]
            y_off = _dot(cg, st_g.astype(BF16)) * grow[:, gs]
            bg_t = jnp.concatenate([bg, jnp.zeros_like(bg)], axis=0).T.astype(BF16)
            st_ref[:, gs] = _rows8(mul, st_g, chunk_decay[:, gs]) + _dot(bg_t, xw_pad[:, gs])
            pair_out = []
            for kk in range(GROUP_W // LANES):
                i = g * (GROUP_W // LANES) + kk
                ps = slice(i * LANES, (i + 1) * LANES)
                a_row = jnp.broadcast_to(a_cum_t[i:i + 1, :], (8, LANES))
                decay = jnp.exp2(jnp.minimum(_rows8(sub, a_exp[:, ps], a_row), 0.0))
                m2 = jnp.where(causal2, cb2 * decay, 0.0).astype(BF16)
                xp = xdt_b[:, ps]
                zero = jnp.zeros_like(xp)
                xbd = jnp.concatenate([jnp.where(lo_half, xp, zero), jnp.where(lo_half, zero, xp)], axis=0)
                pair_out.append(_dot(m2, xbd))
            y_diag = jnp.concatenate(pair_out, axis=1)
            y = y_diag + y_off + _rows8(mul, xs[:, gs], dskip_ref[:, gs])
            zg = pad_rows(z_ref[0, r0:r0 + rows, gs].astype(F32))
            y = y * (zg * _sigmoid(zg))
            ms = jnp.mean(y * y, axis=-1, keepdims=True)
            y_parts.append(_rows8(mul, y * lax.rsqrt(ms + EPS), gn_ref[:, gs]))
        y_all = jnp.concatenate(y_parts, axis=1)
        y_ref[0, r0:r0 + rows, :] = y_all[:rows].astype(y_ref.dtype)

    @pl.when(c == n_chunks - 1)
    def _():
        for j in range(n_blk):
            hfin_ref[0, j * LANES:(j + 1) * LANES, :] = st_ref[:, j * LANES:(j + 1) * LANES].T


def _ssd_constants():
    Q = SSD_CHUNK
    e = np.zeros((LANES, SSD_WIDTH), np.float32)
    for h in range(SSD_HEADS):
        e[h, h * SSD_HEAD_DIM:(h + 1) * SSD_HEAD_DIM] = 1.0
    e3 = np.concatenate([e, e, e], axis=0)
    t = np.arange(Q)
    ltri = (t[:, None] >= t[None, :]).astype(np.float32)
    ltri3 = np.concatenate([ltri, ltri, ltri], axis=1)
    ublk = np.zeros((LANES, LANES), np.float32)
    ublk[:Q, :Q] = ltri.T
    ublk[Q:, Q:] = ltri.T
    ublk3 = np.concatenate([ublk, ublk, ublk], axis=0)
    shift = np.zeros((SSD_CONV * Q, 2 * Q), np.float32)
    for i in range(SSD_CONV):
        for r in range(Q):
            src = r - (SSD_CONV - 1 - i)
            if src >= 0:
                shift[i * Q + r, src] = 1.0
            else:
                shift[i * Q + r, Q + 16 + src] = 1.0
                shift[i * Q + r, Q + 32 + src] = 1.0
    return jnp.asarray(e3, BF16), jnp.asarray(ltri3, BF16), jnp.asarray(ublk3, BF16), jnp.asarray(shift, BF16)


def ssd_mixer(big, xbc_blk, z_blk, dt_raw, conv_prev, h0, p):
    b, s, _ = big.shape
    Q = SSD_CHUNK
    rows = min(s, Q)
    n_sub = SSD_STEP_CHUNKS if s % (SSD_STEP_CHUNKS * Q) == 0 else 1
    blk = rows * n_sub
    n_chunks = -(-s // blk)
    dtp = dt_raw[:, :, :SSD_HEADS]
    if s < Q:
        dtp = jnp.pad(dtp, ((0, 0), (0, Q - s), (0, 0)))
    dt_t = dtp.reshape(b, n_chunks * n_sub, Q, SSD_HEADS).transpose(0, 1, 3, 2).reshape(
        b, n_chunks * n_sub, SSD_HEADS // 2, 2 * Q)
    e3, ltri3, ublk3, shift = _ssd_constants()
    prev16 = jnp.pad(conv_prev, ((0, 0), (16 - (SSD_CONV - 1), 0), (0, 0)))
    rep8 = lambda v: jnp.repeat(v, 8, axis=0)
    const = lambda shape: pl.BlockSpec(shape, lambda bi, ci: (0,) * len(shape))
    in_specs = [
        pl.BlockSpec((1, blk, SSD_CONV_CH), lambda bi, ci: (bi, ci, xbc_blk)),
        pl.BlockSpec((1, 16, SSD_CONV_CH), lambda bi, ci: (bi, jnp.maximum(ci * (blk // 16) - 1, 0), xbc_blk)),
        pl.BlockSpec((1, blk, SSD_WIDTH), lambda bi, ci: (bi, ci, z_blk)),
        pl.BlockSpec((1, blk, LANES), lambda bi, ci: (bi, ci, 0)),
        pl.BlockSpec((1, n_sub, SSD_HEADS // 2, 2 * Q), lambda bi, ci: (bi, ci, 0, 0)),
        pl.BlockSpec((1, 16, SSD_CONV_CH), lambda bi, ci: (bi, 0, 0)),
        pl.BlockSpec((1, SSD_WIDTH, SSD_STATE), lambda bi, ci: (bi, 0, 0)),
        const((8 * SSD_CONV, SSD_CONV_CH)), const((8, SSD_CONV_CH)),
        const((1, LANES)), const((SSD_HEADS // 2, 2 * Q)),
        const((1, LANES)), const((SSD_HEADS // 2, 2 * Q)),
        const((8, SSD_WIDTH)), const((8, SSD_WIDTH)),
        const(e3.shape), const(ltri3.shape), const(ublk3.shape), const(shift.shape),
    ]
    y, h_fin = pl.pallas_call(
        functools.partial(_ssd_kernel, rows=rows, n_sub=n_sub, n_chunks=n_chunks),
        grid=(b, n_chunks),
        in_specs=in_specs,
        out_specs=[pl.BlockSpec((1, blk, SSD_WIDTH), lambda bi, ci: (bi, ci, 0)),
                   pl.BlockSpec((1, SSD_WIDTH, SSD_STATE), lambda bi, ci: (bi, 0, 0))],
        out_shape=[jax.ShapeDtypeStruct((b, s, SSD_WIDTH), BF16),
                   jax.ShapeDtypeStruct((b, SSD_WIDTH, SSD_STATE), F32)],
        scratch_shapes=[pltpu.VMEM((SSD_STATE, SSD_WIDTH), F32)],
        compiler_params=_cparams(("parallel", "arbitrary")),
        name="ssd_mixer",
    )(big, big, big, dt_raw, dt_t, prev16, h0, rep8(p['conv_w_ssd']), rep8(p['conv_b_ssd']), p['dt_bias'], p['dt_bias_t'],
      p['a_log'], p['a_log_t'], rep8(p['d_skip']), rep8(p['g_ssd_norm']), e3, ltri3, ublk3, shift)
    return y, h_fin


def _merge_kernel(att_ref, ssd_ref, ga_ref, gs_ref, h_ref, wa_ref, ws_ref, wo_ref, o_ref, wa_b, ws_b, wo_b):
    @pl.when(pl.program_id(0) == 0)
    def _():
        wa_b[...] = wa_ref[...].astype(BF16)
        ws_b[...] = ws_ref[...].astype(BF16)
        wo_b[...] = wo_ref[...].astype(BF16)

    a = _dot(att_ref[...], wa_b[...])
    s = _dot(ssd_ref[...], ws_b[...])
    merged = _sigmoid(ga_ref[...].astype(F32)) * a + _sigmoid(gs_ref[...].astype(F32)) * s
    o_ref[...] = h_ref[...] + _dot(merged.astype(BF16), wo_b[...])


def _resident(a):
    return pl.BlockSpec(a.shape, lambda i: (0,) * a.ndim, pipeline_mode=pl.Buffered(1))


def merge(att, ssd, big, h, p):
    m = h.shape[0]
    tm = _pick(m, (512, 256, 128))
    row = lambda w: pl.BlockSpec((tm, w), lambda i: (i, 0))
    gate_blk = BIG_GATE // D_MODEL
    ws = [p['w_br_att'], p['w_br_ssd'], p['w_out']]
    return pl.pallas_call(
        _merge_kernel,
        grid=(m // tm,),
        in_specs=[row(SB_WIDTH), row(SSD_WIDTH),
                  pl.BlockSpec((tm, D_MODEL), lambda i: (i, gate_blk)),
                  pl.BlockSpec((tm, D_MODEL), lambda i: (i, gate_blk + 1)),
                  row(D_MODEL)] + [_resident(w) for w in ws],
        out_specs=row(D_MODEL),
        out_shape=jax.ShapeDtypeStruct((m, D_MODEL), F32),
        scratch_shapes=[pltpu.VMEM(w.shape, BF16) for w in ws],
        compiler_params=_cparams(("arbitrary",)),
        name="merge_out_proj",
    )(att, ssd, big, big, h, *ws)


def _ffn_act_kernel(ug_ref, uv_ref, pg_ref, pv_ref, wg_ref, wv_ref, bg_ref, bv_ref, o_ref, gbuf_ref, vbuf_ref,
                    *, ts):
    t = pl.program_id(1)
    k1 = FFN_CONV - 1

    def conv(u_ref, p_ref, w_ref, b_ref, buf_ref):
        @pl.when(t == 0)
        def _():
            buf_ref[8 - k1:8, :] = p_ref[0]

        buf_ref[8:8 + ts, :] = u_ref[0].astype(F32)
        y = b_ref[...]
        for i in range(FFN_CONV):
            y = y + buf_ref[8 - k1 + i:8 - k1 + i + ts, :] * w_ref[i:i + 1, :]
        buf_ref[8 - k1:8, :] = buf_ref[8 + ts - k1:8 + ts, :]
        return y

    gate = conv(ug_ref, pg_ref, wg_ref, bg_ref, gbuf_ref)
    val = conv(uv_ref, pv_ref, wv_ref, bv_ref, vbuf_ref)
    o_ref[0] = (gate * _sigmoid(gate) * val).astype(o_ref.dtype)


FFN_SUB = 64


UPF_TM = 512
UPF_TN = D_FF // 2


def _up_ffn_kernel(h_ref, h16_ref, g_ref, wg_ref, wv_ref, pg_ref, pv_ref, cwg_ref, cwv_ref, cbg_ref, cbv_ref,
                   o_ref, gl_ref, vl_ref, wg_b, wv_b, gbuf_ref, vbuf_ref):
    i = pl.program_id(2)
    tm = h_ref.shape[1]

    @pl.when(jnp.logical_and(pl.program_id(1) == 0, i == 0))
    def _():
        wg_b[...] = wg_ref[...].astype(BF16)
        wv_b[...] = wv_ref[...].astype(BF16)

    u = _rms_norm_bf16(h_ref[0], g_ref[...])
    u16 = _rms_norm_bf16(h16_ref[0], g_ref[...])

    def conv(w_b, p_ref, last_ref, buf_ref, cw_ref, cb_ref):
        up = _dot(u, w_b[...])
        buf_ref[0:16, :] = jnp.where(i == 0, p_ref[0], _dot(u16, w_b[...]))
        buf_ref[16:16 + tm, :] = up
        last_ref[0] = up[tm - 16:tm, :]
        y = _rows8(jnp.multiply, up, cw_ref[8 * (FFN_CONV - 1):8 * FFN_CONV, :])
        for t in range(FFN_CONV - 1):
            r0 = 16 - (FFN_CONV - 1) + t
            y = y + _rows8(jnp.multiply, buf_ref[r0:r0 + tm, :], cw_ref[8 * t:8 * t + 8, :])
        return _rows8(jnp.add, y, cb_ref[...])

    gate = conv(wg_b, pg_ref, gl_ref, gbuf_ref, cwg_ref, cbg_ref)
    val = conv(wv_b, pv_ref, vl_ref, vbuf_ref, cwv_ref, cbv_ref)
    o_ref[0] = (gate * _sigmoid(gate) * val).astype(o_ref.dtype)


def up_ffn_long(h, prev, p):
    b, s, d = h.shape
    tm, tn = UPF_TM, UPF_TN
    nj = D_FF // tn
    prev16 = jnp.pad(prev, ((0, 0), (16 - (FFN_CONV - 1), 0), (0, 0)))
    rep8 = lambda v: jnp.repeat(v, 8, axis=0)
    w8, b8 = rep8(p['conv_w_ffn']), rep8(p['conv_b_ffn'])
    cols = lambda rows, half: pl.BlockSpec((rows, tn), lambda j, bi, i: (0, half * nj + j))
    w_spec = lambda half: pl.BlockSpec((d, tn), lambda j, bi, i: (0, half * nj + j), pipeline_mode=pl.Buffered(1))
    prevs = lambda half: pl.BlockSpec((1, 16, tn), lambda j, bi, i: (bi, 0, half * nj + j))
    last_spec = pl.BlockSpec((1, 16, tn), lambda j, bi, i: (bi, 0, j))
    return pl.pallas_call(
        _up_ffn_kernel,
        grid=(nj, b, s // tm),
        in_specs=[pl.BlockSpec((1, tm, d), lambda j, bi, i: (bi, i, 0)),
                  pl.BlockSpec((1, 16, d), lambda j, bi, i: (bi, jnp.maximum(i * (tm // 16) - 1, 0), 0)),
                  pl.BlockSpec((1, d), lambda j, bi, i: (0, 0)),
                  w_spec(0), w_spec(1), prevs(0), prevs(1),
                  cols(8 * FFN_CONV, 0), cols(8 * FFN_CONV, 1), cols(8, 0), cols(8, 1)],
        out_specs=[pl.BlockSpec((1, tm, tn), lambda j, bi, i: (bi, i, j)), last_spec, last_spec],
        out_shape=[jax.ShapeDtypeStruct((b, s, D_FF), BF16)] + [jax.ShapeDtypeStruct((b, 16, D_FF), F32)] * 2,
        scratch_shapes=[pltpu.VMEM((d, tn), BF16)] * 2 + [pltpu.VMEM((16 + tm, tn), F32)] * 2,
        compiler_params=_cparams(("arbitrary", "arbitrary", "arbitrary")),
        name="up_proj_conv_ffn",
    )(h, h, p['g_ffn'], p['w_up'], p['w_up'], prev16, prev16, w8, w8, b8, b8)


def ffn_act(up, prev, p):
    b, s, _ = up.shape
    ts = _pick(s, (512, 256, 128))
    half = lambda blk: pl.BlockSpec((1, ts, D_FF), lambda bi, ti: (bi, ti, blk))
    prevs = lambda blk: pl.BlockSpec((1, FFN_CONV - 1, D_FF), lambda bi, ti: (bi, 0, blk))
    wspec = lambda blk: pl.BlockSpec((FFN_CONV, D_FF), lambda bi, ti: (0, blk))
    bspec = lambda blk: pl.BlockSpec((1, D_FF), lambda bi, ti: (0, blk))
    return pl.pallas_call(
        functools.partial(_ffn_act_kernel, ts=ts),
        grid=(b, s // ts),
        in_specs=[half(0), half(1), prevs(0), prevs(1), wspec(0), wspec(1), bspec(0), bspec(1)],
        out_specs=pl.BlockSpec((1, ts, D_FF), lambda bi, ti: (bi, ti, 0)),
        out_shape=jax.ShapeDtypeStruct((b, s, D_FF), BF16),
        scratch_shapes=[pltpu.VMEM((8 + ts, D_FF), F32)] * 2,
        compiler_params=_cparams(("parallel", "arbitrary")),
        name="ffn_conv_act",
    )(up, up, prev, prev, p['conv_w_ffn'], p['conv_w_ffn'], p['conv_b_ffn'], p['conv_b_ffn'])


def _down_kernel(a_ref, w_ref, h_ref, g_ref, o_ref, w_b):
    @pl.when(pl.program_id(0) == 0)
    def _():
        w_b[...] = w_ref[...].astype(BF16)

    h = h_ref[...] + _dot(a_ref[...], w_b[...])
    ms = jnp.mean(h * h, axis=-1, keepdims=True)
    o_ref[...] = h * lax.rsqrt(ms + EPS) * g_ref[...]


def down_norm(act, h, p):
    m = h.shape[0]
    tm = _pick(m, (512, 256, 128))
    return pl.pallas_call(
        _down_kernel,
        grid=(m // tm,),
        in_specs=[pl.BlockSpec((tm, D_FF), lambda i: (i, 0)),
                  _resident(p['w_down']),
                  pl.BlockSpec((tm, D_MODEL), lambda i: (i, 0)),
                  pl.BlockSpec((1, D_MODEL), lambda i: (0, 0))],
        out_specs=pl.BlockSpec((tm, D_MODEL), lambda i: (i, 0)),
        out_shape=jax.ShapeDtypeStruct((m, D_MODEL), F32),
        scratch_shapes=[pltpu.VMEM(p['w_down'].shape, BF16)],
        compiler_params=_cparams(("arbitrary",)),
        name="down_proj_norm",
    )(act, p['w_down'], h, p['g_final'])


def mix_and_ffn(big3, att, dt_raw, hf, ssd_h0, ssd_conv_prev, ffn_conv_prev, p):
    b, s, _ = big3.shape
    m = b * s
    ssd, ssd_state = ssd_mixer(big3, BIG_XBC // SSD_CONV_CH, BIG_Z // SSD_WIDTH, dt_raw, ssd_conv_prev, ssd_h0, p)
    h1 = merge(att.reshape(m, SB_WIDTH), ssd.reshape(m, SSD_WIDTH), big3.reshape(m, BIG_W), hf, p)
    if s % UPF_TM == 0:
        act, up_g, up_v = up_ffn_long(h1.reshape(b, s, D_MODEL), ffn_conv_prev, p)
        up_last = jnp.concatenate([up_g, up_v], axis=2)
    else:
        up_last = norm_mm(h1, p['g_ffn'], p['w_up'], BF16).reshape(b, s, 2 * D_FF)
        act = ffn_act(up_last, ffn_conv_prev, p)
    y = down_norm(act.reshape(m, D_FF), h1, p).reshape(b, s, D_MODEL)
    xbc_rows = jnp.concatenate([ssd_conv_prev, big3[:, :, BIG_XBC:BIG_XBC + SSD_CONV_CH][:, -(SSD_CONV - 1):].astype(F32)],
                               axis=1)[:, -(SSD_CONV - 1):]
    up_rows = jnp.concatenate([ffn_conv_prev, up_last[:, -(FFN_CONV - 1):].astype(F32)], axis=1)[:, -(FFN_CONV - 1):]
    return y, ssd_state, xbc_rows, up_rows


def _prep_params(g_mix, w_in, conv_w_ssd, conv_b_ssd, dt_bias, a_log, d_skip, g_ssd_norm, w_br_att, w_br_ssd,
                 w_out, g_ffn, w_up, conv_w_ffn, conv_b_ffn, w_down, g_final):
    w_t = w_in.T
    dt0 = 3 * SB_WIDTH + SSD_WIDTH + SSD_CONV_CH
    w_tail_t = jnp.concatenate([w_t[dt0 + SSD_HEADS:], w_t[dt0:dt0 + SSD_HEADS],
                                jnp.zeros((LANES - SSD_HEADS, D_MODEL), F32)], axis=0)
    Q = SSD_CHUNK
    lanes_t = lambda v: jnp.repeat(v.reshape(SSD_HEADS // 2, 2, 1), Q, axis=2).reshape(SSD_HEADS // 2, 2 * Q)
    pad_l = lambda v: jnp.pad(v, (0, LANES - SSD_HEADS)).reshape(1, LANES)
    return {
        'g_mix': g_mix.reshape(1, -1), 'g_ffn': g_ffn.reshape(1, -1), 'g_final': g_final.reshape(1, -1),
        'w_t': w_t, 'w_tail_t': w_tail_t,
        'conv_w_ssd': conv_w_ssd, 'conv_b_ssd': conv_b_ssd.reshape(1, -1),
        'dt_bias': pad_l(dt_bias), 'dt_bias_t': lanes_t(dt_bias),
        'a_log': pad_l(a_log), 'a_log_t': lanes_t(a_log),
        'd_skip': jnp.repeat(d_skip, SSD_HEAD_DIM).reshape(1, -1), 'g_ssd_norm': g_ssd_norm.reshape(1, -1),
        'w_br_att': w_br_att, 'w_br_ssd': w_br_ssd, 'w_out': w_out,
        'w_up': w_up, 'conv_w_ffn': conv_w_ffn, 'conv_b_ffn': conv_b_ffn.reshape(1, -1),
        'w_down': w_down,
    }


def kernel(x_prompt, x_sample, cache_k, cache_v, state_ssm, state_ssm_conv, state_ffn_conv, meta_tokens, g_mix, w_in, conv_w_ssd, conv_b_ssd, dt_bias, a_log, d_skip, g_ssd_norm, w_br_att, w_br_ssd, w_out, g_ffn, w_up, conv_w_ffn, conv_b_ffn, w_down, g_final):
    bp, seq, _ = x_prompt.shape
    bs, dec = x_sample.shape[:2]
    assert dec == N_META and seq % KVT_TS == 0
    p = _prep_params(g_mix[0], w_in[0], conv_w_ssd[0], conv_b_ssd[0], dt_bias[0], a_log[0], d_skip[0],
                     g_ssd_norm[0], w_br_att[0], w_br_ssd[0], w_out[0], g_ffn[0], w_up[0], conv_w_ffn[0],
                     conv_b_ffn[0], w_down[0], g_final)
    proj = functools.partial(in_proj, g=p['g_mix'], w_t=p['w_t'], w_tail_t=p['w_tail_t'])
    state_shape = (SSD_WIDTH, SSD_STATE)
    rows = lambda a, b: a.reshape(b, -1, a.shape[-1])

    k_m, v_m, big_m, dt_m = proj(meta_tokens, with_kv=True)
    big_m, k_m, v_m = big_m[None], k_m[None], v_m[None]
    att_m = attention_step(big_m, 0, k_m, v_m, None, None, 0, SB_WIDTH)
    _, ssm_m, conv_m, ffn_m = mix_and_ffn(
        big_m, att_m, dt_m[None], meta_tokens, jnp.zeros((1,) + state_shape, F32),
        jnp.zeros((1, SSD_CONV - 1, SSD_CONV_CH), F32), jnp.zeros((1, FFN_CONV - 1, 2 * D_FF), F32), p)

    xf = x_prompt.reshape(bp * seq, D_MODEL)
    big_x, dt_x = proj(xf, with_kv=False)
    big_x = rows(big_x, bp)
    k_t, v_t, k_end, v_end = kv_transposed(x_prompt, meta_tokens, p['g_mix'], p['w_t'])
    att_x = attention_prompt(big_x, big_m, k_t, v_t)
    att_end = attention_step(big_x, seq // N_META - 1, k_end, v_end, k_t, v_t, seq // KEY_TILE, ATTN_PROMPT_WIDTH)
    att_x = lax.dynamic_update_slice(att_x, att_end, (0, seq - N_META, 0))
    rep = lambda a: jnp.broadcast_to(a, (bp,) + a.shape[1:])
    y_prompt, ssm_p, conv_p, ffn_p = mix_and_ffn(
        big_x, att_x, rows(dt_x, bp), xf, rep(ssm_m), rep(conv_m), rep(ffn_m), p)

    n_rows = cache_k.shape[2]
    n_tiles = (n_rows - N_META) // KEY_TILE
    assert n_tiles * KEY_TILE + N_META == n_rows
    k_s, v_s, big_s, dt_s = proj(x_sample.reshape(bs * dec, D_MODEL), with_kv=True)
    big_s, k_s, v_s = rows(big_s, bs), rows(k_s, bs), rows(v_s, bs)
    cache_t = lambda c: c[0].transpose(0, 2, 3, 1).reshape(bs, SB_WIDTH, n_rows)
    nearest = lambda c, new: jnp.concatenate([c[0][:, n_tiles * KEY_TILE:].reshape(bs, N_META, SB_WIDTH), new], axis=1)
    att_s = attention_step(big_s, 0, nearest(cache_k, k_s), nearest(cache_v, v_s), cache_t(cache_k), cache_t(cache_v),
                           n_tiles, SB_WIDTH)
    y_sample, ssm_s, conv_s, ffn_s = mix_and_ffn(
        big_s, att_s, rows(dt_s, bs), x_sample.reshape(bs * dec, D_MODEL),
        state_ssm[0].reshape((bs,) + state_shape), state_ssm_conv[0], state_ffn_conv[0], p)

    heads = lambda a: a.reshape(a.shape[0], a.shape[1], SB_HEADS, SB_HEAD_DIM)[None]
    heads_t = lambda a: a.reshape(a.shape[0], SB_HEADS, SB_HEAD_DIM, a.shape[2]).transpose(0, 3, 1, 2)[None]
    state5 = lambda a: a.reshape(a.shape[0], SSD_HEADS, SSD_HEAD_DIM, SSD_STATE)[None]
    return (y_prompt, y_sample, heads_t(k_t), heads_t(v_t), state5(ssm_p), conv_p[None], ffn_p[None],
            heads(k_s), heads(v_s), state5(ssm_s), conv_s[None], ffn_s[None])
```

```python
import functools

import numpy as np
import jax
import jax.numpy as jnp
from jax import lax
from jax.experimental import pallas as pl
from jax.experimental.pallas import tpu as pltpu

F32 = jnp.float32
BF16 = jnp.bfloat16

D_MODEL = 1024
N_META = 16
SB_HEADS = 16
SB_HEAD_DIM = 64
SB_WIDTH = SB_HEADS * SB_HEAD_DIM
SSD_WIDTH = 2 * D_MODEL
SSD_HEAD_DIM = 64
SSD_HEADS = SSD_WIDTH // SSD_HEAD_DIM
SSD_GROUPS = 4
SSD_STATE = 128
SSD_CONV = 4
SSD_CONV_CH = SSD_WIDTH + 2 * SSD_GROUPS * SSD_STATE
D_FF = 2816
FFN_CONV = 3
EPS = 1e-6

LANES = 128
SSD_CHUNK = 64
SSD_STEP_CHUNKS = 4
KEY_TILE = 128
GROUP_W = SSD_WIDTH // SSD_GROUPS
DEAD_LOG = -104.0
LOG2E = 1.4426950408889634
LN2 = 0.6931471805599453
ATTN_STATIC_TILES = 2
ATTN_GROUP_TILES = 3
VMEM_LIMIT = 56 * 1024 * 1024


def _cparams(sem):
    return pltpu.CompilerParams(dimension_semantics=sem, vmem_limit_bytes=VMEM_LIMIT)


def _pick(n, cands):
    for c in cands:
        if n % c == 0:
            return c
    return n


def _split2(x):
    hi = x.astype(BF16)
    lo = (x - hi.astype(F32)).astype(BF16)
    return hi, lo


def _split3(x):
    hi = x.astype(BF16)
    r = x - hi.astype(F32)
    mid = r.astype(BF16)
    lo = (r - mid.astype(F32)).astype(BF16)
    return hi, mid, lo


def _dot(a, b):
    return jnp.dot(a, b, preferred_element_type=F32)


def _dot_nt(a, b):
    return lax.dot_general(a, b, (((1,), (1,)), ((), ())), preferred_element_type=F32)


def _softplus(x):
    return jnp.maximum(x, 0.0) + jnp.log(1.0 + jnp.exp(-jnp.abs(x)))


def _sigmoid(x):
    return 1.0 / (1.0 + jnp.exp2(-LOG2E * x))


def _rms_norm_bf16(x, g):
    ms = jnp.mean(x * x, axis=-1, keepdims=True)
    return (x * lax.rsqrt(ms + EPS) * g).astype(BF16)


def _norm_mm_kernel(x_ref, g_ref, w_ref, o_ref, u_ref):
    @pl.when(pl.program_id(1) == 0)
    def _():
        u_ref[...] = _rms_norm_bf16(x_ref[...], g_ref[...])

    o_ref[...] = _dot(u_ref[...], w_ref[...].astype(BF16)).astype(o_ref.dtype)


PROJ_TN = 1024
UP_TN = 1408


def norm_mm(x, g, w, out_dtype):
    m, d = x.shape
    n = w.shape[1]
    tm = _pick(m, (2048, 1024, 512, 256, 128))
    tn = UP_TN
    return pl.pallas_call(
        _norm_mm_kernel,
        grid=(m // tm, n // tn),
        in_specs=[pl.BlockSpec((tm, d), lambda i, j: (i, 0)),
                  pl.BlockSpec((1, d), lambda i, j: (0, 0)),
                  pl.BlockSpec((d, tn), lambda i, j: (0, j))],
        out_specs=pl.BlockSpec((tm, tn), lambda i, j: (i, j)),
        out_shape=jax.ShapeDtypeStruct((m, n), out_dtype),
        scratch_shapes=[pltpu.VMEM((tm, d), BF16)],
        compiler_params=_cparams(("parallel", "arbitrary")),
        name="norm_up_proj",
    )(x, g, w)


_NQ = SB_WIDTH // PROJ_TN
_J_K, _J_V, _J_Z = _NQ, 2 * _NQ, 3 * _NQ
_J_X = _J_Z + SSD_WIDTH // PROJ_TN
_J_G = _J_X + SSD_CONV_CH // PROJ_TN
_J_END = _J_G + 2 * D_MODEL // PROJ_TN
BIG_Z, BIG_Q, BIG_XBC, BIG_GATE = 0, SSD_WIDTH, SSD_WIDTH + SB_WIDTH, SSD_WIDTH + SB_WIDTH + SSD_CONV_CH
BIG_W = BIG_GATE + 2 * D_MODEL


def _big_block(j):
    return jnp.where(j < _J_K, BIG_Q // PROJ_TN + j,
                     jnp.where(j < _J_Z, BIG_Q // PROJ_TN + _NQ - 1,
                               jnp.where(j < _J_X, j - _J_Z + BIG_Z // PROJ_TN,
                                         jnp.where(j < _J_G, j - _J_X + BIG_XBC // PROJ_TN,
                                                   j - _J_G + BIG_GATE // PROJ_TN))))


def _sweep_step(j, with_kv):
    return j if with_kv else jnp.where(j >= _J_K, j + (_J_Z - _J_K), j)


def _in_proj_kernel(*refs, with_kv):
    if with_kv:
        x_ref, g_ref, win_ref, wgate_ref, wdt_ref, k_ref, v_ref, big_ref, dt_ref, u_ref = refs
    else:
        x_ref, g_ref, win_ref, wgate_ref, wdt_ref, big_ref, dt_ref, u_ref = refs
    j = _sweep_step(pl.program_id(1), with_kv)

    def proj(w_ref):
        return _dot_nt(u_ref[...], w_ref[...].astype(BF16))

    @pl.when(pl.program_id(1) == 0)
    def _():
        u_ref[...] = _rms_norm_bf16(x_ref[...], g_ref[...])
        dt_ref[...] = proj(wdt_ref)

    if with_kv:
        @pl.when(jnp.logical_and(j >= _J_K, j < _J_V))
        def _():
            k_ref[...] = proj(win_ref)

        @pl.when(jnp.logical_and(j >= _J_V, j < _J_Z))
        def _():
            v_ref[...] = proj(win_ref)

    @pl.when(jnp.logical_or(j < _J_K, jnp.logical_and(j >= _J_Z, j < _J_G)))
    def _():
        big_ref[...] = proj(win_ref).astype(BF16)

    @pl.when(j >= _J_G)
    def _():
        big_ref[...] = proj(wgate_ref).astype(BF16)


def in_proj(x, g, w_t, w_tail_t, with_kv):
    m, d = x.shape
    tm = _pick(m, (2048, 1024, 512, 256, 128))
    tn = PROJ_TN
    step = functools.partial(_sweep_step, with_kv=with_kv)
    clip = lambda j, lo, n: jnp.clip(step(j) - lo, 0, n - 1)
    kv_specs = [pl.BlockSpec((tm, tn), lambda i, j: (i, clip(j, _J_K, _NQ))),
                pl.BlockSpec((tm, tn), lambda i, j: (i, clip(j, _J_V, _NQ)))]
    kv_shapes = [jax.ShapeDtypeStruct((m, SB_WIDTH), F32)] * 2
    return pl.pallas_call(
        functools.partial(_in_proj_kernel, with_kv=with_kv),
        grid=(m // tm, _J_END if with_kv else _J_END - (_J_Z - _J_K)),
        in_specs=[pl.BlockSpec((tm, d), lambda i, j: (i, 0), pipeline_mode=pl.Buffered(1)),
                  pl.BlockSpec((1, d), lambda i, j: (0, 0)),
                  pl.BlockSpec((tn, d), lambda i, j: (jnp.minimum(step(j), _J_G - 1), 0)),
                  pl.BlockSpec((tn, d), lambda i, j: (clip(j, _J_G, _J_END - _J_G), 0)),
                  pl.BlockSpec((LANES, d), lambda i, j: (2 * D_MODEL // LANES, 0))],
        out_specs=(kv_specs if with_kv else []) + [
            pl.BlockSpec((tm, tn), lambda i, j: (i, _big_block(step(j)))),
            pl.BlockSpec((tm, LANES), lambda i, j: (i, 0))],
        out_shape=(kv_shapes if with_kv else []) + [
            jax.ShapeDtypeStruct((m, BIG_W), BF16), jax.ShapeDtypeStruct((m, LANES), F32)],
        scratch_shapes=[pltpu.VMEM((tm, d), BF16)],
        compiler_params=_cparams(("parallel", "arbitrary")),
        name="in_proj",
    )(x, g, w_t, w_tail_t, w_tail_t)


KVT_TS = 1024


def _kv_t_kernel(x16_ref, xt_ref, meta_ref, g_ref, wk_ref, wv_ref, kt_ref, vt_ref, kl_ref, vl_ref, wk_b, wv_b):
    j = pl.program_id(1)

    @pl.when(jnp.logical_and(pl.program_id(0) == 0, j == 0))
    def _():
        wk_b[...] = wk_ref[...].astype(BF16)
        wv_b[...] = wv_ref[...].astype(BF16)

    first = jnp.where(j == 0, meta_ref[...], x16_ref[0])
    x = jnp.concatenate([first, xt_ref[0, :KVT_TS - N_META, :]], axis=0)
    u = _rms_norm_bf16(x, g_ref[...])
    kt_ref[0] = _dot_nt(wk_b[...], u)
    vt_ref[0] = _dot_nt(wv_b[...], u)

    @pl.when(j == pl.num_programs(1) - 1)
    def _():
        kl_ref[0] = _dot_nt(u[0:N_META, :], wk_b[...])
        vl_ref[0] = _dot_nt(u[0:N_META, :], wv_b[...])


def kv_transposed(x, meta, g, w_t):
    b, s, d = x.shape
    n_pos = N_META + s
    n_tiles = -(-n_pos // KVT_TS)
    assert n_pos - (n_tiles - 1) * KVT_TS == N_META
    per_tile = KVT_TS // N_META
    out_spec = pl.BlockSpec((1, SB_WIDTH, KVT_TS), lambda bi, j: (bi, 0, j))
    last_spec = pl.BlockSpec((1, N_META, SB_WIDTH), lambda bi, j: (bi, 0, 0))
    w_spec = lambda blk: pl.BlockSpec((SB_WIDTH, d), lambda bi, j: (blk, 0), pipeline_mode=pl.Buffered(1))
    return pl.pallas_call(
        _kv_t_kernel,
        grid=(b, n_tiles),
        in_specs=[pl.BlockSpec((1, N_META, d), lambda bi, j: (bi, jnp.maximum(j * per_tile - 1, 0), 0)),
                  pl.BlockSpec((1, KVT_TS, d), lambda bi, j: (bi, jnp.minimum(j, s // KVT_TS - 1), 0)),
                  pl.BlockSpec((N_META, d), lambda bi, j: (0, 0)),
                  pl.BlockSpec((1, d), lambda bi, j: (0, 0)),
                  w_spec(1), w_spec(2)],
        out_specs=[out_spec, out_spec, last_spec, last_spec],
        out_shape=[jax.ShapeDtypeStruct((b, SB_WIDTH, n_pos), F32)] * 2
        + [jax.ShapeDtypeStruct((b, N_META, SB_WIDTH), F32)] * 2,
        scratch_shapes=[pltpu.VMEM((SB_WIDTH, d), BF16)] * 2,
        compiler_params=_cparams(("arbitrary", "arbitrary")),
        name="kv_transposed",
    )(x, x, meta, g, w_t, w_t)


def _stick_breaking_tiles(qs, u2, tiles, carries, accs):
    n_heads = len(qs)
    units = [(t, h) for t in range(len(tiles)) for h in range(n_heads)]
    ps = lambda h: slice((h // 2) * LANES, (h // 2 + 1) * LANES)

    def scores(t, h):
        kind, keys = tiles[t][0], tiles[t][1]
        return _dot_nt(qs[h], keys[:, ps(h)]) if kind == 'rows' else _dot(qs[h], keys[ps(h), :])

    z = {u: scores(*u) for u in units}
    log_beta, cat = {}, {}
    for u in units:
        mask = tiles[u[0]][3]
        soft = jnp.log(1.0 + jnp.exp2(-LOG2E * jnp.abs(z[u])))
        log_beta[u] = jnp.minimum(z[u], 0.0) - soft
        log_keep = log_beta[u] - z[u]
        if mask is not None:
            log_keep = jnp.where(mask, log_keep, 0.0)
        cat[u] = jnp.concatenate(_split2(log_keep), axis=1)
    r = {u: _dot(cat[u], u2) for u in units}
    carries, accs = list(carries), list(accs)
    w = {}
    for t, h in units:
        mask = tiles[t][3]
        wt = jnp.exp2(LOG2E * (log_beta[(t, h)] + r[(t, h)][:, :LANES] + carries[h]))
        if mask is not None:
            wt = jnp.where(mask, wt, 0.0)
        w[(t, h)] = wt.astype(BF16)
        carries[h] = carries[h] + r[(t, h)][:, LANES:]
    for t, h in units:
        kind, vals = tiles[t][0], tiles[t][2]
        pv = _dot(w[(t, h)], vals[:, ps(h)]) if kind == 'rows' else _dot_nt(w[(t, h)], vals[ps(h), :])
        accs[h] = accs[h] + pv
    return carries, accs


def _head_queries(q, lo_half):
    qs = []
    for h in range(q.shape[1] // SB_HEAD_DIM):
        qp = q[:, (h // 2) * LANES:(h // 2 + 1) * LANES]
        keep = lo_half if h % 2 == 0 else jnp.logical_not(lo_half)
        qs.append(jnp.where(keep, qp, jnp.zeros_like(qp)) * jnp.asarray(SB_HEAD_DIM ** -0.5, BF16))
    return qs


def _any_alive(carries):
    m = carries[0]
    for c in carries[1:]:
        m = jnp.maximum(m, c)
    return (jnp.max(m) > DEAD_LOG).astype(jnp.int32)


def _attend(qs, u2, first_tiles, load_tile, j0, lane):
    tq = qs[0].shape[0]

    def cache_tile(j):
        if isinstance(j, int):
            mask = None if j >= 0 else lane < 0
            kt, vt = load_tile(max(j, 0))
        else:
            mask = jnp.logical_and(j >= 0, lane >= 0)
            kt, vt = load_tile(jnp.maximum(j, 0))
        return 'cols', kt.astype(BF16), vt.astype(BF16), mask

    tiles = list(first_tiles)
    if load_tile is not None:
        for _ in range(ATTN_STATIC_TILES):
            tiles.append(cache_tile(j0))
            j0 = j0 - 1
    zeros = jnp.zeros((tq, LANES), F32)
    carries, accs = [zeros] * len(qs), [zeros] * len(qs)
    for g in range(0, len(tiles), ATTN_GROUP_TILES):
        carries, accs = _stick_breaking_tiles(qs, u2, tiles[g:g + ATTN_GROUP_TILES], carries, accs)
    if load_tile is not None:
        def cond(s):
            return jnp.logical_and(s[0] >= 0, s[1] > 0)

        def body(s):
            j, _, carries, accs = s
            carries, accs = _stick_breaking_tiles(qs, u2, [cache_tile(j)], carries, accs)
            return j - 1, _any_alive(carries), tuple(carries), tuple(accs)

        state = (jnp.asarray(j0, jnp.int32), _any_alive(carries), tuple(carries), tuple(accs))
        _, _, carries, accs = lax.while_loop(cond, body, state)
    return accs


def _pair_outputs(accs, lo_half, dtype):
    return [jnp.where(lo_half, accs[2 * p], accs[2 * p + 1]).astype(dtype) for p in range(len(accs) // 2)]


def _attn_prompt_kernel(q16_ref, qt_ref, qm_ref, u2_ref, kt_ref, vt_ref, o_ref):
    step = pl.program_id(2)
    tq = KEY_TILE
    lane = lax.broadcasted_iota(jnp.int32, (tq, LANES), 1)
    row = lax.broadcasted_iota(jnp.int32, (tq, LANES), 0)
    lo_half = lane < SB_HEAD_DIM

    def load_tile(j):
        off = pl.multiple_of(j * KEY_TILE, KEY_TILE)
        return kt_ref[0, :, pl.ds(off, KEY_TILE)], vt_ref[0, :, pl.ds(off, KEY_TILE)]

    for sub in range(ATTN_STEP_TILES):
        m = step * ATTN_STEP_TILES + sub
        if sub == 0:
            first = jnp.where(step == 0, qm_ref[0], q16_ref[0])
            q = jnp.concatenate([first, qt_ref[0, :tq - N_META, :]], axis=0)
        else:
            q = qt_ref[0, sub * tq - N_META:(sub + 1) * tq - N_META, :]
        qs = _head_queries(q, lo_half)
        kd, vd = load_tile(m)
        diag = ('cols', kd.astype(BF16), vd.astype(BF16), lane < row)
        accs = _attend(qs, u2_ref[...], [diag], load_tile, m - 1, lane)
        out = jnp.concatenate(_pair_outputs(accs, lo_half, o_ref.dtype), axis=1)
        if sub == 0:
            @pl.when(step == 0)
            def _(out=out):
                o_ref[0, 0:tq - N_META, :] = out[N_META:, :]
                o_ref[0, o_ref.shape[1] - N_META:, :] = jnp.zeros((N_META, o_ref.shape[2]), o_ref.dtype)

            @pl.when(step > 0)
            def _(out=out, m=m):
                o_ref[0, pl.ds(pl.multiple_of(m * KEY_TILE - N_META, N_META), tq), :] = out
        else:
            o_ref[0, pl.ds(pl.multiple_of(m * KEY_TILE - N_META, N_META), tq), :] = out


def _attn_step_kernel(*refs, n_free, n_cache_tiles):
    if n_cache_tiles:
        (q_ref, kd_ref, vd_ref, u2_ref, kn_ref, vn_ref, kt_any, vt_any, o_ref,
         kd_scr, vd_scr, kbuf, vbuf) = refs
        width = q_ref.shape[2]
        n_near = kn_ref.shape[2] // KEY_TILE

        def load_tile(j):
            if isinstance(j, int) and j >= n_cache_tiles - n_near:
                ls = slice((j - (n_cache_tiles - n_near)) * KEY_TILE, (j - (n_cache_tiles - n_near) + 1) * KEY_TILE)
                return kn_ref[0, :, ls], vn_ref[0, :, ls]
            off = j * KEY_TILE if isinstance(j, int) else pl.multiple_of(j * KEY_TILE, KEY_TILE)
            src = (pl.program_id(0), pl.ds(pl.program_id(1) * width, width), pl.ds(off, KEY_TILE))
            pltpu.sync_copy(kt_any.at[src], kbuf)
            pltpu.sync_copy(vt_any.at[src], vbuf)
            return kbuf[...], vbuf[...]
    else:
        q_ref, kd_ref, vd_ref, u2_ref, o_ref, kd_scr, vd_scr = refs
        load_tile = None
    tq = q_ref.shape[1]
    n_rows = kd_ref.shape[1]
    lane = lax.broadcasted_iota(jnp.int32, (tq, LANES), 1)
    row = lax.broadcasted_iota(jnp.int32, (tq, LANES), 0)
    lo_half = lane < SB_HEAD_DIM
    kd_scr[...] = jnp.zeros_like(kd_scr)
    vd_scr[...] = jnp.zeros_like(vd_scr)
    kd_scr[0:n_rows, :] = kd_ref[0].astype(BF16)
    vd_scr[0:n_rows, :] = vd_ref[0].astype(BF16)
    diag = ('rows', kd_scr[...], vd_scr[...], lane < row + n_free)
    qs = _head_queries(q_ref[0], lo_half)
    accs = _attend(qs, u2_ref[...], [diag], load_tile, n_cache_tiles - 1, lane)
    for p, out in enumerate(_pair_outputs(accs, lo_half, o_ref.dtype)):
        o_ref[0, :, p * LANES:(p + 1) * LANES] = out


def _cumsum_rhs():
    s = np.arange(KEY_TILE)
    strict = (s[:, None] > s[None, :]).astype(np.float32)
    half = np.concatenate([strict, np.ones((KEY_TILE, KEY_TILE), np.float32)], axis=1)
    return jnp.asarray(np.concatenate([half, half], axis=0), BF16)


ATTN_PROMPT_WIDTH = 8 * SB_HEAD_DIM
ATTN_STEP_TILES = 2


def attention_prompt(big, big_meta, k_t, v_t):
    b, s, _ = big.shape
    width = ATTN_PROMPT_WIDTH
    qb0 = BIG_Q // width
    rows = ATTN_STEP_TILES * KEY_TILE
    per_step = rows // N_META
    n_pos = k_t.shape[2]
    cache = pl.BlockSpec((1, width, n_pos), lambda bi, hp, m: (bi, hp, 0))
    return pl.pallas_call(
        _attn_prompt_kernel,
        grid=(b, SB_WIDTH // width, s // rows),
        in_specs=[pl.BlockSpec((1, N_META, width), lambda bi, hp, m: (bi, jnp.maximum(m * per_step - 1, 0), qb0 + hp)),
                  pl.BlockSpec((1, rows, width), lambda bi, hp, m: (bi, m, qb0 + hp)),
                  pl.BlockSpec((1, N_META, width), lambda bi, hp, m: (0, 0, qb0 + hp)),
                  pl.BlockSpec((2 * KEY_TILE, 2 * KEY_TILE), lambda bi, hp, m: (0, 0)),
                  cache, cache],
        out_specs=pl.BlockSpec((1, s, width), lambda bi, hp, m: (bi, 0, hp)),
        out_shape=jax.ShapeDtypeStruct((b, s, SB_WIDTH), BF16),
        compiler_params=_cparams(("parallel", "parallel", "arbitrary")),
        name="stick_breaking_prompt",
    )(big, big, big_meta, _cumsum_rhs(), k_t, v_t)


def attention_step(big, q_row_blk, k_rows, v_rows, k_t, v_t, n_cache_tiles, width):
    b = big.shape[0]
    n_rows = k_rows.shape[1]
    qb0 = BIG_Q // width
    rows = pl.BlockSpec((1, n_rows, width), lambda bi, hp: (bi, 0, hp))
    in_specs = [pl.BlockSpec((1, N_META, width), lambda bi, hp: (bi, q_row_blk, qb0 + hp)), rows, rows,
                pl.BlockSpec((2 * KEY_TILE, 2 * KEY_TILE), lambda bi, hp: (0, 0))]
    args = [big, k_rows, v_rows, _cumsum_rhs()]
    scratch = [pltpu.VMEM((KEY_TILE, width), BF16)] * 2
    if n_cache_tiles:
        near = ATTN_STATIC_TILES * KEY_TILE
        assert n_cache_tiles % ATTN_STATIC_TILES == 0
        near_blk = n_cache_tiles // ATTN_STATIC_TILES - 1
        in_specs += [pl.BlockSpec((1, width, near), lambda bi, hp: (bi, hp, near_blk))] * 2
        in_specs += [pl.BlockSpec(memory_space=pl.ANY)] * 2
        args += [k_t, v_t, k_t, v_t]
        scratch += [pltpu.VMEM((width, KEY_TILE), F32)] * 2
    return pl.pallas_call(
        functools.partial(_attn_step_kernel, n_free=n_rows - N_META, n_cache_tiles=n_cache_tiles),
        grid=(b, SB_WIDTH // width),
        in_specs=in_specs,
        out_specs=pl.BlockSpec((1, N_META, width), lambda bi, hp: (bi, 0, hp)),
        out_shape=jax.ShapeDtypeStruct((b, N_META, SB_WIDTH), BF16),
        scratch_shapes=scratch,
        compiler_params=_cparams(("parallel", "parallel")),
        name="stick_breaking_step",
    )(*args)


def _rows8(op, x, r8):
    return op(x.reshape(x.shape[0] // 8, 8, x.shape[1]), r8[None]).reshape(x.shape)


def _ssd_kernel(xbc_ref, x16_ref, z_ref, dt_ref, dtT_ref, prev_ref, h0_ref, cw_ref, cb_ref, dtb_ref, dtbT_ref,
                alog_ref, alogT_ref, dskip_ref, gn_ref, e3_ref, ltri_ref, ublk_ref, shift_ref,
                y_ref, hfin_ref, st_ref, *, rows, n_sub, n_chunks):
    Q = SSD_CHUNK
    c = pl.program_id(1)
    n_blk = SSD_WIDTH // LANES
    mul, sub = jnp.multiply, jnp.subtract

    @pl.when(c == 0)
    def _():
        for j in range(n_blk):
            st_ref[:, j * LANES:(j + 1) * LANES] = h0_ref[0, j * LANES:(j + 1) * LANES, :].T

    def pad_rows(v):
        if rows == Q:
            return v
        return jnp.concatenate([v, jnp.zeros((Q - rows, v.shape[1]), v.dtype)], axis=0)

    lane = lax.broadcasted_iota(jnp.int32, (Q, LANES), 1)
    rowq = lax.broadcasted_iota(jnp.int32, (Q, LANES), 0)
    causal2 = (lane % Q) <= rowq
    lo_half = lane < SSD_HEAD_DIM
    decay_rate = -LOG2E * jnp.exp(alog_ref[...])
    decay_rate_t = -LOG2E * jnp.exp(alogT_ref[...])

    for k in range(n_sub):
        r0 = k * Q
        hi, lo = _split2(prev_ref[0])
        if k == 0:
            hi = jnp.where(c == 0, hi, x16_ref[0])
            lo = jnp.where(c == 0, lo, jnp.zeros_like(lo))
        else:
            hi, lo = xbc_ref[0, r0 - 16:r0, :], jnp.zeros_like(lo)
        window = jnp.concatenate([pad_rows(xbc_ref[0, r0:r0 + rows, :]), hi, lo,
                                  jnp.zeros((Q - 32, SSD_CONV_CH), BF16)], axis=0)
        shifted = _dot(shift_ref[...], window)
        conv = _rows8(mul, shifted[0:Q], cw_ref[0:8, :])
        for i in range(1, SSD_CONV):
            conv = conv + _rows8(mul, shifted[i * Q:(i + 1) * Q], cw_ref[8 * i:8 * i + 8, :])
        conv = _rows8(jnp.add, conv, cb_ref[...])
        xc = conv * _sigmoid(conv)
        xs = xc[:, :SSD_WIDTH]
        b_all = xc[:, SSD_WIDTH:SSD_WIDTH + SSD_GROUPS * SSD_STATE]
        c_all = xc[:, SSD_WIDTH + SSD_GROUPS * SSD_STATE:]

        dt = _softplus(pad_rows(dt_ref[0, r0:r0 + rows, :]) + dtb_ref[...])
        dt_t = _softplus(dtT_ref[0, k] + dtbT_ref[...])
        if rows < Q:
            rowi = lax.broadcasted_iota(jnp.int32, (Q, 1), 0)
            dt = jnp.where(rowi < rows, dt, 0.0)
            xs = jnp.where(rowi < rows, xs, 0.0)
            lane_t = lax.broadcasted_iota(jnp.int32, dt_t.shape, 1)
            dt_t = jnp.where(lane_t % Q < rows, dt_t, 0.0)
        da = dt * decay_rate
        da_t = dt_t * decay_rate_t

        a_cum = _dot(ltri_ref[...], jnp.concatenate(_split3(da), axis=0))
        a_cum_t = _dot(jnp.concatenate(_split3(da_t), axis=1), ublk_ref[...])
        dt_exp = _dot(jnp.concatenate(_split2(dt), axis=1), e3_ref[0:2 * LANES, :])
        a_exp = _dot(jnp.concatenate(_split3(a_cum), axis=1), e3_ref[...])
        a_last = jnp.broadcast_to(a_exp[Q - 1:Q, :], (8, SSD_WIDTH))

        xdt = xs * dt_exp
        xdt_b = xdt.astype(BF16)
        xw = (xdt * jnp.exp2(-_rows8(sub, a_exp, a_last))).astype(BF16)
        xw_pad = jnp.concatenate([xw, jnp.zeros_like(xw)], axis=0)
        chunk_decay = jnp.exp2(a_last)
        grow = jnp.exp2(a_exp)

        y_parts = []
        for g in range(SSD_GROUPS):
            bg = b_all[:, g * SSD_STATE:(g + 1) * SSD_STATE]
            cg = c_all[:, g * SSD_STATE:(g + 1) * SSD_STATE].astype(BF16)
            bg_b = bg.astype(BF16)
            cb2 = _dot_nt(cg, jnp.concatenate([bg_b, bg_b], axis=0))
            gs = slice(g * GROUP_W, (g + 1) * GROUP_W)
            st_g = st_ref[:, gs]
            y_off = _dot(cg, st_g.astype(BF16)) * grow[:, gs]
            bg_t = jnp.concatenate([bg, jnp.zeros_like(bg)], axis=0).T.astype(BF16)
            st_ref[:, gs] = _rows8(mul, st_g, chunk_decay[:, gs]) + _dot(bg_t, xw_pad[:, gs])
            pair_out = []
            for kk in range(GROUP_W // LANES):
                i = g * (GROUP_W // LANES) + kk
                ps = slice(i * LANES, (i + 1) * LANES)
                a_row = jnp.broadcast_to(a_cum_t[i:i + 1, :], (8, LANES))
                decay = jnp.exp2(jnp.minimum(_rows8(sub, a_exp[:, ps], a_row), 0.0))
                m2 = jnp.where(causal2, cb2 * decay, 0.0).astype(BF16)
                xp = xdt_b[:, ps]
                zero = jnp.zeros_like(xp)
                xbd = jnp.concatenate([jnp.where(lo_half, xp, zero), jnp.where(lo_half, zero, xp)], axis=0)
                pair_out.append(_dot(m2, xbd))
            y_diag = jnp.concatenate(pair_out, axis=1)
            y = y_diag + y_off + _rows8(mul, xs[:, gs], dskip_ref[:, gs])
            zg = pad_rows(z_ref[0, r0:r0 + rows, gs].astype(F32))
            y = y * (zg * _sigmoid(zg))
            ms = jnp.mean(y * y, axis=-1, keepdims=True)
            y_parts.append(_rows8(mul, y * lax.rsqrt(ms + EPS), gn_ref[:, gs]))
        y_all = jnp.concatenate(y_parts, axis=1)
        y_ref[0, r0:r0 + rows, :] = y_all[:rows].astype(y_ref.dtype)

    @pl.when(c == n_chunks - 1)
    def _():
        for j in range(n_blk):
            hfin_ref[0, j * LANES:(j + 1) * LANES, :] = st_ref[:, j * LANES:(j + 1) * LANES].T


def _ssd_constants():
    Q = SSD_CHUNK
    e = np.zeros((LANES, SSD_WIDTH), np.float32)
    for h in range(SSD_HEADS):
        e[h, h * SSD_HEAD_DIM:(h + 1) * SSD_HEAD_DIM] = 1.0
    e3 = np.concatenate([e, e, e], axis=0)
    t = np.arange(Q)
    ltri = (t[:, None] >= t[None, :]).astype(np.float32)
    ltri3 = np.concatenate([ltri, ltri, ltri], axis=1)
    ublk = np.zeros((LANES, LANES), np.float32)
    ublk[:Q, :Q] = ltri.T
    ublk[Q:, Q:] = ltri.T
    ublk3 = np.concatenate([ublk, ublk, ublk], axis=0)
    shift = np.zeros((SSD_CONV * Q, 2 * Q), np.float32)
    for i in range(SSD_CONV):
        for r in range(Q):
            src = r - (SSD_CONV - 1 - i)
            if src >= 0:
                shift[i * Q + r, src] = 1.0
            else:
                shift[i * Q + r, Q + 16 + src] = 1.0
                shift[i * Q + r, Q + 32 + src] = 1.0
    return jnp.asarray(e3, BF16), jnp.asarray(ltri3, BF16), jnp.asarray(ublk3, BF16), jnp.asarray(shift, BF16)


def ssd_mixer(big, xbc_blk, z_blk, dt_raw, conv_prev, h0, p):
    b, s, _ = big.shape
    Q = SSD_CHUNK
    rows = min(s, Q)
    n_sub = SSD_STEP_CHUNKS if s % (SSD_STEP_CHUNKS * Q) == 0 else 1
    blk = rows * n_sub
    n_chunks = -(-s // blk)
    dtp = dt_raw[:, :, :SSD_HEADS]
    if s < Q:
        dtp = jnp.pad(dtp, ((0, 0), (0, Q - s), (0, 0)))
    dt_t = dtp.reshape(b, n_chunks * n_sub, Q, SSD_HEADS).transpose(0, 1, 3, 2).reshape(
        b, n_chunks * n_sub, SSD_HEADS // 2, 2 * Q)
    e3, ltri3, ublk3, shift = _ssd_constants()
    prev16 = jnp.pad(conv_prev, ((0, 0), (16 - (SSD_CONV - 1), 0), (0, 0)))
    rep8 = lambda v: jnp.repeat(v, 8, axis=0)
    const = lambda shape: pl.BlockSpec(shape, lambda bi, ci: (0,) * len(shape))
    in_specs = [
        pl.BlockSpec((1, blk, SSD_CONV_CH), lambda bi, ci: (bi, ci, xbc_blk)),
        pl.BlockSpec((1, 16, SSD_CONV_CH), lambda bi, ci: (bi, jnp.maximum(ci * (blk // 16) - 1, 0), xbc_blk)),
        pl.BlockSpec((1, blk, SSD_WIDTH), lambda bi, ci: (bi, ci, z_blk)),
        pl.BlockSpec((1, blk, LANES), lambda bi, ci: (bi, ci, 0)),
        pl.BlockSpec((1, n_sub, SSD_HEADS // 2, 2 * Q), lambda bi, ci: (bi, ci, 0, 0)),
        pl.BlockSpec((1, 16, SSD_CONV_CH), lambda bi, ci: (bi, 0, 0)),
        pl.BlockSpec((1, SSD_WIDTH, SSD_STATE), lambda bi, ci: (bi, 0, 0)),
        const((8 * SSD_CONV, SSD_CONV_CH)), const((8, SSD_CONV_CH)),
        const((1, LANES)), const((SSD_HEADS // 2, 2 * Q)),
        const((1, LANES)), const((SSD_HEADS // 2, 2 * Q)),
        const((8, SSD_WIDTH)), const((8, SSD_WIDTH)),
        const(e3.shape), const(ltri3.shape), const(ublk3.shape), const(shift.shape),
    ]
    y, h_fin = pl.pallas_call(
        functools.partial(_ssd_kernel, rows=rows, n_sub=n_sub, n_chunks=n_chunks),
        grid=(b, n_chunks),
        in_specs=in_specs,
        out_specs=[pl.BlockSpec((1, blk, SSD_WIDTH), lambda bi, ci: (bi, ci, 0)),
                   pl.BlockSpec((1, SSD_WIDTH, SSD_STATE), lambda bi, ci: (bi, 0, 0))],
        out_shape=[jax.ShapeDtypeStruct((b, s, SSD_WIDTH), BF16),
                   jax.ShapeDtypeStruct((b, SSD_WIDTH, SSD_STATE), F32)],
        scratch_shapes=[pltpu.VMEM((SSD_STATE, SSD_WIDTH), F32)],
        compiler_params=_cparams(("parallel", "arbitrary")),
        name="ssd_mixer",
    )(big, big, big, dt_raw, dt_t, prev16, h0, rep8(p['conv_w_ssd']), rep8(p['conv_b_ssd']), p['dt_bias'], p['dt_bias_t'],
      p['a_log'], p['a_log_t'], rep8(p['d_skip']), rep8(p['g_ssd_norm']), e3, ltri3, ublk3, shift)
    return y, h_fin


def _merge_kernel(att_ref, ssd_ref, ga_ref, gs_ref, h_ref, wa_ref, ws_ref, wo_ref, o_ref, wa_b, ws_b, wo_b):
    @pl.when(pl.program_id(0) == 0)
    def _():
        wa_b[...] = wa_ref[...].astype(BF16)
        ws_b[...] = ws_ref[...].astype(BF16)
        wo_b[...] = wo_ref[...].astype(BF16)

    a = _dot(att_ref[...], wa_b[...])
    s = _dot(ssd_ref[...], ws_b[...])
    merged = _sigmoid(ga_ref[...].astype(F32)) * a + _sigmoid(gs_ref[...].astype(F32)) * s
    o_ref[...] = h_ref[...] + _dot(merged.astype(BF16), wo_b[...])


def _resident(a):
    return pl.BlockSpec(a.shape, lambda i: (0,) * a.ndim, pipeline_mode=pl.Buffered(1))


def merge(att, ssd, big, h, p):
    m = h.shape[0]
    tm = _pick(m, (512, 256, 128))
    row = lambda w: pl.BlockSpec((tm, w), lambda i: (i, 0))
    gate_blk = BIG_GATE // D_MODEL
    ws = [p['w_br_att'], p['w_br_ssd'], p['w_out']]
    return pl.pallas_call(
        _merge_kernel,
        grid=(m // tm,),
        in_specs=[row(SB_WIDTH), row(SSD_WIDTH),
                  pl.BlockSpec((tm, D_MODEL), lambda i: (i, gate_blk)),
                  pl.BlockSpec((tm, D_MODEL), lambda i: (i, gate_blk + 1)),
                  row(D_MODEL)] + [_resident(w) for w in ws],
        out_specs=row(D_MODEL),
        out_shape=jax.ShapeDtypeStruct((m, D_MODEL), F32),
        scratch_shapes=[pltpu.VMEM(w.shape, BF16) for w in ws],
        compiler_params=_cparams(("arbitrary",)),
        name="merge_out_proj",
    )(att, ssd, big, big, h, *ws)


def _ffn_act_kernel(ug_ref, uv_ref, pg_ref, pv_ref, wg_ref, wv_ref, bg_ref, bv_ref, o_ref, gbuf_ref, vbuf_ref,
                    *, ts):
    t = pl.program_id(1)
    k1 = FFN_CONV - 1

    def conv(u_ref, p_ref, w_ref, b_ref, buf_ref):
        @pl.when(t == 0)
        def _():
            buf_ref[8 - k1:8, :] = p_ref[0]

        buf_ref[8:8 + ts, :] = u_ref[0].astype(F32)
        y = b_ref[...]
        for i in range(FFN_CONV):
            y = y + buf_ref[8 - k1 + i:8 - k1 + i + ts, :] * w_ref[i:i + 1, :]
        buf_ref[8 - k1:8, :] = buf_ref[8 + ts - k1:8 + ts, :]
        return y

    gate = conv(ug_ref, pg_ref, wg_ref, bg_ref, gbuf_ref)
    val = conv(uv_ref, pv_ref, wv_ref, bv_ref, vbuf_ref)
    o_ref[0] = (gate * _sigmoid(gate) * val).astype(o_ref.dtype)


FFN_SUB = 64


UPF_TM = 512
UPF_TN = D_FF // 2


def _up_ffn_kernel(h_ref, h16_ref, g_ref, wg_ref, wv_ref, pg_ref, pv_ref, cwg_ref, cwv_ref, cbg_ref, cbv_ref,
                   o_ref, gl_ref, vl_ref, wg_b, wv_b, gbuf_ref, vbuf_ref):
    i = pl.program_id(2)
    tm = h_ref.shape[1]

    @pl.when(jnp.logical_and(pl.program_id(1) == 0, i == 0))
    def _():
        wg_b[...] = wg_ref[...].astype(BF16)
        wv_b[...] = wv_ref[...].astype(BF16)

    u = _rms_norm_bf16(h_ref[0], g_ref[...])
    u16 = _rms_norm_bf16(h16_ref[0], g_ref[...])

    def conv(w_b, p_ref, last_ref, buf_ref, cw_ref, cb_ref):
        up = _dot(u, w_b[...])
        buf_ref[0:16, :] = jnp.where(i == 0, p_ref[0], _dot(u16, w_b[...]))
        buf_ref[16:16 + tm, :] = up
        last_ref[0] = up[tm - 16:tm, :]
        y = _rows8(jnp.multiply, up, cw_ref[8 * (FFN_CONV - 1):8 * FFN_CONV, :])
        for t in range(FFN_CONV - 1):
            r0 = 16 - (FFN_CONV - 1) + t
            y = y + _rows8(jnp.multiply, buf_ref[r0:r0 + tm, :], cw_ref[8 * t:8 * t + 8, :])
        return _rows8(jnp.add, y, cb_ref[...])

    gate = conv(wg_b, pg_ref, gl_ref, gbuf_ref, cwg_ref, cbg_ref)
    val = conv(wv_b, pv_ref, vl_ref, vbuf_ref, cwv_ref, cbv_ref)
    o_ref[0] = (gate * _sigmoid(gate) * val).astype(o_ref.dtype)


def up_ffn_long(h, prev, p):
    b, s, d = h.shape
    tm, tn = UPF_TM, UPF_TN
    nj = D_FF // tn
    prev16 = jnp.pad(prev, ((0, 0), (16 - (FFN_CONV - 1), 0), (0, 0)))
    rep8 = lambda v: jnp.repeat(v, 8, axis=0)
    w8, b8 = rep8(p['conv_w_ffn']), rep8(p['conv_b_ffn'])
    cols = lambda rows, half: pl.BlockSpec((rows, tn), lambda j, bi, i: (0, half * nj + j))
    w_spec = lambda half: pl.BlockSpec((d, tn), lambda j, bi, i: (0, half * nj + j), pipeline_mode=pl.Buffered(1))
    prevs = lambda half: pl.BlockSpec((1, 16, tn), lambda j, bi, i: (bi, 0, half * nj + j))
    last_spec = pl.BlockSpec((1, 16, tn), lambda j, bi, i: (bi, 0, j))
    return pl.pallas_call(
        _up_ffn_kernel,
        grid=(nj, b, s // tm),
        in_specs=[pl.BlockSpec((1, tm, d), lambda j, bi, i: (bi, i, 0)),
                  pl.BlockSpec((1, 16, d), lambda j, bi, i: (bi, jnp.maximum(i * (tm // 16) - 1, 0), 0)),
                  pl.BlockSpec((1, d), lambda j, bi, i: (0, 0)),
                  w_spec(0), w_spec(1), prevs(0), prevs(1),
                  cols(8 * FFN_CONV, 0), cols(8 * FFN_CONV, 1), cols(8, 0), cols(8, 1)],
        out_specs=[pl.BlockSpec((1, tm, tn), lambda j, bi, i: (bi, i, j)), last_spec, last_spec],
        out_shape=[jax.ShapeDtypeStruct((b, s, D_FF), BF16)] + [jax.ShapeDtypeStruct((b, 16, D_FF), F32)] * 2,
        scratch_shapes=[pltpu.VMEM((d, tn), BF16)] * 2 + [pltpu.VMEM((16 + tm, tn), F32)] * 2,
        compiler_params=_cparams(("arbitrary", "arbitrary", "arbitrary")),
        name="up_proj_conv_ffn",
    )(h, h, p['g_ffn'], p['w_up'], p['w_up'], prev16, prev16, w8, w8, b8, b8)


def ffn_act(up, prev, p):
    b, s, _ = up.shape
    ts = _pick(s, (512, 256, 128))
    half = lambda blk: pl.BlockSpec((1, ts, D_FF), lambda bi, ti: (bi, ti, blk))
    prevs = lambda blk: pl.BlockSpec((1, FFN_CONV - 1, D_FF), lambda bi, ti: (bi, 0, blk))
    wspec = lambda blk: pl.BlockSpec((FFN_CONV, D_FF), lambda bi, ti: (0, blk))
    bspec = lambda blk: pl.BlockSpec((1, D_FF), lambda bi, ti: (0, blk))
    return pl.pallas_call(
        functools.partial(_ffn_act_kernel, ts=ts),
        grid=(b, s // ts),
        in_specs=[half(0), half(1), prevs(0), prevs(1), wspec(0), wspec(1), bspec(0), bspec(1)],
        out_specs=pl.BlockSpec((1, ts, D_FF), lambda bi, ti: (bi, ti, 0)),
        out_shape=jax.ShapeDtypeStruct((b, s, D_FF), BF16),
        scratch_shapes=[pltpu.VMEM((8 + ts, D_FF), F32)] * 2,
        compiler_params=_cparams(("parallel", "arbitrary")),
        name="ffn_conv_act",
    )(up, up, prev, prev, p['conv_w_ffn'], p['conv_w_ffn'], p['conv_b_ffn'], p['conv_b_ffn'])


def _down_kernel(a_ref, w_ref, h_ref, g_ref, o_ref, w_b):
    @pl.when(pl.program_id(0) == 0)
    def _():
        w_b[...] = w_ref[...].astype(BF16)

    h = h_ref[...] + _dot(a_ref[...], w_b[...])
    ms = jnp.mean(h * h, axis=-1, keepdims=True)
    o_ref[...] = h * lax.rsqrt(ms + EPS) * g_ref[...]


def down_norm(act, h, p):
    m = h.shape[0]
    tm = _pick(m, (512, 256, 128))
    return pl.pallas_call(
        _down_kernel,
        grid=(m // tm,),
        in_specs=[pl.BlockSpec((tm, D_FF), lambda i: (i, 0)),
                  _resident(p['w_down']),
                  pl.BlockSpec((tm, D_MODEL), lambda i: (i, 0)),
                  pl.BlockSpec((1, D_MODEL), lambda i: (0, 0))],
        out_specs=pl.BlockSpec((tm, D_MODEL), lambda i: (i, 0)),
        out_shape=jax.ShapeDtypeStruct((m, D_MODEL), F32),
        scratch_shapes=[pltpu.VMEM(p['w_down'].shape, BF16)],
        compiler_params=_cparams(("arbitrary",)),
        name="down_proj_norm",
    )(act, p['w_down'], h, p['g_final'])


def mix_and_ffn(big3, att, dt_raw, hf, ssd_h0, ssd_conv_prev, ffn_conv_prev, p):
    b, s, _ = big3.shape
    m = b * s
    ssd, ssd_state = ssd_mixer(big3, BIG_XBC // SSD_CONV_CH, BIG_Z // SSD_WIDTH, dt_raw, ssd_conv_prev, ssd_h0, p)
    h1 = merge(att.reshape(m, SB_WIDTH), ssd.reshape(m, SSD_WIDTH), big3.reshape(m, BIG_W), hf, p)
    if s % UPF_TM == 0:
        act, up_g, up_v = up_ffn_long(h1.reshape(b, s, D_MODEL), ffn_conv_prev, p)
        up_last = jnp.concatenate([up_g, up_v], axis=2)
    else:
        up_last = norm_mm(h1, p['g_ffn'], p['w_up'], BF16).reshape(b, s, 2 * D_FF)
        act = ffn_act(up_last, ffn_conv_prev, p)
    y = down_norm(act.reshape(m, D_FF), h1, p).reshape(b, s, D_MODEL)
    xbc_rows = jnp.concatenate([ssd_conv_prev, big3[:, :, BIG_XBC:BIG_XBC + SSD_CONV_CH][:, -(SSD_CONV - 1):].astype(F32)],
                               axis=1)[:, -(SSD_CONV - 1):]
    up_rows = jnp.concatenate([ffn_conv_prev, up_last[:, -(FFN_CONV - 1):].astype(F32)], axis=1)[:, -(FFN_CONV - 1):]
    return y, ssd_state, xbc_rows, up_rows


def _prep_params(g_mix, w_in, conv_w_ssd, conv_b_ssd, dt_bias, a_log, d_skip, g_ssd_norm, w_br_att, w_br_ssd,
                 w_out, g_ffn, w_up, conv_w_ffn, conv_b_ffn, w_down, g_final):
    w_t = w_in.T
    dt0 = 3 * SB_WIDTH + SSD_WIDTH + SSD_CONV_CH
    w_tail_t = jnp.concatenate([w_t[dt0 + SSD_HEADS:], w_t[dt0:dt0 + SSD_HEADS],
                                jnp.zeros((LANES - SSD_HEADS, D_MODEL), F32)], axis=0)
    Q = SSD_CHUNK
    lanes_t = lambda v: jnp.repeat(v.reshape(SSD_HEADS // 2, 2, 1), Q, axis=2).reshape(SSD_HEADS // 2, 2 * Q)
    pad_l = lambda v: jnp.pad(v, (0, LANES - SSD_HEADS)).reshape(1, LANES)
    return {
        'g_mix': g_mix.reshape(1, -1), 'g_ffn': g_ffn.reshape(1, -1), 'g_final': g_final.reshape(1, -1),
        'w_t': w_t, 'w_tail_t': w_tail_t,
        'conv_w_ssd': conv_w_ssd, 'conv_b_ssd': conv_b_ssd.reshape(1, -1),
        'dt_bias': pad_l(dt_bias), 'dt_bias_t': lanes_t(dt_bias),
        'a_log': pad_l(a_log), 'a_log_t': lanes_t(a_log),
        'd_skip': jnp.repeat(d_skip, SSD_HEAD_DIM).reshape(1, -1), 'g_ssd_norm': g_ssd_norm.reshape(1, -1),
        'w_br_att': w_br_att, 'w_br_ssd': w_br_ssd, 'w_out': w_out,
        'w_up': w_up, 'conv_w_ffn': conv_w_ffn, 'conv_b_ffn': conv_b_ffn.reshape(1, -1),
        'w_down': w_down,
    }


def kernel(x_prompt, x_sample, cache_k, cache_v, state_ssm, state_ssm_conv, state_ffn_conv, meta_tokens, g_mix, w_in, conv_w_ssd, conv_b_ssd, dt_bias, a_log, d_skip, g_ssd_norm, w_br_att, w_br_ssd, w_out, g_ffn, w_up, conv_w_ffn, conv_b_ffn, w_down, g_final):
    bp, seq, _ = x_prompt.shape
    bs, dec = x_sample.shape[:2]
    assert dec == N_META and seq % KVT_TS == 0
    p = _prep_params(g_mix[0], w_in[0], conv_w_ssd[0], conv_b_ssd[0], dt_bias[0], a_log[0], d_skip[0],
                     g_ssd_norm[0], w_br_att[0], w_br_ssd[0], w_out[0], g_ffn[0], w_up[0], conv_w_ffn[0],
                     conv_b_ffn[0], w_down[0], g_final)
    proj = functools.partial(in_proj, g=p['g_mix'], w_t=p['w_t'], w_tail_t=p['w_tail_t'])
    state_shape = (SSD_WIDTH, SSD_STATE)
    rows = lambda a, b: a.reshape(b, -1, a.shape[-1])

    k_m, v_m, big_m, dt_m = proj(meta_tokens, with_kv=True)
    big_m, k_m, v_m = big_m[None], k_m[None], v_m[None]
    att_m = attention_step(big_m, 0, k_m, v_m, None, None, 0, SB_WIDTH)
    _, ssm_m, conv_m, ffn_m = mix_and_ffn(
        big_m, att_m, dt_m[None], meta_tokens, jnp.zeros((1,) + state_shape, F32),
        jnp.zeros((1, SSD_CONV - 1, SSD_CONV_CH), F32), jnp.zeros((1, FFN_CONV - 1, 2 * D_FF), F32), p)

    xf = x_prompt.reshape(bp * seq, D_MODEL)
    big_x, dt_x = proj(xf, with_kv=False)
    big_x = rows(big_x, bp)
    k_t, v_t, k_end, v_end = kv_transposed(x_prompt, meta_tokens, p['g_mix'], p['w_t'])
    att_x = attention_prompt(big_x, big_m, k_t, v_t)
    att_end = attention_step(big_x, seq // N_META - 1, k_end, v_end, k_t, v_t, seq // KEY_TILE, ATTN_PROMPT_WIDTH)
    att_x = lax.dynamic_update_slice(att_x, att_end, (0, seq - N_META, 0))
    rep = lambda a: jnp.broadcast_to(a, (bp,) + a.shape[1:])
    y_prompt, ssm_p, conv_p, ffn_p = mix_and_ffn(
        big_x, att_x, rows(dt_x, bp), xf, rep(ssm_m), rep(conv_m), rep(ffn_m), p)

    n_rows = cache_k.shape[2]
    n_tiles = (n_rows - N_META) // KEY_TILE
    assert n_tiles * KEY_TILE + N_META == n_rows
    k_s, v_s, big_s, dt_s = proj(x_sample.reshape(bs * dec, D_MODEL), with_kv=True)
    big_s, k_s, v_s = rows(big_s, bs), rows(k_s, bs), rows(v_s, bs)
    cache_t = lambda c: c[0].transpose(0, 2, 3, 1).reshape(bs, SB_WIDTH, n_rows)
    nearest = lambda c, new: jnp.concatenate([c[0][:, n_tiles * KEY_TILE:].reshape(bs, N_META, SB_WIDTH), new], axis=1)
    att_s = attention_step(big_s, 0, nearest(cache_k, k_s), nearest(cache_v, v_s), cache_t(cache_k), cache_t(cache_v),
                           n_tiles, SB_WIDTH)
    y_sample, ssm_s, conv_s, ffn_s = mix_and_ffn(
        big_s, att_s, rows(dt_s, bs), x_sample.reshape(bs * dec, D_MODEL),
        state_ssm[0].reshape((bs,) + state_shape), state_ssm_conv[0], state_ffn_conv[0], p)

    heads = lambda a: a.reshape(a.shape[0], a.shape[1], SB_HEADS, SB_HEAD_DIM)[None]
    heads_t = lambda a: a.reshape(a.shape[0], SB_HEADS, SB_HEAD_DIM, a.shape[2]).transpose(0, 3, 1, 2)[None]
    state5 = lambda a: a.reshape(a.shape[0], SSD_HEADS, SSD_HEAD_DIM, SSD_STATE)[None]
    return (y_prompt, y_sample, heads_t(k_t), heads_t(v_t), state5(ssm_p), conv_p[None], ffn_p[None],
            heads(k_s), heads(v_s), state5(ssm_s), conv_s[None], ffn_s[None])
```

```python
import functools

import numpy as np
import jax
import jax.numpy as jnp
from jax import lax
from jax.experimental import pallas as pl
from jax.experimental.pallas import tpu as pltpu

F32 = jnp.float32
BF16 = jnp.bfloat16

D_MODEL = 1024
N_META = 16
SB_HEADS = 16
SB_HEAD_DIM = 64
SB_WIDTH = SB_HEADS * SB_HEAD_DIM
SSD_WIDTH = 2 * D_MODEL
SSD_HEAD_DIM = 64
SSD_HEADS = SSD_WIDTH // SSD_HEAD_DIM
SSD_GROUPS = 4
SSD_STATE = 128
SSD_CONV = 4
SSD_CONV_CH = SSD_WIDTH + 2 * SSD_GROUPS * SSD_STATE
D_FF = 2816
FFN_CONV = 3
EPS = 1e-6

LANES = 128
SSD_CHUNK = 64
SSD_STEP_CHUNKS = 4
KEY_TILE = 128
GROUP_W = SSD_WIDTH // SSD_GROUPS
DEAD_LOG = -104.0
LOG2E = 1.4426950408889634
ATTN_STATIC_TILES = 2
ATTN_GROUP_TILES = 3
VMEM_LIMIT = 56 * 1024 * 1024


def _cparams(sem):
    return pltpu.CompilerParams(dimension_semantics=sem, vmem_limit_bytes=VMEM_LIMIT)


def _pick(n, cands):
    for c in cands:
        if n % c == 0:
            return c
    return n


def _split2(x):
    hi = x.astype(BF16)
    lo = (x - hi.astype(F32)).astype(BF16)
    return hi, lo


def _split3(x):
    hi = x.astype(BF16)
    r = x - hi.astype(F32)
    mid = r.astype(BF16)
    lo = (r - mid.astype(F32)).astype(BF16)
    return hi, mid, lo


def _dot(a, b):
    return jnp.dot(a, b, preferred_element_type=F32)


def _dot_nt(a, b):
    return lax.dot_general(a, b, (((1,), (1,)), ((), ())), preferred_element_type=F32)


def _softplus(x):
    return jnp.maximum(x, 0.0) + jnp.log(1.0 + jnp.exp(-jnp.abs(x)))


def _sigmoid(x):
    return 1.0 / (1.0 + jnp.exp2(-LOG2E * x))


def _rms_norm_bf16(x, g):
    ms = jnp.mean(x * x, axis=-1, keepdims=True)
    return (x * lax.rsqrt(ms + EPS) * g).astype(BF16)


def _norm_mm_kernel(x_ref, g_ref, w_ref, o_ref, u_ref):
    @pl.when(pl.program_id(1) == 0)
    def _():
        u_ref[...] = _rms_norm_bf16(x_ref[...], g_ref[...])

    o_ref[...] = _dot(u_ref[...], w_ref[...].astype(BF16)).astype(o_ref.dtype)


PROJ_TN = 1024
UP_TN = 1408


def norm_mm(x, g, w, out_dtype):
    m, d = x.shape
    n = w.shape[1]
    tm = _pick(m, (2048, 1024, 512, 256, 128))
    tn = UP_TN
    return pl.pallas_call(
        _norm_mm_kernel,
        grid=(m // tm, n // tn),
        in_specs=[pl.BlockSpec((tm, d), lambda i, j: (i, 0)),
                  pl.BlockSpec((1, d), lambda i, j: (0, 0)),
                  pl.BlockSpec((d, tn), lambda i, j: (0, j))],
        out_specs=pl.BlockSpec((tm, tn), lambda i, j: (i, j)),
        out_shape=jax.ShapeDtypeStruct((m, n), out_dtype),
        scratch_shapes=[pltpu.VMEM((tm, d), BF16)],
        compiler_params=_cparams(("parallel", "arbitrary")),
        name="norm_up_proj",
    )(x, g, w)


_NQ = SB_WIDTH // PROJ_TN
_J_K, _J_V, _J_Z = _NQ, 2 * _NQ, 3 * _NQ
_J_X = _J_Z + SSD_WIDTH // PROJ_TN
_J_G = _J_X + SSD_CONV_CH // PROJ_TN
_J_END = _J_G + 2 * D_MODEL // PROJ_TN
BIG_Z, BIG_Q, BIG_XBC, BIG_GATE = 0, SSD_WIDTH, SSD_WIDTH + SB_WIDTH, SSD_WIDTH + SB_WIDTH + SSD_CONV_CH
BIG_W = BIG_GATE + 2 * D_MODEL


def _big_block(j):
    return jnp.where(j < _J_K, BIG_Q // PROJ_TN + j,
                     jnp.where(j < _J_Z, BIG_Q // PROJ_TN + _NQ - 1,
                               jnp.where(j < _J_X, j - _J_Z + BIG_Z // PROJ_TN,
                                         jnp.where(j < _J_G, j - _J_X + BIG_XBC // PROJ_TN,
                                                   j - _J_G + BIG_GATE // PROJ_TN))))


def _sweep_step(j, with_kv):
    return j if with_kv else jnp.where(j >= _J_K, j + (_J_Z - _J_K), j)


def _in_proj_kernel(*refs, with_kv):
    if with_kv:
        x_ref, g_ref, win_ref, wgate_ref, wdt_ref, k_ref, v_ref, big_ref, dt_ref, u_ref = refs
    else:
        x_ref, g_ref, win_ref, wgate_ref, wdt_ref, big_ref, dt_ref, u_ref = refs
    j = _sweep_step(pl.program_id(1), with_kv)

    def proj(w_ref):
        return _dot_nt(u_ref[...], w_ref[...].astype(BF16))

    @pl.when(pl.program_id(1) == 0)
    def _():
        u_ref[...] = _rms_norm_bf16(x_ref[...], g_ref[...])
        dt_ref[...] = proj(wdt_ref)

    if with_kv:
        @pl.when(jnp.logical_and(j >= _J_K, j < _J_V))
        def _():
            k_ref[...] = proj(win_ref)

        @pl.when(jnp.logical_and(j >= _J_V, j < _J_Z))
        def _():
            v_ref[...] = proj(win_ref)

    @pl.when(jnp.logical_or(j < _J_K, jnp.logical_and(j >= _J_Z, j < _J_G)))
    def _():
        big_ref[...] = proj(win_ref).astype(BF16)

    @pl.when(j >= _J_G)
    def _():
        big_ref[...] = proj(wgate_ref).astype(BF16)


def in_proj(x, g, w_t, w_tail_t, with_kv):
    m, d = x.shape
    tm = _pick(m, (2048, 1024, 512, 256, 128))
    tn = PROJ_TN
    step = functools.partial(_sweep_step, with_kv=with_kv)
    clip = lambda j, lo, n: jnp.clip(step(j) - lo, 0, n - 1)
    kv_specs = [pl.BlockSpec((tm, tn), lambda i, j: (i, clip(j, _J_K, _NQ))),
                pl.BlockSpec((tm, tn), lambda i, j: (i, clip(j, _J_V, _NQ)))]
    kv_shapes = [jax.ShapeDtypeStruct((m, SB_WIDTH), F32)] * 2
    return pl.pallas_call(
        functools.partial(_in_proj_kernel, with_kv=with_kv),
        grid=(m // tm, _J_END if with_kv else _J_END - (_J_Z - _J_K)),
        in_specs=[pl.BlockSpec((tm, d), lambda i, j: (i, 0), pipeline_mode=pl.Buffered(1)),
                  pl.BlockSpec((1, d), lambda i, j: (0, 0)),
                  pl.BlockSpec((tn, d), lambda i, j: (jnp.minimum(step(j), _J_G - 1), 0)),
                  pl.BlockSpec((tn, d), lambda i, j: (clip(j, _J_G, _J_END - _J_G), 0)),
                  pl.BlockSpec((LANES, d), lambda i, j: (2 * D_MODEL // LANES, 0))],
        out_specs=(kv_specs if with_kv else []) + [
            pl.BlockSpec((tm, tn), lambda i, j: (i, _big_block(step(j)))),
            pl.BlockSpec((tm, LANES), lambda i, j: (i, 0))],
        out_shape=(kv_shapes if with_kv else []) + [
            jax.ShapeDtypeStruct((m, BIG_W), BF16), jax.ShapeDtypeStruct((m, LANES), F32)],
        scratch_shapes=[pltpu.VMEM((tm, d), BF16)],
        compiler_params=_cparams(("parallel", "arbitrary")),
        name="in_proj",
    )(x, g, w_t, w_tail_t, w_tail_t)


KVT_TS = 1024


def _kv_t_kernel(x16_ref, xt_ref, meta_ref, g_ref, wk_ref, wv_ref, kt_ref, vt_ref, kl_ref, vl_ref, wk_b, wv_b):
    j = pl.program_id(1)

    @pl.when(jnp.logical_and(pl.program_id(0) == 0, j == 0))
    def _():
        wk_b[...] = wk_ref[...].astype(BF16)
        wv_b[...] = wv_ref[...].astype(BF16)

    first = jnp.where(j == 0, meta_ref[...], x16_ref[0])
    x = jnp.concatenate([first, xt_ref[0, :KVT_TS - N_META, :]], axis=0)
    u = _rms_norm_bf16(x, g_ref[...])
    kt_ref[0] = _dot_nt(wk_b[...], u)
    vt_ref[0] = _dot_nt(wv_b[...], u)

    @pl.when(j == pl.num_programs(1) - 1)
    def _():
        kl_ref[0] = _dot_nt(u[0:N_META, :], wk_b[...])
        vl_ref[0] = _dot_nt(u[0:N_META, :], wv_b[...])


def kv_transposed(x, meta, g, w_t):
    b, s, d = x.shape
    n_pos = N_META + s
    n_tiles = -(-n_pos // KVT_TS)
    assert n_pos - (n_tiles - 1) * KVT_TS == N_META
    per_tile = KVT_TS // N_META
    out_spec = pl.BlockSpec((1, SB_WIDTH, KVT_TS), lambda bi, j: (bi, 0, j))
    last_spec = pl.BlockSpec((1, N_META, SB_WIDTH), lambda bi, j: (bi, 0, 0))
    w_spec = lambda blk: pl.BlockSpec((SB_WIDTH, d), lambda bi, j: (blk, 0), pipeline_mode=pl.Buffered(1))
    return pl.pallas_call(
        _kv_t_kernel,
        grid=(b, n_tiles),
        in_specs=[pl.BlockSpec((1, N_META, d), lambda bi, j: (bi, jnp.maximum(j * per_tile - 1, 0), 0)),
                  pl.BlockSpec((1, KVT_TS, d), lambda bi, j: (bi, jnp.minimum(j, s // KVT_TS - 1), 0)),
                  pl.BlockSpec((N_META, d), lambda bi, j: (0, 0)),
                  pl.BlockSpec((1, d), lambda bi, j: (0, 0)),
                  w_spec(1), w_spec(2)],
        out_specs=[out_spec, out_spec, last_spec, last_spec],
        out_shape=[jax.ShapeDtypeStruct((b, SB_WIDTH, n_pos), F32)] * 2
        + [jax.ShapeDtypeStruct((b, N_META, SB_WIDTH), F32)] * 2,
        scratch_shapes=[pltpu.VMEM((SB_WIDTH, d), BF16)] * 2,
        compiler_params=_cparams(("arbitrary", "arbitrary")),
        name="kv_transposed",
    )(x, x, meta, g, w_t, w_t)


def _stick_breaking_tiles(qs, u2, tiles, carries, accs):
    n_heads = len(qs)
    units = [(t, h) for t in range(len(tiles)) for h in range(n_heads)]
    ps = lambda h: slice((h // 2) * LANES, (h // 2 + 1) * LANES)

    def scores(t, h):
        kind, keys = tiles[t][0], tiles[t][1]
        return _dot_nt(qs[h], keys[:, ps(h)]) if kind == 'rows' else _dot(qs[h], keys[ps(h), :])

    z = {u: scores(*u) for u in units}
    log_beta, cat = {}, {}
    for u in units:
        mask = tiles[u[0]][3]
        soft = jnp.log(1.0 + jnp.exp2(-LOG2E * jnp.abs(z[u])))
        log_beta[u] = jnp.minimum(z[u], 0.0) - soft
        log_keep = log_beta[u] - z[u]
        if mask is not None:
            log_keep = jnp.where(mask, log_keep, 0.0)
        cat[u] = jnp.concatenate(_split2(log_keep), axis=1)
    r = {u: _dot(cat[u], u2) for u in units}
    carries, accs = list(carries), list(accs)
    w = {}
    for t, h in units:
        mask = tiles[t][3]
        wt = jnp.exp2(LOG2E * (log_beta[(t, h)] + r[(t, h)][:, :LANES] + carries[h]))
        if mask is not None:
            wt = jnp.where(mask, wt, 0.0)
        w[(t, h)] = wt.astype(BF16)
        carries[h] = carries[h] + r[(t, h)][:, LANES:]
    for t, h in units:
        kind, vals = tiles[t][0], tiles[t][2]
        pv = _dot(w[(t, h)], vals[:, ps(h)]) if kind == 'rows' else _dot_nt(w[(t, h)], vals[ps(h), :])
        accs[h] = accs[h] + pv
    return carries, accs


def _head_queries(q, lo_half):
    qs = []
    for h in range(q.shape[1] // SB_HEAD_DIM):
        qp = q[:, (h // 2) * LANES:(h // 2 + 1) * LANES]
        keep = lo_half if h % 2 == 0 else jnp.logical_not(lo_half)
        qs.append(jnp.where(keep, qp, jnp.zeros_like(qp)) * jnp.asarray(SB_HEAD_DIM ** -0.5, BF16))
    return qs


def _any_alive(carries):
    m = carries[0]
    for c in carries[1:]:
        m = jnp.maximum(m, c)
    return (jnp.max(m) > DEAD_LOG).astype(jnp.int32)


def _attend(qs, u2, first_tiles, load_tile, j0, lane):
    tq = qs[0].shape[0]

    def cache_tile(j):
        if isinstance(j, int):
            mask = None if j >= 0 else lane < 0
            kt, vt = load_tile(max(j, 0))
        else:
            mask = jnp.logical_and(j >= 0, lane >= 0)
            kt, vt = load_tile(jnp.maximum(j, 0))
        return 'cols', kt.astype(BF16), vt.astype(BF16), mask

    tiles = list(first_tiles)
    if load_tile is not None:
        for _ in range(ATTN_STATIC_TILES):
            tiles.append(cache_tile(j0))
            j0 = j0 - 1
    zeros = jnp.zeros((tq, LANES), F32)
    carries, accs = [zeros] * len(qs), [zeros] * len(qs)
    for g in range(0, len(tiles), ATTN_GROUP_TILES):
        carries, accs = _stick_breaking_tiles(qs, u2, tiles[g:g + ATTN_GROUP_TILES], carries, accs)
    if load_tile is not None:
        def cond(s):
            return jnp.logical_and(s[0] >= 0, s[1] > 0)

        def body(s):
            j, _, carries, accs = s
            carries, accs = _stick_breaking_tiles(qs, u2, [cache_tile(j)], carries, accs)
            return j - 1, _any_alive(carries), tuple(carries), tuple(accs)

        state = (jnp.asarray(j0, jnp.int32), _any_alive(carries), tuple(carries), tuple(accs))
        _, _, carries, accs = lax.while_loop(cond, body, state)
    return accs


def _pair_outputs(accs, lo_half, dtype):
    return [jnp.where(lo_half, accs[2 * p], accs[2 * p + 1]).astype(dtype) for p in range(len(accs) // 2)]


def _attn_prompt_kernel(q16_ref, qt_ref, qm_ref, u2_ref, kt_ref, vt_ref, o_ref):
    step = pl.program_id(2)
    tq = KEY_TILE
    lane = lax.broadcasted_iota(jnp.int32, (tq, LANES), 1)
    row = lax.broadcasted_iota(jnp.int32, (tq, LANES), 0)
    lo_half = lane < SB_HEAD_DIM

    def load_tile(j):
        off = pl.multiple_of(j * KEY_TILE, KEY_TILE)
        return kt_ref[0, :, pl.ds(off, KEY_TILE)], vt_ref[0, :, pl.ds(off, KEY_TILE)]

    for sub in range(ATTN_STEP_TILES):
        m = step * ATTN_STEP_TILES + sub
        if sub == 0:
            first = jnp.where(step == 0, qm_ref[0], q16_ref[0])
            q = jnp.concatenate([first, qt_ref[0, :tq - N_META, :]], axis=0)
        else:
            q = qt_ref[0, sub * tq - N_META:(sub + 1) * tq - N_META, :]
        qs = _head_queries(q, lo_half)
        kd, vd = load_tile(m)
        diag = ('cols', kd.astype(BF16), vd.astype(BF16), lane < row)
        accs = _attend(qs, u2_ref[...], [diag], load_tile, m - 1, lane)
        out = jnp.concatenate(_pair_outputs(accs, lo_half, o_ref.dtype), axis=1)
        if sub == 0:
            @pl.when(step == 0)
            def _(out=out):
                o_ref[0, 0:tq - N_META, :] = out[N_META:, :]
                o_ref[0, o_ref.shape[1] - N_META:, :] = jnp.zeros((N_META, o_ref.shape[2]), o_ref.dtype)

            @pl.when(step > 0)
            def _(out=out, m=m):
                o_ref[0, pl.ds(pl.multiple_of(m * KEY_TILE - N_META, N_META), tq), :] = out
        else:
            o_ref[0, pl.ds(pl.multiple_of(m * KEY_TILE - N_META, N_META), tq), :] = out


def _attn_step_kernel(*refs, n_free, n_cache_tiles):
    if n_cache_tiles:
        (q_ref, kd_ref, vd_ref, u2_ref, kn_ref, vn_ref, kt_any, vt_any, o_ref,
         kd_scr, vd_scr, kbuf, vbuf) = refs
        width = q_ref.shape[2]
        n_near = kn_ref.shape[2] // KEY_TILE

        def load_tile(j):
            if isinstance(j, int) and j >= n_cache_tiles - n_near:
                ls = slice((j - (n_cache_tiles - n_near)) * KEY_TILE, (j - (n_cache_tiles - n_near) + 1) * KEY_TILE)
                return kn_ref[0, :, ls], vn_ref[0, :, ls]
            off = j * KEY_TILE if isinstance(j, int) else pl.multiple_of(j * KEY_TILE, KEY_TILE)
            src = (pl.program_id(0), pl.ds(pl.program_id(1) * width, width), pl.ds(off, KEY_TILE))
            pltpu.sync_copy(kt_any.at[src], kbuf)
            pltpu.sync_copy(vt_any.at[src], vbuf)
            return kbuf[...], vbuf[...]
    else:
        q_ref, kd_ref, vd_ref, u2_ref, o_ref, kd_scr, vd_scr = refs
        load_tile = None
    tq = q_ref.shape[1]
    n_rows = kd_ref.shape[1]
    lane = lax.broadcasted_iota(jnp.int32, (tq, LANES), 1)
    row = lax.broadcasted_iota(jnp.int32, (tq, LANES), 0)
    lo_half = lane < SB_HEAD_DIM
    kd_scr[...] = jnp.zeros_like(kd_scr)
    vd_scr[...] = jnp.zeros_like(vd_scr)
    kd_scr[0:n_rows, :] = kd_ref[0].astype(BF16)
    vd_scr[0:n_rows, :] = vd_ref[0].astype(BF16)
    diag = ('rows', kd_scr[...], vd_scr[...], lane < row + n_free)
    qs = _head_queries(q_ref[0], lo_half)
    accs = _attend(qs, u2_ref[...], [diag], load_tile, n_cache_tiles - 1, lane)
    for p, out in enumerate(_pair_outputs(accs, lo_half, o_ref.dtype)):
        o_ref[0, :, p * LANES:(p + 1) * LANES] = out


def _cumsum_rhs():
    s = np.arange(KEY_TILE)
    strict = (s[:, None] > s[None, :]).astype(np.float32)
    half = np.concatenate([strict, np.ones((KEY_TILE, KEY_TILE), np.float32)], axis=1)
    return jnp.asarray(np.concatenate([half, half], axis=0), BF16)


ATTN_PROMPT_WIDTH = 8 * SB_HEAD_DIM
ATTN_STEP_TILES = 2


def attention_prompt(big, big_meta, k_t, v_t):
    b, s, _ = big.shape
    width = ATTN_PROMPT_WIDTH
    qb0 = BIG_Q // width
    rows = ATTN_STEP_TILES * KEY_TILE
    per_step = rows // N_META
    n_pos = k_t.shape[2]
    cache = pl.BlockSpec((1, width, n_pos), lambda bi, hp, m: (bi, hp, 0))
    return pl.pallas_call(
        _attn_prompt_kernel,
        grid=(b, SB_WIDTH // width, s // rows),
        in_specs=[pl.BlockSpec((1, N_META, width), lambda bi, hp, m: (bi, jnp.maximum(m * per_step - 1, 0), qb0 + hp)),
                  pl.BlockSpec((1, rows, width), lambda bi, hp, m: (bi, m, qb0 + hp)),
                  pl.BlockSpec((1, N_META, width), lambda bi, hp, m: (0, 0, qb0 + hp)),
                  pl.BlockSpec((2 * KEY_TILE, 2 * KEY_TILE), lambda bi, hp, m: (0, 0)),
                  cache, cache],
        out_specs=pl.BlockSpec((1, s, width), lambda bi, hp, m: (bi, 0, hp)),
        out_shape=jax.ShapeDtypeStruct((b, s, SB_WIDTH), BF16),
        compiler_params=_cparams(("parallel", "parallel", "arbitrary")),
        name="stick_breaking_prompt",
    )(big, big, big_meta, _cumsum_rhs(), k_t, v_t)


def attention_step(big, q_row_blk, k_rows, v_rows, k_t, v_t, n_cache_tiles, width):
    b = big.shape[0]
    n_rows = k_rows.shape[1]
    qb0 = BIG_Q // width
    rows = pl.BlockSpec((1, n_rows, width), lambda bi, hp: (bi, 0, hp))
    in_specs = [pl.BlockSpec((1, N_META, width), lambda bi, hp: (bi, q_row_blk, qb0 + hp)), rows, rows,
                pl.BlockSpec((2 * KEY_TILE, 2 * KEY_TILE), lambda bi, hp: (0, 0))]
    args = [big, k_rows, v_rows, _cumsum_rhs()]
    scratch = [pltpu.VMEM((KEY_TILE, width), BF16)] * 2
    if n_cache_tiles:
        near = ATTN_STATIC_TILES * KEY_TILE
        assert n_cache_tiles % ATTN_STATIC_TILES == 0
        near_blk = n_cache_tiles // ATTN_STATIC_TILES - 1
        in_specs += [pl.BlockSpec((1, width, near), lambda bi, hp: (bi, hp, near_blk))] * 2
        in_specs += [pl.BlockSpec(memory_space=pl.ANY)] * 2
        args += [k_t, v_t, k_t, v_t]
        scratch += [pltpu.VMEM((width, KEY_TILE), F32)] * 2
    return pl.pallas_call(
        functools.partial(_attn_step_kernel, n_free=n_rows - N_META, n_cache_tiles=n_cache_tiles),
        grid=(b, SB_WIDTH // width),
        in_specs=in_specs,
        out_specs=pl.BlockSpec((1, N_META, width), lambda bi, hp: (bi, 0, hp)),
        out_shape=jax.ShapeDtypeStruct((b, N_META, SB_WIDTH), BF16),
        scratch_shapes=scratch,
        compiler_params=_cparams(("parallel", "parallel")),
        name="stick_breaking_step",
    )(*args)


def _rows8(op, x, r8):
    return op(x.reshape(x.shape[0] // 8, 8, x.shape[1]), r8[None]).reshape(x.shape)


def _ssd_kernel(xbc_ref, x16_ref, z_ref, dt_ref, dtT_ref, prev_ref, h0_ref, cw_ref, cb_ref, dtb_ref, dtbT_ref,
                alog_ref, alogT_ref, dskip_ref, gn_ref, e3_ref, ltri_ref, ublk_ref, shift_ref,
                y_ref, hfin_ref, st_ref, *, rows, n_sub, n_chunks):
    Q = SSD_CHUNK
    c = pl.program_id(1)
    n_blk = SSD_WIDTH // LANES
    mul, sub = jnp.multiply, jnp.subtract

    @pl.when(c == 0)
    def _():
        for j in range(n_blk):
            st_ref[:, j * LANES:(j + 1) * LANES] = h0_ref[0, j * LANES:(j + 1) * LANES, :].T

    def pad_rows(v):
        if rows == Q:
            return v
        return jnp.concatenate([v, jnp.zeros((Q - rows, v.shape[1]), v.dtype)], axis=0)

    lane = lax.broadcasted_iota(jnp.int32, (Q, LANES), 1)
    rowq = lax.broadcasted_iota(jnp.int32, (Q, LANES), 0)
    causal2 = (lane % Q) <= rowq
    lo_half = lane < SSD_HEAD_DIM
    decay_rate = -LOG2E * jnp.exp(alog_ref[...])
    decay_rate_t = -LOG2E * jnp.exp(alogT_ref[...])

    for k in range(n_sub):
        r0 = k * Q
        hi, lo = _split2(prev_ref[0])
        if k == 0:
            hi = jnp.where(c == 0, hi, x16_ref[0])
            lo = jnp.where(c == 0, lo, jnp.zeros_like(lo))
        else:
            hi, lo = xbc_ref[0, r0 - 16:r0, :], jnp.zeros_like(lo)
        window = jnp.concatenate([pad_rows(xbc_ref[0, r0:r0 + rows, :]), hi, lo,
                                  jnp.zeros((Q - 32, SSD_CONV_CH), BF16)], axis=0)
        shifted = _dot(shift_ref[...], window)
        conv = _rows8(mul, shifted[0:Q], cw_ref[0:8, :])
        for i in range(1, SSD_CONV):
            conv = conv + _rows8(mul, shifted[i * Q:(i + 1) * Q], cw_ref[8 * i:8 * i + 8, :])
        conv = _rows8(jnp.add, conv, cb_ref[...])
        xc = conv * _sigmoid(conv)
        xs = xc[:, :SSD_WIDTH]
        b_all = xc[:, SSD_WIDTH:SSD_WIDTH + SSD_GROUPS * SSD_STATE]
        c_all = xc[:, SSD_WIDTH + SSD_GROUPS * SSD_STATE:]

        dt = _softplus(pad_rows(dt_ref[0, r0:r0 + rows, :]) + dtb_ref[...])
        dt_t = _softplus(dtT_ref[0, k] + dtbT_ref[...])
        if rows < Q:
            rowi = lax.broadcasted_iota(jnp.int32, (Q, 1), 0)
            dt = jnp.where(rowi < rows, dt, 0.0)
            xs = jnp.where(rowi < rows, xs, 0.0)
            lane_t = lax.broadcasted_iota(jnp.int32, dt_t.shape, 1)
            dt_t = jnp.where(lane_t % Q < rows, dt_t, 0.0)
        da = dt * decay_rate
        da_t = dt_t * decay_rate_t

        a_cum = _dot(ltri_ref[...], jnp.concatenate(_split3(da), axis=0))
        a_cum_t = _dot(jnp.concatenate(_split3(da_t), axis=1), ublk_ref[...])
        dt_exp = _dot(jnp.concatenate(_split2(dt), axis=1), e3_ref[0:2 * LANES, :])
        a_exp = _dot(jnp.concatenate(_split3(a_cum), axis=1), e3_ref[...])
        a_last = jnp.broadcast_to(a_exp[Q - 1:Q, :], (8, SSD_WIDTH))

        xdt = xs * dt_exp
        xdt_b = xdt.astype(BF16)
        xw = (xdt * jnp.exp2(-_rows8(sub, a_exp, a_last))).astype(BF16)
        xw_pad = jnp.concatenate([xw, jnp.zeros_like(xw)], axis=0)
        chunk_decay = jnp.exp2(a_last)
        grow = jnp.exp2(a_exp)

        y_parts = []
        for g in range(SSD_GROUPS):
            bg = b_all[:, g * SSD_STATE:(g + 1) * SSD_STATE]
            cg = c_all[:, g * SSD_STATE:(g + 1) * SSD_STATE].astype(BF16)
            bg_b = bg.astype(BF16)
            cb2 = _dot_nt(cg, jnp.concatenate([bg_b, bg_b], axis=0))
            gs = slice(g * GROUP_W, (g + 1) * GROUP_W)
            st_g = st_ref[:, gs]
            y_off = _dot(cg, st_g.astype(BF16)) * grow[:, gs]
            bg_t = jnp.concatenate([bg, jnp.zeros_like(bg)], axis=0).T.astype(BF16)
            st_ref[:, gs] = _rows8(mul, st_g, chunk_decay[:, gs]) + _dot(bg_t, xw_pad[:, gs])
            pair_out = []
            for kk in range(GROUP_W // LANES):
                i = g * (GROUP_W // LANES) + kk
                ps = slice(i * LANES, (i + 1) * LANES)
                a_row = jnp.broadcast_to(a_cum_t[i:i + 1, :], (8, LANES))
                decay = jnp.exp2(jnp.minimum(_rows8(sub, a_exp[:, ps], a_row), 0.0))
                m2 = jnp.where(causal2, cb2 * decay, 0.0).astype(BF16)
                xp = xdt_b[:, ps]
                zero = jnp.zeros_like(xp)
                xbd = jnp.concatenate([jnp.where(lo_half, xp, zero), jnp.where(lo_half, zero, xp)], axis=0)
                pair_out.append(_dot(m2, xbd))
            y_diag = jnp.concatenate(pair_out, axis=1)
            y = y_diag + y_off + _rows8(mul, xs[:, gs], dskip_ref[:, gs])
            zg = pad_rows(z_ref[0, r0:r0 + rows, gs].astype(F32))
            y = y * (zg * _sigmoid(zg))
            ms = jnp.mean(y * y, axis=-1, keepdims=True)
            y_parts.append(_rows8(mul, y * lax.rsqrt(ms + EPS), gn_ref[:, gs]))
        y_all = jnp.concatenate(y_parts, axis=1)
        y_ref[0, r0:r0 + rows, :] = y_all[:rows].astype(y_ref.dtype)

    @pl.when(c == n_chunks - 1)
    def _():
        for j in range(n_blk):
            hfin_ref[0, j * LANES:(j + 1) * LANES, :] = st_ref[:, j * LANES:(j + 1) * LANES].T


def _ssd_constants():
    Q = SSD_CHUNK
    e = np.zeros((LANES, SSD_WIDTH), np.float32)
    for h in range(SSD_HEADS):
        e[h, h * SSD_HEAD_DIM:(h + 1) * SSD_HEAD_DIM] = 1.0
    e3 = np.concatenate([e, e, e], axis=0)
    t = np.arange(Q)
    ltri = (t[:, None] >= t[None, :]).astype(np.float32)
    ltri3 = np.concatenate([ltri, ltri, ltri], axis=1)
    ublk = np.zeros((LANES, LANES), np.float32)
    ublk[:Q, :Q] = ltri.T
    ublk[Q:, Q:] = ltri.T
    ublk3 = np.concatenate([ublk, ublk, ublk], axis=0)
    shift = np.zeros((SSD_CONV * Q, 2 * Q), np.float32)
    for i in range(SSD_CONV):
        for r in range(Q):
            src = r - (SSD_CONV - 1 - i)
            if src >= 0:
                shift[i * Q + r, src] = 1.0
            else:
                shift[i * Q + r, Q + 16 + src] = 1.0
                shift[i * Q + r, Q + 32 + src] = 1.0
    return jnp.asarray(e3, BF16), jnp.asarray(ltri3, BF16), jnp.asarray(ublk3, BF16), jnp.asarray(shift, BF16)


def ssd_mixer(big, xbc_blk, z_blk, dt_raw, conv_prev, h0, p):
    b, s, _ = big.shape
    Q = SSD_CHUNK
    rows = min(s, Q)
    n_sub = SSD_STEP_CHUNKS if s % (SSD_STEP_CHUNKS * Q) == 0 else 1
    blk = rows * n_sub
    n_chunks = -(-s // blk)
    dtp = dt_raw[:, :, :SSD_HEADS]
    if s < Q:
        dtp = jnp.pad(dtp, ((0, 0), (0, Q - s), (0, 0)))
    dt_t = dtp.reshape(b, n_chunks * n_sub, Q, SSD_HEADS).transpose(0, 1, 3, 2).reshape(
        b, n_chunks * n_sub, SSD_HEADS // 2, 2 * Q)
    e3, ltri3, ublk3, shift = _ssd_constants()
    prev16 = jnp.pad(conv_prev, ((0, 0), (16 - (SSD_CONV - 1), 0), (0, 0)))
    rep8 = lambda v: jnp.repeat(v, 8, axis=0)
    const = lambda shape: pl.BlockSpec(shape, lambda bi, ci: (0,) * len(shape))
    in_specs = [
        pl.BlockSpec((1, blk, SSD_CONV_CH), lambda bi, ci: (bi, ci, xbc_blk)),
        pl.BlockSpec((1, 16, SSD_CONV_CH), lambda bi, ci: (bi, jnp.maximum(ci * (blk // 16) - 1, 0), xbc_blk)),
        pl.BlockSpec((1, blk, SSD_WIDTH), lambda bi, ci: (bi, ci, z_blk)),
        pl.BlockSpec((1, blk, LANES), lambda bi, ci: (bi, ci, 0)),
        pl.BlockSpec((1, n_sub, SSD_HEADS // 2, 2 * Q), lambda bi, ci: (bi, ci, 0, 0)),
        pl.BlockSpec((1, 16, SSD_CONV_CH), lambda bi, ci: (bi, 0, 0)),
        pl.BlockSpec((1, SSD_WIDTH, SSD_STATE), lambda bi, ci: (bi, 0, 0)),
        const((8 * SSD_CONV, SSD_CONV_CH)), const((8, SSD_CONV_CH)),
        const((1, LANES)), const((SSD_HEADS // 2, 2 * Q)),
        const((1, LANES)), const((SSD_HEADS // 2, 2 * Q)),
        const((8, SSD_WIDTH)), const((8, SSD_WIDTH)),
        const(e3.shape), const(ltri3.shape), const(ublk3.shape), const(shift.shape),
    ]
    y, h_fin = pl.pallas_call(
        functools.partial(_ssd_kernel, rows=rows, n_sub=n_sub, n_chunks=n_chunks),
        grid=(b, n_chunks),
        in_specs=in_specs,
        out_specs=[pl.BlockSpec((1, blk, SSD_WIDTH), lambda bi, ci: (bi, ci, 0)),
                   pl.BlockSpec((1, SSD_WIDTH, SSD_STATE), lambda bi, ci: (bi, 0, 0))],
        out_shape=[jax.ShapeDtypeStruct((b, s, SSD_WIDTH), BF16),
                   jax.ShapeDtypeStruct((b, SSD_WIDTH, SSD_STATE), F32)],
        scratch_shapes=[pltpu.VMEM((SSD_STATE, SSD_WIDTH), F32)],
        compiler_params=_cparams(("parallel", "arbitrary")),
        name="ssd_mixer",
    )(big, big, big, dt_raw, dt_t, prev16, h0, rep8(p['conv_w_ssd']), rep8(p['conv_b_ssd']), p['dt_bias'], p['dt_bias_t'],
      p['a_log'], p['a_log_t'], rep8(p['d_skip']), rep8(p['g_ssd_norm']), e3, ltri3, ublk3, shift)
    return y, h_fin


def _merge_kernel(att_ref, ssd_ref, ga_ref, gs_ref, h_ref, wa_ref, ws_ref, wo_ref, o_ref, wa_b, ws_b, wo_b):
    @pl.when(pl.program_id(0) == 0)
    def _():
        wa_b[...] = wa_ref[...].astype(BF16)
        ws_b[...] = ws_ref[...].astype(BF16)
        wo_b[...] = wo_ref[...].astype(BF16)

    a = _dot(att_ref[...], wa_b[...])
    s = _dot(ssd_ref[...], ws_b[...])
    merged = _sigmoid(ga_ref[...].astype(F32)) * a + _sigmoid(gs_ref[...].astype(F32)) * s
    o_ref[...] = h_ref[...] + _dot(merged.astype(BF16), wo_b[...])


def _resident(a):
    return pl.BlockSpec(a.shape, lambda i: (0,) * a.ndim, pipeline_mode=pl.Buffered(1))


def merge(att, ssd, big, h, p):
    m = h.shape[0]
    tm = _pick(m, (512, 256, 128))
    row = lambda w: pl.BlockSpec((tm, w), lambda i: (i, 0))
    gate_blk = BIG_GATE // D_MODEL
    ws = [p['w_br_att'], p['w_br_ssd'], p['w_out']]
    return pl.pallas_call(
        _merge_kernel,
        grid=(m // tm,),
        in_specs=[row(SB_WIDTH), row(SSD_WIDTH),
                  pl.BlockSpec((tm, D_MODEL), lambda i: (i, gate_blk)),
                  pl.BlockSpec((tm, D_MODEL), lambda i: (i, gate_blk + 1)),
                  row(D_MODEL)] + [_resident(w) for w in ws],
        out_specs=row(D_MODEL),
        out_shape=jax.ShapeDtypeStruct((m, D_MODEL), F32),
        scratch_shapes=[pltpu.VMEM(w.shape, BF16) for w in ws],
        compiler_params=_cparams(("arbitrary",)),
        name="merge_out_proj",
    )(att, ssd, big, big, h, *ws)


def _ffn_act_kernel(ug_ref, uv_ref, pg_ref, pv_ref, wg_ref, wv_ref, bg_ref, bv_ref, o_ref, gbuf_ref, vbuf_ref,
                    *, ts):
    t = pl.program_id(1)
    k1 = FFN_CONV - 1

    def conv(u_ref, p_ref, w_ref, b_ref, buf_ref):
        @pl.when(t == 0)
        def _():
            buf_ref[8 - k1:8, :] = p_ref[0]

        buf_ref[8:8 + ts, :] = u_ref[0].astype(F32)
        y = b_ref[...]
        for i in range(FFN_CONV):
            y = y + buf_ref[8 - k1 + i:8 - k1 + i + ts, :] * w_ref[i:i + 1, :]
        buf_ref[8 - k1:8, :] = buf_ref[8 + ts - k1:8 + ts, :]
        return y

    gate = conv(ug_ref, pg_ref, wg_ref, bg_ref, gbuf_ref)
    val = conv(uv_ref, pv_ref, wv_ref, bv_ref, vbuf_ref)
    o_ref[0] = (gate * _sigmoid(gate) * val).astype(o_ref.dtype)


UPF_TM = 512
UPF_TN = D_FF // 2


def _up_ffn_kernel(h_ref, h16_ref, g_ref, wg_ref, wv_ref, pg_ref, pv_ref, cwg_ref, cwv_ref, cbg_ref, cbv_ref,
                   o_ref, gl_ref, vl_ref, wg_b, wv_b, gbuf_ref, vbuf_ref):
    i = pl.program_id(2)
    tm = h_ref.shape[1]

    @pl.when(jnp.logical_and(pl.program_id(1) == 0, i == 0))
    def _():
        wg_b[...] = wg_ref[...].astype(BF16)
        wv_b[...] = wv_ref[...].astype(BF16)

    u = _rms_norm_bf16(h_ref[0], g_ref[...])
    u16 = _rms_norm_bf16(h16_ref[0], g_ref[...])

    def conv(w_b, p_ref, last_ref, buf_ref, cw_ref, cb_ref):
        up = _dot(u, w_b[...])
        buf_ref[0:16, :] = jnp.where(i == 0, p_ref[0], _dot(u16, w_b[...]))
        buf_ref[16:16 + tm, :] = up
        last_ref[0] = up[tm - 16:tm, :]
        y = _rows8(jnp.multiply, up, cw_ref[8 * (FFN_CONV - 1):8 * FFN_CONV, :])
        for t in range(FFN_CONV - 1):
            r0 = 16 - (FFN_CONV - 1) + t
            y = y + _rows8(jnp.multiply, buf_ref[r0:r0 + tm, :], cw_ref[8 * t:8 * t + 8, :])
        return _rows8(jnp.add, y, cb_ref[...])

    gate = conv(wg_b, pg_ref, gl_ref, gbuf_ref, cwg_ref, cbg_ref)
    val = conv(wv_b, pv_ref, vl_ref, vbuf_ref, cwv_ref, cbv_ref)
    o_ref[0] = (gate * _sigmoid(gate) * val).astype(o_ref.dtype)


def up_ffn_long(h, prev, p):
    b, s, d = h.shape
    tm, tn = UPF_TM, UPF_TN
    nj = D_FF // tn
    prev16 = jnp.pad(prev, ((0, 0), (16 - (FFN_CONV - 1), 0), (0, 0)))
    rep8 = lambda v: jnp.repeat(v, 8, axis=0)
    w8, b8 = rep8(p['conv_w_ffn']), rep8(p['conv_b_ffn'])
    cols = lambda rows, half: pl.BlockSpec((rows, tn), lambda j, bi, i: (0, half * nj + j))
    w_spec = lambda half: pl.BlockSpec((d, tn), lambda j, bi, i: (0, half * nj + j), pipeline_mode=pl.Buffered(1))
    prevs = lambda half: pl.BlockSpec((1, 16, tn), lambda j, bi, i: (bi, 0, half * nj + j))
    last_spec = pl.BlockSpec((1, 16, tn), lambda j, bi, i: (bi, 0, j))
    return pl.pallas_call(
        _up_ffn_kernel,
        grid=(nj, b, s // tm),
        in_specs=[pl.BlockSpec((1, tm, d), lambda j, bi, i: (bi, i, 0)),
                  pl.BlockSpec((1, 16, d), lambda j, bi, i: (bi, jnp.maximum(i * (tm // 16) - 1, 0), 0)),
                  pl.BlockSpec((1, d), lambda j, bi, i: (0, 0)),
                  w_spec(0), w_spec(1), prevs(0), prevs(1),
                  cols(8 * FFN_CONV, 0), cols(8 * FFN_CONV, 1), cols(8, 0), cols(8, 1)],
        out_specs=[pl.BlockSpec((1, tm, tn), lambda j, bi, i: (bi, i, j)), last_spec, last_spec],
        out_shape=[jax.ShapeDtypeStruct((b, s, D_FF), BF16)] + [jax.ShapeDtypeStruct((b, 16, D_FF), F32)] * 2,
        scratch_shapes=[pltpu.VMEM((d, tn), BF16)] * 2 + [pltpu.VMEM((16 + tm, tn), F32)] * 2,
        compiler_params=_cparams(("arbitrary", "arbitrary", "arbitrary")),
        name="up_proj_conv_ffn",
    )(h, h, p['g_ffn'], p['w_up'], p['w_up'], prev16, prev16, w8, w8, b8, b8)


def ffn_act(up, prev, p):
    b, s, _ = up.shape
    ts = _pick(s, (512, 256, 128))
    half = lambda blk: pl.BlockSpec((1, ts, D_FF), lambda bi, ti: (bi, ti, blk))
    prevs = lambda blk: pl.BlockSpec((1, FFN_CONV - 1, D_FF), lambda bi, ti: (bi, 0, blk))
    wspec = lambda blk: pl.BlockSpec((FFN_CONV, D_FF), lambda bi, ti: (0, blk))
    bspec = lambda blk: pl.BlockSpec((1, D_FF), lambda bi, ti: (0, blk))
    return pl.pallas_call(
        functools.partial(_ffn_act_kernel, ts=ts),
        grid=(b, s // ts),
        in_specs=[half(0), half(1), prevs(0), prevs(1), wspec(0), wspec(1), bspec(0), bspec(1)],
        out_specs=pl.BlockSpec((1, ts, D_FF), lambda bi, ti: (bi, ti, 0)),
        out_shape=jax.ShapeDtypeStruct((b, s, D_FF), BF16),
        scratch_shapes=[pltpu.VMEM((8 + ts, D_FF), F32)] * 2,
        compiler_params=_cparams(("parallel", "arbitrary")),
        name="ffn_conv_act",
    )(up, up, prev, prev, p['conv_w_ffn'], p['conv_w_ffn'], p['conv_b_ffn'], p['conv_b_ffn'])


def _down_kernel(a_ref, w_ref, h_ref, g_ref, o_ref, w_b):
    @pl.when(pl.program_id(0) == 0)
    def _():
        w_b[...] = w_ref[...].astype(BF16)

    h = h_ref[...] + _dot(a_ref[...], w_b[...])
    ms = jnp.mean(h * h, axis=-1, keepdims=True)
    o_ref[...] = h * lax.rsqrt(ms + EPS) * g_ref[...]


def down_norm(act, h, p):
    m = h.shape[0]
    tm = _pick(m, (512, 256, 128))
    return pl.pallas_call(
        _down_kernel,
        grid=(m // tm,),
        in_specs=[pl.BlockSpec((tm, D_FF), lambda i: (i, 0)),
                  _resident(p['w_down']),
                  pl.BlockSpec((tm, D_MODEL), lambda i: (i, 0)),
                  pl.BlockSpec((1, D_MODEL), lambda i: (0, 0))],
        out_specs=pl.BlockSpec((tm, D_MODEL), lambda i: (i, 0)),
        out_shape=jax.ShapeDtypeStruct((m, D_MODEL), F32),
        scratch_shapes=[pltpu.VMEM(p['w_down'].shape, BF16)],
        compiler_params=_cparams(("arbitrary",)),
        name="down_proj_norm",
    )(act, p['w_down'], h, p['g_final'])


def ffn_and_out(big3, att, ssd, hf, ssd_conv_prev, ffn_conv_prev, p):
    b, s, _ = big3.shape
    m = b * s
    h1 = merge(att.reshape(m, SB_WIDTH), ssd.reshape(m, SSD_WIDTH), big3.reshape(m, BIG_W), hf, p)
    if s % UPF_TM == 0:
        act, up_g, up_v = up_ffn_long(h1.reshape(b, s, D_MODEL), ffn_conv_prev, p)
        up_last = jnp.concatenate([up_g, up_v], axis=2)
    else:
        up_last = norm_mm(h1, p['g_ffn'], p['w_up'], BF16).reshape(b, s, 2 * D_FF)
        act = ffn_act(up_last, ffn_conv_prev, p)
    y = down_norm(act.reshape(m, D_FF), h1, p).reshape(b, s, D_MODEL)
    xbc_rows = jnp.concatenate([ssd_conv_prev, big3[:, :, BIG_XBC:BIG_XBC + SSD_CONV_CH][:, -(SSD_CONV - 1):].astype(F32)],
                               axis=1)[:, -(SSD_CONV - 1):]
    up_rows = jnp.concatenate([ffn_conv_prev, up_last[:, -(FFN_CONV - 1):].astype(F32)], axis=1)[:, -(FFN_CONV - 1):]
    return y, xbc_rows, up_rows


def _prep_params(g_mix, w_in, conv_w_ssd, conv_b_ssd, dt_bias, a_log, d_skip, g_ssd_norm, w_br_att, w_br_ssd,
                 w_out, g_ffn, w_up, conv_w_ffn, conv_b_ffn, w_down, g_final):
    w_t = w_in.T
    dt0 = 3 * SB_WIDTH + SSD_WIDTH + SSD_CONV_CH
    w_tail_t = jnp.concatenate([w_t[dt0 + SSD_HEADS:], w_t[dt0:dt0 + SSD_HEADS],
                                jnp.zeros((LANES - SSD_HEADS, D_MODEL), F32)], axis=0)
    Q = SSD_CHUNK
    lanes_t = lambda v: jnp.repeat(v.reshape(SSD_HEADS // 2, 2, 1), Q, axis=2).reshape(SSD_HEADS // 2, 2 * Q)
    pad_l = lambda v: jnp.pad(v, (0, LANES - SSD_HEADS)).reshape(1, LANES)
    return {
        'g_mix': g_mix.reshape(1, -1), 'g_ffn': g_ffn.reshape(1, -1), 'g_final': g_final.reshape(1, -1),
        'w_t': w_t, 'w_tail_t': w_tail_t,
        'conv_w_ssd': conv_w_ssd, 'conv_b_ssd': conv_b_ssd.reshape(1, -1),
        'dt_bias': pad_l(dt_bias), 'dt_bias_t': lanes_t(dt_bias),
        'a_log': pad_l(a_log), 'a_log_t': lanes_t(a_log),
        'd_skip': jnp.repeat(d_skip, SSD_HEAD_DIM).reshape(1, -1), 'g_ssd_norm': g_ssd_norm.reshape(1, -1),
        'w_br_att': w_br_att, 'w_br_ssd': w_br_ssd, 'w_out': w_out,
        'w_up': w_up, 'conv_w_ffn': conv_w_ffn, 'conv_b_ffn': conv_b_ffn.reshape(1, -1),
        'w_down': w_down,
    }


def kernel(x_prompt, x_sample, cache_k, cache_v, state_ssm, state_ssm_conv, state_ffn_conv, meta_tokens, g_mix, w_in, conv_w_ssd, conv_b_ssd, dt_bias, a_log, d_skip, g_ssd_norm, w_br_att, w_br_ssd, w_out, g_ffn, w_up, conv_w_ffn, conv_b_ffn, w_down, g_final):
    bp, seq, _ = x_prompt.shape
    bs, dec = x_sample.shape[:2]
    assert dec == N_META and seq % KVT_TS == 0
    p = _prep_params(g_mix[0], w_in[0], conv_w_ssd[0], conv_b_ssd[0], dt_bias[0], a_log[0], d_skip[0],
                     g_ssd_norm[0], w_br_att[0], w_br_ssd[0], w_out[0], g_ffn[0], w_up[0], conv_w_ffn[0],
                     conv_b_ffn[0], w_down[0], g_final)
    proj = functools.partial(in_proj, g=p['g_mix'], w_t=p['w_t'], w_tail_t=p['w_tail_t'])
    state_shape = (SSD_WIDTH, SSD_STATE)
    rows = lambda a, b: a.reshape(b, -1, a.shape[-1])

    n_rows = cache_k.shape[2]
    n_tiles = (n_rows - N_META) // KEY_TILE
    assert n_tiles * KEY_TILE + N_META == n_rows
    x16 = jnp.concatenate([meta_tokens, x_sample.reshape(bs * dec, D_MODEL)], axis=0)
    k16, v16, big16, dt16 = (rows(a, 1 + bs) for a in proj(x16, with_kv=True))
    big_m, k_m, v_m, big_s, k_s, v_s = big16[:1], k16[:1], v16[:1], big16[1:], k16[1:], v16[1:]
    att_m = attention_step(big_m, 0, k_m, v_m, None, None, 0, SB_WIDTH)
    cache_t = lambda c: c[0].transpose(0, 2, 3, 1).reshape(bs, SB_WIDTH, n_rows)
    nearest = lambda c, new: jnp.concatenate([c[0][:, n_tiles * KEY_TILE:].reshape(bs, N_META, SB_WIDTH), new], axis=1)
    att_s = attention_step(big_s, 0, nearest(cache_k, k_s), nearest(cache_v, v_s), cache_t(cache_k), cache_t(cache_v),
                           n_tiles, SB_WIDTH)
    conv16 = jnp.concatenate([jnp.zeros((1, SSD_CONV - 1, SSD_CONV_CH), F32), state_ssm_conv[0]], axis=0)
    ffn16 = jnp.concatenate([jnp.zeros((1, FFN_CONV - 1, 2 * D_FF), F32), state_ffn_conv[0]], axis=0)
    xbc = BIG_XBC // SSD_CONV_CH, BIG_Z // SSD_WIDTH
    ssd_m, ssm_m = ssd_mixer(big_m, *xbc, dt16[:1], conv16[:1], jnp.zeros((1,) + state_shape, F32), p)
    ssd_s, ssm_s = ssd_mixer(big_s, *xbc, dt16[1:], conv16[1:], state_ssm[0].reshape((bs,) + state_shape), p)
    y16, conv16, ffn16 = ffn_and_out(big16, jnp.concatenate([att_m, att_s], axis=0),
                                     jnp.concatenate([ssd_m, ssd_s], axis=0), x16, conv16, ffn16, p)
    conv_m, ffn_m, y_sample, conv_s, ffn_s = conv16[:1], ffn16[:1], y16[1:], conv16[1:], ffn16[1:]

    xf = x_prompt.reshape(bp * seq, D_MODEL)
    big_x, dt_x = proj(xf, with_kv=False)
    big_x = rows(big_x, bp)
    k_t, v_t, k_end, v_end = kv_transposed(x_prompt, meta_tokens, p['g_mix'], p['w_t'])
    att_x = attention_prompt(big_x, big_m, k_t, v_t)
    att_end = attention_step(big_x, seq // N_META - 1, k_end, v_end, k_t, v_t, seq // KEY_TILE, ATTN_PROMPT_WIDTH)
    att_x = lax.dynamic_update_slice(att_x, att_end, (0, seq - N_META, 0))
    rep = lambda a: jnp.broadcast_to(a, (bp,) + a.shape[1:])
    ssd_x, ssm_p = ssd_mixer(big_x, *xbc, rows(dt_x, bp), rep(conv_m), rep(ssm_m), p)
    y_prompt, conv_p, ffn_p = ffn_and_out(big_x, att_x, ssd_x, xf, rep(conv_m), rep(ffn_m), p)

    heads = lambda a: a.reshape(a.shape[0], a.shape[1], SB_HEADS, SB_HEAD_DIM)[None]
    heads_t = lambda a: a.reshape(a.shape[0], SB_HEADS, SB_HEAD_DIM, a.shape[2]).transpose(0, 3, 1, 2)[None]
    state5 = lambda a: a.reshape(a.shape[0], SSD_HEADS, SSD_HEAD_DIM, SSD_STATE)[None]
    return (y_prompt, y_sample, heads_t(k_t), heads_t(v_t), state5(ssm_p), conv_p[None], ffn_p[None],
            heads(k_s), heads(v_s), state5(ssm_s), conv_s[None], ffn_s[None])
```

```python
import functools

import numpy as np
import jax
import jax.numpy as jnp
from jax import lax
from jax.experimental import pallas as pl
from jax.experimental.pallas import tpu as pltpu

F32 = jnp.float32
BF16 = jnp.bfloat16

D_MODEL = 1024
N_META = 16
SB_HEADS = 16
SB_HEAD_DIM = 64
SB_WIDTH = SB_HEADS * SB_HEAD_DIM
SSD_WIDTH = 2 * D_MODEL
SSD_HEAD_DIM = 64
SSD_HEADS = SSD_WIDTH // SSD_HEAD_DIM
SSD_GROUPS = 4
SSD_STATE = 128
SSD_CONV = 4
SSD_CONV_CH = SSD_WIDTH + 2 * SSD_GROUPS * SSD_STATE
D_FF = 2816
FFN_CONV = 3
EPS = 1e-6

LANES = 128
SSD_CHUNK = 64
SSD_STEP_CHUNKS = 4
KEY_TILE = 128
GROUP_W = SSD_WIDTH // SSD_GROUPS
DEAD_LOG = -104.0
LOG2E = 1.4426950408889634
ATTN_STATIC_TILES = 2
ATTN_GROUP_TILES = 3
VMEM_LIMIT = 56 * 1024 * 1024


def _cparams(sem):
    return pltpu.CompilerParams(dimension_semantics=sem, vmem_limit_bytes=VMEM_LIMIT)


def _pick(n, cands):
    for c in cands:
        if n % c == 0:
            return c
    return n


def _split2(x):
    hi = x.astype(BF16)
    lo = (x - hi.astype(F32)).astype(BF16)
    return hi, lo


def _split3(x):
    hi = x.astype(BF16)
    r = x - hi.astype(F32)
    mid = r.astype(BF16)
    lo = (r - mid.astype(F32)).astype(BF16)
    return hi, mid, lo


def _dot(a, b):
    return jnp.dot(a, b, preferred_element_type=F32)


def _dot_nt(a, b):
    return lax.dot_general(a, b, (((1,), (1,)), ((), ())), preferred_element_type=F32)


def _softplus(x):
    return jnp.maximum(x, 0.0) + jnp.log(1.0 + jnp.exp(-jnp.abs(x)))


def _sigmoid(x):
    return 1.0 / (1.0 + jnp.exp2(-LOG2E * x))


def _rms_norm_bf16(x, g):
    ms = jnp.mean(x * x, axis=-1, keepdims=True)
    return (x * lax.rsqrt(ms + EPS) * g).astype(BF16)


def _norm_mm_kernel(x_ref, g_ref, w_ref, o_ref, u_ref):
    @pl.when(pl.program_id(1) == 0)
    def _():
        u_ref[...] = _rms_norm_bf16(x_ref[...], g_ref[...])

    o_ref[...] = _dot(u_ref[...], w_ref[...].astype(BF16)).astype(o_ref.dtype)


PROJ_TN = 1024
UP_TN = 1408


def norm_mm(x, g, w, out_dtype):
    m, d = x.shape
    n = w.shape[1]
    tm = _pick(m, (2048, 1024, 512, 256, 128))
    tn = UP_TN
    return pl.pallas_call(
        _norm_mm_kernel,
        grid=(m // tm, n // tn),
        in_specs=[pl.BlockSpec((tm, d), lambda i, j: (i, 0)),
                  pl.BlockSpec((1, d), lambda i, j: (0, 0)),
                  pl.BlockSpec((d, tn), lambda i, j: (0, j))],
        out_specs=pl.BlockSpec((tm, tn), lambda i, j: (i, j)),
        out_shape=jax.ShapeDtypeStruct((m, n), out_dtype),
        scratch_shapes=[pltpu.VMEM((tm, d), BF16)],
        compiler_params=_cparams(("parallel", "arbitrary")),
        name="norm_up_proj",
    )(x, g, w)


_NQ = SB_WIDTH // PROJ_TN
_J_K, _J_V, _J_Z = _NQ, 2 * _NQ, 3 * _NQ
_J_X = _J_Z + SSD_WIDTH // PROJ_TN
_J_G = _J_X + SSD_CONV_CH // PROJ_TN
_J_END = _J_G + 2 * D_MODEL // PROJ_TN
BIG_Z, BIG_Q, BIG_XBC, BIG_GATE = 0, SSD_WIDTH, SSD_WIDTH + SB_WIDTH, SSD_WIDTH + SB_WIDTH + SSD_CONV_CH
BIG_W = BIG_GATE + 2 * D_MODEL


def _big_block(j):
    return jnp.where(j < _J_K, BIG_Q // PROJ_TN + j,
                     jnp.where(j < _J_Z, BIG_Q // PROJ_TN + _NQ - 1,
                               jnp.where(j < _J_X, j - _J_Z + BIG_Z // PROJ_TN,
                                         jnp.where(j < _J_G, j - _J_X + BIG_XBC // PROJ_TN,
                                                   j - _J_G + BIG_GATE // PROJ_TN))))


def _sweep_step(j, with_kv):
    return j if with_kv else jnp.where(j >= _J_K, j + (_J_Z - _J_K), j)


def _in_proj_kernel(*refs, with_kv):
    if with_kv:
        x_ref, g_ref, win_ref, wgate_ref, wdt_ref, k_ref, v_ref, big_ref, dt_ref, u_ref = refs
    else:
        x_ref, g_ref, win_ref, wgate_ref, wdt_ref, big_ref, dt_ref, u_ref = refs
    j = _sweep_step(pl.program_id(1), with_kv)

    def proj(w_ref):
        return _dot_nt(u_ref[...], w_ref[...].astype(BF16))

    @pl.when(pl.program_id(1) == 0)
    def _():
        u_ref[...] = _rms_norm_bf16(x_ref[...], g_ref[...])
        dt_ref[...] = proj(wdt_ref)

    if with_kv:
        @pl.when(jnp.logical_and(j >= _J_K, j < _J_V))
        def _():
            k_ref[...] = proj(win_ref)

        @pl.when(jnp.logical_and(j >= _J_V, j < _J_Z))
        def _():
            v_ref[...] = proj(win_ref)

    @pl.when(jnp.logical_or(j < _J_K, jnp.logical_and(j >= _J_Z, j < _J_G)))
    def _():
        big_ref[...] = proj(win_ref).astype(BF16)

    @pl.when(j >= _J_G)
    def _():
        big_ref[...] = proj(wgate_ref).astype(BF16)


def in_proj(x, g, w_t, w_tail_t, with_kv):
    m, d = x.shape
    tm = _pick(m, (2048, 1024, 512, 256, 128))
    tn = PROJ_TN
    step = functools.partial(_sweep_step, with_kv=with_kv)
    clip = lambda j, lo, n: jnp.clip(step(j) - lo, 0, n - 1)
    kv_specs = [pl.BlockSpec((tm, tn), lambda i, j: (i, clip(j, _J_K, _NQ))),
                pl.BlockSpec((tm, tn), lambda i, j: (i, clip(j, _J_V, _NQ)))]
    kv_shapes = [jax.ShapeDtypeStruct((m, SB_WIDTH), F32)] * 2
    return pl.pallas_call(
        functools.partial(_in_proj_kernel, with_kv=with_kv),
        grid=(m // tm, _J_END if with_kv else _J_END - (_J_Z - _J_K)),
        in_specs=[pl.BlockSpec((tm, d), lambda i, j: (i, 0), pipeline_mode=pl.Buffered(1)),
                  pl.BlockSpec((1, d), lambda i, j: (0, 0)),
                  pl.BlockSpec((tn, d), lambda i, j: (jnp.minimum(step(j), _J_G - 1), 0)),
                  pl.BlockSpec((tn, d), lambda i, j: (clip(j, _J_G, _J_END - _J_G), 0)),
                  pl.BlockSpec((LANES, d), lambda i, j: (2 * D_MODEL // LANES, 0))],
        out_specs=(kv_specs if with_kv else []) + [
            pl.BlockSpec((tm, tn), lambda i, j: (i, _big_block(step(j)))),
            pl.BlockSpec((tm, LANES), lambda i, j: (i, 0))],
        out_shape=(kv_shapes if with_kv else []) + [
            jax.ShapeDtypeStruct((m, BIG_W), BF16), jax.ShapeDtypeStruct((m, LANES), F32)],
        scratch_shapes=[pltpu.VMEM((tm, d), BF16)],
        compiler_params=_cparams(("parallel", "arbitrary")),
        name="in_proj",
    )(x, g, w_t, w_tail_t, w_tail_t)


KVT_TS = 1024


def _kv_t_kernel(x16_ref, xt_ref, meta_ref, g_ref, wk_ref, wv_ref, kt_ref, vt_ref, kl_ref, vl_ref, wk_b, wv_b):
    j = pl.program_id(1)

    @pl.when(jnp.logical_and(pl.program_id(0) == 0, j == 0))
    def _():
        wk_b[...] = wk_ref[...].astype(BF16)
        wv_b[...] = wv_ref[...].astype(BF16)

    first = jnp.where(j == 0, meta_ref[...], x16_ref[0])
    x = jnp.concatenate([first, xt_ref[0, :KVT_TS - N_META, :]], axis=0)
    u = _rms_norm_bf16(x, g_ref[...])
    kt_ref[0] = _dot_nt(wk_b[...], u)
    vt_ref[0] = _dot_nt(wv_b[...], u)

    @pl.when(j == pl.num_programs(1) - 1)
    def _():
        kl_ref[0] = _dot_nt(u[0:N_META, :], wk_b[...])
        vl_ref[0] = _dot_nt(u[0:N_META, :], wv_b[...])


def kv_transposed(x, meta, g, w_t):
    b, s, d = x.shape
    n_pos = N_META + s
    n_tiles = -(-n_pos // KVT_TS)
    assert n_pos - (n_tiles - 1) * KVT_TS == N_META
    per_tile = KVT_TS // N_META
    out_spec = pl.BlockSpec((1, SB_WIDTH, KVT_TS), lambda bi, j: (bi, 0, j))
    last_spec = pl.BlockSpec((1, N_META, SB_WIDTH), lambda bi, j: (bi, 0, 0))
    w_spec = lambda blk: pl.BlockSpec((SB_WIDTH, d), lambda bi, j: (blk, 0), pipeline_mode=pl.Buffered(1))
    return pl.pallas_call(
        _kv_t_kernel,
        grid=(b, n_tiles),
        in_specs=[pl.BlockSpec((1, N_META, d), lambda bi, j: (bi, jnp.maximum(j * per_tile - 1, 0), 0)),
                  pl.BlockSpec((1, KVT_TS, d), lambda bi, j: (bi, jnp.minimum(j, s // KVT_TS - 1), 0)),
                  pl.BlockSpec((N_META, d), lambda bi, j: (0, 0)),
                  pl.BlockSpec((1, d), lambda bi, j: (0, 0)),
                  w_spec(1), w_spec(2)],
        out_specs=[out_spec, out_spec, last_spec, last_spec],
        out_shape=[jax.ShapeDtypeStruct((b, SB_WIDTH, n_pos), F32)] * 2
        + [jax.ShapeDtypeStruct((b, N_META, SB_WIDTH), F32)] * 2,
        scratch_shapes=[pltpu.VMEM((SB_WIDTH, d), BF16)] * 2,
        compiler_params=_cparams(("arbitrary", "arbitrary")),
        name="kv_transposed",
    )(x, x, meta, g, w_t, w_t)


def _stick_breaking_tiles(qs, u2, tiles, carries, accs):
    n_heads = len(qs)
    units = [(t, h) for t in range(len(tiles)) for h in range(n_heads)]
    ps = lambda h: slice((h // 2) * LANES, (h // 2 + 1) * LANES)

    def scores(t, h):
        kind, keys = tiles[t][0], tiles[t][1]
        return _dot_nt(qs[h], keys[:, ps(h)]) if kind == 'rows' else _dot(qs[h], keys[ps(h), :])

    z = {u: scores(*u) for u in units}
    log_beta, cat = {}, {}
    for u in units:
        mask = tiles[u[0]][3]
        soft = jnp.log(1.0 + jnp.exp2(-LOG2E * jnp.abs(z[u])))
        log_beta[u] = jnp.minimum(z[u], 0.0) - soft
        log_keep = log_beta[u] - z[u]
        if mask is not None:
            log_keep = jnp.where(mask, log_keep, 0.0)
        cat[u] = jnp.concatenate(_split2(log_keep), axis=1)
    r = {u: _dot(cat[u], u2) for u in units}
    carries, accs = list(carries), list(accs)
    w = {}
    for t, h in units:
        mask = tiles[t][3]
        wt = jnp.exp2(LOG2E * (log_beta[(t, h)] + r[(t, h)][:, :LANES] + carries[h]))
        if mask is not None:
            wt = jnp.where(mask, wt, 0.0)
        w[(t, h)] = wt.astype(BF16)
        carries[h] = carries[h] + r[(t, h)][:, LANES:]
    for t, h in units:
        kind, vals = tiles[t][0], tiles[t][2]
        pv = _dot(w[(t, h)], vals[:, ps(h)]) if kind == 'rows' else _dot_nt(w[(t, h)], vals[ps(h), :])
        accs[h] = accs[h] + pv
    return carries, accs


def _head_queries(q, lo_half):
    qs = []
    for h in range(q.shape[1] // SB_HEAD_DIM):
        qp = q[:, (h // 2) * LANES:(h // 2 + 1) * LANES]
        keep = lo_half if h % 2 == 0 else jnp.logical_not(lo_half)
        qs.append(jnp.where(keep, qp, jnp.zeros_like(qp)) * jnp.asarray(SB_HEAD_DIM ** -0.5, BF16))
    return qs


def _any_alive(carries):
    m = carries[0]
    for c in carries[1:]:
        m = jnp.maximum(m, c)
    return (jnp.max(m) > DEAD_LOG).astype(jnp.int32)


def _attend(qs, u2, first_tiles, load_tile, j0, lane):
    tq = qs[0].shape[0]

    def cache_tile(j):
        if isinstance(j, int):
            mask = None if j >= 0 else lane < 0
            kt, vt = load_tile(max(j, 0))
        else:
            mask = jnp.logical_and(j >= 0, lane >= 0)
            kt, vt = load_tile(jnp.maximum(j, 0))
        return 'cols', kt.astype(BF16), vt.astype(BF16), mask

    tiles = list(first_tiles)
    if load_tile is not None:
        for _ in range(ATTN_STATIC_TILES):
            tiles.append(cache_tile(j0))
            j0 = j0 - 1
    zeros = jnp.zeros((tq, LANES), F32)
    carries, accs = [zeros] * len(qs), [zeros] * len(qs)
    for g in range(0, len(tiles), ATTN_GROUP_TILES):
        carries, accs = _stick_breaking_tiles(qs, u2, tiles[g:g + ATTN_GROUP_TILES], carries, accs)
    if load_tile is not None:
        def cond(s):
            return jnp.logical_and(s[0] >= 0, s[1] > 0)

        def body(s):
            j, _, carries, accs = s
            carries, accs = _stick_breaking_tiles(qs, u2, [cache_tile(j)], carries, accs)
            return j - 1, _any_alive(carries), tuple(carries), tuple(accs)

        state = (jnp.asarray(j0, jnp.int32), _any_alive(carries), tuple(carries), tuple(accs))
        _, _, carries, accs = lax.while_loop(cond, body, state)
    return accs


def _pair_outputs(accs, lo_half, dtype):
    return [jnp.where(lo_half, accs[2 * p], accs[2 * p + 1]).astype(dtype) for p in range(len(accs) // 2)]


def _attn_prompt_kernel(q16_ref, qt_ref, qm_ref, u2_ref, kt_ref, vt_ref, o_ref):
    step = pl.program_id(2)
    tq = KEY_TILE
    lane = lax.broadcasted_iota(jnp.int32, (tq, LANES), 1)
    row = lax.broadcasted_iota(jnp.int32, (tq, LANES), 0)
    lo_half = lane < SB_HEAD_DIM

    def load_tile(j):
        off = pl.multiple_of(j * KEY_TILE, KEY_TILE)
        return kt_ref[0, :, pl.ds(off, KEY_TILE)], vt_ref[0, :, pl.ds(off, KEY_TILE)]

    for sub in range(ATTN_STEP_TILES):
        m = step * ATTN_STEP_TILES + sub
        if sub == 0:
            first = jnp.where(step == 0, qm_ref[0], q16_ref[0])
            q = jnp.concatenate([first, qt_ref[0, :tq - N_META, :]], axis=0)
        else:
            q = qt_ref[0, sub * tq - N_META:(sub + 1) * tq - N_META, :]
        qs = _head_queries(q, lo_half)
        kd, vd = load_tile(m)
        diag = ('cols', kd.astype(BF16), vd.astype(BF16), lane < row)
        accs = _attend(qs, u2_ref[...], [diag], load_tile, m - 1, lane)
        out = jnp.concatenate(_pair_outputs(accs, lo_half, o_ref.dtype), axis=1)
        if sub == 0:
            @pl.when(step == 0)
            def _(out=out):
                o_ref[0, 0:tq - N_META, :] = out[N_META:, :]
                o_ref[0, o_ref.shape[1] - N_META:, :] = jnp.zeros((N_META, o_ref.shape[2]), o_ref.dtype)

            @pl.when(step > 0)
            def _(out=out, m=m):
                o_ref[0, pl.ds(pl.multiple_of(m * KEY_TILE - N_META, N_META), tq), :] = out
        else:
            o_ref[0, pl.ds(pl.multiple_of(m * KEY_TILE - N_META, N_META), tq), :] = out


def _attn_step_kernel(*refs, n_tail, n_cache_tiles):
    if n_cache_tiles:
        (q_ref, kd_ref, vd_ref, u2_ref, ktl_ref, vtl_ref, kn_ref, vn_ref, kt_any, vt_any, o_ref,
         kd_scr, vd_scr, kbuf, vbuf) = refs
        width = q_ref.shape[2]
        n_near = kn_ref.shape[2] // KEY_TILE

        def load_tile(j):
            if isinstance(j, int) and j >= n_cache_tiles - n_near:
                ls = slice((j - (n_cache_tiles - n_near)) * KEY_TILE, (j - (n_cache_tiles - n_near) + 1) * KEY_TILE)
                return kn_ref[0, :, ls], vn_ref[0, :, ls]
            off = j * KEY_TILE if isinstance(j, int) else pl.multiple_of(j * KEY_TILE, KEY_TILE)
            src = (pl.program_id(0), pl.ds(pl.program_id(1) * width, width), pl.ds(off, KEY_TILE))
            pltpu.sync_copy(kt_any.at[src], kbuf)
            pltpu.sync_copy(vt_any.at[src], vbuf)
            return kbuf[...], vbuf[...]
    else:
        q_ref, kd_ref, vd_ref, u2_ref, o_ref, kd_scr, vd_scr = refs
        load_tile = None
    tq = q_ref.shape[1]
    n_rows = kd_ref.shape[1]
    lane = lax.broadcasted_iota(jnp.int32, (tq, LANES), 1)
    row = lax.broadcasted_iota(jnp.int32, (tq, LANES), 0)
    lo_half = lane < SB_HEAD_DIM
    kd_scr[...] = jnp.zeros_like(kd_scr)
    vd_scr[...] = jnp.zeros_like(vd_scr)
    kd_scr[0:n_rows, :] = kd_ref[0].astype(BF16)
    vd_scr[0:n_rows, :] = vd_ref[0].astype(BF16)
    first = [('rows', kd_scr[...], vd_scr[...], lane < row)]
    if n_cache_tiles and n_tail:
        valid = lax.broadcasted_iota(jnp.int32, ktl_ref.shape[1:], 1) < n_tail
        zero = jnp.zeros(ktl_ref.shape[1:], BF16)
        first.append(('cols', jnp.where(valid, ktl_ref[0].astype(BF16), zero),
                      jnp.where(valid, vtl_ref[0].astype(BF16), zero), lane < n_tail))
    qs = _head_queries(q_ref[0], lo_half)
    accs = _attend(qs, u2_ref[...], first, load_tile, n_cache_tiles - 1, lane)
    for p, out in enumerate(_pair_outputs(accs, lo_half, o_ref.dtype)):
        o_ref[0, :, p * LANES:(p + 1) * LANES] = out


def _cumsum_rhs():
    s = np.arange(KEY_TILE)
    strict = (s[:, None] > s[None, :]).astype(np.float32)
    half = np.concatenate([strict, np.ones((KEY_TILE, KEY_TILE), np.float32)], axis=1)
    return jnp.asarray(np.concatenate([half, half], axis=0), BF16)


ATTN_PROMPT_WIDTH = 8 * SB_HEAD_DIM
ATTN_STEP_TILES = 2


def attention_prompt(big, big_meta, k_t, v_t):
    b, s, _ = big.shape
    width = ATTN_PROMPT_WIDTH
    qb0 = BIG_Q // width
    rows = ATTN_STEP_TILES * KEY_TILE
    per_step = rows // N_META
    n_pos = k_t.shape[2]
    cache = pl.BlockSpec((1, width, n_pos), lambda bi, hp, m: (bi, hp, 0))
    return pl.pallas_call(
        _attn_prompt_kernel,
        grid=(b, SB_WIDTH // width, s // rows),
        in_specs=[pl.BlockSpec((1, N_META, width), lambda bi, hp, m: (bi, jnp.maximum(m * per_step - 1, 0), qb0 + hp)),
                  pl.BlockSpec((1, rows, width), lambda bi, hp, m: (bi, m, qb0 + hp)),
                  pl.BlockSpec((1, N_META, width), lambda bi, hp, m: (0, 0, qb0 + hp)),
                  pl.BlockSpec((2 * KEY_TILE, 2 * KEY_TILE), lambda bi, hp, m: (0, 0)),
                  cache, cache],
        out_specs=pl.BlockSpec((1, s, width), lambda bi, hp, m: (bi, 0, hp)),
        out_shape=jax.ShapeDtypeStruct((b, s, SB_WIDTH), BF16),
        compiler_params=_cparams(("parallel", "parallel", "arbitrary")),
        name="stick_breaking_prompt",
    )(big, big, big_meta, _cumsum_rhs(), k_t, v_t)


def attention_step(big, q_row_blk, k_rows, v_rows, k_t, v_t, n_cache_tiles, n_tail, width):
    b = big.shape[0]
    n_rows = k_rows.shape[1]
    qb0 = BIG_Q // width
    rows = pl.BlockSpec((1, n_rows, width), lambda bi, hp: (bi, 0, hp))
    in_specs = [pl.BlockSpec((1, N_META, width), lambda bi, hp: (bi, q_row_blk, qb0 + hp)), rows, rows,
                pl.BlockSpec((2 * KEY_TILE, 2 * KEY_TILE), lambda bi, hp: (0, 0))]
    args = [big, k_rows, v_rows, _cumsum_rhs()]
    scratch = [pltpu.VMEM((KEY_TILE, width), BF16)] * 2
    if n_cache_tiles:
        near = ATTN_STATIC_TILES * KEY_TILE
        assert n_cache_tiles % ATTN_STATIC_TILES == 0
        near_blk = n_cache_tiles // ATTN_STATIC_TILES - 1
        in_specs += [pl.BlockSpec((1, width, KEY_TILE), lambda bi, hp: (bi, hp, n_cache_tiles))] * 2
        in_specs += [pl.BlockSpec((1, width, near), lambda bi, hp: (bi, hp, near_blk))] * 2
        in_specs += [pl.BlockSpec(memory_space=pl.ANY)] * 2
        args += [k_t, v_t, k_t, v_t, k_t, v_t]
        scratch += [pltpu.VMEM((width, KEY_TILE), F32)] * 2
    return pl.pallas_call(
        functools.partial(_attn_step_kernel, n_tail=n_tail, n_cache_tiles=n_cache_tiles),
        grid=(b, SB_WIDTH // width),
        in_specs=in_specs,
        out_specs=pl.BlockSpec((1, N_META, width), lambda bi, hp: (bi, 0, hp)),
        out_shape=jax.ShapeDtypeStruct((b, N_META, SB_WIDTH), BF16),
        scratch_shapes=scratch,
        compiler_params=_cparams(("parallel", "parallel")),
        name="stick_breaking_step",
    )(*args)


def _rows8(op, x, r8):
    return op(x.reshape(x.shape[0] // 8, 8, x.shape[1]), r8[None]).reshape(x.shape)


def _ssd_kernel(xbc_ref, x16_ref, z_ref, dt_ref, dtT_ref, prev_ref, h0_ref, cw_ref, cb_ref, dtb_ref, dtbT_ref,
                alog_ref, alogT_ref, dskip_ref, gn_ref, e3_ref, ltri_ref, ublk_ref, shift_ref,
                y_ref, hfin_ref, st_ref, *, rows, n_sub, n_chunks):
    Q = SSD_CHUNK
    c = pl.program_id(1)
    n_blk = SSD_WIDTH // LANES
    mul, sub = jnp.multiply, jnp.subtract

    @pl.when(c == 0)
    def _():
        for j in range(n_blk):
            st_ref[:, j * LANES:(j + 1) * LANES] = h0_ref[0, j * LANES:(j + 1) * LANES, :].T

    def pad_rows(v):
        if rows == Q:
            return v
        return jnp.concatenate([v, jnp.zeros((Q - rows, v.shape[1]), v.dtype)], axis=0)

    lane = lax.broadcasted_iota(jnp.int32, (Q, LANES), 1)
    rowq = lax.broadcasted_iota(jnp.int32, (Q, LANES), 0)
    causal2 = (lane % Q) <= rowq
    lo_half = lane < SSD_HEAD_DIM
    decay_rate = -LOG2E * jnp.exp(alog_ref[...])
    decay_rate_t = -LOG2E * jnp.exp(alogT_ref[...])

    for k in range(n_sub):
        r0 = k * Q
        hi, lo = _split2(prev_ref[0])
        if k == 0:
            hi = jnp.where(c == 0, hi, x16_ref[0])
            lo = jnp.where(c == 0, lo, jnp.zeros_like(lo))
        else:
            hi, lo = xbc_ref[0, r0 - 16:r0, :], jnp.zeros_like(lo)
        window = jnp.concatenate([pad_rows(xbc_ref[0, r0:r0 + rows, :]), hi, lo,
                                  jnp.zeros((Q - 32, SSD_CONV_CH), BF16)], axis=0)
        shifted = _dot(shift_ref[...], window)
        conv = _rows8(mul, shifted[0:Q], cw_ref[0:8, :])
        for i in range(1, SSD_CONV):
            conv = conv + _rows8(mul, shifted[i * Q:(i + 1) * Q], cw_ref[8 * i:8 * i + 8, :])
        conv = _rows8(jnp.add, conv, cb_ref[...])
        xc = conv * _sigmoid(conv)
        xs = xc[:, :SSD_WIDTH]
        b_all = xc[:, SSD_WIDTH:SSD_WIDTH + SSD_GROUPS * SSD_STATE]
        c_all = xc[:, SSD_WIDTH + SSD_GROUPS * SSD_STATE:]

        dt = _softplus(pad_rows(dt_ref[0, r0:r0 + rows, :]) + dtb_ref[...])
        dt_t = _softplus(dtT_ref[0, k] + dtbT_ref[...])
        if rows < Q:
            rowi = lax.broadcasted_iota(jnp.int32, (Q, 1), 0)
            dt = jnp.where(rowi < rows, dt, 0.0)
            xs = jnp.where(rowi < rows, xs, 0.0)
            lane_t = lax.broadcasted_iota(jnp.int32, dt_t.shape, 1)
            dt_t = jnp.where(lane_t % Q < rows, dt_t, 0.0)
        da = dt * decay_rate
        da_t = dt_t * decay_rate_t

        a_cum = _dot(ltri_ref[...], jnp.concatenate(_split3(da), axis=0))
        a_cum_t = _dot(jnp.concatenate(_split3(da_t), axis=1), ublk_ref[...])
        dt_exp = _dot(jnp.concatenate(_split2(dt), axis=1), e3_ref[0:2 * LANES, :])
        a_exp = _dot(jnp.concatenate(_split3(a_cum), axis=1), e3_ref[...])
        a_last = jnp.broadcast_to(a_exp[Q - 1:Q, :], (8, SSD_WIDTH))

        xdt = xs * dt_exp
        xdt_b = xdt.astype(BF16)
        xw = (xdt * jnp.exp2(-_rows8(sub, a_exp, a_last))).astype(BF16)
        xw_pad = jnp.concatenate([xw, jnp.zeros_like(xw)], axis=0)
        chunk_decay = jnp.exp2(a_last)
        grow = jnp.exp2(a_exp)

        y_parts = []
        for g in range(SSD_GROUPS):
            bg = b_all[:, g * SSD_STATE:(g + 1) * SSD_STATE]
            cg = c_all[:, g * SSD_STATE:(g + 1) * SSD_STATE].astype(BF16)
            bg_b = bg.astype(BF16)
            cb2 = _dot_nt(cg, jnp.concatenate([bg_b, bg_b], axis=0))
            gs = slice(g * GROUP_W, (g + 1) * GROUP_W)
            st_g = st_ref[:, gs]
            y_off = _dot(cg, st_g.astype(BF16)) * grow[:, gs]
            bg_t = jnp.concatenate([bg, jnp.zeros_like(bg)], axis=0).T.astype(BF16)
            st_ref[:, gs] = _rows8(mul, st_g, chunk_decay[:, gs]) + _dot(bg_t, xw_pad[:, gs])
            pair_out = []
            for kk in range(GROUP_W // LANES):
                i = g * (GROUP_W // LANES) + kk
                ps = slice(i * LANES, (i + 1) * LANES)
                a_row = jnp.broadcast_to(a_cum_t[i:i + 1, :], (8, LANES))
                decay = jnp.exp2(jnp.minimum(_rows8(sub, a_exp[:, ps], a_row), 0.0))
                m2 = jnp.where(causal2, cb2 * decay, 0.0).astype(BF16)
                xp = xdt_b[:, ps]
                zero = jnp.zeros_like(xp)
                xbd = jnp.concatenate([jnp.where(lo_half, xp, zero), jnp.where(lo_half, zero, xp)], axis=0)
                pair_out.append(_dot(m2, xbd))
            y_diag = jnp.concatenate(pair_out, axis=1)
            y = y_diag + y_off + _rows8(mul, xs[:, gs], dskip_ref[:, gs])
            zg = pad_rows(z_ref[0, r0:r0 + rows, gs].astype(F32))
            y = y * (zg * _sigmoid(zg))
            ms = jnp.mean(y * y, axis=-1, keepdims=True)
            y_parts.append(_rows8(mul, y * lax.rsqrt(ms + EPS), gn_ref[:, gs]))
        y_all = jnp.concatenate(y_parts, axis=1)
        y_ref[0, r0:r0 + rows, :] = y_all[:rows].astype(y_ref.dtype)

    @pl.when(c == n_chunks - 1)
    def _():
        for j in range(n_blk):
            hfin_ref[0, j * LANES:(j + 1) * LANES, :] = st_ref[:, j * LANES:(j + 1) * LANES].T


def _ssd_constants():
    Q = SSD_CHUNK
    e = np.zeros((LANES, SSD_WIDTH), np.float32)
    for h in range(SSD_HEADS):
        e[h, h * SSD_HEAD_DIM:(h + 1) * SSD_HEAD_DIM] = 1.0
    e3 = np.concatenate([e, e, e], axis=0)
    t = np.arange(Q)
    ltri = (t[:, None] >= t[None, :]).astype(np.float32)
    ltri3 = np.concatenate([ltri, ltri, ltri], axis=1)
    ublk = np.zeros((LANES, LANES), np.float32)
    ublk[:Q, :Q] = ltri.T
    ublk[Q:, Q:] = ltri.T
    ublk3 = np.concatenate([ublk, ublk, ublk], axis=0)
    shift = np.zeros((SSD_CONV * Q, 2 * Q), np.float32)
    for i in range(SSD_CONV):
        for r in range(Q):
            src = r - (SSD_CONV - 1 - i)
            if src >= 0:
                shift[i * Q + r, src] = 1.0
            else:
                shift[i * Q + r, Q + 16 + src] = 1.0
                shift[i * Q + r, Q + 32 + src] = 1.0
    return jnp.asarray(e3, BF16), jnp.asarray(ltri3, BF16), jnp.asarray(ublk3, BF16), jnp.asarray(shift, BF16)


def ssd_mixer(big, xbc_blk, z_blk, dt_raw, conv_prev, h0, p):
    b, s, _ = big.shape
    Q = SSD_CHUNK
    rows = min(s, Q)
    n_sub = SSD_STEP_CHUNKS if s % (SSD_STEP_CHUNKS * Q) == 0 else 1
    blk = rows * n_sub
    n_chunks = -(-s // blk)
    dtp = dt_raw[:, :, :SSD_HEADS]
    if s < Q:
        dtp = jnp.pad(dtp, ((0, 0), (0, Q - s), (0, 0)))
    dt_t = dtp.reshape(b, n_chunks * n_sub, Q, SSD_HEADS).transpose(0, 1, 3, 2).reshape(
        b, n_chunks * n_sub, SSD_HEADS // 2, 2 * Q)
    e3, ltri3, ublk3, shift = _ssd_constants()
    prev16 = jnp.pad(conv_prev, ((0, 0), (16 - (SSD_CONV - 1), 0), (0, 0)))
    rep8 = lambda v: jnp.repeat(v, 8, axis=0)
    const = lambda shape: pl.BlockSpec(shape, lambda bi, ci: (0,) * len(shape))
    in_specs = [
        pl.BlockSpec((1, blk, SSD_CONV_CH), lambda bi, ci: (bi, ci, xbc_blk)),
        pl.BlockSpec((1, 16, SSD_CONV_CH), lambda bi, ci: (bi, jnp.maximum(ci * (blk // 16) - 1, 0), xbc_blk)),
        pl.BlockSpec((1, blk, SSD_WIDTH), lambda bi, ci: (bi, ci, z_blk)),
        pl.BlockSpec((1, blk, LANES), lambda bi, ci: (bi, ci, 0)),
        pl.BlockSpec((1, n_sub, SSD_HEADS // 2, 2 * Q), lambda bi, ci: (bi, ci, 0, 0)),
        pl.BlockSpec((1, 16, SSD_CONV_CH), lambda bi, ci: (bi, 0, 0)),
        pl.BlockSpec((1, SSD_WIDTH, SSD_STATE), lambda bi, ci: (bi, 0, 0)),
        const((8 * SSD_CONV, SSD_CONV_CH)), const((8, SSD_CONV_CH)),
        const((1, LANES)), const((SSD_HEADS // 2, 2 * Q)),
        const((1, LANES)), const((SSD_HEADS // 2, 2 * Q)),
        const((8, SSD_WIDTH)), const((8, SSD_WIDTH)),
        const(e3.shape), const(ltri3.shape), const(ublk3.shape), const(shift.shape),
    ]
    y, h_fin = pl.pallas_call(
        functools.partial(_ssd_kernel, rows=rows, n_sub=n_sub, n_chunks=n_chunks),
        grid=(b, n_chunks),
        in_specs=in_specs,
        out_specs=[pl.BlockSpec((1, blk, SSD_WIDTH), lambda bi, ci: (bi, ci, 0)),
                   pl.BlockSpec((1, SSD_WIDTH, SSD_STATE), lambda bi, ci: (bi, 0, 0))],
        out_shape=[jax.ShapeDtypeStruct((b, s, SSD_WIDTH), BF16),
                   jax.ShapeDtypeStruct((b, SSD_WIDTH, SSD_STATE), F32)],
        scratch_shapes=[pltpu.VMEM((SSD_STATE, SSD_WIDTH), F32)],
        compiler_params=_cparams(("parallel", "arbitrary")),
        name="ssd_mixer",
    )(big, big, big, dt_raw, dt_t, prev16, h0, rep8(p['conv_w_ssd']), rep8(p['conv_b_ssd']), p['dt_bias'], p['dt_bias_t'],
      p['a_log'], p['a_log_t'], rep8(p['d_skip']), rep8(p['g_ssd_norm']), e3, ltri3, ublk3, shift)
    return y, h_fin


def _merge_kernel(att_ref, ssd_ref, ga_ref, gs_ref, h_ref, wa_ref, ws_ref, wo_ref, o_ref, wa_b, ws_b, wo_b):
    @pl.when(pl.program_id(0) == 0)
    def _():
        wa_b[...] = wa_ref[...].astype(BF16)
        ws_b[...] = ws_ref[...].astype(BF16)
        wo_b[...] = wo_ref[...].astype(BF16)

    a = _dot(att_ref[...], wa_b[...])
    s = _dot(ssd_ref[...], ws_b[...])
    merged = _sigmoid(ga_ref[...].astype(F32)) * a + _sigmoid(gs_ref[...].astype(F32)) * s
    o_ref[...] = h_ref[...] + _dot(merged.astype(BF16), wo_b[...])


def _resident(a):
    return pl.BlockSpec(a.shape, lambda i: (0,) * a.ndim, pipeline_mode=pl.Buffered(1))


def merge(att, ssd, big, h, p):
    m = h.shape[0]
    tm = _pick(m, (512, 256, 128))
    row = lambda w: pl.BlockSpec((tm, w), lambda i: (i, 0))
    gate_blk = BIG_GATE // D_MODEL
    ws = [p['w_br_att'], p['w_br_ssd'], p['w_out']]
    return pl.pallas_call(
        _merge_kernel,
        grid=(m // tm,),
        in_specs=[row(SB_WIDTH), row(SSD_WIDTH),
                  pl.BlockSpec((tm, D_MODEL), lambda i: (i, gate_blk)),
                  pl.BlockSpec((tm, D_MODEL), lambda i: (i, gate_blk + 1)),
                  row(D_MODEL)] + [_resident(w) for w in ws],
        out_specs=row(D_MODEL),
        out_shape=jax.ShapeDtypeStruct((m, D_MODEL), F32),
        scratch_shapes=[pltpu.VMEM(w.shape, BF16) for w in ws],
        compiler_params=_cparams(("arbitrary",)),
        name="merge_out_proj",
    )(att, ssd, big, big, h, *ws)


def _ffn_act_kernel(ug_ref, uv_ref, pg_ref, pv_ref, wg_ref, wv_ref, bg_ref, bv_ref, o_ref, gbuf_ref, vbuf_ref,
                    *, ts):
    t = pl.program_id(1)
    k1 = FFN_CONV - 1

    def conv(u_ref, p_ref, w_ref, b_ref, buf_ref):
        @pl.when(t == 0)
        def _():
            buf_ref[8 - k1:8, :] = p_ref[0]

        buf_ref[8:8 + ts, :] = u_ref[0].astype(F32)
        y = b_ref[...]
        for i in range(FFN_CONV):
            y = y + buf_ref[8 - k1 + i:8 - k1 + i + ts, :] * w_ref[i:i + 1, :]
        buf_ref[8 - k1:8, :] = buf_ref[8 + ts - k1:8 + ts, :]
        return y

    gate = conv(ug_ref, pg_ref, wg_ref, bg_ref, gbuf_ref)
    val = conv(uv_ref, pv_ref, wv_ref, bv_ref, vbuf_ref)
    o_ref[0] = (gate * _sigmoid(gate) * val).astype(o_ref.dtype)


UPF_TM = 512
UPF_TN = D_FF // 2


def _up_ffn_kernel(h_ref, h16_ref, g_ref, wg_ref, wv_ref, pg_ref, pv_ref, cwg_ref, cwv_ref, cbg_ref, cbv_ref,
                   o_ref, gl_ref, vl_ref, wg_b, wv_b, gbuf_ref, vbuf_ref):
    i = pl.program_id(2)
    tm = h_ref.shape[1]

    @pl.when(jnp.logical_and(pl.program_id(1) == 0, i == 0))
    def _():
        wg_b[...] = wg_ref[...].astype(BF16)
        wv_b[...] = wv_ref[...].astype(BF16)

    u = _rms_norm_bf16(h_ref[0], g_ref[...])
    u16 = _rms_norm_bf16(h16_ref[0], g_ref[...])

    def conv(w_b, p_ref, last_ref, buf_ref, cw_ref, cb_ref):
        up = _dot(u, w_b[...])
        buf_ref[0:16, :] = jnp.where(i == 0, p_ref[0], _dot(u16, w_b[...]))
        buf_ref[16:16 + tm, :] = up
        last_ref[0] = up[tm - 16:tm, :]
        y = _rows8(jnp.multiply, up, cw_ref[8 * (FFN_CONV - 1):8 * FFN_CONV, :])
        for t in range(FFN_CONV - 1):
            r0 = 16 - (FFN_CONV - 1) + t
            y = y + _rows8(jnp.multiply, buf_ref[r0:r0 + tm, :], cw_ref[8 * t:8 * t + 8, :])
        return _rows8(jnp.add, y, cb_ref[...])

    gate = conv(wg_b, pg_ref, gl_ref, gbuf_ref, cwg_ref, cbg_ref)
    val = conv(wv_b, pv_ref, vl_ref, vbuf_ref, cwv_ref, cbv_ref)
    o_ref[0] = (gate * _sigmoid(gate) * val).astype(o_ref.dtype)


def up_ffn_long(h, prev, p):
    b, s, d = h.shape
    tm, tn = UPF_TM, UPF_TN
    nj = D_FF // tn
    prev16 = jnp.pad(prev, ((0, 0), (16 - (FFN_CONV - 1), 0), (0, 0)))
    rep8 = lambda v: jnp.repeat(v, 8, axis=0)
    w8, b8 = rep8(p['conv_w_ffn']), rep8(p['conv_b_ffn'])
    cols = lambda rows, half: pl.BlockSpec((rows, tn), lambda j, bi, i: (0, half * nj + j))
    w_spec = lambda half: pl.BlockSpec((d, tn), lambda j, bi, i: (0, half * nj + j), pipeline_mode=pl.Buffered(1))
    prevs = lambda half: pl.BlockSpec((1, 16, tn), lambda j, bi, i: (bi, 0, half * nj + j))
    last_spec = pl.BlockSpec((1, 16, tn), lambda j, bi, i: (bi, 0, j))
    return pl.pallas_call(
        _up_ffn_kernel,
        grid=(nj, b, s // tm),
        in_specs=[pl.BlockSpec((1, tm, d), lambda j, bi, i: (bi, i, 0)),
                  pl.BlockSpec((1, 16, d), lambda j, bi, i: (bi, jnp.maximum(i * (tm // 16) - 1, 0), 0)),
                  pl.BlockSpec((1, d), lambda j, bi, i: (0, 0)),
                  w_spec(0), w_spec(1), prevs(0), prevs(1),
                  cols(8 * FFN_CONV, 0), cols(8 * FFN_CONV, 1), cols(8, 0), cols(8, 1)],
        out_specs=[pl.BlockSpec((1, tm, tn), lambda j, bi, i: (bi, i, j)), last_spec, last_spec],
        out_shape=[jax.ShapeDtypeStruct((b, s, D_FF), BF16)] + [jax.ShapeDtypeStruct((b, 16, D_FF), F32)] * 2,
        scratch_shapes=[pltpu.VMEM((d, tn), BF16)] * 2 + [pltpu.VMEM((16 + tm, tn), F32)] * 2,
        compiler_params=_cparams(("arbitrary", "arbitrary", "arbitrary")),
        name="up_proj_conv_ffn",
    )(h, h, p['g_ffn'], p['w_up'], p['w_up'], prev16, prev16, w8, w8, b8, b8)


def ffn_act(up, prev, p):
    b, s, _ = up.shape
    ts = _pick(s, (512, 256, 128))
    half = lambda blk: pl.BlockSpec((1, ts, D_FF), lambda bi, ti: (bi, ti, blk))
    prevs = lambda blk: pl.BlockSpec((1, FFN_CONV - 1, D_FF), lambda bi, ti: (bi, 0, blk))
    wspec = lambda blk: pl.BlockSpec((FFN_CONV, D_FF), lambda bi, ti: (0, blk))
    bspec = lambda blk: pl.BlockSpec((1, D_FF), lambda bi, ti: (0, blk))
    return pl.pallas_call(
        functools.partial(_ffn_act_kernel, ts=ts),
        grid=(b, s // ts),
        in_specs=[half(0), half(1), prevs(0), prevs(1), wspec(0), wspec(1), bspec(0), bspec(1)],
        out_specs=pl.BlockSpec((1, ts, D_FF), lambda bi, ti: (bi, ti, 0)),
        out_shape=jax.ShapeDtypeStruct((b, s, D_FF), BF16),
        scratch_shapes=[pltpu.VMEM((8 + ts, D_FF), F32)] * 2,
        compiler_params=_cparams(("parallel", "arbitrary")),
        name="ffn_conv_act",
    )(up, up, prev, prev, p['conv_w_ffn'], p['conv_w_ffn'], p['conv_b_ffn'], p['conv_b_ffn'])


def _down_kernel(a_ref, w_ref, h_ref, g_ref, o_ref, w_b):
    @pl.when(pl.program_id(0) == 0)
    def _():
        w_b[...] = w_ref[...].astype(BF16)

    h = h_ref[...] + _dot(a_ref[...], w_b[...])
    ms = jnp.mean(h * h, axis=-1, keepdims=True)
    o_ref[...] = h * lax.rsqrt(ms + EPS) * g_ref[...]


def down_norm(act, h, p):
    m = h.shape[0]
    tm = _pick(m, (512, 256, 128))
    return pl.pallas_call(
        _down_kernel,
        grid=(m // tm,),
        in_specs=[pl.BlockSpec((tm, D_FF), lambda i: (i, 0)),
                  _resident(p['w_down']),
                  pl.BlockSpec((tm, D_MODEL), lambda i: (i, 0)),
                  pl.BlockSpec((1, D_MODEL), lambda i: (0, 0))],
        out_specs=pl.BlockSpec((tm, D_MODEL), lambda i: (i, 0)),
        out_shape=jax.ShapeDtypeStruct((m, D_MODEL), F32),
        scratch_shapes=[pltpu.VMEM(p['w_down'].shape, BF16)],
        compiler_params=_cparams(("arbitrary",)),
        name="down_proj_norm",
    )(act, p['w_down'], h, p['g_final'])


def ffn_and_out(big3, att, ssd, hf, ssd_conv_prev, ffn_conv_prev, p):
    b, s, _ = big3.shape
    m = b * s
    h1 = merge(att.reshape(m, SB_WIDTH), ssd.reshape(m, SSD_WIDTH), big3.reshape(m, BIG_W), hf, p)
    if s % UPF_TM == 0:
        act, up_g, up_v = up_ffn_long(h1.reshape(b, s, D_MODEL), ffn_conv_prev, p)
        up_last = jnp.concatenate([up_g, up_v], axis=2)
    else:
        up_last = norm_mm(h1, p['g_ffn'], p['w_up'], BF16).reshape(b, s, 2 * D_FF)
        act = ffn_act(up_last, ffn_conv_prev, p)
    y = down_norm(act.reshape(m, D_FF), h1, p).reshape(b, s, D_MODEL)
    xbc_rows = jnp.concatenate([ssd_conv_prev, big3[:, :, BIG_XBC:BIG_XBC + SSD_CONV_CH][:, -(SSD_CONV - 1):].astype(F32)],
                               axis=1)[:, -(SSD_CONV - 1):]
    up_rows = jnp.concatenate([ffn_conv_prev, up_last[:, -(FFN_CONV - 1):].astype(F32)], axis=1)[:, -(FFN_CONV - 1):]
    return y, xbc_rows, up_rows


def _prep_params(g_mix, w_in, conv_w_ssd, conv_b_ssd, dt_bias, a_log, d_skip, g_ssd_norm, w_br_att, w_br_ssd,
                 w_out, g_ffn, w_up, conv_w_ffn, conv_b_ffn, w_down, g_final):
    w_t = w_in.T
    dt0 = 3 * SB_WIDTH + SSD_WIDTH + SSD_CONV_CH
    w_tail_t = jnp.concatenate([w_t[dt0 + SSD_HEADS:], w_t[dt0:dt0 + SSD_HEADS],
                                jnp.zeros((LANES - SSD_HEADS, D_MODEL), F32)], axis=0)
    Q = SSD_CHUNK
    lanes_t = lambda v: jnp.repeat(v.reshape(SSD_HEADS // 2, 2, 1), Q, axis=2).reshape(SSD_HEADS // 2, 2 * Q)
    pad_l = lambda v: jnp.pad(v, (0, LANES - SSD_HEADS)).reshape(1, LANES)
    return {
        'g_mix': g_mix.reshape(1, -1), 'g_ffn': g_ffn.reshape(1, -1), 'g_final': g_final.reshape(1, -1),
        'w_t': w_t, 'w_tail_t': w_tail_t,
        'conv_w_ssd': conv_w_ssd, 'conv_b_ssd': conv_b_ssd.reshape(1, -1),
        'dt_bias': pad_l(dt_bias), 'dt_bias_t': lanes_t(dt_bias),
        'a_log': pad_l(a_log), 'a_log_t': lanes_t(a_log),
        'd_skip': jnp.repeat(d_skip, SSD_HEAD_DIM).reshape(1, -1), 'g_ssd_norm': g_ssd_norm.reshape(1, -1),
        'w_br_att': w_br_att, 'w_br_ssd': w_br_ssd, 'w_out': w_out,
        'w_up': w_up, 'conv_w_ffn': conv_w_ffn, 'conv_b_ffn': conv_b_ffn.reshape(1, -1),
        'w_down': w_down,
    }


def kernel(x_prompt, x_sample, cache_k, cache_v, state_ssm, state_ssm_conv, state_ffn_conv, meta_tokens, g_mix, w_in, conv_w_ssd, conv_b_ssd, dt_bias, a_log, d_skip, g_ssd_norm, w_br_att, w_br_ssd, w_out, g_ffn, w_up, conv_w_ffn, conv_b_ffn, w_down, g_final):
    bp, seq, _ = x_prompt.shape
    bs, dec = x_sample.shape[:2]
    assert dec == N_META and seq % KVT_TS == 0
    p = _prep_params(g_mix[0], w_in[0], conv_w_ssd[0], conv_b_ssd[0], dt_bias[0], a_log[0], d_skip[0],
                     g_ssd_norm[0], w_br_att[0], w_br_ssd[0], w_out[0], g_ffn[0], w_up[0], conv_w_ffn[0],
                     conv_b_ffn[0], w_down[0], g_final)
    proj = functools.partial(in_proj, g=p['g_mix'], w_t=p['w_t'], w_tail_t=p['w_tail_t'])
    state_shape = (SSD_WIDTH, SSD_STATE)
    rows = lambda a, b: a.reshape(b, -1, a.shape[-1])

    n_rows = cache_k.shape[2]
    n_tiles = (n_rows - N_META) // KEY_TILE
    assert n_tiles * KEY_TILE + N_META == n_rows
    x16 = jnp.concatenate([meta_tokens, x_sample.reshape(bs * dec, D_MODEL)], axis=0)
    k16, v16, big16, dt16 = (rows(a, 1 + bs) for a in proj(x16, with_kv=True))
    big_m, k_m, v_m, big_s, k_s, v_s = big16[:1], k16[:1], v16[:1], big16[1:], k16[1:], v16[1:]
    att_m = attention_step(big_m, 0, k_m, v_m, None, None, 0, 0, SB_WIDTH)
    cache_t = lambda c: c[0].transpose(0, 2, 3, 1).reshape(bs, SB_WIDTH, n_rows)
    att_s = attention_step(big_s, 0, k_s, v_s, cache_t(cache_k), cache_t(cache_v), n_tiles, N_META, SB_WIDTH)
    conv16 = jnp.concatenate([jnp.zeros((1, SSD_CONV - 1, SSD_CONV_CH), F32), state_ssm_conv[0]], axis=0)
    ffn16 = jnp.concatenate([jnp.zeros((1, FFN_CONV - 1, 2 * D_FF), F32), state_ffn_conv[0]], axis=0)
    xbc = BIG_XBC // SSD_CONV_CH, BIG_Z // SSD_WIDTH
    ssd_m, ssm_m = ssd_mixer(big_m, *xbc, dt16[:1], conv16[:1], jnp.zeros((1,) + state_shape, F32), p)
    ssd_s, ssm_s = ssd_mixer(big_s, *xbc, dt16[1:], conv16[1:], state_ssm[0].reshape((bs,) + state_shape), p)
    y16, conv16, ffn16 = ffn_and_out(big16, jnp.concatenate([att_m, att_s], axis=0),
                                     jnp.concatenate([ssd_m, ssd_s], axis=0), x16, conv16, ffn16, p)
    conv_m, ffn_m, y_sample, conv_s, ffn_s = conv16[:1], ffn16[:1], y16[1:], conv16[1:], ffn16[1:]

    xf = x_prompt.reshape(bp * seq, D_MODEL)
    big_x, dt_x = proj(xf, with_kv=False)
    big_x = rows(big_x, bp)
    k_t, v_t, k_end, v_end = kv_transposed(x_prompt, meta_tokens, p['g_mix'], p['w_t'])
    att_x = attention_prompt(big_x, big_m, k_t, v_t)
    att_end = attention_step(big_x, seq // N_META - 1, k_end, v_end, k_t, v_t, seq // KEY_TILE, 0, ATTN_PROMPT_WIDTH)
    att_x = lax.dynamic_update_slice(att_x, att_end, (0, seq - N_META, 0))
    rep = lambda a: jnp.broadcast_to(a, (bp,) + a.shape[1:])
    ssd_x, ssm_p = ssd_mixer(big_x, *xbc, rows(dt_x, bp), rep(conv_m), rep(ssm_m), p)
    y_prompt, conv_p, ffn_p = ffn_and_out(big_x, att_x, ssd_x, xf, rep(conv_m), rep(ffn_m), p)

    heads = lambda a: a.reshape(a.shape[0], a.shape[1], SB_HEADS, SB_HEAD_DIM)[None]
    heads_t = lambda a: a.reshape(a.shape[0], SB_HEADS, SB_HEAD_DIM, a.shape[2]).transpose(0, 3, 1, 2)[None]
    state5 = lambda a: a.reshape(a.shape[0], SSD_HEADS, SSD_HEAD_DIM, SSD_STATE)[None]
    return (y_prompt, y_sample, heads_t(k_t), heads_t(v_t), state5(ssm_p), conv_p[None], ffn_p[None],
            heads(k_s), heads(v_s), state5(ssm_s), conv_s[None], ffn_s[None])
```

```python
import functools

import numpy as np
import jax
import jax.numpy as jnp
from jax import lax
from jax.experimental import pallas as pl
from jax.experimental.pallas import tpu as pltpu

F32 = jnp.float32
BF16 = jnp.bfloat16

D_MODEL = 1024
N_META = 16
SB_HEADS = 16
SB_HEAD_DIM = 64
SB_WIDTH = SB_HEADS * SB_HEAD_DIM
SSD_WIDTH = 2 * D_MODEL
SSD_HEAD_DIM = 64
SSD_HEADS = SSD_WIDTH // SSD_HEAD_DIM
SSD_GROUPS = 4
SSD_STATE = 128
SSD_CONV = 4
SSD_CONV_CH = SSD_WIDTH + 2 * SSD_GROUPS * SSD_STATE
D_FF = 2816
FFN_CONV = 3
EPS = 1e-6

LANES = 128
SSD_CHUNK = 64
SSD_STEP_CHUNKS = 4
KEY_TILE = 128
GROUP_W = SSD_WIDTH // SSD_GROUPS
DEAD_LOG = -104.0
LOG2E = 1.4426950408889634
ATTN_STATIC_TILES = 2
ATTN_GROUP_TILES = 3
VMEM_LIMIT = 56 * 1024 * 1024


def _cparams(sem):
    return pltpu.CompilerParams(dimension_semantics=sem, vmem_limit_bytes=VMEM_LIMIT)


def _pick(n, cands):
    for c in cands:
        if n % c == 0:
            return c
    return n


def _split2(x):
    hi = x.astype(BF16)
    lo = (x - hi.astype(F32)).astype(BF16)
    return hi, lo


def _split3(x):
    hi = x.astype(BF16)
    r = x - hi.astype(F32)
    mid = r.astype(BF16)
    lo = (r - mid.astype(F32)).astype(BF16)
    return hi, mid, lo


def _dot(a, b):
    return jnp.dot(a, b, preferred_element_type=F32)


def _dot_nt(a, b):
    return lax.dot_general(a, b, (((1,), (1,)), ((), ())), preferred_element_type=F32)


def _softplus(x):
    return jnp.maximum(x, 0.0) + jnp.log(1.0 + jnp.exp(-jnp.abs(x)))


def _sigmoid(x):
    return 1.0 / (1.0 + jnp.exp2(-LOG2E * x))


def _rms_norm_bf16(x, g):
    ms = jnp.mean(x * x, axis=-1, keepdims=True)
    return (x * lax.rsqrt(ms + EPS) * g).astype(BF16)


def _norm_mm_kernel(x_ref, g_ref, w_ref, o_ref, u_ref):
    @pl.when(pl.program_id(1) == 0)
    def _():
        u_ref[...] = _rms_norm_bf16(x_ref[...], g_ref[...])

    o_ref[...] = _dot(u_ref[...], w_ref[...].astype(BF16)).astype(o_ref.dtype)


PROJ_TN = 1024
UP_TN = 1408


def norm_mm(x, g, w, out_dtype):
    m, d = x.shape
    n = w.shape[1]
    tm = _pick(m, (2048, 1024, 512, 256, 128))
    tn = UP_TN
    return pl.pallas_call(
        _norm_mm_kernel,
        grid=(m // tm, n // tn),
        in_specs=[pl.BlockSpec((tm, d), lambda i, j: (i, 0)),
                  pl.BlockSpec((1, d), lambda i, j: (0, 0)),
                  pl.BlockSpec((d, tn), lambda i, j: (0, j))],
        out_specs=pl.BlockSpec((tm, tn), lambda i, j: (i, j)),
        out_shape=jax.ShapeDtypeStruct((m, n), out_dtype),
        scratch_shapes=[pltpu.VMEM((tm, d), BF16)],
        compiler_params=_cparams(("parallel", "arbitrary")),
        name="norm_up_proj",
    )(x, g, w)


_NQ = SB_WIDTH // PROJ_TN
_J_K, _J_V, _J_Z = _NQ, 2 * _NQ, 3 * _NQ
_J_X = _J_Z + SSD_WIDTH // PROJ_TN
_J_G = _J_X + SSD_CONV_CH // PROJ_TN
_J_END = _J_G + 2 * D_MODEL // PROJ_TN
BIG_Z, BIG_Q, BIG_XBC, BIG_GATE = 0, SSD_WIDTH, SSD_WIDTH + SB_WIDTH, SSD_WIDTH + SB_WIDTH + SSD_CONV_CH
BIG_W = BIG_GATE + 2 * D_MODEL


def _big_block(j):
    return jnp.where(j < _J_K, BIG_Q // PROJ_TN + j,
                     jnp.where(j < _J_Z, BIG_Q // PROJ_TN + _NQ - 1,
                               jnp.where(j < _J_X, j - _J_Z + BIG_Z // PROJ_TN,
                                         jnp.where(j < _J_G, j - _J_X + BIG_XBC // PROJ_TN,
                                                   j - _J_G + BIG_GATE // PROJ_TN))))


def _sweep_step(j, with_kv):
    return j if with_kv else jnp.where(j >= _J_K, j + (_J_Z - _J_K), j)


def _in_proj_kernel(*refs, with_kv):
    if with_kv:
        x_ref, g_ref, win_ref, wgate_ref, wdt_ref, k_ref, v_ref, big_ref, dt_ref, u_ref = refs
    else:
        x_ref, g_ref, win_ref, wgate_ref, wdt_ref, big_ref, dt_ref, u_ref = refs
    j = _sweep_step(pl.program_id(1), with_kv)

    def proj(w_ref):
        return _dot_nt(u_ref[...], w_ref[...].astype(BF16))

    @pl.when(pl.program_id(1) == 0)
    def _():
        u_ref[...] = _rms_norm_bf16(x_ref[...], g_ref[...])
        dt_ref[...] = proj(wdt_ref)

    if with_kv:
        @pl.when(jnp.logical_and(j >= _J_K, j < _J_V))
        def _():
            k_ref[...] = proj(win_ref)

        @pl.when(jnp.logical_and(j >= _J_V, j < _J_Z))
        def _():
            v_ref[...] = proj(win_ref)

    @pl.when(jnp.logical_or(j < _J_K, jnp.logical_and(j >= _J_Z, j < _J_G)))
    def _():
        big_ref[...] = proj(win_ref).astype(BF16)

    @pl.when(j >= _J_G)
    def _():
        big_ref[...] = proj(wgate_ref).astype(BF16)


def in_proj(x, g, w_t, w_tail_t, with_kv):
    m, d = x.shape
    tm = _pick(m, (2048, 1024, 512, 256, 128))
    tn = PROJ_TN
    step = functools.partial(_sweep_step, with_kv=with_kv)
    clip = lambda j, lo, n: jnp.clip(step(j) - lo, 0, n - 1)
    kv_specs = [pl.BlockSpec((tm, tn), lambda i, j: (i, clip(j, _J_K, _NQ))),
                pl.BlockSpec((tm, tn), lambda i, j: (i, clip(j, _J_V, _NQ)))]
    kv_shapes = [jax.ShapeDtypeStruct((m, SB_WIDTH), F32)] * 2
    return pl.pallas_call(
        functools.partial(_in_proj_kernel, with_kv=with_kv),
        grid=(m // tm, _J_END if with_kv else _J_END - (_J_Z - _J_K)),
        in_specs=[pl.BlockSpec((tm, d), lambda i, j: (i, 0), pipeline_mode=pl.Buffered(1)),
                  pl.BlockSpec((1, d), lambda i, j: (0, 0)),
                  pl.BlockSpec((tn, d), lambda i, j: (jnp.minimum(step(j), _J_G - 1), 0)),
                  pl.BlockSpec((tn, d), lambda i, j: (clip(j, _J_G, _J_END - _J_G), 0)),
                  pl.BlockSpec((LANES, d), lambda i, j: (2 * D_MODEL // LANES, 0))],
        out_specs=(kv_specs if with_kv else []) + [
            pl.BlockSpec((tm, tn), lambda i, j: (i, _big_block(step(j)))),
            pl.BlockSpec((tm, LANES), lambda i, j: (i, 0))],
        out_shape=(kv_shapes if with_kv else []) + [
            jax.ShapeDtypeStruct((m, BIG_W), BF16), jax.ShapeDtypeStruct((m, LANES), F32)],
        scratch_shapes=[pltpu.VMEM((tm, d), BF16)],
        compiler_params=_cparams(("parallel", "arbitrary")),
        name="in_proj",
    )(x, g, w_t, w_tail_t, w_tail_t)


KVT_TS = 1024


def _kv_t_kernel(x16_ref, xt_ref, meta_ref, g_ref, wk_ref, wv_ref, kt_ref, vt_ref, kl_ref, vl_ref, wk_b, wv_b):
    j = pl.program_id(1)

    @pl.when(jnp.logical_and(pl.program_id(0) == 0, j == 0))
    def _():
        wk_b[...] = wk_ref[...].astype(BF16)
        wv_b[...] = wv_ref[...].astype(BF16)

    first = jnp.where(j == 0, meta_ref[...], x16_ref[0])
    x = jnp.concatenate([first, xt_ref[0, :KVT_TS - N_META, :]], axis=0)
    u = _rms_norm_bf16(x, g_ref[...])
    kt_ref[0] = _dot_nt(wk_b[...], u)
    vt_ref[0] = _dot_nt(wv_b[...], u)

    @pl.when(j == pl.num_programs(1) - 1)
    def _():
        kl_ref[0] = _dot_nt(u[0:N_META, :], wk_b[...])
        vl_ref[0] = _dot_nt(u[0:N_META, :], wv_b[...])


def kv_transposed(x, meta, g, w_t):
    b, s, d = x.shape
    n_pos = N_META + s
    n_tiles = -(-n_pos // KVT_TS)
    assert n_pos - (n_tiles - 1) * KVT_TS == N_META
    per_tile = KVT_TS // N_META
    out_spec = pl.BlockSpec((1, SB_WIDTH, KVT_TS), lambda bi, j: (bi, 0, j))
    last_spec = pl.BlockSpec((1, N_META, SB_WIDTH), lambda bi, j: (bi, 0, 0))
    w_spec = lambda blk: pl.BlockSpec((SB_WIDTH, d), lambda bi, j: (blk, 0), pipeline_mode=pl.Buffered(1))
    return pl.pallas_call(
        _kv_t_kernel,
        grid=(b, n_tiles),
        in_specs=[pl.BlockSpec((1, N_META, d), lambda bi, j: (bi, jnp.maximum(j * per_tile - 1, 0), 0)),
                  pl.BlockSpec((1, KVT_TS, d), lambda bi, j: (bi, jnp.minimum(j, s // KVT_TS - 1), 0)),
                  pl.BlockSpec((N_META, d), lambda bi, j: (0, 0)),
                  pl.BlockSpec((1, d), lambda bi, j: (0, 0)),
                  w_spec(1), w_spec(2)],
        out_specs=[out_spec, out_spec, last_spec, last_spec],
        out_shape=[jax.ShapeDtypeStruct((b, SB_WIDTH, n_pos), F32)] * 2
        + [jax.ShapeDtypeStruct((b, N_META, SB_WIDTH), F32)] * 2,
        scratch_shapes=[pltpu.VMEM((SB_WIDTH, d), BF16)] * 2,
        compiler_params=_cparams(("arbitrary", "arbitrary")),
        name="kv_transposed",
    )(x, x, meta, g, w_t, w_t)


def _stick_breaking_tiles(qs, u2, tiles, carries, accs):
    n_heads = len(qs)
    units = [(t, h) for t in range(len(tiles)) for h in range(n_heads)]
    ps = lambda h: slice((h // 2) * LANES, (h // 2 + 1) * LANES)

    def scores(t, h):
        kind, keys = tiles[t][0], tiles[t][1]
        return _dot_nt(qs[h], keys[:, ps(h)]) if kind == 'rows' else _dot(qs[h], keys[ps(h), :])

    z = {u: scores(*u) for u in units}
    log_beta, cat = {}, {}
    for u in units:
        mask = tiles[u[0]][3]
        soft = jnp.log(1.0 + jnp.exp2(-LOG2E * jnp.abs(z[u])))
        log_beta[u] = jnp.minimum(z[u], 0.0) - soft
        log_keep = log_beta[u] - z[u]
        if mask is not None:
            log_keep = jnp.where(mask, log_keep, 0.0)
        cat[u] = jnp.concatenate(_split2(log_keep), axis=1)
    r = {u: _dot(cat[u], u2) for u in units}
    carries, accs = list(carries), list(accs)
    w = {}
    for t, h in units:
        mask = tiles[t][3]
        wt = jnp.exp2(LOG2E * (log_beta[(t, h)] + r[(t, h)][:, :LANES] + carries[h]))
        if mask is not None:
            wt = jnp.where(mask, wt, 0.0)
        w[(t, h)] = wt.astype(BF16)
        carries[h] = carries[h] + r[(t, h)][:, LANES:]
    for t, h in units:
        kind, vals = tiles[t][0], tiles[t][2]
        pv = _dot(w[(t, h)], vals[:, ps(h)]) if kind == 'rows' else _dot_nt(w[(t, h)], vals[ps(h), :])
        accs[h] = accs[h] + pv
    return carries, accs


def _head_queries(q, lo_half):
    qs = []
    for h in range(q.shape[1] // SB_HEAD_DIM):
        qp = q[:, (h // 2) * LANES:(h // 2 + 1) * LANES]
        keep = lo_half if h % 2 == 0 else jnp.logical_not(lo_half)
        qs.append(jnp.where(keep, qp, jnp.zeros_like(qp)) * jnp.asarray(SB_HEAD_DIM ** -0.5, BF16))
    return qs


def _any_alive(carries):
    m = carries[0]
    for c in carries[1:]:
        m = jnp.maximum(m, c)
    return (jnp.max(m) > DEAD_LOG).astype(jnp.int32)


def _attend(qs, u2, first_tiles, load_tile, j0, lane):
    tq = qs[0].shape[0]

    def cache_tile(j):
        if isinstance(j, int):
            mask = None if j >= 0 else lane < 0
            kt, vt = load_tile(max(j, 0))
        else:
            mask = jnp.logical_and(j >= 0, lane >= 0)
            kt, vt = load_tile(jnp.maximum(j, 0))
        return 'cols', kt.astype(BF16), vt.astype(BF16), mask

    tiles = list(first_tiles)
    if load_tile is not None:
        for _ in range(ATTN_STATIC_TILES):
            tiles.append(cache_tile(j0))
            j0 = j0 - 1
    zeros = jnp.zeros((tq, LANES), F32)
    carries, accs = [zeros] * len(qs), [zeros] * len(qs)
    for g in range(0, len(tiles), ATTN_GROUP_TILES):
        carries, accs = _stick_breaking_tiles(qs, u2, tiles[g:g + ATTN_GROUP_TILES], carries, accs)
    if load_tile is not None:
        def cond(s):
            return jnp.logical_and(s[0] >= 0, s[1] > 0)

        def body(s):
            j, _, carries, accs = s
            carries, accs = _stick_breaking_tiles(qs, u2, [cache_tile(j)], carries, accs)
            return j - 1, _any_alive(carries), tuple(carries), tuple(accs)

        state = (jnp.asarray(j0, jnp.int32), _any_alive(carries), tuple(carries), tuple(accs))
        _, _, carries, accs = lax.while_loop(cond, body, state)
    return accs


def _pair_outputs(accs, lo_half, dtype):
    return [jnp.where(lo_half, accs[2 * p], accs[2 * p + 1]).astype(dtype) for p in range(len(accs) // 2)]


def _attn_prompt_kernel(q16_ref, qt_ref, qm_ref, u2_ref, kt_ref, vt_ref, o_ref):
    step = pl.program_id(2)
    tq = KEY_TILE
    lane = lax.broadcasted_iota(jnp.int32, (tq, LANES), 1)
    row = lax.broadcasted_iota(jnp.int32, (tq, LANES), 0)
    lo_half = lane < SB_HEAD_DIM

    def load_tile(j):
        off = pl.multiple_of(j * KEY_TILE, KEY_TILE)
        return kt_ref[0, :, pl.ds(off, KEY_TILE)], vt_ref[0, :, pl.ds(off, KEY_TILE)]

    for sub in range(ATTN_STEP_TILES):
        m = step * ATTN_STEP_TILES + sub
        if sub == 0:
            first = jnp.where(step == 0, qm_ref[0], q16_ref[0])
            q = jnp.concatenate([first, qt_ref[0, :tq - N_META, :]], axis=0)
        else:
            q = qt_ref[0, sub * tq - N_META:(sub + 1) * tq - N_META, :]
        qs = _head_queries(q, lo_half)
        kd, vd = load_tile(m)
        diag = ('cols', kd.astype(BF16), vd.astype(BF16), lane < row)
        accs = _attend(qs, u2_ref[...], [diag], load_tile, m - 1, lane)
        out = jnp.concatenate(_pair_outputs(accs, lo_half, o_ref.dtype), axis=1)
        if sub == 0:
            @pl.when(step == 0)
            def _(out=out):
                o_ref[0, 0:tq - N_META, :] = out[N_META:, :]
                o_ref[0, o_ref.shape[1] - N_META:, :] = jnp.zeros((N_META, o_ref.shape[2]), o_ref.dtype)

            @pl.when(step > 0)
            def _(out=out, m=m):
                o_ref[0, pl.ds(pl.multiple_of(m * KEY_TILE - N_META, N_META), tq), :] = out
        else:
            o_ref[0, pl.ds(pl.multiple_of(m * KEY_TILE - N_META, N_META), tq), :] = out


def _attn_step_kernel(*refs, n_tail, n_cache_tiles):
    if n_cache_tiles:
        (q_ref, kd_ref, vd_ref, u2_ref, ktl_ref, vtl_ref, kn_ref, vn_ref, kt_any, vt_any, o_ref,
         kd_scr, vd_scr, kbuf, vbuf) = refs
        width = q_ref.shape[2]
        n_near = kn_ref.shape[2] // KEY_TILE

        def load_tile(j):
            if isinstance(j, int) and j >= n_cache_tiles - n_near:
                ls = slice((j - (n_cache_tiles - n_near)) * KEY_TILE, (j - (n_cache_tiles - n_near) + 1) * KEY_TILE)
                return kn_ref[0, :, ls], vn_ref[0, :, ls]
            off = j * KEY_TILE if isinstance(j, int) else pl.multiple_of(j * KEY_TILE, KEY_TILE)
            src = (pl.program_id(0), pl.ds(pl.program_id(1) * width, width), pl.ds(off, KEY_TILE))
            pltpu.sync_copy(kt_any.at[src], kbuf)
            pltpu.sync_copy(vt_any.at[src], vbuf)
            return kbuf[...], vbuf[...]
    else:
        q_ref, kd_ref, vd_ref, u2_ref, o_ref, kd_scr, vd_scr = refs
        load_tile = None
    tq = q_ref.shape[1]
    n_rows = kd_ref.shape[1]
    lane = lax.broadcasted_iota(jnp.int32, (tq, LANES), 1)
    row = lax.broadcasted_iota(jnp.int32, (tq, LANES), 0)
    lo_half = lane < SB_HEAD_DIM
    n_free = n_tail if n_cache_tiles else 0
    kd_scr[...] = jnp.zeros_like(kd_scr)
    vd_scr[...] = jnp.zeros_like(vd_scr)
    if n_free:
        for scr, tl_ref in ((kd_scr, ktl_ref), (vd_scr, vtl_ref)):
            for c in range(width // LANES):
                scr[0:n_free, c * LANES:(c + 1) * LANES] = (
                    tl_ref[0, c * LANES:(c + 1) * LANES, :].T[0:n_free, :].astype(BF16))
    kd_scr[n_free:n_free + n_rows, :] = kd_ref[0].astype(BF16)
    vd_scr[n_free:n_free + n_rows, :] = vd_ref[0].astype(BF16)
    first = [('rows', kd_scr[...], vd_scr[...], lane < row + n_free)]
    qs = _head_queries(q_ref[0], lo_half)
    accs = _attend(qs, u2_ref[...], first, load_tile, n_cache_tiles - 1, lane)
    for p, out in enumerate(_pair_outputs(accs, lo_half, o_ref.dtype)):
        o_ref[0, :, p * LANES:(p + 1) * LANES] = out


def _cumsum_rhs():
    s = np.arange(KEY_TILE)
    strict = (s[:, None] > s[None, :]).astype(np.float32)
    half = np.concatenate([strict, np.ones((KEY_TILE, KEY_TILE), np.float32)], axis=1)
    return jnp.asarray(np.concatenate([half, half], axis=0), BF16)


ATTN_PROMPT_WIDTH = 8 * SB_HEAD_DIM
ATTN_STEP_TILES = 2


def attention_prompt(big, big_meta, k_t, v_t):
    b, s, _ = big.shape
    width = ATTN_PROMPT_WIDTH
    qb0 = BIG_Q // width
    rows = ATTN_STEP_TILES * KEY_TILE
    per_step = rows // N_META
    n_pos = k_t.shape[2]
    cache = pl.BlockSpec((1, width, n_pos), lambda bi, hp, m: (bi, hp, 0))
    return pl.pallas_call(
        _attn_prompt_kernel,
        grid=(b, SB_WIDTH // width, s // rows),
        in_specs=[pl.BlockSpec((1, N_META, width), lambda bi, hp, m: (bi, jnp.maximum(m * per_step - 1, 0), qb0 + hp)),
                  pl.BlockSpec((1, rows, width), lambda bi, hp, m: (bi, m, qb0 + hp)),
                  pl.BlockSpec((1, N_META, width), lambda bi, hp, m: (0, 0, qb0 + hp)),
                  pl.BlockSpec((2 * KEY_TILE, 2 * KEY_TILE), lambda bi, hp, m: (0, 0)),
                  cache, cache],
        out_specs=pl.BlockSpec((1, s, width), lambda bi, hp, m: (bi, 0, hp)),
        out_shape=jax.ShapeDtypeStruct((b, s, SB_WIDTH), BF16),
        compiler_params=_cparams(("parallel", "parallel", "arbitrary")),
        name="stick_breaking_prompt",
    )(big, big, big_meta, _cumsum_rhs(), k_t, v_t)


def attention_step(big, q_row_blk, k_rows, v_rows, k_t, v_t, n_cache_tiles, n_tail, width):
    b = big.shape[0]
    n_rows = k_rows.shape[1]
    qb0 = BIG_Q // width
    rows = pl.BlockSpec((1, n_rows, width), lambda bi, hp: (bi, 0, hp))
    in_specs = [pl.BlockSpec((1, N_META, width), lambda bi, hp: (bi, q_row_blk, qb0 + hp)), rows, rows,
                pl.BlockSpec((2 * KEY_TILE, 2 * KEY_TILE), lambda bi, hp: (0, 0))]
    args = [big, k_rows, v_rows, _cumsum_rhs()]
    scratch = [pltpu.VMEM((KEY_TILE, width), BF16)] * 2
    if n_cache_tiles:
        near = ATTN_STATIC_TILES * KEY_TILE
        assert n_cache_tiles % ATTN_STATIC_TILES == 0
        near_blk = n_cache_tiles // ATTN_STATIC_TILES - 1
        in_specs += [pl.BlockSpec((1, width, KEY_TILE), lambda bi, hp: (bi, hp, n_cache_tiles))] * 2
        in_specs += [pl.BlockSpec((1, width, near), lambda bi, hp: (bi, hp, near_blk))] * 2
        in_specs += [pl.BlockSpec(memory_space=pl.ANY)] * 2
        args += [k_t, v_t, k_t, v_t, k_t, v_t]
        scratch += [pltpu.VMEM((width, KEY_TILE), F32)] * 2
    return pl.pallas_call(
        functools.partial(_attn_step_kernel, n_tail=n_tail, n_cache_tiles=n_cache_tiles),
        grid=(b, SB_WIDTH // width),
        in_specs=in_specs,
        out_specs=pl.BlockSpec((1, N_META, width), lambda bi, hp: (bi, 0, hp)),
        out_shape=jax.ShapeDtypeStruct((b, N_META, SB_WIDTH), BF16),
        scratch_shapes=scratch,
        compiler_params=_cparams(("parallel", "parallel")),
        name="stick_breaking_step",
    )(*args)


def _rows8(op, x, r8):
    return op(x.reshape(x.shape[0] // 8, 8, x.shape[1]), r8[None]).reshape(x.shape)


def _ssd_kernel(xbc_ref, x16_ref, z_ref, dt_ref, dtT_ref, prev_ref, h0_ref, cw_ref, cb_ref, dtb_ref, dtbT_ref,
                alog_ref, alogT_ref, dskip_ref, gn_ref, e3_ref, ltri_ref, ublk_ref, shift_ref,
                y_ref, hfin_ref, st_ref, *, rows, n_sub, n_chunks):
    Q = SSD_CHUNK
    c = pl.program_id(1)
    n_blk = SSD_WIDTH // LANES
    mul, sub = jnp.multiply, jnp.subtract

    @pl.when(c == 0)
    def _():
        for j in range(n_blk):
            st_ref[:, j * LANES:(j + 1) * LANES] = h0_ref[0, j * LANES:(j + 1) * LANES, :].T

    def pad_rows(v):
        if rows == Q:
            return v
        return jnp.concatenate([v, jnp.zeros((Q - rows, v.shape[1]), v.dtype)], axis=0)

    lane = lax.broadcasted_iota(jnp.int32, (Q, LANES), 1)
    rowq = lax.broadcasted_iota(jnp.int32, (Q, LANES), 0)
    causal2 = (lane % Q) <= rowq
    lo_half = lane < SSD_HEAD_DIM
    decay_rate = -LOG2E * jnp.exp(alog_ref[...])
    decay_rate_t = -LOG2E * jnp.exp(alogT_ref[...])

    for k in range(n_sub):
        r0 = k * Q
        hi, lo = _split2(prev_ref[0])
        if k == 0:
            hi = jnp.where(c == 0, hi, x16_ref[0])
            lo = jnp.where(c == 0, lo, jnp.zeros_like(lo))
        else:
            hi, lo = xbc_ref[0, r0 - 16:r0, :], jnp.zeros_like(lo)
        window = jnp.concatenate([pad_rows(xbc_ref[0, r0:r0 + rows, :]), hi, lo,
                                  jnp.zeros((Q - 32, SSD_CONV_CH), BF16)], axis=0)
        shifted = _dot(shift_ref[...], window)
        conv = _rows8(mul, shifted[0:Q], cw_ref[0:8, :])
        for i in range(1, SSD_CONV):
            conv = conv + _rows8(mul, shifted[i * Q:(i + 1) * Q], cw_ref[8 * i:8 * i + 8, :])
        conv = _rows8(jnp.add, conv, cb_ref[...])
        xc = conv * _sigmoid(conv)
        xs = xc[:, :SSD_WIDTH]
        b_all = xc[:, SSD_WIDTH:SSD_WIDTH + SSD_GROUPS * SSD_STATE]
        c_all = xc[:, SSD_WIDTH + SSD_GROUPS * SSD_STATE:]

        dt = _softplus(pad_rows(dt_ref[0, r0:r0 + rows, :]) + dtb_ref[...])
        dt_t = _softplus(dtT_ref[0, k] + dtbT_ref[...])
        if rows < Q:
            rowi = lax.broadcasted_iota(jnp.int32, (Q, 1), 0)
            dt = jnp.where(rowi < rows, dt, 0.0)
            xs = jnp.where(rowi < rows, xs, 0.0)
            lane_t = lax.broadcasted_iota(jnp.int32, dt_t.shape, 1)
            dt_t = jnp.where(lane_t % Q < rows, dt_t, 0.0)
        da = dt * decay_rate
        da_t = dt_t * decay_rate_t

        a_cum = _dot(ltri_ref[...], jnp.concatenate(_split3(da), axis=0))
        a_cum_t = _dot(jnp.concatenate(_split3(da_t), axis=1), ublk_ref[...])
        dt_exp = _dot(jnp.concatenate(_split2(dt), axis=1), e3_ref[0:2 * LANES, :])
        a_exp = _dot(jnp.concatenate(_split3(a_cum), axis=1), e3_ref[...])
        a_last = jnp.broadcast_to(a_exp[Q - 1:Q, :], (8, SSD_WIDTH))

        xdt = xs * dt_exp
        xdt_b = xdt.astype(BF16)
        xw = (xdt * jnp.exp2(-_rows8(sub, a_exp, a_last))).astype(BF16)
        xw_pad = jnp.concatenate([xw, jnp.zeros_like(xw)], axis=0)
        chunk_decay = jnp.exp2(a_last)
        grow = jnp.exp2(a_exp)

        y_parts = []
        for g in range(SSD_GROUPS):
            bg = b_all[:, g * SSD_STATE:(g + 1) * SSD_STATE]
            cg = c_all[:, g * SSD_STATE:(g + 1) * SSD_STATE].astype(BF16)
            bg_b = bg.astype(BF16)
            cb2 = _dot_nt(cg, jnp.concatenate([bg_b, bg_b], axis=0))
            gs = slice(g * GROUP_W, (g + 1) * GROUP_W)
            st_g = st_ref[:, gs]
            y_off = _dot(cg, st_g.astype(BF16)) * grow[:, gs]
            bg_t = jnp.concatenate([bg, jnp.zeros_like(bg)], axis=0).T.astype(BF16)
            st_ref[:, gs] = _rows8(mul, st_g, chunk_decay[:, gs]) + _dot(bg_t, xw_pad[:, gs])
            pair_out = []
            for kk in range(GROUP_W // LANES):
                i = g * (GROUP_W // LANES) + kk
                ps = slice(i * LANES, (i + 1) * LANES)
                a_row = jnp.broadcast_to(a_cum_t[i:i + 1, :], (8, LANES))
                decay = jnp.exp2(jnp.minimum(_rows8(sub, a_exp[:, ps], a_row), 0.0))
                m2 = jnp.where(causal2, cb2 * decay, 0.0).astype(BF16)
                xp = xdt_b[:, ps]
                zero = jnp.zeros_like(xp)
                xbd = jnp.concatenate([jnp.where(lo_half, xp, zero), jnp.where(lo_half, zero, xp)], axis=0)
                pair_out.append(_dot(m2, xbd))
            y_diag = jnp.concatenate(pair_out, axis=1)
            y = y_diag + y_off + _rows8(mul, xs[:, gs], dskip_ref[:, gs])
            zg = pad_rows(z_ref[0, r0:r0 + rows, gs].astype(F32))
            y = y * (zg * _sigmoid(zg))
            ms = jnp.mean(y * y, axis=-1, keepdims=True)
            y_parts.append(_rows8(mul, y * lax.rsqrt(ms + EPS), gn_ref[:, gs]))
        y_all = jnp.concatenate(y_parts, axis=1)
        y_ref[0, r0:r0 + rows, :] = y_all[:rows].astype(y_ref.dtype)

    @pl.when(c == n_chunks - 1)
    def _():
        for j in range(n_blk):
            hfin_ref[0, j * LANES:(j + 1) * LANES, :] = st_ref[:, j * LANES:(j + 1) * LANES].T


def _ssd_constants():
    Q = SSD_CHUNK
    e = np.zeros((LANES, SSD_WIDTH), np.float32)
    for h in range(SSD_HEADS):
        e[h, h * SSD_HEAD_DIM:(h + 1) * SSD_HEAD_DIM] = 1.0
    e3 = np.concatenate([e, e, e], axis=0)
    t = np.arange(Q)
    ltri = (t[:, None] >= t[None, :]).astype(np.float32)
    ltri3 = np.concatenate([ltri, ltri, ltri], axis=1)
    ublk = np.zeros((LANES, LANES), np.float32)
    ublk[:Q, :Q] = ltri.T
    ublk[Q:, Q:] = ltri.T
    ublk3 = np.concatenate([ublk, ublk, ublk], axis=0)
    shift = np.zeros((SSD_CONV * Q, 2 * Q), np.float32)
    for i in range(SSD_CONV):
        for r in range(Q):
            src = r - (SSD_CONV - 1 - i)
            if src >= 0:
                shift[i * Q + r, src] = 1.0
            else:
                shift[i * Q + r, Q + 16 + src] = 1.0
                shift[i * Q + r, Q + 32 + src] = 1.0
    return jnp.asarray(e3, BF16), jnp.asarray(ltri3, BF16), jnp.asarray(ublk3, BF16), jnp.asarray(shift, BF16)


def ssd_mixer(big, xbc_blk, z_blk, dt_raw, conv_prev, h0, p):
    b, s, _ = big.shape
    Q = SSD_CHUNK
    rows = min(s, Q)
    n_sub = SSD_STEP_CHUNKS if s % (SSD_STEP_CHUNKS * Q) == 0 else 1
    blk = rows * n_sub
    n_chunks = -(-s // blk)
    dtp = dt_raw[:, :, :SSD_HEADS]
    if s < Q:
        dtp = jnp.pad(dtp, ((0, 0), (0, Q - s), (0, 0)))
    dt_t = dtp.reshape(b, n_chunks * n_sub, Q, SSD_HEADS).transpose(0, 1, 3, 2).reshape(
        b, n_chunks * n_sub, SSD_HEADS // 2, 2 * Q)
    e3, ltri3, ublk3, shift = _ssd_constants()
    prev16 = jnp.pad(conv_prev, ((0, 0), (16 - (SSD_CONV - 1), 0), (0, 0)))
    rep8 = lambda v: jnp.repeat(v, 8, axis=0)
    const = lambda shape: pl.BlockSpec(shape, lambda bi, ci: (0,) * len(shape))
    in_specs = [
        pl.BlockSpec((1, blk, SSD_CONV_CH), lambda bi, ci: (bi, ci, xbc_blk)),
        pl.BlockSpec((1, 16, SSD_CONV_CH), lambda bi, ci: (bi, jnp.maximum(ci * (blk // 16) - 1, 0), xbc_blk)),
        pl.BlockSpec((1, blk, SSD_WIDTH), lambda bi, ci: (bi, ci, z_blk)),
        pl.BlockSpec((1, blk, LANES), lambda bi, ci: (bi, ci, 0)),
        pl.BlockSpec((1, n_sub, SSD_HEADS // 2, 2 * Q), lambda bi, ci: (bi, ci, 0, 0)),
        pl.BlockSpec((1, 16, SSD_CONV_CH), lambda bi, ci: (bi, 0, 0)),
        pl.BlockSpec((1, SSD_WIDTH, SSD_STATE), lambda bi, ci: (bi, 0, 0)),
        const((8 * SSD_CONV, SSD_CONV_CH)), const((8, SSD_CONV_CH)),
        const((1, LANES)), const((SSD_HEADS // 2, 2 * Q)),
        const((1, LANES)), const((SSD_HEADS // 2, 2 * Q)),
        const((8, SSD_WIDTH)), const((8, SSD_WIDTH)),
        const(e3.shape), const(ltri3.shape), const(ublk3.shape), const(shift.shape),
    ]
    y, h_fin = pl.pallas_call(
        functools.partial(_ssd_kernel, rows=rows, n_sub=n_sub, n_chunks=n_chunks),
        grid=(b, n_chunks),
        in_specs=in_specs,
        out_specs=[pl.BlockSpec((1, blk, SSD_WIDTH), lambda bi, ci: (bi, ci, 0)),
                   pl.BlockSpec((1, SSD_WIDTH, SSD_STATE), lambda bi, ci: (bi, 0, 0))],
        out_shape=[jax.ShapeDtypeStruct((b, s, SSD_WIDTH), BF16),
                   jax.ShapeDtypeStruct((b, SSD_WIDTH, SSD_STATE), F32)],
        scratch_shapes=[pltpu.VMEM((SSD_STATE, SSD_WIDTH), F32)],
        compiler_params=_cparams(("parallel", "arbitrary")),
        name="ssd_mixer",
    )(big, big, big, dt_raw, dt_t, prev16, h0, rep8(p['conv_w_ssd']), rep8(p['conv_b_ssd']), p['dt_bias'], p['dt_bias_t'],
      p['a_log'], p['a_log_t'], rep8(p['d_skip']), rep8(p['g_ssd_norm']), e3, ltri3, ublk3, shift)
    return y, h_fin


def _merge_kernel(att_ref, ssd_ref, ga_ref, gs_ref, h_ref, wa_ref, ws_ref, wo_ref, o_ref, wa_b, ws_b, wo_b):
    @pl.when(pl.program_id(0) == 0)
    def _():
        wa_b[...] = wa_ref[...].astype(BF16)
        ws_b[...] = ws_ref[...].astype(BF16)
        wo_b[...] = wo_ref[...].astype(BF16)

    a = _dot(att_ref[...], wa_b[...])
    s = _dot(ssd_ref[...], ws_b[...])
    merged = _sigmoid(ga_ref[...].astype(F32)) * a + _sigmoid(gs_ref[...].astype(F32)) * s
    o_ref[...] = h_ref[...] + _dot(merged.astype(BF16), wo_b[...])


def _resident(a):
    return pl.BlockSpec(a.shape, lambda i: (0,) * a.ndim, pipeline_mode=pl.Buffered(1))


def merge(att, ssd, big, h, p):
    m = h.shape[0]
    tm = _pick(m, (512, 256, 128))
    row = lambda w: pl.BlockSpec((tm, w), lambda i: (i, 0))
    gate_blk = BIG_GATE // D_MODEL
    ws = [p['w_br_att'], p['w_br_ssd'], p['w_out']]
    return pl.pallas_call(
        _merge_kernel,
        grid=(m // tm,),
        in_specs=[row(SB_WIDTH), row(SSD_WIDTH),
                  pl.BlockSpec((tm, D_MODEL), lambda i: (i, gate_blk)),
                  pl.BlockSpec((tm, D_MODEL), lambda i: (i, gate_blk + 1)),
                  row(D_MODEL)] + [_resident(w) for w in ws],
        out_specs=row(D_MODEL),
        out_shape=jax.ShapeDtypeStruct((m, D_MODEL), F32),
        scratch_shapes=[pltpu.VMEM(w.shape, BF16) for w in ws],
        compiler_params=_cparams(("arbitrary",)),
        name="merge_out_proj",
    )(att, ssd, big, big, h, *ws)


def _ffn_act_kernel(ug_ref, uv_ref, pg_ref, pv_ref, wg_ref, wv_ref, bg_ref, bv_ref, o_ref, gbuf_ref, vbuf_ref,
                    *, ts):
    t = pl.program_id(1)
    k1 = FFN_CONV - 1

    def conv(u_ref, p_ref, w_ref, b_ref, buf_ref):
        @pl.when(t == 0)
        def _():
            buf_ref[8 - k1:8, :] = p_ref[0]

        buf_ref[8:8 + ts, :] = u_ref[0].astype(F32)
        y = b_ref[...]
        for i in range(FFN_CONV):
            y = y + buf_ref[8 - k1 + i:8 - k1 + i + ts, :] * w_ref[i:i + 1, :]
        buf_ref[8 - k1:8, :] = buf_ref[8 + ts - k1:8 + ts, :]
        return y

    gate = conv(ug_ref, pg_ref, wg_ref, bg_ref, gbuf_ref)
    val = conv(uv_ref, pv_ref, wv_ref, bv_ref, vbuf_ref)
    o_ref[0] = (gate * _sigmoid(gate) * val).astype(o_ref.dtype)


UPF_TM = 512
UPF_TN = D_FF // 2


def _up_ffn_kernel(h_ref, h16_ref, g_ref, wg_ref, wv_ref, pg_ref, pv_ref, cwg_ref, cwv_ref, cbg_ref, cbv_ref,
                   o_ref, gl_ref, vl_ref, wg_b, wv_b, gbuf_ref, vbuf_ref):
    i = pl.program_id(2)
    tm = h_ref.shape[1]

    @pl.when(jnp.logical_and(pl.program_id(1) == 0, i == 0))
    def _():
        wg_b[...] = wg_ref[...].astype(BF16)
        wv_b[...] = wv_ref[...].astype(BF16)

    u = _rms_norm_bf16(h_ref[0], g_ref[...])
    u16 = _rms_norm_bf16(h16_ref[0], g_ref[...])

    def conv(w_b, p_ref, last_ref, buf_ref, cw_ref, cb_ref):
        up = _dot(u, w_b[...])
        buf_ref[0:16, :] = jnp.where(i == 0, p_ref[0], _dot(u16, w_b[...]))
        buf_ref[16:16 + tm, :] = up
        last_ref[0] = up[tm - 16:tm, :]
        y = _rows8(jnp.multiply, up, cw_ref[8 * (FFN_CONV - 1):8 * FFN_CONV, :])
        for t in range(FFN_CONV - 1):
            r0 = 16 - (FFN_CONV - 1) + t
            y = y + _rows8(jnp.multiply, buf_ref[r0:r0 + tm, :], cw_ref[8 * t:8 * t + 8, :])
        return _rows8(jnp.add, y, cb_ref[...])

    gate = conv(wg_b, pg_ref, gl_ref, gbuf_ref, cwg_ref, cbg_ref)
    val = conv(wv_b, pv_ref, vl_ref, vbuf_ref, cwv_ref, cbv_ref)
    o_ref[0] = (gate * _sigmoid(gate) * val).astype(o_ref.dtype)


def up_ffn_long(h, prev, p):
    b, s, d = h.shape
    tm, tn = UPF_TM, UPF_TN
    nj = D_FF // tn
    prev16 = jnp.pad(prev, ((0, 0), (16 - (FFN_CONV - 1), 0), (0, 0)))
    rep8 = lambda v: jnp.repeat(v, 8, axis=0)
    w8, b8 = rep8(p['conv_w_ffn']), rep8(p['conv_b_ffn'])
    cols = lambda rows, half: pl.BlockSpec((rows, tn), lambda j, bi, i: (0, half * nj + j))
    w_spec = lambda half: pl.BlockSpec((d, tn), lambda j, bi, i: (0, half * nj + j), pipeline_mode=pl.Buffered(1))
    prevs = lambda half: pl.BlockSpec((1, 16, tn), lambda j, bi, i: (bi, 0, half * nj + j))
    last_spec = pl.BlockSpec((1, 16, tn), lambda j, bi, i: (bi, 0, j))
    return pl.pallas_call(
        _up_ffn_kernel,
        grid=(nj, b, s // tm),
        in_specs=[pl.BlockSpec((1, tm, d), lambda j, bi, i: (bi, i, 0)),
                  pl.BlockSpec((1, 16, d), lambda j, bi, i: (bi, jnp.maximum(i * (tm // 16) - 1, 0), 0)),
                  pl.BlockSpec((1, d), lambda j, bi, i: (0, 0)),
                  w_spec(0), w_spec(1), prevs(0), prevs(1),
                  cols(8 * FFN_CONV, 0), cols(8 * FFN_CONV, 1), cols(8, 0), cols(8, 1)],
        out_specs=[pl.BlockSpec((1, tm, tn), lambda j, bi, i: (bi, i, j)), last_spec, last_spec],
        out_shape=[jax.ShapeDtypeStruct((b, s, D_FF), BF16)] + [jax.ShapeDtypeStruct((b, 16, D_FF), F32)] * 2,
        scratch_shapes=[pltpu.VMEM((d, tn), BF16)] * 2 + [pltpu.VMEM((16 + tm, tn), F32)] * 2,
        compiler_params=_cparams(("arbitrary", "arbitrary", "arbitrary")),
        name="up_proj_conv_ffn",
    )(h, h, p['g_ffn'], p['w_up'], p['w_up'], prev16, prev16, w8, w8, b8, b8)


def ffn_act(up, prev, p):
    b, s, _ = up.shape
    ts = _pick(s, (512, 256, 128))
    half = lambda blk: pl.BlockSpec((1, ts, D_FF), lambda bi, ti: (bi, ti, blk))
    prevs = lambda blk: pl.BlockSpec((1, FFN_CONV - 1, D_FF), lambda bi, ti: (bi, 0, blk))
    wspec = lambda blk: pl.BlockSpec((FFN_CONV, D_FF), lambda bi, ti: (0, blk))
    bspec = lambda blk: pl.BlockSpec((1, D_FF), lambda bi, ti: (0, blk))
    return pl.pallas_call(
        functools.partial(_ffn_act_kernel, ts=ts),
        grid=(b, s // ts),
        in_specs=[half(0), half(1), prevs(0), prevs(1), wspec(0), wspec(1), bspec(0), bspec(1)],
        out_specs=pl.BlockSpec((1, ts, D_FF), lambda bi, ti: (bi, ti, 0)),
        out_shape=jax.ShapeDtypeStruct((b, s, D_FF), BF16),
        scratch_shapes=[pltpu.VMEM((8 + ts, D_FF), F32)] * 2,
        compiler_params=_cparams(("parallel", "arbitrary")),
        name="ffn_conv_act",
    )(up, up, prev, prev, p['conv_w_ffn'], p['conv_w_ffn'], p['conv_b_ffn'], p['conv_b_ffn'])


def _down_kernel(a_ref, w_ref, h_ref, g_ref, o_ref, w_b):
    @pl.when(pl.program_id(0) == 0)
    def _():
        w_b[...] = w_ref[...].astype(BF16)

    h = h_ref[...] + _dot(a_ref[...], w_b[...])
    ms = jnp.mean(h * h, axis=-1, keepdims=True)
    o_ref[...] = h * lax.rsqrt(ms + EPS) * g_ref[...]


def down_norm(act, h, p):
    m = h.shape[0]
    tm = _pick(m, (512, 256, 128))
    return pl.pallas_call(
        _down_kernel,
        grid=(m // tm,),
        in_specs=[pl.BlockSpec((tm, D_FF), lambda i: (i, 0)),
                  _resident(p['w_down']),
                  pl.BlockSpec((tm, D_MODEL), lambda i: (i, 0)),
                  pl.BlockSpec((1, D_MODEL), lambda i: (0, 0))],
        out_specs=pl.BlockSpec((tm, D_MODEL), lambda i: (i, 0)),
        out_shape=jax.ShapeDtypeStruct((m, D_MODEL), F32),
        scratch_shapes=[pltpu.VMEM(p['w_down'].shape, BF16)],
        compiler_params=_cparams(("arbitrary",)),
        name="down_proj_norm",
    )(act, p['w_down'], h, p['g_final'])


def ffn_and_out(big3, att, ssd, hf, ssd_conv_prev, ffn_conv_prev, p):
    b, s, _ = big3.shape
    m = b * s
    h1 = merge(att.reshape(m, SB_WIDTH), ssd.reshape(m, SSD_WIDTH), big3.reshape(m, BIG_W), hf, p)
    if s % UPF_TM == 0:
        act, up_g, up_v = up_ffn_long(h1.reshape(b, s, D_MODEL), ffn_conv_prev, p)
        up_last = jnp.concatenate([up_g, up_v], axis=2)
    else:
        up_last = norm_mm(h1, p['g_ffn'], p['w_up'], BF16).reshape(b, s, 2 * D_FF)
        act = ffn_act(up_last, ffn_conv_prev, p)
    y = down_norm(act.reshape(m, D_FF), h1, p).reshape(b, s, D_MODEL)
    xbc_rows = jnp.concatenate([ssd_conv_prev, big3[:, :, BIG_XBC:BIG_XBC + SSD_CONV_CH][:, -(SSD_CONV - 1):].astype(F32)],
                               axis=1)[:, -(SSD_CONV - 1):]
    up_rows = jnp.concatenate([ffn_conv_prev, up_last[:, -(FFN_CONV - 1):].astype(F32)], axis=1)[:, -(FFN_CONV - 1):]
    return y, xbc_rows, up_rows


def _prep_params(g_mix, w_in, conv_w_ssd, conv_b_ssd, dt_bias, a_log, d_skip, g_ssd_norm, w_br_att, w_br_ssd,
                 w_out, g_ffn, w_up, conv_w_ffn, conv_b_ffn, w_down, g_final):
    w_t = w_in.T
    dt0 = 3 * SB_WIDTH + SSD_WIDTH + SSD_CONV_CH
    w_tail_t = jnp.concatenate([w_t[dt0 + SSD_HEADS:], w_t[dt0:dt0 + SSD_HEADS],
                                jnp.zeros((LANES - SSD_HEADS, D_MODEL), F32)], axis=0)
    Q = SSD_CHUNK
    lanes_t = lambda v: jnp.repeat(v.reshape(SSD_HEADS // 2, 2, 1), Q, axis=2).reshape(SSD_HEADS // 2, 2 * Q)
    pad_l = lambda v: jnp.pad(v, (0, LANES - SSD_HEADS)).reshape(1, LANES)
    return {
        'g_mix': g_mix.reshape(1, -1), 'g_ffn': g_ffn.reshape(1, -1), 'g_final': g_final.reshape(1, -1),
        'w_t': w_t, 'w_tail_t': w_tail_t,
        'conv_w_ssd': conv_w_ssd, 'conv_b_ssd': conv_b_ssd.reshape(1, -1),
        'dt_bias': pad_l(dt_bias), 'dt_bias_t': lanes_t(dt_bias),
        'a_log': pad_l(a_log), 'a_log_t': lanes_t(a_log),
        'd_skip': jnp.repeat(d_skip, SSD_HEAD_DIM).reshape(1, -1), 'g_ssd_norm': g_ssd_norm.reshape(1, -1),
        'w_br_att': w_br_att, 'w_br_ssd': w_br_ssd, 'w_out': w_out,
        'w_up': w_up, 'conv_w_ffn': conv_w_ffn, 'conv_b_ffn': conv_b_ffn.reshape(1, -1),
        'w_down': w_down,
    }


def kernel(x_prompt, x_sample, cache_k, cache_v, state_ssm, state_ssm_conv, state_ffn_conv, meta_tokens, g_mix, w_in, conv_w_ssd, conv_b_ssd, dt_bias, a_log, d_skip, g_ssd_norm, w_br_att, w_br_ssd, w_out, g_ffn, w_up, conv_w_ffn, conv_b_ffn, w_down, g_final):
    bp, seq, _ = x_prompt.shape
    bs, dec = x_sample.shape[:2]
    assert dec == N_META and seq % KVT_TS == 0
    p = _prep_params(g_mix[0], w_in[0], conv_w_ssd[0], conv_b_ssd[0], dt_bias[0], a_log[0], d_skip[0],
                     g_ssd_norm[0], w_br_att[0], w_br_ssd[0], w_out[0], g_ffn[0], w_up[0], conv_w_ffn[0],
                     conv_b_ffn[0], w_down[0], g_final)
    proj = functools.partial(in_proj, g=p['g_mix'], w_t=p['w_t'], w_tail_t=p['w_tail_t'])
    state_shape = (SSD_WIDTH, SSD_STATE)
    rows = lambda a, b: a.reshape(b, -1, a.shape[-1])

    n_rows = cache_k.shape[2]
    n_tiles = (n_rows - N_META) // KEY_TILE
    assert n_tiles * KEY_TILE + N_META == n_rows
    x16 = jnp.concatenate([meta_tokens, x_sample.reshape(bs * dec, D_MODEL)], axis=0)
    k16, v16, big16, dt16 = (rows(a, 1 + bs) for a in proj(x16, with_kv=True))
    big_m, k_m, v_m, big_s, k_s, v_s = big16[:1], k16[:1], v16[:1], big16[1:], k16[1:], v16[1:]
    att_m = attention_step(big_m, 0, k_m, v_m, None, None, 0, 0, SB_WIDTH)
    cache_t = lambda c: c[0].transpose(0, 2, 3, 1).reshape(bs, SB_WIDTH, n_rows)
    att_s = attention_step(big_s, 0, k_s, v_s, cache_t(cache_k), cache_t(cache_v), n_tiles, N_META, SB_WIDTH)
    conv16 = jnp.concatenate([jnp.zeros((1, SSD_CONV - 1, SSD_CONV_CH), F32), state_ssm_conv[0]], axis=0)
    ffn16 = jnp.concatenate([jnp.zeros((1, FFN_CONV - 1, 2 * D_FF), F32), state_ffn_conv[0]], axis=0)
    xbc = BIG_XBC // SSD_CONV_CH, BIG_Z // SSD_WIDTH
    ssd_m, ssm_m = ssd_mixer(big_m, *xbc, dt16[:1], conv16[:1], jnp.zeros((1,) + state_shape, F32), p)
    ssd_s, ssm_s = ssd_mixer(big_s, *xbc, dt16[1:], conv16[1:], state_ssm[0].reshape((bs,) + state_shape), p)
    y16, conv16, ffn16 = ffn_and_out(big16, jnp.concatenate([att_m, att_s], axis=0),
                                     jnp.concatenate([ssd_m, ssd_s], axis=0), x16, conv16, ffn16, p)
    conv_m, ffn_m, y_sample, conv_s, ffn_s = conv16[:1], ffn16[:1], y16[1:], conv16[1:], ffn16[1:]

    xf = x_prompt.reshape(bp * seq, D_MODEL)
    big_x, dt_x = proj(xf, with_kv=False)
    big_x = rows(big_x, bp)
    k_t, v_t, k_end, v_end = kv_transposed(x_prompt, meta_tokens, p['g_mix'], p['w_t'])
    att_x = attention_prompt(big_x, big_m, k_t, v_t)
    att_end = attention_step(big_x, seq // N_META - 1, k_end, v_end, k_t, v_t, seq // KEY_TILE, 0, ATTN_PROMPT_WIDTH)
    att_x = lax.dynamic_update_slice(att_x, att_end, (0, seq - N_META, 0))
    rep = lambda a: jnp.broadcast_to(a, (bp,) + a.shape[1:])
    ssd_x, ssm_p = ssd_mixer(big_x, *xbc, rows(dt_x, bp), rep(conv_m), rep(ssm_m), p)
    y_prompt, conv_p, ffn_p = ffn_and_out(big_x, att_x, ssd_x, xf, rep(conv_m), rep(ffn_m), p)

    heads = lambda a: a.reshape(a.shape[0], a.shape[1], SB_HEADS, SB_HEAD_DIM)[None]
    heads_t = lambda a: a.reshape(a.shape[0], SB_HEADS, SB_HEAD_DIM, a.shape[2]).transpose(0, 3, 1, 2)[None]
    state5 = lambda a: a.reshape(a.shape[0], SSD_HEADS, SSD_HEAD_DIM, SSD_STATE)[None]
    return (y_prompt, y_sample, heads_t(k_t), heads_t(v_t), state5(ssm_p), conv_p[None], ffn_p[None],
            heads(k_s), heads(v_s), state5(ssm_s), conv_s[None], ffn_s[None])
```

```python
import functools

import numpy as np
import jax
import jax.numpy as jnp
from jax import lax
from jax.experimental import pallas as pl
from jax.experimental.pallas import tpu as pltpu

F32 = jnp.float32
BF16 = jnp.bfloat16

D_MODEL = 1024
N_META = 16
SB_HEADS = 16
SB_HEAD_DIM = 64
SB_WIDTH = SB_HEADS * SB_HEAD_DIM
SSD_WIDTH = 2 * D_MODEL
SSD_HEAD_DIM = 64
SSD_HEADS = SSD_WIDTH // SSD_HEAD_DIM
SSD_GROUPS = 4
SSD_STATE = 128
SSD_CONV = 4
SSD_CONV_CH = SSD_WIDTH + 2 * SSD_GROUPS * SSD_STATE
D_FF = 2816
FFN_CONV = 3
EPS = 1e-6

LANES = 128
SSD_CHUNK = 64
SSD_STEP_CHUNKS = 4
KEY_TILE = 128
GROUP_W = SSD_WIDTH // SSD_GROUPS
DEAD_LOG = -104.0
LOG2E = 1.4426950408889634
ATTN_STATIC_TILES = 2
ATTN_GROUP_TILES = 1
VMEM_LIMIT = 56 * 1024 * 1024


def _cparams(sem):
    return pltpu.CompilerParams(dimension_semantics=sem, vmem_limit_bytes=VMEM_LIMIT)


def _pick(n, cands):
    for c in cands:
        if n % c == 0:
            return c
    return n


def _split2(x):
    hi = x.astype(BF16)
    lo = (x - hi.astype(F32)).astype(BF16)
    return hi, lo


def _split3(x):
    hi = x.astype(BF16)
    r = x - hi.astype(F32)
    mid = r.astype(BF16)
    lo = (r - mid.astype(F32)).astype(BF16)
    return hi, mid, lo


def _dot(a, b):
    return jnp.dot(a, b, preferred_element_type=F32)


def _dot_nt(a, b):
    return lax.dot_general(a, b, (((1,), (1,)), ((), ())), preferred_element_type=F32)


def _softplus(x):
    return jnp.maximum(x, 0.0) + jnp.log(1.0 + jnp.exp(-jnp.abs(x)))


def _sigmoid(x):
    return 1.0 / (1.0 + jnp.exp2(-LOG2E * x))


def _rms_norm_bf16(x, g):
    ms = jnp.mean(x * x, axis=-1, keepdims=True)
    return (x * lax.rsqrt(ms + EPS) * g).astype(BF16)


def _norm_mm_kernel(x_ref, g_ref, w_ref, o_ref, u_ref):
    @pl.when(pl.program_id(1) == 0)
    def _():
        u_ref[...] = _rms_norm_bf16(x_ref[...], g_ref[...])

    o_ref[...] = _dot(u_ref[...], w_ref[...].astype(BF16)).astype(o_ref.dtype)


PROJ_TN = 1024
UP_TN = 1408


def norm_mm(x, g, w, out_dtype):
    m, d = x.shape
    n = w.shape[1]
    tm = _pick(m, (2048, 1024, 512, 256, 128))
    tn = UP_TN
    return pl.pallas_call(
        _norm_mm_kernel,
        grid=(m // tm, n // tn),
        in_specs=[pl.BlockSpec((tm, d), lambda i, j: (i, 0)),
                  pl.BlockSpec((1, d), lambda i, j: (0, 0)),
                  pl.BlockSpec((d, tn), lambda i, j: (0, j))],
        out_specs=pl.BlockSpec((tm, tn), lambda i, j: (i, j)),
        out_shape=jax.ShapeDtypeStruct((m, n), out_dtype),
        scratch_shapes=[pltpu.VMEM((tm, d), BF16)],
        compiler_params=_cparams(("parallel", "arbitrary")),
        name="norm_up_proj",
    )(x, g, w)


_NQ = SB_WIDTH // PROJ_TN
_J_K, _J_V, _J_Z = _NQ, 2 * _NQ, 3 * _NQ
_J_X = _J_Z + SSD_WIDTH // PROJ_TN
_J_G = _J_X + SSD_CONV_CH // PROJ_TN
_J_END = _J_G + 2 * D_MODEL // PROJ_TN
BIG_Z, BIG_Q, BIG_XBC, BIG_GATE = 0, SSD_WIDTH, SSD_WIDTH + SB_WIDTH, SSD_WIDTH + SB_WIDTH + SSD_CONV_CH
BIG_W = BIG_GATE + 2 * D_MODEL


def _big_block(j):
    return jnp.where(j < _J_K, BIG_Q // PROJ_TN + j,
                     jnp.where(j < _J_Z, BIG_Q // PROJ_TN + _NQ - 1,
                               jnp.where(j < _J_X, j - _J_Z + BIG_Z // PROJ_TN,
                                         jnp.where(j < _J_G, j - _J_X + BIG_XBC // PROJ_TN,
                                                   j - _J_G + BIG_GATE // PROJ_TN))))


def _sweep_step(j, with_kv):
    return j if with_kv else jnp.where(j >= _J_K, j + (_J_Z - _J_K), j)


def _in_proj_kernel(*refs, with_kv):
    if with_kv:
        x_ref, g_ref, win_ref, wgate_ref, wdt_ref, k_ref, v_ref, big_ref, dt_ref, u_ref = refs
    else:
        x_ref, g_ref, win_ref, wgate_ref, wdt_ref, big_ref, dt_ref, u_ref = refs
    j = _sweep_step(pl.program_id(1), with_kv)

    def proj(w_ref):
        return _dot_nt(u_ref[...], w_ref[...].astype(BF16))

    @pl.when(pl.program_id(1) == 0)
    def _():
        u_ref[...] = _rms_norm_bf16(x_ref[...], g_ref[...])
        dt_ref[...] = proj(wdt_ref)

    if with_kv:
        @pl.when(jnp.logical_and(j >= _J_K, j < _J_V))
        def _():
            k_ref[...] = proj(win_ref)

        @pl.when(jnp.logical_and(j >= _J_V, j < _J_Z))
        def _():
            v_ref[...] = proj(win_ref)

    @pl.when(jnp.logical_or(j < _J_K, jnp.logical_and(j >= _J_Z, j < _J_G)))
    def _():
        big_ref[...] = proj(win_ref).astype(BF16)

    @pl.when(j >= _J_G)
    def _():
        big_ref[...] = proj(wgate_ref).astype(BF16)


def in_proj(x, g, w_t, w_tail_t, with_kv):
    m, d = x.shape
    tm = _pick(m, (2048, 1024, 512, 256, 128))
    tn = PROJ_TN
    step = functools.partial(_sweep_step, with_kv=with_kv)
    clip = lambda j, lo, n: jnp.clip(step(j) - lo, 0, n - 1)
    kv_specs = [pl.BlockSpec((tm, tn), lambda i, j: (i, clip(j, _J_K, _NQ))),
                pl.BlockSpec((tm, tn), lambda i, j: (i, clip(j, _J_V, _NQ)))]
    kv_shapes = [jax.ShapeDtypeStruct((m, SB_WIDTH), F32)] * 2
    return pl.pallas_call(
        functools.partial(_in_proj_kernel, with_kv=with_kv),
        grid=(m // tm, _J_END if with_kv else _J_END - (_J_Z - _J_K)),
        in_specs=[pl.BlockSpec((tm, d), lambda i, j: (i, 0), pipeline_mode=pl.Buffered(1)),
                  pl.BlockSpec((1, d), lambda i, j: (0, 0)),
                  pl.BlockSpec((tn, d), lambda i, j: (jnp.minimum(step(j), _J_G - 1), 0)),
                  pl.BlockSpec((tn, d), lambda i, j: (clip(j, _J_G, _J_END - _J_G), 0)),
                  pl.BlockSpec((LANES, d), lambda i, j: (2 * D_MODEL // LANES, 0))],
        out_specs=(kv_specs if with_kv else []) + [
            pl.BlockSpec((tm, tn), lambda i, j: (i, _big_block(step(j)))),
            pl.BlockSpec((tm, LANES), lambda i, j: (i, 0))],
        out_shape=(kv_shapes if with_kv else []) + [
            jax.ShapeDtypeStruct((m, BIG_W), BF16), jax.ShapeDtypeStruct((m, LANES), F32)],
        scratch_shapes=[pltpu.VMEM((tm, d), BF16)],
        compiler_params=_cparams(("parallel", "arbitrary")),
        name="in_proj",
    )(x, g, w_t, w_tail_t, w_tail_t)


KVT_TS = 1024


def _kv_t_kernel(x16_ref, xt_ref, meta_ref, g_ref, wk_ref, wv_ref, kt_ref, vt_ref, kl_ref, vl_ref, wk_b, wv_b):
    j = pl.program_id(1)

    @pl.when(jnp.logical_and(pl.program_id(0) == 0, j == 0))
    def _():
        wk_b[...] = wk_ref[...].astype(BF16)
        wv_b[...] = wv_ref[...].astype(BF16)

    first = jnp.where(j == 0, meta_ref[...], x16_ref[0])
    x = jnp.concatenate([first, xt_ref[0, :KVT_TS - N_META, :]], axis=0)
    u = _rms_norm_bf16(x, g_ref[...])
    kt_ref[0] = _dot_nt(wk_b[...], u)
    vt_ref[0] = _dot_nt(wv_b[...], u)

    @pl.when(j == pl.num_programs(1) - 1)
    def _():
        kl_ref[0] = _dot_nt(u[0:N_META, :], wk_b[...])
        vl_ref[0] = _dot_nt(u[0:N_META, :], wv_b[...])


def kv_transposed(x, meta, g, w_t):
    b, s, d = x.shape
    n_pos = N_META + s
    n_tiles = -(-n_pos // KVT_TS)
    assert n_pos - (n_tiles - 1) * KVT_TS == N_META
    per_tile = KVT_TS // N_META
    out_spec = pl.BlockSpec((1, SB_WIDTH, KVT_TS), lambda bi, j: (bi, 0, j))
    last_spec = pl.BlockSpec((1, N_META, SB_WIDTH), lambda bi, j: (bi, 0, 0))
    w_spec = lambda blk: pl.BlockSpec((SB_WIDTH, d), lambda bi, j: (blk, 0), pipeline_mode=pl.Buffered(1))
    return pl.pallas_call(
        _kv_t_kernel,
        grid=(b, n_tiles),
        in_specs=[pl.BlockSpec((1, N_META, d), lambda bi, j: (bi, jnp.maximum(j * per_tile - 1, 0), 0)),
                  pl.BlockSpec((1, KVT_TS, d), lambda bi, j: (bi, jnp.minimum(j, s // KVT_TS - 1), 0)),
                  pl.BlockSpec((N_META, d), lambda bi, j: (0, 0)),
                  pl.BlockSpec((1, d), lambda bi, j: (0, 0)),
                  w_spec(1), w_spec(2)],
        out_specs=[out_spec, out_spec, last_spec, last_spec],
        out_shape=[jax.ShapeDtypeStruct((b, SB_WIDTH, n_pos), F32)] * 2
        + [jax.ShapeDtypeStruct((b, N_META, SB_WIDTH), F32)] * 2,
        scratch_shapes=[pltpu.VMEM((SB_WIDTH, d), BF16)] * 2,
        compiler_params=_cparams(("arbitrary", "arbitrary")),
        name="kv_transposed",
    )(x, x, meta, g, w_t, w_t)


def _stick_breaking_tiles(qs, u2, tiles, carries, accs):
    n_heads = len(qs)
    units = [(t, h) for t in range(len(tiles)) for h in range(n_heads)]
    ps = lambda h: slice((h // 2) * LANES, (h // 2 + 1) * LANES)

    def scores(t, h):
        kind, keys = tiles[t][0], tiles[t][1]
        return _dot_nt(qs[h], keys[:, ps(h)]) if kind == 'rows' else _dot(qs[h], keys[ps(h), :])

    z = {u: scores(*u) for u in units}
    log_beta, cat = {}, {}
    for u in units:
        mask = tiles[u[0]][3]
        soft = jnp.log(1.0 + jnp.exp2(-LOG2E * jnp.abs(z[u])))
        log_beta[u] = jnp.minimum(z[u], 0.0) - soft
        log_keep = log_beta[u] - z[u]
        if mask is not None:
            log_keep = jnp.where(mask, log_keep, 0.0)
        cat[u] = jnp.concatenate(_split2(log_keep), axis=1)
    r = {u: _dot(cat[u], u2) for u in units}
    carries, accs = list(carries), list(accs)
    w = {}
    for t, h in units:
        mask = tiles[t][3]
        wt = jnp.exp2(LOG2E * (log_beta[(t, h)] + r[(t, h)][:, :LANES] + carries[h]))
        if mask is not None:
            wt = jnp.where(mask, wt, 0.0)
        w[(t, h)] = wt.astype(BF16)
        carries[h] = carries[h] + r[(t, h)][:, LANES:]
    for t, h in units:
        kind, vals = tiles[t][0], tiles[t][2]
        pv = _dot(w[(t, h)], vals[:, ps(h)]) if kind == 'rows' else _dot_nt(w[(t, h)], vals[ps(h), :])
        accs[h] = accs[h] + pv
    return carries, accs


def _head_queries(q, lo_half):
    qs = []
    for h in range(q.shape[1] // SB_HEAD_DIM):
        qp = q[:, (h // 2) * LANES:(h // 2 + 1) * LANES]
        keep = lo_half if h % 2 == 0 else jnp.logical_not(lo_half)
        qs.append(jnp.where(keep, qp, jnp.zeros_like(qp)) * jnp.asarray(SB_HEAD_DIM ** -0.5, BF16))
    return qs


def _any_alive(carries):
    m = carries[0]
    for c in carries[1:]:
        m = jnp.maximum(m, c)
    return (jnp.max(m) > DEAD_LOG).astype(jnp.int32)


def _attend(qs, u2, first_tiles, load_tile, j0, lane):
    tq = qs[0].shape[0]

    def cache_tile(j):
        if isinstance(j, int):
            mask = None if j >= 0 else lane < 0
            kt, vt = load_tile(max(j, 0))
        else:
            mask = jnp.logical_and(j >= 0, lane >= 0)
            kt, vt = load_tile(jnp.maximum(j, 0))
        return 'cols', kt.astype(BF16), vt.astype(BF16), mask

    tiles = list(first_tiles)
    if load_tile is not None:
        for _ in range(ATTN_STATIC_TILES):
            tiles.append(cache_tile(j0))
            j0 = j0 - 1
    zeros = jnp.zeros((tq, LANES), F32)
    carries, accs = [zeros] * len(qs), [zeros] * len(qs)
    for g in range(0, len(tiles), ATTN_GROUP_TILES):
        carries, accs = _stick_breaking_tiles(qs, u2, tiles[g:g + ATTN_GROUP_TILES], carries, accs)
    if load_tile is not None:
        def cond(s):
            return jnp.logical_and(s[0] >= 0, s[1] > 0)

        def body(s):
            j, _, carries, accs = s
            carries, accs = _stick_breaking_tiles(qs, u2, [cache_tile(j)], carries, accs)
            return j - 1, _any_alive(carries), tuple(carries), tuple(accs)

        state = (jnp.asarray(j0, jnp.int32), _any_alive(carries), tuple(carries), tuple(accs))
        _, _, carries, accs = lax.while_loop(cond, body, state)
    return accs


def _pair_outputs(accs, lo_half, dtype):
    return [jnp.where(lo_half, accs[2 * p], accs[2 * p + 1]).astype(dtype) for p in range(len(accs) // 2)]


def _attn_prompt_kernel(q16_ref, qt_ref, qm_ref, u2_ref, kt_ref, vt_ref, o_ref):
    step = pl.program_id(2)
    tq = KEY_TILE
    lane = lax.broadcasted_iota(jnp.int32, (tq, LANES), 1)
    row = lax.broadcasted_iota(jnp.int32, (tq, LANES), 0)
    lo_half = lane < SB_HEAD_DIM

    def load_tile(j):
        off = pl.multiple_of(j * KEY_TILE, KEY_TILE)
        return kt_ref[0, :, pl.ds(off, KEY_TILE)], vt_ref[0, :, pl.ds(off, KEY_TILE)]

    for sub in range(ATTN_STEP_TILES):
        m = step * ATTN_STEP_TILES + sub
        if sub == 0:
            first = jnp.where(step == 0, qm_ref[0], q16_ref[0])
            q = jnp.concatenate([first, qt_ref[0, :tq - N_META, :]], axis=0)
        else:
            q = qt_ref[0, sub * tq - N_META:(sub + 1) * tq - N_META, :]
        qs = _head_queries(q, lo_half)
        kd, vd = load_tile(m)
        diag = ('cols', kd.astype(BF16), vd.astype(BF16), lane < row)
        accs = _attend(qs, u2_ref[...], [diag], load_tile, m - 1, lane)
        out = jnp.concatenate(_pair_outputs(accs, lo_half, o_ref.dtype), axis=1)
        if sub == 0:
            @pl.when(step == 0)
            def _(out=out):
                o_ref[0, 0:tq - N_META, :] = out[N_META:, :]
                o_ref[0, o_ref.shape[1] - N_META:, :] = jnp.zeros((N_META, o_ref.shape[2]), o_ref.dtype)

            @pl.when(step > 0)
            def _(out=out, m=m):
                o_ref[0, pl.ds(pl.multiple_of(m * KEY_TILE - N_META, N_META), tq), :] = out
        else:
            o_ref[0, pl.ds(pl.multiple_of(m * KEY_TILE - N_META, N_META), tq), :] = out


def _attn_step_kernel(*refs, n_tail, n_cache_tiles):
    if n_cache_tiles:
        (q_ref, kd_ref, vd_ref, u2_ref, ktl_ref, vtl_ref, kn_ref, vn_ref, kt_any, vt_any, o_ref,
         kd_scr, vd_scr, kbuf, vbuf) = refs
        width = q_ref.shape[2]
        n_near = kn_ref.shape[2] // KEY_TILE

        def load_tile(j):
            if isinstance(j, int) and j >= n_cache_tiles - n_near:
                ls = slice((j - (n_cache_tiles - n_near)) * KEY_TILE, (j - (n_cache_tiles - n_near) + 1) * KEY_TILE)
                return kn_ref[0, :, ls], vn_ref[0, :, ls]
            off = j * KEY_TILE if isinstance(j, int) else pl.multiple_of(j * KEY_TILE, KEY_TILE)
            src = (pl.program_id(0), pl.ds(pl.program_id(1) * width, width), pl.ds(off, KEY_TILE))
            pltpu.sync_copy(kt_any.at[src], kbuf)
            pltpu.sync_copy(vt_any.at[src], vbuf)
            return kbuf[...], vbuf[...]
    else:
        q_ref, kd_ref, vd_ref, u2_ref, o_ref, kd_scr, vd_scr = refs
        load_tile = None
    tq = q_ref.shape[1]
    n_rows = kd_ref.shape[1]
    lane = lax.broadcasted_iota(jnp.int32, (tq, LANES), 1)
    row = lax.broadcasted_iota(jnp.int32, (tq, LANES), 0)
    lo_half = lane < SB_HEAD_DIM
    n_free = n_tail if n_cache_tiles else 0
    kd_scr[...] = jnp.zeros_like(kd_scr)
    vd_scr[...] = jnp.zeros_like(vd_scr)
    if n_free:
        for scr, tl_ref in ((kd_scr, ktl_ref), (vd_scr, vtl_ref)):
            for c in range(width // LANES):
                scr[0:n_free, c * LANES:(c + 1) * LANES] = (
                    tl_ref[0, c * LANES:(c + 1) * LANES, :].T[0:n_free, :].astype(BF16))
    kd_scr[n_free:n_free + n_rows, :] = kd_ref[0].astype(BF16)
    vd_scr[n_free:n_free + n_rows, :] = vd_ref[0].astype(BF16)
    first = [('rows', kd_scr[...], vd_scr[...], lane < row + n_free)]
    qs = _head_queries(q_ref[0], lo_half)
    accs = _attend(qs, u2_ref[...], first, load_tile, n_cache_tiles - 1, lane)
    for p, out in enumerate(_pair_outputs(accs, lo_half, o_ref.dtype)):
        o_ref[0, :, p * LANES:(p + 1) * LANES] = out


def _cumsum_rhs():
    s = np.arange(KEY_TILE)
    strict = (s[:, None] > s[None, :]).astype(np.float32)
    half = np.concatenate([strict, np.ones((KEY_TILE, KEY_TILE), np.float32)], axis=1)
    return jnp.asarray(np.concatenate([half, half], axis=0), BF16)


ATTN_PROMPT_WIDTH = 8 * SB_HEAD_DIM
ATTN_STEP_TILES = 2


def attention_prompt(big, big_meta, k_t, v_t):
    b, s, _ = big.shape
    width = ATTN_PROMPT_WIDTH
    qb0 = BIG_Q // width
    rows = ATTN_STEP_TILES * KEY_TILE
    per_step = rows // N_META
    n_pos = k_t.shape[2]
    cache = pl.BlockSpec((1, width, n_pos), lambda bi, hp, m: (bi, hp, 0))
    return pl.pallas_call(
        _attn_prompt_kernel,
        grid=(b, SB_WIDTH // width, s // rows),
        in_specs=[pl.BlockSpec((1, N_META, width), lambda bi, hp, m: (bi, jnp.maximum(m * per_step - 1, 0), qb0 + hp)),
                  pl.BlockSpec((1, rows, width), lambda bi, hp, m: (bi, m, qb0 + hp)),
                  pl.BlockSpec((1, N_META, width), lambda bi, hp, m: (0, 0, qb0 + hp)),
                  pl.BlockSpec((2 * KEY_TILE, 2 * KEY_TILE), lambda bi, hp, m: (0, 0)),
                  cache, cache],
        out_specs=pl.BlockSpec((1, s, width), lambda bi, hp, m: (bi, 0, hp)),
        out_shape=jax.ShapeDtypeStruct((b, s, SB_WIDTH), BF16),
        compiler_params=_cparams(("parallel", "parallel", "arbitrary")),
        name="stick_breaking_prompt",
    )(big, big, big_meta, _cumsum_rhs(), k_t, v_t)


def attention_step(big, q_row_blk, k_rows, v_rows, k_t, v_t, n_cache_tiles, n_tail, width):
    b = big.shape[0]
    n_rows = k_rows.shape[1]
    qb0 = BIG_Q // width
    rows = pl.BlockSpec((1, n_rows, width), lambda bi, hp: (bi, 0, hp))
    in_specs = [pl.BlockSpec((1, N_META, width), lambda bi, hp: (bi, q_row_blk, qb0 + hp)), rows, rows,
                pl.BlockSpec((2 * KEY_TILE, 2 * KEY_TILE), lambda bi, hp: (0, 0))]
    args = [big, k_rows, v_rows, _cumsum_rhs()]
    scratch = [pltpu.VMEM((KEY_TILE, width), BF16)] * 2
    if n_cache_tiles:
        near = ATTN_STATIC_TILES * KEY_TILE
        assert n_cache_tiles % ATTN_STATIC_TILES == 0
        near_blk = n_cache_tiles // ATTN_STATIC_TILES - 1
        in_specs += [pl.BlockSpec((1, width, KEY_TILE), lambda bi, hp: (bi, hp, n_cache_tiles))] * 2
        in_specs += [pl.BlockSpec((1, width, near), lambda bi, hp: (bi, hp, near_blk))] * 2
        in_specs += [pl.BlockSpec(memory_space=pl.ANY)] * 2
        args += [k_t, v_t, k_t, v_t, k_t, v_t]
        scratch += [pltpu.VMEM((width, KEY_TILE), F32)] * 2
    return pl.pallas_call(
        functools.partial(_attn_step_kernel, n_tail=n_tail, n_cache_tiles=n_cache_tiles),
        grid=(b, SB_WIDTH // width),
        in_specs=in_specs,
        out_specs=pl.BlockSpec((1, N_META, width), lambda bi, hp: (bi, 0, hp)),
        out_shape=jax.ShapeDtypeStruct((b, N_META, SB_WIDTH), BF16),
        scratch_shapes=scratch,
        compiler_params=_cparams(("parallel", "parallel")),
        name="stick_breaking_step",
    )(*args)


def _rows8(op, x, r8):
    return op(x.reshape(x.shape[0] // 8, 8, x.shape[1]), r8[None]).reshape(x.shape)


def _ssd_kernel(xbc_ref, x16_ref, z_ref, dt_ref, dtT_ref, prev_ref, h0_ref, cw_ref, cb_ref, dtb_ref, dtbT_ref,
                alog_ref, alogT_ref, dskip_ref, gn_ref, e3_ref, ltri_ref, ublk_ref, shift_ref,
                y_ref, hfin_ref, st_ref, *, rows, n_sub, n_chunks):
    Q = SSD_CHUNK
    c = pl.program_id(1)
    n_blk = SSD_WIDTH // LANES
    mul, sub = jnp.multiply, jnp.subtract

    @pl.when(c == 0)
    def _():
        for j in range(n_blk):
            st_ref[:, j * LANES:(j + 1) * LANES] = h0_ref[0, j * LANES:(j + 1) * LANES, :].T

    def pad_rows(v):
        if rows == Q:
            return v
        return jnp.concatenate([v, jnp.zeros((Q - rows, v.shape[1]), v.dtype)], axis=0)

    lane = lax.broadcasted_iota(jnp.int32, (Q, LANES), 1)
    rowq = lax.broadcasted_iota(jnp.int32, (Q, LANES), 0)
    causal2 = (lane % Q) <= rowq
    lo_half = lane < SSD_HEAD_DIM
    decay_rate = -LOG2E * jnp.exp(alog_ref[...])
    decay_rate_t = -LOG2E * jnp.exp(alogT_ref[...])

    for k in range(n_sub):
        r0 = k * Q
        hi, lo = _split2(prev_ref[0])
        if k == 0:
            hi = jnp.where(c == 0, hi, x16_ref[0])
            lo = jnp.where(c == 0, lo, jnp.zeros_like(lo))
        else:
            hi, lo = xbc_ref[0, r0 - 16:r0, :], jnp.zeros_like(lo)
        window = jnp.concatenate([pad_rows(xbc_ref[0, r0:r0 + rows, :]), hi, lo,
                                  jnp.zeros((Q - 32, SSD_CONV_CH), BF16)], axis=0)
        shifted = _dot(shift_ref[...], window)
        conv = _rows8(mul, shifted[0:Q], cw_ref[0:8, :])
        for i in range(1, SSD_CONV):
            conv = conv + _rows8(mul, shifted[i * Q:(i + 1) * Q], cw_ref[8 * i:8 * i + 8, :])
        conv = _rows8(jnp.add, conv, cb_ref[...])
        xc = conv * _sigmoid(conv)
        xs = xc[:, :SSD_WIDTH]
        b_all = xc[:, SSD_WIDTH:SSD_WIDTH + SSD_GROUPS * SSD_STATE]
        c_all = xc[:, SSD_WIDTH + SSD_GROUPS * SSD_STATE:]

        dt = _softplus(pad_rows(dt_ref[0, r0:r0 + rows, :]) + dtb_ref[...])
        dt_t = _softplus(dtT_ref[0, k] + dtbT_ref[...])
        if rows < Q:
            rowi = lax.broadcasted_iota(jnp.int32, (Q, 1), 0)
            dt = jnp.where(rowi < rows, dt, 0.0)
            xs = jnp.where(rowi < rows, xs, 0.0)
            lane_t = lax.broadcasted_iota(jnp.int32, dt_t.shape, 1)
            dt_t = jnp.where(lane_t % Q < rows, dt_t, 0.0)
        da = dt * decay_rate
        da_t = dt_t * decay_rate_t

        a_cum = _dot(ltri_ref[...], jnp.concatenate(_split3(da), axis=0))
        a_cum_t = _dot(jnp.concatenate(_split3(da_t), axis=1), ublk_ref[...])
        dt_exp = _dot(jnp.concatenate(_split2(dt), axis=1), e3_ref[0:2 * LANES, :])
        a_exp = _dot(jnp.concatenate(_split3(a_cum), axis=1), e3_ref[...])
        a_last = jnp.broadcast_to(a_exp[Q - 1:Q, :], (8, SSD_WIDTH))

        xdt = xs * dt_exp
        xdt_b = xdt.astype(BF16)
        xw = (xdt * jnp.exp2(-_rows8(sub, a_exp, a_last))).astype(BF16)
        xw_pad = jnp.concatenate([xw, jnp.zeros_like(xw)], axis=0)
        chunk_decay = jnp.exp2(a_last)
        grow = jnp.exp2(a_exp)

        y_parts = []
        for g in range(SSD_GROUPS):
            bg = b_all[:, g * SSD_STATE:(g + 1) * SSD_STATE]
            cg = c_all[:, g * SSD_STATE:(g + 1) * SSD_STATE].astype(BF16)
            bg_b = bg.astype(BF16)
            cb2 = _dot_nt(cg, jnp.concatenate([bg_b, bg_b], axis=0))
            gs = slice(g * GROUP_W, (g + 1) * GROUP_W)
            st_g = st_ref[:, gs]
            y_off = _dot(cg, st_g.astype(BF16)) * grow[:, gs]
            bg_t = jnp.concatenate([bg, jnp.zeros_like(bg)], axis=0).T.astype(BF16)
            st_ref[:, gs] = _rows8(mul, st_g, chunk_decay[:, gs]) + _dot(bg_t, xw_pad[:, gs])
            pair_out = []
            for kk in range(GROUP_W // LANES):
                i = g * (GROUP_W // LANES) + kk
                ps = slice(i * LANES, (i + 1) * LANES)
                a_row = jnp.broadcast_to(a_cum_t[i:i + 1, :], (8, LANES))
                decay = jnp.exp2(jnp.minimum(_rows8(sub, a_exp[:, ps], a_row), 0.0))
                m2 = jnp.where(causal2, cb2 * decay, 0.0).astype(BF16)
                xp = xdt_b[:, ps]
                zero = jnp.zeros_like(xp)
                xbd = jnp.concatenate([jnp.where(lo_half, xp, zero), jnp.where(lo_half, zero, xp)], axis=0)
                pair_out.append(_dot(m2, xbd))
            y_diag = jnp.concatenate(pair_out, axis=1)
            y = y_diag + y_off + _rows8(mul, xs[:, gs], dskip_ref[:, gs])
            zg = pad_rows(z_ref[0, r0:r0 + rows, gs].astype(F32))
            y = y * (zg * _sigmoid(zg))
            ms = jnp.mean(y * y, axis=-1, keepdims=True)
            y_parts.append(_rows8(mul, y * lax.rsqrt(ms + EPS), gn_ref[:, gs]))
        y_all = jnp.concatenate(y_parts, axis=1)
        y_ref[0, r0:r0 + rows, :] = y_all[:rows].astype(y_ref.dtype)

    @pl.when(c == n_chunks - 1)
    def _():
        for j in range(n_blk):
            hfin_ref[0, j * LANES:(j + 1) * LANES, :] = st_ref[:, j * LANES:(j + 1) * LANES].T


def _ssd_constants():
    Q = SSD_CHUNK
    e = np.zeros((LANES, SSD_WIDTH), np.float32)
    for h in range(SSD_HEADS):
        e[h, h * SSD_HEAD_DIM:(h + 1) * SSD_HEAD_DIM] = 1.0
    e3 = np.concatenate([e, e, e], axis=0)
    t = np.arange(Q)
    ltri = (t[:, None] >= t[None, :]).astype(np.float32)
    ltri3 = np.concatenate([ltri, ltri, ltri], axis=1)
    ublk = np.zeros((LANES, LANES), np.float32)
    ublk[:Q, :Q] = ltri.T
    ublk[Q:, Q:] = ltri.T
    ublk3 = np.concatenate([ublk, ublk, ublk], axis=0)
    shift = np.zeros((SSD_CONV * Q, 2 * Q), np.float32)
    for i in range(SSD_CONV):
        for r in range(Q):
            src = r - (SSD_CONV - 1 - i)
            if src >= 0:
                shift[i * Q + r, src] = 1.0
            else:
                shift[i * Q + r, Q + 16 + src] = 1.0
                shift[i * Q + r, Q + 32 + src] = 1.0
    return jnp.asarray(e3, BF16), jnp.asarray(ltri3, BF16), jnp.asarray(ublk3, BF16), jnp.asarray(shift, BF16)


def ssd_mixer(big, xbc_blk, z_blk, dt_raw, conv_prev, h0, p):
    b, s, _ = big.shape
    Q = SSD_CHUNK
    rows = min(s, Q)
    n_sub = SSD_STEP_CHUNKS if s % (SSD_STEP_CHUNKS * Q) == 0 else 1
    blk = rows * n_sub
    n_chunks = -(-s // blk)
    dtp = dt_raw[:, :, :SSD_HEADS]
    if s < Q:
        dtp = jnp.pad(dtp, ((0, 0), (0, Q - s), (0, 0)))
    dt_t = dtp.reshape(b, n_chunks * n_sub, Q, SSD_HEADS).transpose(0, 1, 3, 2).reshape(
        b, n_chunks * n_sub, SSD_HEADS // 2, 2 * Q)
    e3, ltri3, ublk3, shift = _ssd_constants()
    prev16 = jnp.pad(conv_prev, ((0, 0), (16 - (SSD_CONV - 1), 0), (0, 0)))
    rep8 = lambda v: jnp.repeat(v, 8, axis=0)
    const = lambda shape: pl.BlockSpec(shape, lambda bi, ci: (0,) * len(shape))
    in_specs = [
        pl.BlockSpec((1, blk, SSD_CONV_CH), lambda bi, ci: (bi, ci, xbc_blk)),
        pl.BlockSpec((1, 16, SSD_CONV_CH), lambda bi, ci: (bi, jnp.maximum(ci * (blk // 16) - 1, 0), xbc_blk)),
        pl.BlockSpec((1, blk, SSD_WIDTH), lambda bi, ci: (bi, ci, z_blk)),
        pl.BlockSpec((1, blk, LANES), lambda bi, ci: (bi, ci, 0)),
        pl.BlockSpec((1, n_sub, SSD_HEADS // 2, 2 * Q), lambda bi, ci: (bi, ci, 0, 0)),
        pl.BlockSpec((1, 16, SSD_CONV_CH), lambda bi, ci: (bi, 0, 0)),
        pl.BlockSpec((1, SSD_WIDTH, SSD_STATE), lambda bi, ci: (bi, 0, 0)),
        const((8 * SSD_CONV, SSD_CONV_CH)), const((8, SSD_CONV_CH)),
        const((1, LANES)), const((SSD_HEADS // 2, 2 * Q)),
        const((1, LANES)), const((SSD_HEADS // 2, 2 * Q)),
        const((8, SSD_WIDTH)), const((8, SSD_WIDTH)),
        const(e3.shape), const(ltri3.shape), const(ublk3.shape), const(shift.shape),
    ]
    y, h_fin = pl.pallas_call(
        functools.partial(_ssd_kernel, rows=rows, n_sub=n_sub, n_chunks=n_chunks),
        grid=(b, n_chunks),
        in_specs=in_specs,
        out_specs=[pl.BlockSpec((1, blk, SSD_WIDTH), lambda bi, ci: (bi, ci, 0)),
                   pl.BlockSpec((1, SSD_WIDTH, SSD_STATE), lambda bi, ci: (bi, 0, 0))],
        out_shape=[jax.ShapeDtypeStruct((b, s, SSD_WIDTH), BF16),
                   jax.ShapeDtypeStruct((b, SSD_WIDTH, SSD_STATE), F32)],
        scratch_shapes=[pltpu.VMEM((SSD_STATE, SSD_WIDTH), F32)],
        compiler_params=_cparams(("parallel", "arbitrary")),
        name="ssd_mixer",
    )(big, big, big, dt_raw, dt_t, prev16, h0, rep8(p['conv_w_ssd']), rep8(p['conv_b_ssd']), p['dt_bias'], p['dt_bias_t'],
      p['a_log'], p['a_log_t'], rep8(p['d_skip']), rep8(p['g_ssd_norm']), e3, ltri3, ublk3, shift)
    return y, h_fin


def _merge_kernel(att_ref, ssd_ref, ga_ref, gs_ref, h_ref, wa_ref, ws_ref, wo_ref, o_ref, wa_b, ws_b, wo_b):
    @pl.when(pl.program_id(0) == 0)
    def _():
        wa_b[...] = wa_ref[...].astype(BF16)
        ws_b[...] = ws_ref[...].astype(BF16)
        wo_b[...] = wo_ref[...].astype(BF16)

    a = _dot(att_ref[...], wa_b[...])
    s = _dot(ssd_ref[...], ws_b[...])
    merged = _sigmoid(ga_ref[...].astype(F32)) * a + _sigmoid(gs_ref[...].astype(F32)) * s
    o_ref[...] = h_ref[...] + _dot(merged.astype(BF16), wo_b[...])


def _resident(a):
    return pl.BlockSpec(a.shape, lambda i: (0,) * a.ndim, pipeline_mode=pl.Buffered(1))


def merge(att, ssd, big, h, p):
    m = h.shape[0]
    tm = _pick(m, (512, 256, 128))
    row = lambda w: pl.BlockSpec((tm, w), lambda i: (i, 0))
    gate_blk = BIG_GATE // D_MODEL
    ws = [p['w_br_att'], p['w_br_ssd'], p['w_out']]
    return pl.pallas_call(
        _merge_kernel,
        grid=(m // tm,),
        in_specs=[row(SB_WIDTH), row(SSD_WIDTH),
                  pl.BlockSpec((tm, D_MODEL), lambda i: (i, gate_blk)),
                  pl.BlockSpec((tm, D_MODEL), lambda i: (i, gate_blk + 1)),
                  row(D_MODEL)] + [_resident(w) for w in ws],
        out_specs=row(D_MODEL),
        out_shape=jax.ShapeDtypeStruct((m, D_MODEL), F32),
        scratch_shapes=[pltpu.VMEM(w.shape, BF16) for w in ws],
        compiler_params=_cparams(("arbitrary",)),
        name="merge_out_proj",
    )(att, ssd, big, big, h, *ws)


def _ffn_act_kernel(ug_ref, uv_ref, pg_ref, pv_ref, wg_ref, wv_ref, bg_ref, bv_ref, o_ref, gbuf_ref, vbuf_ref):
    nb, ts = ug_ref.shape[:2]
    k1 = FFN_CONV - 1

    def conv(n, u_ref, p_ref, w_ref, b_ref, buf_ref):
        buf_ref[n, 8 - k1:8, :] = p_ref[n]
        buf_ref[n, 8:8 + ts, :] = u_ref[n].astype(F32)
        y = b_ref[...]
        for i in range(FFN_CONV):
            y = y + buf_ref[n, 8 - k1 + i:8 - k1 + i + ts, :] * w_ref[i:i + 1, :]
        return y

    for n in range(nb):
        gate = conv(n, ug_ref, pg_ref, wg_ref, bg_ref, gbuf_ref)
        val = conv(n, uv_ref, pv_ref, wv_ref, bv_ref, vbuf_ref)
        o_ref[n] = (gate * _sigmoid(gate) * val).astype(o_ref.dtype)


UPF_TM = 512
UPF_TN = D_FF // 2


def _up_ffn_kernel(h_ref, h16_ref, g_ref, wg_ref, wv_ref, pg_ref, pv_ref, cwg_ref, cwv_ref, cbg_ref, cbv_ref,
                   o_ref, gl_ref, vl_ref, wg_b, wv_b, gbuf_ref, vbuf_ref):
    i = pl.program_id(2)
    tm = h_ref.shape[1]

    @pl.when(jnp.logical_and(pl.program_id(1) == 0, i == 0))
    def _():
        wg_b[...] = wg_ref[...].astype(BF16)
        wv_b[...] = wv_ref[...].astype(BF16)

    u = _rms_norm_bf16(h_ref[0], g_ref[...])
    u16 = _rms_norm_bf16(h16_ref[0], g_ref[...])

    def conv(w_b, p_ref, last_ref, buf_ref, cw_ref, cb_ref):
        up = _dot(u, w_b[...])
        buf_ref[0:16, :] = jnp.where(i == 0, p_ref[0], _dot(u16, w_b[...]))
        buf_ref[16:16 + tm, :] = up
        last_ref[0] = up[tm - 16:tm, :]
        y = _rows8(jnp.multiply, up, cw_ref[8 * (FFN_CONV - 1):8 * FFN_CONV, :])
        for t in range(FFN_CONV - 1):
            r0 = 16 - (FFN_CONV - 1) + t
            y = y + _rows8(jnp.multiply, buf_ref[r0:r0 + tm, :], cw_ref[8 * t:8 * t + 8, :])
        return _rows8(jnp.add, y, cb_ref[...])

    gate = conv(wg_b, pg_ref, gl_ref, gbuf_ref, cwg_ref, cbg_ref)
    val = conv(wv_b, pv_ref, vl_ref, vbuf_ref, cwv_ref, cbv_ref)
    o_ref[0] = (gate * _sigmoid(gate) * val).astype(o_ref.dtype)


def up_ffn_long(h, prev, p):
    b, s, d = h.shape
    tm, tn = UPF_TM, UPF_TN
    nj = D_FF // tn
    prev16 = jnp.pad(prev, ((0, 0), (16 - (FFN_CONV - 1), 0), (0, 0)))
    rep8 = lambda v: jnp.repeat(v, 8, axis=0)
    w8, b8 = rep8(p['conv_w_ffn']), rep8(p['conv_b_ffn'])
    cols = lambda rows, half: pl.BlockSpec((rows, tn), lambda j, bi, i: (0, half * nj + j))
    w_spec = lambda half: pl.BlockSpec((d, tn), lambda j, bi, i: (0, half * nj + j), pipeline_mode=pl.Buffered(1))
    prevs = lambda half: pl.BlockSpec((1, 16, tn), lambda j, bi, i: (bi, 0, half * nj + j))
    last_spec = pl.BlockSpec((1, 16, tn), lambda j, bi, i: (bi, 0, j))
    return pl.pallas_call(
        _up_ffn_kernel,
        grid=(nj, b, s // tm),
        in_specs=[pl.BlockSpec((1, tm, d), lambda j, bi, i: (bi, i, 0)),
                  pl.BlockSpec((1, 16, d), lambda j, bi, i: (bi, jnp.maximum(i * (tm // 16) - 1, 0), 0)),
                  pl.BlockSpec((1, d), lambda j, bi, i: (0, 0)),
                  w_spec(0), w_spec(1), prevs(0), prevs(1),
                  cols(8 * FFN_CONV, 0), cols(8 * FFN_CONV, 1), cols(8, 0), cols(8, 1)],
        out_specs=[pl.BlockSpec((1, tm, tn), lambda j, bi, i: (bi, i, j)), last_spec, last_spec],
        out_shape=[jax.ShapeDtypeStruct((b, s, D_FF), BF16)] + [jax.ShapeDtypeStruct((b, 16, D_FF), F32)] * 2,
        scratch_shapes=[pltpu.VMEM((d, tn), BF16)] * 2 + [pltpu.VMEM((16 + tm, tn), F32)] * 2,
        compiler_params=_cparams(("arbitrary", "arbitrary", "arbitrary")),
        name="up_proj_conv_ffn",
    )(h, h, p['g_ffn'], p['w_up'], p['w_up'], prev16, prev16, w8, w8, b8, b8)


def ffn_act(up, prev, p):
    b, s, _ = up.shape
    nb = _pick(b, (16, 11, 8, 4, 2))
    half = lambda blk: pl.BlockSpec((nb, s, D_FF), lambda bi: (bi, 0, blk))
    prevs = lambda blk: pl.BlockSpec((nb, FFN_CONV - 1, D_FF), lambda bi: (bi, 0, blk))
    wspec = lambda blk: pl.BlockSpec((FFN_CONV, D_FF), lambda bi: (0, blk))
    bspec = lambda blk: pl.BlockSpec((1, D_FF), lambda bi: (0, blk))
    return pl.pallas_call(
        _ffn_act_kernel,
        grid=(b // nb,),
        in_specs=[half(0), half(1), prevs(0), prevs(1), wspec(0), wspec(1), bspec(0), bspec(1)],
        out_specs=pl.BlockSpec((nb, s, D_FF), lambda bi: (bi, 0, 0)),
        out_shape=jax.ShapeDtypeStruct((b, s, D_FF), BF16),
        scratch_shapes=[pltpu.VMEM((nb, 8 + s, D_FF), F32)] * 2,
        compiler_params=_cparams(("parallel",)),
        name="ffn_conv_act",
    )(up, up, prev, prev, p['conv_w_ffn'], p['conv_w_ffn'], p['conv_b_ffn'], p['conv_b_ffn'])


def _down_kernel(a_ref, w_ref, h_ref, g_ref, o_ref, w_b):
    @pl.when(pl.program_id(0) == 0)
    def _():
        w_b[...] = w_ref[...].astype(BF16)

    h = h_ref[...] + _dot(a_ref[...], w_b[...])
    ms = jnp.mean(h * h, axis=-1, keepdims=True)
    o_ref[...] = h * lax.rsqrt(ms + EPS) * g_ref[...]


def down_norm(act, h, p):
    m = h.shape[0]
    tm = _pick(m, (512, 256, 128))
    return pl.pallas_call(
        _down_kernel,
        grid=(m // tm,),
        in_specs=[pl.BlockSpec((tm, D_FF), lambda i: (i, 0)),
                  _resident(p['w_down']),
                  pl.BlockSpec((tm, D_MODEL), lambda i: (i, 0)),
                  pl.BlockSpec((1, D_MODEL), lambda i: (0, 0))],
        out_specs=pl.BlockSpec((tm, D_MODEL), lambda i: (i, 0)),
        out_shape=jax.ShapeDtypeStruct((m, D_MODEL), F32),
        scratch_shapes=[pltpu.VMEM(p['w_down'].shape, BF16)],
        compiler_params=_cparams(("arbitrary",)),
        name="down_proj_norm",
    )(act, p['w_down'], h, p['g_final'])


def ffn_and_out(big3, att, ssd, hf, ssd_conv_prev, ffn_conv_prev, p):
    b, s, _ = big3.shape
    m = b * s
    h1 = merge(att.reshape(m, SB_WIDTH), ssd.reshape(m, SSD_WIDTH), big3.reshape(m, BIG_W), hf, p)
    if s % UPF_TM == 0:
        act, up_g, up_v = up_ffn_long(h1.reshape(b, s, D_MODEL), ffn_conv_prev, p)
        up_last = jnp.concatenate([up_g, up_v], axis=2)
    else:
        up_last = norm_mm(h1, p['g_ffn'], p['w_up'], BF16).reshape(b, s, 2 * D_FF)
        act = ffn_act(up_last, ffn_conv_prev, p)
    y = down_norm(act.reshape(m, D_FF), h1, p).reshape(b, s, D_MODEL)
    xbc_rows = jnp.concatenate([ssd_conv_prev, big3[:, :, BIG_XBC:BIG_XBC + SSD_CONV_CH][:, -(SSD_CONV - 1):].astype(F32)],
                               axis=1)[:, -(SSD_CONV - 1):]
    up_rows = jnp.concatenate([ffn_conv_prev, up_last[:, -(FFN_CONV - 1):].astype(F32)], axis=1)[:, -(FFN_CONV - 1):]
    return y, xbc_rows, up_rows


def _prep_params(g_mix, w_in, conv_w_ssd, conv_b_ssd, dt_bias, a_log, d_skip, g_ssd_norm, w_br_att, w_br_ssd,
                 w_out, g_ffn, w_up, conv_w_ffn, conv_b_ffn, w_down, g_final):
    w_t = w_in.T
    dt0 = 3 * SB_WIDTH + SSD_WIDTH + SSD_CONV_CH
    w_tail_t = jnp.concatenate([w_t[dt0 + SSD_HEADS:], w_t[dt0:dt0 + SSD_HEADS],
                                jnp.zeros((LANES - SSD_HEADS, D_MODEL), F32)], axis=0)
    Q = SSD_CHUNK
    lanes_t = lambda v: jnp.repeat(v.reshape(SSD_HEADS // 2, 2, 1), Q, axis=2).reshape(SSD_HEADS // 2, 2 * Q)
    pad_l = lambda v: jnp.pad(v, (0, LANES - SSD_HEADS)).reshape(1, LANES)
    return {
        'g_mix': g_mix.reshape(1, -1), 'g_ffn': g_ffn.reshape(1, -1), 'g_final': g_final.reshape(1, -1),
        'w_t': w_t, 'w_tail_t': w_tail_t,
        'conv_w_ssd': conv_w_ssd, 'conv_b_ssd': conv_b_ssd.reshape(1, -1),
        'dt_bias': pad_l(dt_bias), 'dt_bias_t': lanes_t(dt_bias),
        'a_log': pad_l(a_log), 'a_log_t': lanes_t(a_log),
        'd_skip': jnp.repeat(d_skip, SSD_HEAD_DIM).reshape(1, -1), 'g_ssd_norm': g_ssd_norm.reshape(1, -1),
        'w_br_att': w_br_att, 'w_br_ssd': w_br_ssd, 'w_out': w_out,
        'w_up': w_up, 'conv_w_ffn': conv_w_ffn, 'conv_b_ffn': conv_b_ffn.reshape(1, -1),
        'w_down': w_down,
    }


def kernel(x_prompt, x_sample, cache_k, cache_v, state_ssm, state_ssm_conv, state_ffn_conv, meta_tokens, g_mix, w_in, conv_w_ssd, conv_b_ssd, dt_bias, a_log, d_skip, g_ssd_norm, w_br_att, w_br_ssd, w_out, g_ffn, w_up, conv_w_ffn, conv_b_ffn, w_down, g_final):
    bp, seq, _ = x_prompt.shape
    bs, dec = x_sample.shape[:2]
    assert dec == N_META and seq % KVT_TS == 0
    p = _prep_params(g_mix[0], w_in[0], conv_w_ssd[0], conv_b_ssd[0], dt_bias[0], a_log[0], d_skip[0],
                     g_ssd_norm[0], w_br_att[0], w_br_ssd[0], w_out[0], g_ffn[0], w_up[0], conv_w_ffn[0],
                     conv_b_ffn[0], w_down[0], g_final)
    proj = functools.partial(in_proj, g=p['g_mix'], w_t=p['w_t'], w_tail_t=p['w_tail_t'])
    state_shape = (SSD_WIDTH, SSD_STATE)
    rows = lambda a, b: a.reshape(b, -1, a.shape[-1])

    n_rows = cache_k.shape[2]
    n_tiles = (n_rows - N_META) // KEY_TILE
    assert n_tiles * KEY_TILE + N_META == n_rows
    x16 = jnp.concatenate([meta_tokens, x_sample.reshape(bs * dec, D_MODEL)], axis=0)
    k16, v16, big16, dt16 = (rows(a, 1 + bs) for a in proj(x16, with_kv=True))
    big_m, k_m, v_m, big_s, k_s, v_s = big16[:1], k16[:1], v16[:1], big16[1:], k16[1:], v16[1:]
    att_m = attention_step(big_m, 0, k_m, v_m, None, None, 0, 0, SB_WIDTH)
    cache_t = lambda c: c[0].transpose(0, 2, 3, 1).reshape(bs, SB_WIDTH, n_rows)
    att_s = attention_step(big_s, 0, k_s, v_s, cache_t(cache_k), cache_t(cache_v), n_tiles, N_META, SB_WIDTH)
    conv16 = jnp.concatenate([jnp.zeros((1, SSD_CONV - 1, SSD_CONV_CH), F32), state_ssm_conv[0]], axis=0)
    ffn16 = jnp.concatenate([jnp.zeros((1, FFN_CONV - 1, 2 * D_FF), F32), state_ffn_conv[0]], axis=0)
    xbc = BIG_XBC // SSD_CONV_CH, BIG_Z // SSD_WIDTH
    ssd_m, ssm_m = ssd_mixer(big_m, *xbc, dt16[:1], conv16[:1], jnp.zeros((1,) + state_shape, F32), p)
    ssd_s, ssm_s = ssd_mixer(big_s, *xbc, dt16[1:], conv16[1:], state_ssm[0].reshape((bs,) + state_shape), p)
    y16, conv16, ffn16 = ffn_and_out(big16, jnp.concatenate([att_m, att_s], axis=0),
                                     jnp.concatenate([ssd_m, ssd_s], axis=0), x16, conv16, ffn16, p)
    conv_m, ffn_m, y_sample, conv_s, ffn_s = conv16[:1], ffn16[:1], y16[1:], conv16[1:], ffn16[1:]

    xf = x_prompt.reshape(bp * seq, D_MODEL)
    big_x, dt_x = proj(xf, with_kv=False)
    big_x = rows(big_x, bp)
    k_t, v_t, k_end, v_end = kv_transposed(x_prompt, meta_tokens, p['g_mix'], p['w_t'])
    att_x = attention_prompt(big_x, big_m, k_t, v_t)
    att_end = attention_step(big_x, seq // N_META - 1, k_end, v_end, k_t, v_t, seq // KEY_TILE, 0, ATTN_PROMPT_WIDTH)
    att_x = lax.dynamic_update_slice(att_x, att_end, (0, seq - N_META, 0))
    rep = lambda a: jnp.broadcast_to(a, (bp,) + a.shape[1:])
    ssd_x, ssm_p = ssd_mixer(big_x, *xbc, rows(dt_x, bp), rep(conv_m), rep(ssm_m), p)
    y_prompt, conv_p, ffn_p = ffn_and_out(big_x, att_x, ssd_x, xf, rep(conv_m), rep(ffn_m), p)

    heads = lambda a: a.reshape(a.shape[0], a.shape[1], SB_HEADS, SB_HEAD_DIM)[None]
    heads_t = lambda a: a.reshape(a.shape[0], SB_HEADS, SB_HEAD_DIM, a.shape[2]).transpose(0, 3, 1, 2)[None]
    state5 = lambda a: a.reshape(a.shape[0], SSD_HEADS, SSD_HEAD_DIM, SSD_STATE)[None]
    return (y_prompt, y_sample, heads_t(k_t), heads_t(v_t), state5(ssm_p), conv_p[None], ffn_p[None],
            heads(k_s), heads(v_s), state5(ssm_s), conv_s[None], ffn_s[None])
```

```python
import functools

import numpy as np
import jax
import jax.numpy as jnp
from jax import lax
from jax.experimental import pallas as pl
from jax.experimental.pallas import tpu as pltpu

F32 = jnp.float32
BF16 = jnp.bfloat16

D_MODEL = 1024
N_META = 16
SB_HEADS = 16
SB_HEAD_DIM = 64
SB_WIDTH = SB_HEADS * SB_HEAD_DIM
SSD_WIDTH = 2 * D_MODEL
SSD_HEAD_DIM = 64
SSD_HEADS = SSD_WIDTH // SSD_HEAD_DIM
SSD_GROUPS = 4
SSD_STATE = 128
SSD_CONV = 4
SSD_CONV_CH = SSD_WIDTH + 2 * SSD_GROUPS * SSD_STATE
D_FF = 2816
FFN_CONV = 3
EPS = 1e-6

LANES = 128
SSD_CHUNK = 64
SSD_STEP_CHUNKS = 4
KEY_TILE = 128
GROUP_W = SSD_WIDTH // SSD_GROUPS
DEAD_LOG = -104.0
LOG2E = 1.4426950408889634
ATTN_STATIC_TILES = 2
ATTN_GROUP_TILES = 3
VMEM_LIMIT = 56 * 1024 * 1024


def _cparams(sem):
    return pltpu.CompilerParams(dimension_semantics=sem, vmem_limit_bytes=VMEM_LIMIT)


def _pick(n, cands):
    for c in cands:
        if n % c == 0:
            return c
    return n


def _split2(x):
    hi = x.astype(BF16)
    lo = (x - hi.astype(F32)).astype(BF16)
    return hi, lo


def _split3(x):
    hi = x.astype(BF16)
    r = x - hi.astype(F32)
    mid = r.astype(BF16)
    lo = (r - mid.astype(F32)).astype(BF16)
    return hi, mid, lo


def _dot(a, b):
    return jnp.dot(a, b, preferred_element_type=F32)


def _dot_nt(a, b):
    return lax.dot_general(a, b, (((1,), (1,)), ((), ())), preferred_element_type=F32)


def _softplus(x):
    return jnp.maximum(x, 0.0) + jnp.log(1.0 + jnp.exp(-jnp.abs(x)))


def _sigmoid(x):
    return 1.0 / (1.0 + jnp.exp2(-LOG2E * x))


def _rms_norm_bf16(x, g):
    ms = jnp.mean(x * x, axis=-1, keepdims=True)
    return (x * lax.rsqrt(ms + EPS) * g).astype(BF16)


def _norm_mm_kernel(x_ref, g_ref, w_ref, o_ref, u_ref):
    @pl.when(pl.program_id(1) == 0)
    def _():
        u_ref[...] = _rms_norm_bf16(x_ref[...], g_ref[...])

    o_ref[...] = _dot(u_ref[...], w_ref[...].astype(BF16)).astype(o_ref.dtype)


PROJ_TN = 1024
UP_TN = 1408


def norm_mm(x, g, w, out_dtype):
    m, d = x.shape
    n = w.shape[1]
    tm = _pick(m, (2048, 1024, 512, 256, 128))
    tn = UP_TN
    return pl.pallas_call(
        _norm_mm_kernel,
        grid=(m // tm, n // tn),
        in_specs=[pl.BlockSpec((tm, d), lambda i, j: (i, 0)),
                  pl.BlockSpec((1, d), lambda i, j: (0, 0)),
                  pl.BlockSpec((d, tn), lambda i, j: (0, j))],
        out_specs=pl.BlockSpec((tm, tn), lambda i, j: (i, j)),
        out_shape=jax.ShapeDtypeStruct((m, n), out_dtype),
        scratch_shapes=[pltpu.VMEM((tm, d), BF16)],
        compiler_params=_cparams(("parallel", "arbitrary")),
        name="norm_up_proj",
    )(x, g, w)


_NQ = SB_WIDTH // PROJ_TN
_J_K, _J_V, _J_Z = _NQ, 2 * _NQ, 3 * _NQ
_J_X = _J_Z + SSD_WIDTH // PROJ_TN
_J_G = _J_X + SSD_CONV_CH // PROJ_TN
_J_END = _J_G + 2 * D_MODEL // PROJ_TN
BIG_Z, BIG_Q, BIG_XBC, BIG_GATE = 0, SSD_WIDTH, SSD_WIDTH + SB_WIDTH, SSD_WIDTH + SB_WIDTH + SSD_CONV_CH
BIG_W = BIG_GATE + 2 * D_MODEL


def _big_block(j):
    return jnp.where(j < _J_K, BIG_Q // PROJ_TN + j,
                     jnp.where(j < _J_Z, BIG_Q // PROJ_TN + _NQ - 1,
                               jnp.where(j < _J_X, j - _J_Z + BIG_Z // PROJ_TN,
                                         jnp.where(j < _J_G, j - _J_X + BIG_XBC // PROJ_TN,
                                                   j - _J_G + BIG_GATE // PROJ_TN))))


def _sweep_step(j, with_kv):
    return j if with_kv else jnp.where(j >= _J_K, j + (_J_Z - _J_K), j)


def _in_proj_kernel(*refs, with_kv):
    if with_kv:
        x_ref, g_ref, win_ref, wgate_ref, wdt_ref, k_ref, v_ref, big_ref, dt_ref, u_ref = refs
    else:
        x_ref, g_ref, win_ref, wgate_ref, wdt_ref, big_ref, dt_ref, u_ref = refs
    j = _sweep_step(pl.program_id(1), with_kv)

    def proj(w_ref):
        return _dot_nt(u_ref[...], w_ref[...].astype(BF16))

    @pl.when(pl.program_id(1) == 0)
    def _():
        u_ref[...] = _rms_norm_bf16(x_ref[...], g_ref[...])
        dt_ref[...] = proj(wdt_ref)

    if with_kv:
        @pl.when(jnp.logical_and(j >= _J_K, j < _J_V))
        def _():
            k_ref[...] = proj(win_ref)

        @pl.when(jnp.logical_and(j >= _J_V, j < _J_Z))
        def _():
            v_ref[...] = proj(win_ref)

    @pl.when(jnp.logical_or(j < _J_K, jnp.logical_and(j >= _J_Z, j < _J_G)))
    def _():
        big_ref[...] = proj(win_ref).astype(BF16)

    @pl.when(j >= _J_G)
    def _():
        big_ref[...] = proj(wgate_ref).astype(BF16)


def in_proj(x, g, w_t, w_tail_t, with_kv):
    m, d = x.shape
    tm = _pick(m, (2048, 1024, 512, 256, 128))
    tn = PROJ_TN
    step = functools.partial(_sweep_step, with_kv=with_kv)
    clip = lambda j, lo, n: jnp.clip(step(j) - lo, 0, n - 1)
    kv_specs = [pl.BlockSpec((tm, tn), lambda i, j: (i, clip(j, _J_K, _NQ))),
                pl.BlockSpec((tm, tn), lambda i, j: (i, clip(j, _J_V, _NQ)))]
    kv_shapes = [jax.ShapeDtypeStruct((m, SB_WIDTH), F32)] * 2
    return pl.pallas_call(
        functools.partial(_in_proj_kernel, with_kv=with_kv),
        grid=(m // tm, _J_END if with_kv else _J_END - (_J_Z - _J_K)),
        in_specs=[pl.BlockSpec((tm, d), lambda i, j: (i, 0), pipeline_mode=pl.Buffered(1)),
                  pl.BlockSpec((1, d), lambda i, j: (0, 0)),
                  pl.BlockSpec((tn, d), lambda i, j: (jnp.minimum(step(j), _J_G - 1), 0)),
                  pl.BlockSpec((tn, d), lambda i, j: (clip(j, _J_G, _J_END - _J_G), 0)),
                  pl.BlockSpec((LANES, d), lambda i, j: (2 * D_MODEL // LANES, 0))],
        out_specs=(kv_specs if with_kv else []) + [
            pl.BlockSpec((tm, tn), lambda i, j: (i, _big_block(step(j)))),
            pl.BlockSpec((tm, LANES), lambda i, j: (i, 0))],
        out_shape=(kv_shapes if with_kv else []) + [
            jax.ShapeDtypeStruct((m, BIG_W), BF16), jax.ShapeDtypeStruct((m, LANES), F32)],
        scratch_shapes=[pltpu.VMEM((tm, d), BF16)],
        compiler_params=_cparams(("parallel", "arbitrary")),
        name="in_proj",
    )(x, g, w_t, w_tail_t, w_tail_t)


KVT_TS = 1024


def _kv_t_kernel(x16_ref, xt_ref, meta_ref, g_ref, wk_ref, wv_ref, kt_ref, vt_ref, kl_ref, vl_ref, wk_b, wv_b):
    j = pl.program_id(1)

    @pl.when(jnp.logical_and(pl.program_id(0) == 0, j == 0))
    def _():
        wk_b[...] = wk_ref[...].astype(BF16)
        wv_b[...] = wv_ref[...].astype(BF16)

    first = jnp.where(j == 0, meta_ref[...], x16_ref[0])
    x = jnp.concatenate([first, xt_ref[0, :KVT_TS - N_META, :]], axis=0)
    u = _rms_norm_bf16(x, g_ref[...])
    kt_ref[0] = _dot_nt(wk_b[...], u)
    vt_ref[0] = _dot_nt(wv_b[...], u)

    @pl.when(j == pl.num_programs(1) - 1)
    def _():
        kl_ref[0] = _dot_nt(u[0:N_META, :], wk_b[...])
        vl_ref[0] = _dot_nt(u[0:N_META, :], wv_b[...])


def kv_transposed(x, meta, g, w_t):
    b, s, d = x.shape
    n_pos = N_META + s
    n_tiles = -(-n_pos // KVT_TS)
    assert n_pos - (n_tiles - 1) * KVT_TS == N_META
    per_tile = KVT_TS // N_META
    out_spec = pl.BlockSpec((1, SB_WIDTH, KVT_TS), lambda bi, j: (bi, 0, j))
    last_spec = pl.BlockSpec((1, N_META, SB_WIDTH), lambda bi, j: (bi, 0, 0))
    w_spec = lambda blk: pl.BlockSpec((SB_WIDTH, d), lambda bi, j: (blk, 0), pipeline_mode=pl.Buffered(1))
    return pl.pallas_call(
        _kv_t_kernel,
        grid=(b, n_tiles),
        in_specs=[pl.BlockSpec((1, N_META, d), lambda bi, j: (bi, jnp.maximum(j * per_tile - 1, 0), 0)),
                  pl.BlockSpec((1, KVT_TS, d), lambda bi, j: (bi, jnp.minimum(j, s // KVT_TS - 1), 0)),
                  pl.BlockSpec((N_META, d), lambda bi, j: (0, 0)),
                  pl.BlockSpec((1, d), lambda bi, j: (0, 0)),
                  w_spec(1), w_spec(2)],
        out_specs=[out_spec, out_spec, last_spec, last_spec],
        out_shape=[jax.ShapeDtypeStruct((b, SB_WIDTH, n_pos), F32)] * 2
        + [jax.ShapeDtypeStruct((b, N_META, SB_WIDTH), F32)] * 2,
        scratch_shapes=[pltpu.VMEM((SB_WIDTH, d), BF16)] * 2,
        compiler_params=_cparams(("arbitrary", "arbitrary")),
        name="kv_transposed",
    )(x, x, meta, g, w_t, w_t)


def _stick_breaking_tiles(qs, u2, tiles, carries, accs):
    n_heads = len(qs)
    units = [(t, h) for t in range(len(tiles)) for h in range(n_heads)]
    ps = lambda h: slice((h // 2) * LANES, (h // 2 + 1) * LANES)

    def scores(t, h):
        kind, keys = tiles[t][0], tiles[t][1]
        return _dot_nt(qs[h], keys[:, ps(h)]) if kind == 'rows' else _dot(qs[h], keys[ps(h), :])

    z = {u: scores(*u) for u in units}
    log_beta, cat = {}, {}
    for u in units:
        mask = tiles[u[0]][3]
        soft = jnp.log(1.0 + jnp.exp2(-LOG2E * jnp.abs(z[u])))
        log_beta[u] = jnp.minimum(z[u], 0.0) - soft
        log_keep = log_beta[u] - z[u]
        if mask is not None:
            log_keep = jnp.where(mask, log_keep, 0.0)
        cat[u] = jnp.concatenate(_split2(log_keep), axis=1)
    r = {u: _dot(cat[u], u2) for u in units}
    carries, accs = list(carries), list(accs)
    w = {}
    for t, h in units:
        mask = tiles[t][3]
        wt = jnp.exp2(LOG2E * (log_beta[(t, h)] + r[(t, h)][:, :LANES] + carries[h]))
        if mask is not None:
            wt = jnp.where(mask, wt, 0.0)
        w[(t, h)] = wt.astype(BF16)
        carries[h] = carries[h] + r[(t, h)][:, LANES:]
    for t, h in units:
        kind, vals = tiles[t][0], tiles[t][2]
        pv = _dot(w[(t, h)], vals[:, ps(h)]) if kind == 'rows' else _dot_nt(w[(t, h)], vals[ps(h), :])
        accs[h] = accs[h] + pv
    return carries, accs


def _head_queries(q, lo_half):
    qs = []
    for h in range(q.shape[1] // SB_HEAD_DIM):
        qp = q[:, (h // 2) * LANES:(h // 2 + 1) * LANES]
        keep = lo_half if h % 2 == 0 else jnp.logical_not(lo_half)
        qs.append(jnp.where(keep, qp, jnp.zeros_like(qp)) * jnp.asarray(SB_HEAD_DIM ** -0.5, BF16))
    return qs


def _any_alive(carries):
    m = carries[0]
    for c in carries[1:]:
        m = jnp.maximum(m, c)
    return (jnp.max(m) > DEAD_LOG).astype(jnp.int32)


def _attend(qs, u2, first_tiles, load_tile, j0, lane):
    tq = qs[0].shape[0]

    def cache_tile(j):
        if isinstance(j, int):
            mask = None if j >= 0 else lane < 0
            kt, vt = load_tile(max(j, 0))
        else:
            mask = jnp.logical_and(j >= 0, lane >= 0)
            kt, vt = load_tile(jnp.maximum(j, 0))
        return 'cols', kt.astype(BF16), vt.astype(BF16), mask

    tiles = list(first_tiles)
    if load_tile is not None:
        for _ in range(ATTN_STATIC_TILES):
            tiles.append(cache_tile(j0))
            j0 = j0 - 1
    zeros = jnp.zeros((tq, LANES), F32)
    carries, accs = [zeros] * len(qs), [zeros] * len(qs)
    for g in range(0, len(tiles), ATTN_GROUP_TILES):
        carries, accs = _stick_breaking_tiles(qs, u2, tiles[g:g + ATTN_GROUP_TILES], carries, accs)
    if load_tile is not None:
        def cond(s):
            return jnp.logical_and(s[0] >= 0, s[1] > 0)

        def body(s):
            j, _, carries, accs = s
            carries, accs = _stick_breaking_tiles(qs, u2, [cache_tile(j)], carries, accs)
            return j - 1, _any_alive(carries), tuple(carries), tuple(accs)

        state = (jnp.asarray(j0, jnp.int32), _any_alive(carries), tuple(carries), tuple(accs))
        _, _, carries, accs = lax.while_loop(cond, body, state)
    return accs


def _pair_outputs(accs, lo_half, dtype):
    return [jnp.where(lo_half, accs[2 * p], accs[2 * p + 1]).astype(dtype) for p in range(len(accs) // 2)]


def _attn_prompt_kernel(q16_ref, qt_ref, qm_ref, u2_ref, kt_ref, vt_ref, o_ref):
    step = pl.program_id(2)
    tq = KEY_TILE
    lane = lax.broadcasted_iota(jnp.int32, (tq, LANES), 1)
    row = lax.broadcasted_iota(jnp.int32, (tq, LANES), 0)
    lo_half = lane < SB_HEAD_DIM

    def load_tile(j):
        off = pl.multiple_of(j * KEY_TILE, KEY_TILE)
        return kt_ref[0, :, pl.ds(off, KEY_TILE)], vt_ref[0, :, pl.ds(off, KEY_TILE)]

    for sub in range(ATTN_STEP_TILES):
        m = step * ATTN_STEP_TILES + sub
        if sub == 0:
            first = jnp.where(step == 0, qm_ref[0], q16_ref[0])
            q = jnp.concatenate([first, qt_ref[0, :tq - N_META, :]], axis=0)
        else:
            q = qt_ref[0, sub * tq - N_META:(sub + 1) * tq - N_META, :]
        qs = _head_queries(q, lo_half)
        kd, vd = load_tile(m)
        diag = ('cols', kd.astype(BF16), vd.astype(BF16), lane < row)
        accs = _attend(qs, u2_ref[...], [diag], load_tile, m - 1, lane)
        out = jnp.concatenate(_pair_outputs(accs, lo_half, o_ref.dtype), axis=1)
        if sub == 0:
            @pl.when(step == 0)
            def _(out=out):
                o_ref[0, 0:tq - N_META, :] = out[N_META:, :]
                o_ref[0, o_ref.shape[1] - N_META:, :] = jnp.zeros((N_META, o_ref.shape[2]), o_ref.dtype)

            @pl.when(step > 0)
            def _(out=out, m=m):
                o_ref[0, pl.ds(pl.multiple_of(m * KEY_TILE - N_META, N_META), tq), :] = out
        else:
            o_ref[0, pl.ds(pl.multiple_of(m * KEY_TILE - N_META, N_META), tq), :] = out


def _attn_step_kernel(*refs, n_tail, n_cache_tiles):
    if n_cache_tiles:
        (q_ref, kd_ref, vd_ref, u2_ref, ktl_ref, vtl_ref, kn_ref, vn_ref, kt_any, vt_any, o_ref,
         kd_scr, vd_scr, kbuf, vbuf) = refs
        width = q_ref.shape[2]
        n_near = kn_ref.shape[2] // KEY_TILE

        def load_tile(j):
            if isinstance(j, int) and j >= n_cache_tiles - n_near:
                ls = slice((j - (n_cache_tiles - n_near)) * KEY_TILE, (j - (n_cache_tiles - n_near) + 1) * KEY_TILE)
                return kn_ref[0, :, ls], vn_ref[0, :, ls]
            off = j * KEY_TILE if isinstance(j, int) else pl.multiple_of(j * KEY_TILE, KEY_TILE)
            src = (pl.program_id(0), pl.ds(pl.program_id(1) * width, width), pl.ds(off, KEY_TILE))
            pltpu.sync_copy(kt_any.at[src], kbuf)
            pltpu.sync_copy(vt_any.at[src], vbuf)
            return kbuf[...], vbuf[...]
    else:
        q_ref, kd_ref, vd_ref, u2_ref, o_ref, kd_scr, vd_scr = refs
        load_tile = None
    tq = q_ref.shape[1]
    n_rows = kd_ref.shape[1]
    lane = lax.broadcasted_iota(jnp.int32, (tq, LANES), 1)
    row = lax.broadcasted_iota(jnp.int32, (tq, LANES), 0)
    lo_half = lane < SB_HEAD_DIM
    n_free = n_tail if n_cache_tiles else 0
    kd_scr[...] = jnp.zeros_like(kd_scr)
    vd_scr[...] = jnp.zeros_like(vd_scr)
    if n_free:
        for scr, tl_ref in ((kd_scr, ktl_ref), (vd_scr, vtl_ref)):
            for c in range(width // LANES):
                scr[0:n_free, c * LANES:(c + 1) * LANES] = (
                    tl_ref[0, c * LANES:(c + 1) * LANES, :].T[0:n_free, :].astype(BF16))
    kd_scr[n_free:n_free + n_rows, :] = kd_ref[0].astype(BF16)
    vd_scr[n_free:n_free + n_rows, :] = vd_ref[0].astype(BF16)
    first = [('rows', kd_scr[...], vd_scr[...], lane < row + n_free)]
    qs = _head_queries(q_ref[0], lo_half)
    accs = _attend(qs, u2_ref[...], first, load_tile, n_cache_tiles - 1, lane)
    for p, out in enumerate(_pair_outputs(accs, lo_half, o_ref.dtype)):
        o_ref[0, :, p * LANES:(p + 1) * LANES] = out


def _cumsum_rhs():
    s = np.arange(KEY_TILE)
    strict = (s[:, None] > s[None, :]).astype(np.float32)
    half = np.concatenate([strict, np.ones((KEY_TILE, KEY_TILE), np.float32)], axis=1)
    return jnp.asarray(np.concatenate([half, half], axis=0), BF16)


ATTN_PROMPT_WIDTH = 8 * SB_HEAD_DIM
ATTN_STEP_TILES = 2


def attention_prompt(big, big_meta, k_t, v_t):
    b, s, _ = big.shape
    width = ATTN_PROMPT_WIDTH
    qb0 = BIG_Q // width
    rows = ATTN_STEP_TILES * KEY_TILE
    per_step = rows // N_META
    n_pos = k_t.shape[2]
    cache = pl.BlockSpec((1, width, n_pos), lambda bi, hp, m: (bi, hp, 0))
    return pl.pallas_call(
        _attn_prompt_kernel,
        grid=(b, SB_WIDTH // width, s // rows),
        in_specs=[pl.BlockSpec((1, N_META, width), lambda bi, hp, m: (bi, jnp.maximum(m * per_step - 1, 0), qb0 + hp)),
                  pl.BlockSpec((1, rows, width), lambda bi, hp, m: (bi, m, qb0 + hp)),
                  pl.BlockSpec((1, N_META, width), lambda bi, hp, m: (0, 0, qb0 + hp)),
                  pl.BlockSpec((2 * KEY_TILE, 2 * KEY_TILE), lambda bi, hp, m: (0, 0)),
                  cache, cache],
        out_specs=pl.BlockSpec((1, s, width), lambda bi, hp, m: (bi, 0, hp)),
        out_shape=jax.ShapeDtypeStruct((b, s, SB_WIDTH), BF16),
        compiler_params=_cparams(("parallel", "parallel", "arbitrary")),
        name="stick_breaking_prompt",
    )(big, big, big_meta, _cumsum_rhs(), k_t, v_t)


def attention_step(big, q_row_blk, k_rows, v_rows, k_t, v_t, n_cache_tiles, n_tail, width):
    b = big.shape[0]
    n_rows = k_rows.shape[1]
    qb0 = BIG_Q // width
    rows = pl.BlockSpec((1, n_rows, width), lambda bi, hp: (bi, 0, hp))
    in_specs = [pl.BlockSpec((1, N_META, width), lambda bi, hp: (bi, q_row_blk, qb0 + hp)), rows, rows,
                pl.BlockSpec((2 * KEY_TILE, 2 * KEY_TILE), lambda bi, hp: (0, 0))]
    args = [big, k_rows, v_rows, _cumsum_rhs()]
    scratch = [pltpu.VMEM((KEY_TILE, width), BF16)] * 2
    if n_cache_tiles:
        near = ATTN_STATIC_TILES * KEY_TILE
        assert n_cache_tiles % ATTN_STATIC_TILES == 0
        near_blk = n_cache_tiles // ATTN_STATIC_TILES - 1
        in_specs += [pl.BlockSpec((1, width, KEY_TILE), lambda bi, hp: (bi, hp, n_cache_tiles))] * 2
        in_specs += [pl.BlockSpec((1, width, near), lambda bi, hp: (bi, hp, near_blk))] * 2
        in_specs += [pl.BlockSpec(memory_space=pl.ANY)] * 2
        args += [k_t, v_t, k_t, v_t, k_t, v_t]
        scratch += [pltpu.VMEM((width, KEY_TILE), F32)] * 2
    return pl.pallas_call(
        functools.partial(_attn_step_kernel, n_tail=n_tail, n_cache_tiles=n_cache_tiles),
        grid=(b, SB_WIDTH // width),
        in_specs=in_specs,
        out_specs=pl.BlockSpec((1, N_META, width), lambda bi, hp: (bi, 0, hp)),
        out_shape=jax.ShapeDtypeStruct((b, N_META, SB_WIDTH), BF16),
        scratch_shapes=scratch,
        compiler_params=_cparams(("parallel", "parallel")),
        name="stick_breaking_step",
    )(*args)


def _rows8(op, x, r8):
    return op(x.reshape(x.shape[0] // 8, 8, x.shape[1]), r8[None]).reshape(x.shape)


def _ssd_kernel(xbc_ref, x16_ref, z_ref, dt_ref, dtT_ref, prev_ref, h0_ref, cw_ref, cb_ref, dtb_ref, dtbT_ref,
                alog_ref, alogT_ref, dskip_ref, gn_ref, e3_ref, ltri_ref, ublk_ref, shift_ref,
                y_ref, hfin_ref, st_ref, *, rows, n_sub, n_chunks):
    Q = SSD_CHUNK
    c = pl.program_id(1)
    n_blk = SSD_WIDTH // LANES
    mul, sub = jnp.multiply, jnp.subtract

    @pl.when(c == 0)
    def _():
        for j in range(n_blk):
            st_ref[:, j * LANES:(j + 1) * LANES] = h0_ref[0, j * LANES:(j + 1) * LANES, :].T

    def pad_rows(v):
        if rows == Q:
            return v
        return jnp.concatenate([v, jnp.zeros((Q - rows, v.shape[1]), v.dtype)], axis=0)

    lane = lax.broadcasted_iota(jnp.int32, (Q, LANES), 1)
    rowq = lax.broadcasted_iota(jnp.int32, (Q, LANES), 0)
    causal2 = (lane % Q) <= rowq
    lo_half = lane < SSD_HEAD_DIM
    decay_rate = -LOG2E * jnp.exp(alog_ref[...])
    decay_rate_t = -LOG2E * jnp.exp(alogT_ref[...])

    for k in range(n_sub):
        r0 = k * Q
        hi, lo = _split2(prev_ref[0])
        if k == 0:
            hi = jnp.where(c == 0, hi, x16_ref[0])
            lo = jnp.where(c == 0, lo, jnp.zeros_like(lo))
        else:
            hi, lo = xbc_ref[0, r0 - 16:r0, :], jnp.zeros_like(lo)
        window = jnp.concatenate([pad_rows(xbc_ref[0, r0:r0 + rows, :]), hi, lo,
                                  jnp.zeros((Q - 32, SSD_CONV_CH), BF16)], axis=0)
        shifted = _dot(shift_ref[...], window)
        conv = _rows8(mul, shifted[0:Q], cw_ref[0:8, :])
        for i in range(1, SSD_CONV):
            conv = conv + _rows8(mul, shifted[i * Q:(i + 1) * Q], cw_ref[8 * i:8 * i + 8, :])
        conv = _rows8(jnp.add, conv, cb_ref[...])
        xc = conv * _sigmoid(conv)
        xs = xc[:, :SSD_WIDTH]
        b_all = xc[:, SSD_WIDTH:SSD_WIDTH + SSD_GROUPS * SSD_STATE]
        c_all = xc[:, SSD_WIDTH + SSD_GROUPS * SSD_STATE:]

        dt = _softplus(pad_rows(dt_ref[0, r0:r0 + rows, :]) + dtb_ref[...])
        dt_t = _softplus(dtT_ref[0, k] + dtbT_ref[...])
        if rows < Q:
            rowi = lax.broadcasted_iota(jnp.int32, (Q, 1), 0)
            dt = jnp.where(rowi < rows, dt, 0.0)
            xs = jnp.where(rowi < rows, xs, 0.0)
            lane_t = lax.broadcasted_iota(jnp.int32, dt_t.shape, 1)
            dt_t = jnp.where(lane_t % Q < rows, dt_t, 0.0)
        da = dt * decay_rate
        da_t = dt_t * decay_rate_t

        a_cum = _dot(ltri_ref[...], jnp.concatenate(_split3(da), axis=0))
        a_cum_t = _dot(jnp.concatenate(_split3(da_t), axis=1), ublk_ref[...])
        dt_exp = _dot(jnp.concatenate(_split2(dt), axis=1), e3_ref[0:2 * LANES, :])
        a_exp = _dot(jnp.concatenate(_split3(a_cum), axis=1), e3_ref[...])
        a_last = jnp.broadcast_to(a_exp[Q - 1:Q, :], (8, SSD_WIDTH))

        xdt = xs * dt_exp
        xdt_b = xdt.astype(BF16)
        xw = (xdt * jnp.exp2(-_rows8(sub, a_exp, a_last))).astype(BF16)
        xw_pad = jnp.concatenate([xw, jnp.zeros_like(xw)], axis=0)
        chunk_decay = jnp.exp2(a_last)
        grow = jnp.exp2(a_exp)

        y_parts = []
        for g in range(SSD_GROUPS):
            bg = b_all[:, g * SSD_STATE:(g + 1) * SSD_STATE]
            cg = c_all[:, g * SSD_STATE:(g + 1) * SSD_STATE].astype(BF16)
            bg_b = bg.astype(BF16)
            cb2 = _dot_nt(cg, jnp.concatenate([bg_b, bg_b], axis=0))
            gs = slice(g * GROUP_W, (g + 1) * GROUP_W)
            st_g = st_ref[:, gs]
            y_off = _dot(cg, st_g.astype(BF16)) * grow[:, gs]
            bg_t = jnp.concatenate([bg, jnp.zeros_like(bg)], axis=0).T.astype(BF16)
            st_ref[:, gs] = _rows8(mul, st_g, chunk_decay[:, gs]) + _dot(bg_t, xw_pad[:, gs])
            pair_out = []
            for kk in range(GROUP_W // LANES):
                i = g * (GROUP_W // LANES) + kk
                ps = slice(i * LANES, (i + 1) * LANES)
                a_row = jnp.broadcast_to(a_cum_t[i:i + 1, :], (8, LANES))
                decay = jnp.exp2(jnp.minimum(_rows8(sub, a_exp[:, ps], a_row), 0.0))
                m2 = jnp.where(causal2, cb2 * decay, 0.0).astype(BF16)
                xp = xdt_b[:, ps]
                zero = jnp.zeros_like(xp)
                xbd = jnp.concatenate([jnp.where(lo_half, xp, zero), jnp.where(lo_half, zero, xp)], axis=0)
                pair_out.append(_dot(m2, xbd))
            y_diag = jnp.concatenate(pair_out, axis=1)
            y = y_diag + y_off + _rows8(mul, xs[:, gs], dskip_ref[:, gs])
            zg = pad_rows(z_ref[0, r0:r0 + rows, gs].astype(F32))
            y = y * (zg * _sigmoid(zg))
            ms = jnp.mean(y * y, axis=-1, keepdims=True)
            y_parts.append(_rows8(mul, y * lax.rsqrt(ms + EPS), gn_ref[:, gs]))
        y_all = jnp.concatenate(y_parts, axis=1)
        y_ref[0, r0:r0 + rows, :] = y_all[:rows].astype(y_ref.dtype)

    @pl.when(c == n_chunks - 1)
    def _():
        for j in range(n_blk):
            hfin_ref[0, j * LANES:(j + 1) * LANES, :] = st_ref[:, j * LANES:(j + 1) * LANES].T


def _ssd_constants():
    Q = SSD_CHUNK
    e = np.zeros((LANES, SSD_WIDTH), np.float32)
    for h in range(SSD_HEADS):
        e[h, h * SSD_HEAD_DIM:(h + 1) * SSD_HEAD_DIM] = 1.0
    e3 = np.concatenate([e, e, e], axis=0)
    t = np.arange(Q)
    ltri = (t[:, None] >= t[None, :]).astype(np.float32)
    ltri3 = np.concatenate([ltri, ltri, ltri], axis=1)
    ublk = np.zeros((LANES, LANES), np.float32)
    ublk[:Q, :Q] = ltri.T
    ublk[Q:, Q:] = ltri.T
    ublk3 = np.concatenate([ublk, ublk, ublk], axis=0)
    shift = np.zeros((SSD_CONV * Q, 2 * Q), np.float32)
    for i in range(SSD_CONV):
        for r in range(Q):
            src = r - (SSD_CONV - 1 - i)
            if src >= 0:
                shift[i * Q + r, src] = 1.0
            else:
                shift[i * Q + r, Q + 16 + src] = 1.0
                shift[i * Q + r, Q + 32 + src] = 1.0
    return jnp.asarray(e3, BF16), jnp.asarray(ltri3, BF16), jnp.asarray(ublk3, BF16), jnp.asarray(shift, BF16)


def ssd_mixer(big, xbc_blk, z_blk, dt_raw, conv_prev, h0, p):
    b, s, _ = big.shape
    Q = SSD_CHUNK
    rows = min(s, Q)
    n_sub = SSD_STEP_CHUNKS if s % (SSD_STEP_CHUNKS * Q) == 0 else 1
    blk = rows * n_sub
    n_chunks = -(-s // blk)
    dtp = dt_raw[:, :, :SSD_HEADS]
    if s < Q:
        dtp = jnp.pad(dtp, ((0, 0), (0, Q - s), (0, 0)))
    dt_t = dtp.reshape(b, n_chunks * n_sub, Q, SSD_HEADS).transpose(0, 1, 3, 2).reshape(
        b, n_chunks * n_sub, SSD_HEADS // 2, 2 * Q)
    e3, ltri3, ublk3, shift = _ssd_constants()
    prev16 = jnp.pad(conv_prev, ((0, 0), (16 - (SSD_CONV - 1), 0), (0, 0)))
    rep8 = lambda v: jnp.repeat(v, 8, axis=0)
    const = lambda shape: pl.BlockSpec(shape, lambda bi, ci: (0,) * len(shape))
    in_specs = [
        pl.BlockSpec((1, blk, SSD_CONV_CH), lambda bi, ci: (bi, ci, xbc_blk)),
        pl.BlockSpec((1, 16, SSD_CONV_CH), lambda bi, ci: (bi, jnp.maximum(ci * (blk // 16) - 1, 0), xbc_blk)),
        pl.BlockSpec((1, blk, SSD_WIDTH), lambda bi, ci: (bi, ci, z_blk)),
        pl.BlockSpec((1, blk, LANES), lambda bi, ci: (bi, ci, 0)),
        pl.BlockSpec((1, n_sub, SSD_HEADS // 2, 2 * Q), lambda bi, ci: (bi, ci, 0, 0)),
        pl.BlockSpec((1, 16, SSD_CONV_CH), lambda bi, ci: (bi, 0, 0)),
        pl.BlockSpec((1, SSD_WIDTH, SSD_STATE), lambda bi, ci: (bi, 0, 0)),
        const((8 * SSD_CONV, SSD_CONV_CH)), const((8, SSD_CONV_CH)),
        const((1, LANES)), const((SSD_HEADS // 2, 2 * Q)),
        const((1, LANES)), const((SSD_HEADS // 2, 2 * Q)),
        const((8, SSD_WIDTH)), const((8, SSD_WIDTH)),
        const(e3.shape), const(ltri3.shape), const(ublk3.shape), const(shift.shape),
    ]
    y, h_fin = pl.pallas_call(
        functools.partial(_ssd_kernel, rows=rows, n_sub=n_sub, n_chunks=n_chunks),
        grid=(b, n_chunks),
        in_specs=in_specs,
        out_specs=[pl.BlockSpec((1, blk, SSD_WIDTH), lambda bi, ci: (bi, ci, 0)),
                   pl.BlockSpec((1, SSD_WIDTH, SSD_STATE), lambda bi, ci: (bi, 0, 0))],
        out_shape=[jax.ShapeDtypeStruct((b, s, SSD_WIDTH), BF16),
                   jax.ShapeDtypeStruct((b, SSD_WIDTH, SSD_STATE), F32)],
        scratch_shapes=[pltpu.VMEM((SSD_STATE, SSD_WIDTH), F32)],
        compiler_params=_cparams(("parallel", "arbitrary")),
        name="ssd_mixer",
    )(big, big, big, dt_raw, dt_t, prev16, h0, rep8(p['conv_w_ssd']), rep8(p['conv_b_ssd']), p['dt_bias'], p['dt_bias_t'],
      p['a_log'], p['a_log_t'], rep8(p['d_skip']), rep8(p['g_ssd_norm']), e3, ltri3, ublk3, shift)
    return y, h_fin


def _merge_kernel(att_ref, ssd_ref, ga_ref, gs_ref, h_ref, wa_ref, ws_ref, wo_ref, o_ref, wa_b, ws_b, wo_b):
    @pl.when(pl.program_id(0) == 0)
    def _():
        wa_b[...] = wa_ref[...].astype(BF16)
        ws_b[...] = ws_ref[...].astype(BF16)
        wo_b[...] = wo_ref[...].astype(BF16)

    a = _dot(att_ref[...], wa_b[...])
    s = _dot(ssd_ref[...], ws_b[...])
    merged = _sigmoid(ga_ref[...].astype(F32)) * a + _sigmoid(gs_ref[...].astype(F32)) * s
    o_ref[...] = h_ref[...] + _dot(merged.astype(BF16), wo_b[...])


def _resident(a):
    return pl.BlockSpec(a.shape, lambda i: (0,) * a.ndim, pipeline_mode=pl.Buffered(1))


def merge(att, ssd, big, h, p):
    m = h.shape[0]
    tm = _pick(m, (512, 256, 128))
    row = lambda w: pl.BlockSpec((tm, w), lambda i: (i, 0))
    gate_blk = BIG_GATE // D_MODEL
    ws = [p['w_br_att'], p['w_br_ssd'], p['w_out']]
    return pl.pallas_call(
        _merge_kernel,
        grid=(m // tm,),
        in_specs=[row(SB_WIDTH), row(SSD_WIDTH),
                  pl.BlockSpec((tm, D_MODEL), lambda i: (i, gate_blk)),
                  pl.BlockSpec((tm, D_MODEL), lambda i: (i, gate_blk + 1)),
                  row(D_MODEL)] + [_resident(w) for w in ws],
        out_specs=row(D_MODEL),
        out_shape=jax.ShapeDtypeStruct((m, D_MODEL), F32),
        scratch_shapes=[pltpu.VMEM(w.shape, BF16) for w in ws],
        compiler_params=_cparams(("arbitrary",)),
        name="merge_out_proj",
    )(att, ssd, big, big, h, *ws)


def _ffn_act_kernel(ug_ref, uv_ref, pg_ref, pv_ref, wg_ref, wv_ref, bg_ref, bv_ref, o_ref, gbuf_ref, vbuf_ref):
    nb, ts = ug_ref.shape[:2]
    k1 = FFN_CONV - 1

    def conv(n, u_ref, p_ref, w_ref, b_ref, buf_ref):
        buf_ref[n, 8 - k1:8, :] = p_ref[n]
        buf_ref[n, 8:8 + ts, :] = u_ref[n].astype(F32)
        y = b_ref[...]
        for i in range(FFN_CONV):
            y = y + buf_ref[n, 8 - k1 + i:8 - k1 + i + ts, :] * w_ref[i:i + 1, :]
        return y

    for n in range(nb):
        gate = conv(n, ug_ref, pg_ref, wg_ref, bg_ref, gbuf_ref)
        val = conv(n, uv_ref, pv_ref, wv_ref, bv_ref, vbuf_ref)
        o_ref[n] = (gate * _sigmoid(gate) * val).astype(o_ref.dtype)


UPF_TM = 512
UPF_TN = D_FF // 2


def _up_ffn_kernel(h_ref, h16_ref, g_ref, wg_ref, wv_ref, pg_ref, pv_ref, cwg_ref, cwv_ref, cbg_ref, cbv_ref,
                   o_ref, gl_ref, vl_ref, wg_b, wv_b, gbuf_ref, vbuf_ref):
    i = pl.program_id(2)
    tm = h_ref.shape[1]

    @pl.when(jnp.logical_and(pl.program_id(1) == 0, i == 0))
    def _():
        wg_b[...] = wg_ref[...].astype(BF16)
        wv_b[...] = wv_ref[...].astype(BF16)

    u = _rms_norm_bf16(h_ref[0], g_ref[...])
    u16 = _rms_norm_bf16(h16_ref[0], g_ref[...])

    def conv(w_b, p_ref, last_ref, buf_ref, cw_ref, cb_ref):
        up = _dot(u, w_b[...])
        buf_ref[0:16, :] = jnp.where(i == 0, p_ref[0], _dot(u16, w_b[...]))
        buf_ref[16:16 + tm, :] = up
        last_ref[0] = up[tm - 16:tm, :]
        y = _rows8(jnp.multiply, up, cw_ref[8 * (FFN_CONV - 1):8 * FFN_CONV, :])
        for t in range(FFN_CONV - 1):
            r0 = 16 - (FFN_CONV - 1) + t
            y = y + _rows8(jnp.multiply, buf_ref[r0:r0 + tm, :], cw_ref[8 * t:8 * t + 8, :])
        return _rows8(jnp.add, y, cb_ref[...])

    gate = conv(wg_b, pg_ref, gl_ref, gbuf_ref, cwg_ref, cbg_ref)
    val = conv(wv_b, pv_ref, vl_ref, vbuf_ref, cwv_ref, cbv_ref)
    o_ref[0] = (gate * _sigmoid(gate) * val).astype(o_ref.dtype)


def up_ffn_long(h, prev, p):
    b, s, d = h.shape
    tm, tn = UPF_TM, UPF_TN
    nj = D_FF // tn
    prev16 = jnp.pad(prev, ((0, 0), (16 - (FFN_CONV - 1), 0), (0, 0)))
    rep8 = lambda v: jnp.repeat(v, 8, axis=0)
    w8, b8 = rep8(p['conv_w_ffn']), rep8(p['conv_b_ffn'])
    cols = lambda rows, half: pl.BlockSpec((rows, tn), lambda j, bi, i: (0, half * nj + j))
    w_spec = lambda half: pl.BlockSpec((d, tn), lambda j, bi, i: (0, half * nj + j), pipeline_mode=pl.Buffered(1))
    prevs = lambda half: pl.BlockSpec((1, 16, tn), lambda j, bi, i: (bi, 0, half * nj + j))
    last_spec = pl.BlockSpec((1, 16, tn), lambda j, bi, i: (bi, 0, j))
    return pl.pallas_call(
        _up_ffn_kernel,
        grid=(nj, b, s // tm),
        in_specs=[pl.BlockSpec((1, tm, d), lambda j, bi, i: (bi, i, 0)),
                  pl.BlockSpec((1, 16, d), lambda j, bi, i: (bi, jnp.maximum(i * (tm // 16) - 1, 0), 0)),
                  pl.BlockSpec((1, d), lambda j, bi, i: (0, 0)),
                  w_spec(0), w_spec(1), prevs(0), prevs(1),
                  cols(8 * FFN_CONV, 0), cols(8 * FFN_CONV, 1), cols(8, 0), cols(8, 1)],
        out_specs=[pl.BlockSpec((1, tm, tn), lambda j, bi, i: (bi, i, j)), last_spec, last_spec],
        out_shape=[jax.ShapeDtypeStruct((b, s, D_FF), BF16)] + [jax.ShapeDtypeStruct((b, 16, D_FF), F32)] * 2,
        scratch_shapes=[pltpu.VMEM((d, tn), BF16)] * 2 + [pltpu.VMEM((16 + tm, tn), F32)] * 2,
        compiler_params=_cparams(("arbitrary", "arbitrary", "arbitrary")),
        name="up_proj_conv_ffn",
    )(h, h, p['g_ffn'], p['w_up'], p['w_up'], prev16, prev16, w8, w8, b8, b8)


def ffn_act(up, prev, p):
    b, s, _ = up.shape
    nb = _pick(b, (16, 11, 8, 4, 2))
    half = lambda blk: pl.BlockSpec((nb, s, D_FF), lambda bi: (bi, 0, blk))
    prevs = lambda blk: pl.BlockSpec((nb, FFN_CONV - 1, D_FF), lambda bi: (bi, 0, blk))
    wspec = lambda blk: pl.BlockSpec((FFN_CONV, D_FF), lambda bi: (0, blk))
    bspec = lambda blk: pl.BlockSpec((1, D_FF), lambda bi: (0, blk))
    return pl.pallas_call(
        _ffn_act_kernel,
        grid=(b // nb,),
        in_specs=[half(0), half(1), prevs(0), prevs(1), wspec(0), wspec(1), bspec(0), bspec(1)],
        out_specs=pl.BlockSpec((nb, s, D_FF), lambda bi: (bi, 0, 0)),
        out_shape=jax.ShapeDtypeStruct((b, s, D_FF), BF16),
        scratch_shapes=[pltpu.VMEM((nb, 8 + s, D_FF), F32)] * 2,
        compiler_params=_cparams(("parallel",)),
        name="ffn_conv_act",
    )(up, up, prev, prev, p['conv_w_ffn'], p['conv_w_ffn'], p['conv_b_ffn'], p['conv_b_ffn'])


def _down_kernel(a_ref, w_ref, h_ref, g_ref, o_ref, w_b):
    @pl.when(pl.program_id(0) == 0)
    def _():
        w_b[...] = w_ref[...].astype(BF16)

    h = h_ref[...] + _dot(a_ref[...], w_b[...])
    ms = jnp.mean(h * h, axis=-1, keepdims=True)
    o_ref[...] = h * lax.rsqrt(ms + EPS) * g_ref[...]


def down_norm(act, h, p):
    m = h.shape[0]
    tm = _pick(m, (512, 256, 128))
    return pl.pallas_call(
        _down_kernel,
        grid=(m // tm,),
        in_specs=[pl.BlockSpec((tm, D_FF), lambda i: (i, 0)),
                  _resident(p['w_down']),
                  pl.BlockSpec((tm, D_MODEL), lambda i: (i, 0)),
                  pl.BlockSpec((1, D_MODEL), lambda i: (0, 0))],
        out_specs=pl.BlockSpec((tm, D_MODEL), lambda i: (i, 0)),
        out_shape=jax.ShapeDtypeStruct((m, D_MODEL), F32),
        scratch_shapes=[pltpu.VMEM(p['w_down'].shape, BF16)],
        compiler_params=_cparams(("arbitrary",)),
        name="down_proj_norm",
    )(act, p['w_down'], h, p['g_final'])


def ffn_and_out(big3, att, ssd, hf, ssd_conv_prev, ffn_conv_prev, p):
    b, s, _ = big3.shape
    m = b * s
    h1 = merge(att.reshape(m, SB_WIDTH), ssd.reshape(m, SSD_WIDTH), big3.reshape(m, BIG_W), hf, p)
    if s % UPF_TM == 0:
        act, up_g, up_v = up_ffn_long(h1.reshape(b, s, D_MODEL), ffn_conv_prev, p)
        up_last = jnp.concatenate([up_g, up_v], axis=2)
    else:
        up_last = norm_mm(h1, p['g_ffn'], p['w_up'], BF16).reshape(b, s, 2 * D_FF)
        act = ffn_act(up_last, ffn_conv_prev, p)
    y = down_norm(act.reshape(m, D_FF), h1, p).reshape(b, s, D_MODEL)
    xbc_rows = jnp.concatenate([ssd_conv_prev, big3[:, :, BIG_XBC:BIG_XBC + SSD_CONV_CH][:, -(SSD_CONV - 1):].astype(F32)],
                               axis=1)[:, -(SSD_CONV - 1):]
    up_rows = jnp.concatenate([ffn_conv_prev, up_last[:, -(FFN_CONV - 1):].astype(F32)], axis=1)[:, -(FFN_CONV - 1):]
    return y, xbc_rows, up_rows


def _prep_params(g_mix, w_in, conv_w_ssd, conv_b_ssd, dt_bias, a_log, d_skip, g_ssd_norm, w_br_att, w_br_ssd,
                 w_out, g_ffn, w_up, conv_w_ffn, conv_b_ffn, w_down, g_final):
    w_t = w_in.T
    dt0 = 3 * SB_WIDTH + SSD_WIDTH + SSD_CONV_CH
    w_tail_t = jnp.concatenate([w_t[dt0 + SSD_HEADS:], w_t[dt0:dt0 + SSD_HEADS],
                                jnp.zeros((LANES - SSD_HEADS, D_MODEL), F32)], axis=0)
    Q = SSD_CHUNK
    lanes_t = lambda v: jnp.repeat(v.reshape(SSD_HEADS // 2, 2, 1), Q, axis=2).reshape(SSD_HEADS // 2, 2 * Q)
    pad_l = lambda v: jnp.pad(v, (0, LANES - SSD_HEADS)).reshape(1, LANES)
    return {
        'g_mix': g_mix.reshape(1, -1), 'g_ffn': g_ffn.reshape(1, -1), 'g_final': g_final.reshape(1, -1),
        'w_t': w_t, 'w_tail_t': w_tail_t,
        'conv_w_ssd': conv_w_ssd, 'conv_b_ssd': conv_b_ssd.reshape(1, -1),
        'dt_bias': pad_l(dt_bias), 'dt_bias_t': lanes_t(dt_bias),
        'a_log': pad_l(a_log), 'a_log_t': lanes_t(a_log),
        'd_skip': jnp.repeat(d_skip, SSD_HEAD_DIM).reshape(1, -1), 'g_ssd_norm': g_ssd_norm.reshape(1, -1),
        'w_br_att': w_br_att, 'w_br_ssd': w_br_ssd, 'w_out': w_out,
        'w_up': w_up, 'conv_w_ffn': conv_w_ffn, 'conv_b_ffn': conv_b_ffn.reshape(1, -1),
        'w_down': w_down,
    }


def kernel(x_prompt, x_sample, cache_k, cache_v, state_ssm, state_ssm_conv, state_ffn_conv, meta_tokens, g_mix, w_in, conv_w_ssd, conv_b_ssd, dt_bias, a_log, d_skip, g_ssd_norm, w_br_att, w_br_ssd, w_out, g_ffn, w_up, conv_w_ffn, conv_b_ffn, w_down, g_final):
    bp, seq, _ = x_prompt.shape
    bs, dec = x_sample.shape[:2]
    assert dec == N_META and seq % KVT_TS == 0
    p = _prep_params(g_mix[0], w_in[0], conv_w_ssd[0], conv_b_ssd[0], dt_bias[0], a_log[0], d_skip[0],
                     g_ssd_norm[0], w_br_att[0], w_br_ssd[0], w_out[0], g_ffn[0], w_up[0], conv_w_ffn[0],
                     conv_b_ffn[0], w_down[0], g_final)
    proj = functools.partial(in_proj, g=p['g_mix'], w_t=p['w_t'], w_tail_t=p['w_tail_t'])
    state_shape = (SSD_WIDTH, SSD_STATE)
    rows = lambda a, b: a.reshape(b, -1, a.shape[-1])

    n_rows = cache_k.shape[2]
    n_tiles = (n_rows - N_META) // KEY_TILE
    assert n_tiles * KEY_TILE + N_META == n_rows
    x16 = jnp.concatenate([meta_tokens, x_sample.reshape(bs * dec, D_MODEL)], axis=0)
    k16, v16, big16, dt16 = (rows(a, 1 + bs) for a in proj(x16, with_kv=True))
    big_m, k_m, v_m, big_s, k_s, v_s = big16[:1], k16[:1], v16[:1], big16[1:], k16[1:], v16[1:]
    att_m = attention_step(big_m, 0, k_m, v_m, None, None, 0, 0, SB_WIDTH)
    cache_t = lambda c: c[0].transpose(0, 2, 3, 1).reshape(bs, SB_WIDTH, n_rows)
    att_s = attention_step(big_s, 0, k_s, v_s, cache_t(cache_k), cache_t(cache_v), n_tiles, N_META, SB_WIDTH)
    conv16 = jnp.concatenate([jnp.zeros((1, SSD_CONV - 1, SSD_CONV_CH), F32), state_ssm_conv[0]], axis=0)
    ffn16 = jnp.concatenate([jnp.zeros((1, FFN_CONV - 1, 2 * D_FF), F32), state_ffn_conv[0]], axis=0)
    xbc = BIG_XBC // SSD_CONV_CH, BIG_Z // SSD_WIDTH
    ssd_m, ssm_m = ssd_mixer(big_m, *xbc, dt16[:1], conv16[:1], jnp.zeros((1,) + state_shape, F32), p)
    ssd_s, ssm_s = ssd_mixer(big_s, *xbc, dt16[1:], conv16[1:], state_ssm[0].reshape((bs,) + state_shape), p)
    y16, conv16, ffn16 = ffn_and_out(big16, jnp.concatenate([att_m, att_s], axis=0),
                                     jnp.concatenate([ssd_m, ssd_s], axis=0), x16, conv16, ffn16, p)
    conv_m, ffn_m, y_sample, conv_s, ffn_s = conv16[:1], ffn16[:1], y16[1:], conv16[1:], ffn16[1:]

    xf = x_prompt.reshape(bp * seq, D_MODEL)
    big_x, dt_x = proj(xf, with_kv=False)
    big_x = rows(big_x, bp)
    k_t, v_t, k_end, v_end = kv_transposed(x_prompt, meta_tokens, p['g_mix'], p['w_t'])
    att_x = attention_prompt(big_x, big_m, k_t, v_t)
    att_end = attention_step(big_x, seq // N_META - 1, k_end, v_end, k_t, v_t, seq // KEY_TILE, 0, ATTN_PROMPT_WIDTH)
    att_x = lax.dynamic_update_slice(att_x, att_end, (0, seq - N_META, 0))
    rep = lambda a: jnp.broadcast_to(a, (bp,) + a.shape[1:])
    ssd_x, ssm_p = ssd_mixer(big_x, *xbc, rows(dt_x, bp), rep(conv_m), rep(ssm_m), p)
    y_prompt, conv_p, ffn_p = ffn_and_out(big_x, att_x, ssd_x, xf, rep(conv_m), rep(ffn_m), p)

    heads = lambda a: a.reshape(a.shape[0], a.shape[1], SB_HEADS, SB_HEAD_DIM)[None]
    heads_t = lambda a: a.reshape(a.shape[0], SB_HEADS, SB_HEAD_DIM, a.shape[2]).transpose(0, 3, 1, 2)[None]
    state5 = lambda a: a.reshape(a.shape[0], SSD_HEADS, SSD_HEAD_DIM, SSD_STATE)[None]
    return (y_prompt, y_sample, heads_t(k_t), heads_t(v_t), state5(ssm_p), conv_p[None], ffn_p[None],
            heads(k_s), heads(v_s), state5(ssm_s), conv_s[None], ffn_s[None])
```
